```python
import jax, jax.numpy as jnp
from jax import lax
import numpy as np

D_MODEL = 2048
BATCH = 8
SEQ = 4096
DEPTH = 2

GDN_HEADS = 8
GDN_HEAD_DIM = 128
GDN_WIDTH = GDN_HEADS * GDN_HEAD_DIM
GDN_CHUNK = 64
CONV_WIDTH = 4
LRU_WIDTH = 1024
LRU_BLOCKS = 8
LRU_BLOCK_DIM = LRU_WIDTH // LRU_BLOCKS
LRU_C = 8.0
D_FF = 4 * D_MODEL
N_BRANCHES = 2
RMS_EPS = 1e-6
L2_EPS = 1e-6

SPLIT_SIZES = (3 * GDN_WIDTH, GDN_WIDTH, GDN_HEADS, GDN_HEADS, LRU_WIDTH, LRU_WIDTH, D_MODEL, D_MODEL)
IN_COLS = 3 * GDN_WIDTH + GDN_WIDTH + 2 * GDN_HEADS + 2 * LRU_WIDTH + N_BRANCHES * D_MODEL

kernel_name = 'hybrid_gdn_rglru_gated_merge'


def rms_norm(x, gain):
    xf = x.astype(jnp.float32)
    y = xf * lax.rsqrt(jnp.mean(xf * xf, axis=-1, keepdims=True) + RMS_EPS)
    return (y * gain.astype(jnp.float32)).astype(x.dtype)


def l2_normalize(x):
    return x * lax.rsqrt(jnp.sum(x * x, axis=-1, keepdims=True) + L2_EPS)


def causal_depthwise_conv(x, w):
    width = w.shape[0]
    seq = x.shape[1]
    xp = jnp.pad(x, ((0, 0), (width - 1, 0), (0, 0)))
    y = xp[:, 0:seq, :] * w[0]
    for j in range(1, width):
        y = y + xp[:, j:j + seq, :] * w[j]
    return y


def gated_delta_rule_chunked(q, k, v, g, beta):
    bsz, heads, seq, dk = q.shape
    dv = v.shape[-1]
    c = GDN_CHUNK
    n = seq // c
    q = q * (dk ** -0.5)

    def chunks(t):
        return t.reshape((bsz, heads, n, c) + t.shape[3:])

    q, k, v, g, beta = chunks(q), chunks(k), chunks(v), chunks(g), chunks(beta)
    g = jnp.cumsum(g, axis=-1)
    k_beta = k * beta[..., None]
    v_beta = v * beta[..., None]
    causal = jnp.tril(jnp.ones((c, c), dtype=bool))
    strict = jnp.tril(jnp.ones((c, c), dtype=bool), -1)
    diff = g[..., :, None] - g[..., None, :]
    decay = jnp.where(causal, jnp.exp(jnp.where(causal, diff, 0.0)), 0.0)
    lower = jnp.where(strict, jnp.einsum('bhnid,bhnjd->bhnij', k_beta, k) * decay, 0.0)
    eye = jnp.eye(c, dtype=q.dtype)
    t_inv = lax.linalg.triangular_solve(eye + lower, jnp.broadcast_to(eye, lower.shape),
                                        left_side=True, lower=True, unit_diagonal=True)
    u = jnp.einsum('bhnij,bhnjv->bhniv', t_inv, v_beta)
    w = jnp.einsum('bhnij,bhnjk->bhnik', t_inv, k_beta * jnp.exp(g)[..., None])
    qk = jnp.where(causal, jnp.einsum('bhnid,bhnjd->bhnij', q, k) * decay, 0.0)
    q_dec = q * jnp.exp(g)[..., None]
    g_last = g[..., -1]
    k_dec = k * jnp.exp(g_last[..., None] - g)[..., None]

    xs = (jnp.moveaxis(u, 2, 0), jnp.moveaxis(w, 2, 0), jnp.moveaxis(qk, 2, 0),
          jnp.moveaxis(q_dec, 2, 0), jnp.moveaxis(k_dec, 2, 0), jnp.moveaxis(g_last, 2, 0))

    def step(state, inp):
        u_n, w_n, qk_n, qd_n, kd_n, gl_n = inp
        v_new = u_n - jnp.einsum('bhck,bhkv->bhcv', w_n, state)
        out = (jnp.einsum('bhck,bhkv->bhcv', qd_n, state)
               + jnp.einsum('bhij,bhjv->bhiv', qk_n, v_new))
        state = state * jnp.exp(gl_n)[..., None, None] + jnp.einsum('bhck,bhcv->bhkv', kd_n, v_new)
        return state, out

    state0 = jnp.zeros((bsz, heads, dk, dv), dtype=q.dtype)
    _, out = lax.scan(step, state0, xs)
    return jnp.moveaxis(out, 0, 2).reshape(bsz, heads, seq, dv)


def gated_deltanet_branch(qkv, z, a, b, conv_w, a_log, dt_bias, norm_gain):
    bsz, seq, _ = qkv.shape
    f32 = jnp.float32
    qkv = jax.nn.silu(causal_depthwise_conv(qkv, conv_w))
    q, k, v = jnp.split(qkv, 3, axis=-1)

    def to_heads(t):
        return t.reshape(bsz, seq, GDN_HEADS, GDN_HEAD_DIM).transpose(0, 2, 1, 3).astype(f32)

    q = l2_normalize(to_heads(q))
    k = l2_normalize(to_heads(k))
    v = to_heads(v)
    g = -jnp.exp(a_log.astype(f32)) * jax.nn.softplus(a.astype(f32) + dt_bias.astype(f32))
    beta = jax.nn.sigmoid(b.astype(f32))
    o = gated_delta_rule_chunked(q, k, v, g.transpose(0, 2, 1), beta.transpose(0, 2, 1))
    o = rms_norm(o.transpose(0, 2, 1, 3), norm_gain)
    o = o.astype(z.dtype) * jax.nn.silu(z).reshape(bsz, seq, GDN_HEADS, GDN_HEAD_DIM)
    return o.reshape(bsz, seq, GDN_WIDTH)


def rglru_branch(xb, yb, conv_w, conv_b, w_gate_a, b_gate_a, w_gate_x, b_gate_x, lam):
    bsz, seq, _ = xb.shape
    f32 = jnp.float32
    xc = causal_depthwise_conv(xb, conv_w) + conv_b
    xh = xc.reshape(bsz, seq, LRU_BLOCKS, LRU_BLOCK_DIM)
    r = jax.nn.sigmoid(jnp.einsum('bsni,nij->bsnj', xh, w_gate_a).reshape(bsz, seq, LRU_WIDTH) + b_gate_a)
    i = jax.nn.sigmoid(jnp.einsum('bsni,nij->bsnj', xh, w_gate_x).reshape(bsz, seq, LRU_WIDTH) + b_gate_x)
    log_a = -LRU_C * r.astype(f32) * jax.nn.softplus(-lam.astype(f32))
    a = jnp.exp(log_a)
    bterm = jnp.sqrt(-jnp.expm1(2.0 * log_a)) * (i * xc).astype(f32)

    def combine(left, right):
        a_l, b_l = left
        a_r, b_r = right
        return a_l * a_r, a_r * b_l + b_r

    _, h = lax.associative_scan(combine, (a, bterm), axis=1)
    return h.astype(xb.dtype) * jax.nn.gelu(yb)


def hybrid_layer(x, attn_norm, w_in, gdn_conv_w, gdn_a_log, gdn_dt_bias, gdn_norm,
                 lru_conv_w, lru_conv_b, lru_w_a, lru_b_a, lru_w_x, lru_b_x, lru_lambda,
                 w_branch_gdn, w_branch_lru, w_out, mlp_norm, w_up, w_down):
    h = rms_norm(x, attn_norm)
    proj = h @ w_in
    offsets = []
    acc = 0
    for s in SPLIT_SIZES[:-1]:
        acc += s
        offsets.append(acc)
    qkv, z, a, b, xb, yb, gate_gdn, gate_lru = jnp.split(proj, offsets, axis=-1)
    o_gdn = gated_deltanet_branch(qkv, z, a, b, gdn_conv_w, gdn_a_log, gdn_dt_bias, gdn_norm)
    o_lru = rglru_branch(xb, yb, lru_conv_w, lru_conv_b, lru_w_a, lru_b_a, lru_w_x, lru_b_x, lru_lambda)
    merged = (jax.nn.sigmoid(gate_gdn) * (o_gdn @ w_branch_gdn)
              + jax.nn.sigmoid(gate_lru) * (o_lru @ w_branch_lru))
    x = x + merged @ w_out
    h = rms_norm(x, mlp_norm)
    x = x + jnp.square(jax.nn.relu(h @ w_up)) @ w_down
    return x


def _fwd_setup_inputs(seed: int = 0) -> dict:
    key = jax.random.key(seed)
    ks = jax.random.split(key, 24)
    f32 = jnp.float32
    d, l = D_MODEL, DEPTH

    def normal(k, shape, scale):
        return jax.random.normal(k, shape, f32) * scale

    def gain(k, shape):
        return 1.0 + 0.02 * jax.random.normal(k, shape, f32)

    x = normal(ks[0], (BATCH, SEQ, d), 1.0)
    attn_norm = gain(ks[1], (l, d))
    w_in = normal(ks[2], (l, d, IN_COLS), d ** -0.5)
    gdn_conv_w = normal(ks[3], (l, CONV_WIDTH, 3 * GDN_WIDTH), CONV_WIDTH ** -0.5)
    gdn_a_log = jnp.log(jax.random.uniform(ks[4], (l, GDN_HEADS), f32, 1.0, 16.0))
    dt = jnp.exp(jax.random.uniform(ks[5], (l, GDN_HEADS), f32, np.log(1e-3), np.log(1e-1)))
    gdn_dt_bias = dt + jnp.log(-jnp.expm1(-dt))
    gdn_norm = gain(ks[6], (l, GDN_HEAD_DIM))
    lru_conv_w = normal(ks[7], (l, CONV_WIDTH, LRU_WIDTH), CONV_WIDTH ** -0.5)
    lru_conv_b = normal(ks[8], (l, LRU_WIDTH), 0.01)
    lru_w_a = normal(ks[9], (l, LRU_BLOCKS, LRU_BLOCK_DIM, LRU_BLOCK_DIM), LRU_BLOCK_DIM ** -0.5)
    lru_b_a = normal(ks[10], (l, LRU_WIDTH), 0.01)
    lru_w_x = normal(ks[11], (l, LRU_BLOCKS, LRU_BLOCK_DIM, LRU_BLOCK_DIM), LRU_BLOCK_DIM ** -0.5)
    lru_b_x = normal(ks[12], (l, LRU_WIDTH), 0.01)
    a0 = jax.random.uniform(ks[13], (l, LRU_WIDTH), f32, 0.9, 0.999)
    s0 = a0 ** (1.0 / LRU_C)
    lru_lambda = jnp.log(s0) - jnp.log1p(-s0)
    w_branch_gdn = normal(ks[14], (l, GDN_WIDTH, d), GDN_WIDTH ** -0.5)
    w_branch_lru = normal(ks[15], (l, LRU_WIDTH, d), LRU_WIDTH ** -0.5)
    w_out = normal(ks[16], (l, d, d), d ** -0.5)
    mlp_norm = gain(ks[17], (l, d))
    w_up = normal(ks[18], (l, d, D_FF), d ** -0.5)
    w_down = normal(ks[19], (l, D_FF, d), D_FF ** -0.5)
    final_norm = gain(ks[20], (d,))
    return {'x': x, 'attn_norm': attn_norm, 'w_in': w_in, 'gdn_conv_w': gdn_conv_w,
            'gdn_a_log': gdn_a_log, 'gdn_dt_bias': gdn_dt_bias, 'gdn_norm': gdn_norm,
            'lru_conv_w': lru_conv_w, 'lru_conv_b': lru_conv_b, 'lru_w_a': lru_w_a, 'lru_b_a': lru_b_a,
            'lru_w_x': lru_w_x, 'lru_b_x': lru_b_x, 'lru_lambda': lru_lambda,
            'w_branch_gdn': w_branch_gdn, 'w_branch_lru': w_branch_lru, 'w_out': w_out,
            'mlp_norm': mlp_norm, 'w_up': w_up, 'w_down': w_down, 'final_norm': final_norm}


def _fwd_reference(x, attn_norm, w_in, gdn_conv_w, gdn_a_log, gdn_dt_bias, gdn_norm,
              lru_conv_w, lru_conv_b, lru_w_a, lru_b_a, lru_w_x, lru_b_x, lru_lambda,
              w_branch_gdn, w_branch_lru, w_out, mlp_norm, w_up, w_down, final_norm):
    for layer in range(DEPTH):
        x = hybrid_layer(x, attn_norm[layer], w_in[layer], gdn_conv_w[layer], gdn_a_log[layer],
                         gdn_dt_bias[layer], gdn_norm[layer], lru_conv_w[layer], lru_conv_b[layer],
                         lru_w_a[layer], lru_b_a[layer], lru_w_x[layer], lru_b_x[layer], lru_lambda[layer],
                         w_branch_gdn[layer], w_branch_lru[layer], w_out[layer], mlp_norm[layer],
                         w_up[layer], w_down[layer])
    return rms_norm(x, final_norm)


import jax as _jax
import jax.numpy as _jnp

TWIN_FORMAT = 'train_step'
FWD_PARAMS = ['x', 'attn_norm', 'w_in', 'gdn_conv_w', 'gdn_a_log', 'gdn_dt_bias', 'gdn_norm', 'lru_conv_w', 'lru_conv_b', 'lru_w_a', 'lru_b_a', 'lru_w_x', 'lru_b_x', 'lru_lambda', 'w_branch_gdn', 'w_branch_lru', 'w_out', 'mlp_norm', 'w_up', 'w_down', 'final_norm']
TWIN_WEIGHTS = ['attn_norm', 'w_in', 'gdn_conv_w', 'gdn_a_log', 'gdn_dt_bias', 'gdn_norm', 'lru_conv_w', 'lru_conv_b', 'lru_w_a', 'lru_b_a', 'lru_w_x', 'lru_b_x', 'lru_lambda', 'w_branch_gdn', 'w_branch_lru', 'w_out', 'mlp_norm', 'w_up', 'w_down', 'final_norm']
TWIN_DIFF_INPUT = 'x'
TWIN_INPUTS = ['x', 'attn_norm', 'w_in', 'gdn_conv_w', 'gdn_a_log', 'gdn_dt_bias', 'gdn_norm', 'lru_conv_w', 'lru_conv_b', 'lru_w_a', 'lru_b_a', 'lru_w_x', 'lru_b_x', 'lru_lambda', 'w_branch_gdn', 'w_branch_lru', 'w_out', 'mlp_norm', 'w_up', 'w_down', 'final_norm', 'loss_target', 'm_attn_norm', 'm_w_in', 'm_gdn_conv_w', 'm_gdn_a_log', 'm_gdn_dt_bias', 'm_gdn_norm', 'm_lru_conv_w', 'm_lru_conv_b', 'm_lru_w_a', 'm_lru_b_a', 'm_lru_w_x', 'm_lru_b_x', 'm_lru_lambda', 'm_w_branch_gdn', 'm_w_branch_lru', 'm_w_out', 'm_mlp_norm', 'm_w_up', 'm_w_down', 'm_final_norm', 'v_attn_norm', 'v_w_in', 'v_gdn_conv_w', 'v_gdn_a_log', 'v_gdn_dt_bias', 'v_gdn_norm', 'v_lru_conv_w', 'v_lru_conv_b', 'v_lru_w_a', 'v_lru_b_a', 'v_lru_w_x', 'v_lru_b_x', 'v_lru_lambda', 'v_w_branch_gdn', 'v_w_branch_lru', 'v_w_out', 'v_mlp_norm', 'v_w_up', 'v_w_down', 'v_final_norm']
TWIN_OUTPUTS = ['loss', 'grad_x', 'grad_attn_norm', 'grad_w_in', 'grad_gdn_conv_w', 'grad_gdn_a_log', 'grad_gdn_dt_bias', 'grad_gdn_norm', 'grad_lru_conv_w', 'grad_lru_conv_b', 'grad_lru_w_a', 'grad_lru_b_a', 'grad_lru_w_x', 'grad_lru_b_x', 'grad_lru_lambda', 'grad_w_branch_gdn', 'grad_w_branch_lru', 'grad_w_out', 'grad_mlp_norm', 'grad_w_up', 'grad_w_down', 'grad_final_norm', 'delta_attn_norm', 'delta_w_in', 'delta_gdn_conv_w', 'delta_gdn_a_log', 'delta_gdn_dt_bias', 'delta_gdn_norm', 'delta_lru_conv_w', 'delta_lru_conv_b', 'delta_lru_w_a', 'delta_lru_b_a', 'delta_lru_w_x', 'delta_lru_b_x', 'delta_lru_lambda', 'delta_w_branch_gdn', 'delta_w_branch_lru', 'delta_w_out', 'delta_mlp_norm', 'delta_w_up', 'delta_w_down', 'delta_final_norm', 'new_m_attn_norm', 'new_m_w_in', 'new_m_gdn_conv_w', 'new_m_gdn_a_log', 'new_m_gdn_dt_bias', 'new_m_gdn_norm', 'new_m_lru_conv_w', 'new_m_lru_conv_b', 'new_m_lru_w_a', 'new_m_lru_b_a', 'new_m_lru_w_x', 'new_m_lru_b_x', 'new_m_lru_lambda', 'new_m_w_branch_gdn', 'new_m_w_branch_lru', 'new_m_w_out', 'new_m_mlp_norm', 'new_m_w_up', 'new_m_w_down', 'new_m_final_norm', 'new_v_attn_norm', 'new_v_w_in', 'new_v_gdn_conv_w', 'new_v_gdn_a_log', 'new_v_gdn_dt_bias', 'new_v_gdn_norm', 'new_v_lru_conv_w', 'new_v_lru_conv_b', 'new_v_lru_w_a', 'new_v_lru_b_a', 'new_v_lru_w_x', 'new_v_lru_b_x', 'new_v_lru_lambda', 'new_v_w_branch_gdn', 'new_v_w_branch_lru', 'new_v_w_out', 'new_v_mlp_norm', 'new_v_w_up', 'new_v_w_down', 'new_v_final_norm']
TWIN_LEAF_KINDS = {'loss': 'loss', 'grad_x': 'grad_x', 'grad_attn_norm': 'grad_w', 'grad_w_in': 'grad_w', 'grad_gdn_conv_w': 'grad_w', 'grad_gdn_a_log': 'grad_w', 'grad_gdn_dt_bias': 'grad_w', 'grad_gdn_norm': 'grad_w', 'grad_lru_conv_w': 'grad_w', 'grad_lru_conv_b': 'grad_w', 'grad_lru_w_a': 'grad_w', 'grad_lru_b_a': 'grad_w', 'grad_lru_w_x': 'grad_w', 'grad_lru_b_x': 'grad_w', 'grad_lru_lambda': 'grad_w', 'grad_w_branch_gdn': 'grad_w', 'grad_w_branch_lru': 'grad_w', 'grad_w_out': 'grad_w', 'grad_mlp_norm': 'grad_w', 'grad_w_up': 'grad_w', 'grad_w_down': 'grad_w', 'grad_final_norm': 'grad_w', 'delta_attn_norm': 'delta_w', 'delta_w_in': 'delta_w', 'delta_gdn_conv_w': 'delta_w', 'delta_gdn_a_log': 'delta_w', 'delta_gdn_dt_bias': 'delta_w', 'delta_gdn_norm': 'delta_w', 'delta_lru_conv_w': 'delta_w', 'delta_lru_conv_b': 'delta_w', 'delta_lru_w_a': 'delta_w', 'delta_lru_b_a': 'delta_w', 'delta_lru_w_x': 'delta_w', 'delta_lru_b_x': 'delta_w', 'delta_lru_lambda': 'delta_w', 'delta_w_branch_gdn': 'delta_w', 'delta_w_branch_lru': 'delta_w', 'delta_w_out': 'delta_w', 'delta_mlp_norm': 'delta_w', 'delta_w_up': 'delta_w', 'delta_w_down': 'delta_w', 'delta_final_norm': 'delta_w', 'new_m_attn_norm': 'new_m', 'new_m_w_in': 'new_m', 'new_m_gdn_conv_w': 'new_m', 'new_m_gdn_a_log': 'new_m', 'new_m_gdn_dt_bias': 'new_m', 'new_m_gdn_norm': 'new_m', 'new_m_lru_conv_w': 'new_m', 'new_m_lru_conv_b': 'new_m', 'new_m_lru_w_a': 'new_m', 'new_m_lru_b_a': 'new_m', 'new_m_lru_w_x': 'new_m', 'new_m_lru_b_x': 'new_m', 'new_m_lru_lambda': 'new_m', 'new_m_w_branch_gdn': 'new_m', 'new_m_w_branch_lru': 'new_m', 'new_m_w_out': 'new_m', 'new_m_mlp_norm': 'new_m', 'new_m_w_up': 'new_m', 'new_m_w_down': 'new_m', 'new_m_final_norm': 'new_m', 'new_v_attn_norm': 'new_v', 'new_v_w_in': 'new_v', 'new_v_gdn_conv_w': 'new_v', 'new_v_gdn_a_log': 'new_v', 'new_v_gdn_dt_bias': 'new_v', 'new_v_gdn_norm': 'new_v', 'new_v_lru_conv_w': 'new_v', 'new_v_lru_conv_b': 'new_v', 'new_v_lru_w_a': 'new_v', 'new_v_lru_b_a': 'new_v', 'new_v_lru_w_x': 'new_v', 'new_v_lru_b_x': 'new_v', 'new_v_lru_lambda': 'new_v', 'new_v_w_branch_gdn': 'new_v', 'new_v_w_branch_lru': 'new_v', 'new_v_w_out': 'new_v', 'new_v_mlp_norm': 'new_v', 'new_v_w_up': 'new_v', 'new_v_w_down': 'new_v', 'new_v_final_norm': 'new_v'}


def _forward(args):
    return _fwd_reference(*[args[k] for k in FWD_PARAMS])


def _output_shape():
    def fwd():
        inp = _fwd_setup_inputs(0)
        return _fwd_reference(*[inp[k] for k in FWD_PARAMS])
    out = _jax.eval_shape(fwd)
    return out.shape, out.dtype

N_MICROBATCH = 1
ADAM_LR = 0.001
ADAM_B1 = 0.9
ADAM_B2 = 0.999
ADAM_EPS = 1e-08
ADAM_WD = 0.01
ADAM_STEP = 10
PER_EXAMPLE_BATCH_AXIS = {'x': 0, 'loss_target': 0}
SHARED_INPUTS = []
_WEIGHT_DTYPES = {'attn_norm': _jnp.float32, 'w_in': _jnp.float32, 'gdn_conv_w': _jnp.float32, 'gdn_a_log': _jnp.float32, 'gdn_dt_bias': _jnp.float32, 'gdn_norm': _jnp.float32, 'lru_conv_w': _jnp.float32, 'lru_conv_b': _jnp.float32, 'lru_w_a': _jnp.float32, 'lru_b_a': _jnp.float32, 'lru_w_x': _jnp.float32, 'lru_b_x': _jnp.float32, 'lru_lambda': _jnp.float32, 'w_branch_gdn': _jnp.float32, 'w_branch_lru': _jnp.float32, 'w_out': _jnp.float32, 'mlp_norm': _jnp.float32, 'w_up': _jnp.float32, 'w_down': _jnp.float32, 'final_norm': _jnp.float32}
MOMENT_SCALE = {'attn_norm': 7.141652e-02, 'w_in': 3.069683e-02, 'gdn_conv_w': 2.799824e-02, 'gdn_a_log': 1.365555e-01, 'gdn_dt_bias': 1.284342e-01, 'gdn_norm': 1.014112e-01, 'lru_conv_w': 6.395193e-02, 'lru_conv_b': 3.482947e-01, 'lru_w_a': 1.014879e-02, 'lru_b_a': 1.443101e-02, 'lru_w_x': 1.925438e-02, 'lru_b_x': 2.478685e-02, 'lru_lambda': 3.677241e-02, 'w_branch_gdn': 2.649288e-02, 'w_branch_lru': 3.957935e-02, 'w_out': 4.600025e-02, 'mlp_norm': 7.533582e-02, 'w_up': 3.748057e-02, 'w_down': 8.483317e-02, 'final_norm': 1.627940e+01}


def _to_microbatches(a, axis):
    t = _jnp.moveaxis(a, axis, 0)
    t = t.reshape((N_MICROBATCH, t.shape[0] // N_MICROBATCH) + t.shape[1:])
    return _jnp.moveaxis(t, 1, axis + 1)


def setup_inputs(seed: int = 0) -> dict:
    inp = _fwd_setup_inputs(seed)
    key = _jax.random.fold_in(_jax.random.key(seed), 7919)
    shape, _ = _output_shape()
    out = dict(inp)
    out["loss_target"] = _jax.random.normal(_jax.random.fold_in(key, 0), shape, _jnp.float32)
    for i, name in enumerate(TWIN_WEIGHTS):
        w = inp[name].astype(_jnp.float32)
        if MOMENT_SCALE is None:
            s = _jnp.sqrt(_jnp.mean(_jnp.square(w)) + 1e-30)
        else:
            s = MOMENT_SCALE[name]
        km, kv = _jax.random.split(_jax.random.fold_in(key, i + 1))
        out[name] = w
        out["m_" + name] = s * _jax.random.normal(km, w.shape, _jnp.float32)
        out["v_" + name] = (s * s) * _jax.random.uniform(kv, w.shape, _jnp.float32, 0.5, 1.5)
    if N_MICROBATCH > 1:
        for name, axis in PER_EXAMPLE_BATCH_AXIS.items():
            out[name] = _to_microbatches(out[name], axis)
    return {'x': out['x'], 'attn_norm': out['attn_norm'], 'w_in': out['w_in'], 'gdn_conv_w': out['gdn_conv_w'], 'gdn_a_log': out['gdn_a_log'], 'gdn_dt_bias': out['gdn_dt_bias'], 'gdn_norm': out['gdn_norm'], 'lru_conv_w': out['lru_conv_w'], 'lru_conv_b': out['lru_conv_b'], 'lru_w_a': out['lru_w_a'], 'lru_b_a': out['lru_b_a'], 'lru_w_x': out['lru_w_x'], 'lru_b_x': out['lru_b_x'], 'lru_lambda': out['lru_lambda'], 'w_branch_gdn': out['w_branch_gdn'], 'w_branch_lru': out['w_branch_lru'], 'w_out': out['w_out'], 'mlp_norm': out['mlp_norm'], 'w_up': out['w_up'], 'w_down': out['w_down'], 'final_norm': out['final_norm'], 'loss_target': out['loss_target'], 'm_attn_norm': out['m_attn_norm'], 'm_w_in': out['m_w_in'], 'm_gdn_conv_w': out['m_gdn_conv_w'], 'm_gdn_a_log': out['m_gdn_a_log'], 'm_gdn_dt_bias': out['m_gdn_dt_bias'], 'm_gdn_norm': out['m_gdn_norm'], 'm_lru_conv_w': out['m_lru_conv_w'], 'm_lru_conv_b': out['m_lru_conv_b'], 'm_lru_w_a': out['m_lru_w_a'], 'm_lru_b_a': out['m_lru_b_a'], 'm_lru_w_x': out['m_lru_w_x'], 'm_lru_b_x': out['m_lru_b_x'], 'm_lru_lambda': out['m_lru_lambda'], 'm_w_branch_gdn': out['m_w_branch_gdn'], 'm_w_branch_lru': out['m_w_branch_lru'], 'm_w_out': out['m_w_out'], 'm_mlp_norm': out['m_mlp_norm'], 'm_w_up': out['m_w_up'], 'm_w_down': out['m_w_down'], 'm_final_norm': out['m_final_norm'], 'v_attn_norm': out['v_attn_norm'], 'v_w_in': out['v_w_in'], 'v_gdn_conv_w': out['v_gdn_conv_w'], 'v_gdn_a_log': out['v_gdn_a_log'], 'v_gdn_dt_bias': out['v_gdn_dt_bias'], 'v_gdn_norm': out['v_gdn_norm'], 'v_lru_conv_w': out['v_lru_conv_w'], 'v_lru_conv_b': out['v_lru_conv_b'], 'v_lru_w_a': out['v_lru_w_a'], 'v_lru_b_a': out['v_lru_b_a'], 'v_lru_w_x': out['v_lru_w_x'], 'v_lru_b_x': out['v_lru_b_x'], 'v_lru_lambda': out['v_lru_lambda'], 'v_w_branch_gdn': out['v_w_branch_gdn'], 'v_w_branch_lru': out['v_w_branch_lru'], 'v_w_out': out['v_w_out'], 'v_mlp_norm': out['v_mlp_norm'], 'v_w_up': out['v_w_up'], 'v_w_down': out['v_w_down'], 'v_final_norm': out['v_final_norm']}


def _loss(weights, diff, rest, loss_target):
    with _jax.named_scope("forward"):
        args = {**rest, TWIN_DIFF_INPUT: diff, **{k: w.astype(_WEIGHT_DTYPES[k]) for k, w in weights.items()}}
        y = _forward(args)
    with _jax.named_scope("loss_head"):
        err = _jnp.square(y.astype(_jnp.float32) - loss_target)
        return 0.5 * _jnp.sum(_jnp.mean(err, axis=-1)) if err.ndim else 0.5 * err


def _adamw(w, g, m, v):
    m = ADAM_B1 * m + (1.0 - ADAM_B1) * g
    v = ADAM_B2 * v + (1.0 - ADAM_B2) * _jnp.square(g)
    m_hat = m / (1.0 - ADAM_B1 ** ADAM_STEP)
    v_hat = v / (1.0 - ADAM_B2 ** ADAM_STEP)
    delta = -ADAM_LR * (m_hat / (_jnp.sqrt(v_hat) + ADAM_EPS) + ADAM_WD * w)
    return delta, m, v


def reference(x, attn_norm, w_in, gdn_conv_w, gdn_a_log, gdn_dt_bias, gdn_norm, lru_conv_w, lru_conv_b, lru_w_a, lru_b_a, lru_w_x, lru_b_x, lru_lambda, w_branch_gdn, w_branch_lru, w_out, mlp_norm, w_up, w_down, final_norm, loss_target, m_attn_norm, m_w_in, m_gdn_conv_w, m_gdn_a_log, m_gdn_dt_bias, m_gdn_norm, m_lru_conv_w, m_lru_conv_b, m_lru_w_a, m_lru_b_a, m_lru_w_x, m_lru_b_x, m_lru_lambda, m_w_branch_gdn, m_w_branch_lru, m_w_out, m_mlp_norm, m_w_up, m_w_down, m_final_norm, v_attn_norm, v_w_in, v_gdn_conv_w, v_gdn_a_log, v_gdn_dt_bias, v_gdn_norm, v_lru_conv_w, v_lru_conv_b, v_lru_w_a, v_lru_b_a, v_lru_w_x, v_lru_b_x, v_lru_lambda, v_w_branch_gdn, v_w_branch_lru, v_w_out, v_mlp_norm, v_w_up, v_w_down, v_final_norm):
    given = dict(x=x, attn_norm=attn_norm, w_in=w_in, gdn_conv_w=gdn_conv_w, gdn_a_log=gdn_a_log, gdn_dt_bias=gdn_dt_bias, gdn_norm=gdn_norm, lru_conv_w=lru_conv_w, lru_conv_b=lru_conv_b, lru_w_a=lru_w_a, lru_b_a=lru_b_a, lru_w_x=lru_w_x, lru_b_x=lru_b_x, lru_lambda=lru_lambda, w_branch_gdn=w_branch_gdn, w_branch_lru=w_branch_lru, w_out=w_out, mlp_norm=mlp_norm, w_up=w_up, w_down=w_down, final_norm=final_norm, loss_target=loss_target, m_attn_norm=m_attn_norm, m_w_in=m_w_in, m_gdn_conv_w=m_gdn_conv_w, m_gdn_a_log=m_gdn_a_log, m_gdn_dt_bias=m_gdn_dt_bias, m_gdn_norm=m_gdn_norm, m_lru_conv_w=m_lru_conv_w, m_lru_conv_b=m_lru_conv_b, m_lru_w_a=m_lru_w_a, m_lru_b_a=m_lru_b_a, m_lru_w_x=m_lru_w_x, m_lru_b_x=m_lru_b_x, m_lru_lambda=m_lru_lambda, m_w_branch_gdn=m_w_branch_gdn, m_w_branch_lru=m_w_branch_lru, m_w_out=m_w_out, m_mlp_norm=m_mlp_norm, m_w_up=m_w_up, m_w_down=m_w_down, m_final_norm=m_final_norm, v_attn_norm=v_attn_norm, v_w_in=v_w_in, v_gdn_conv_w=v_gdn_conv_w, v_gdn_a_log=v_gdn_a_log, v_gdn_dt_bias=v_gdn_dt_bias, v_gdn_norm=v_gdn_norm, v_lru_conv_w=v_lru_conv_w, v_lru_conv_b=v_lru_conv_b, v_lru_w_a=v_lru_w_a, v_lru_b_a=v_lru_b_a, v_lru_w_x=v_lru_w_x, v_lru_b_x=v_lru_b_x, v_lru_lambda=v_lru_lambda, v_w_branch_gdn=v_w_branch_gdn, v_w_branch_lru=v_w_branch_lru, v_w_out=v_w_out, v_mlp_norm=v_mlp_norm, v_w_up=v_w_up, v_w_down=v_w_down, v_final_norm=v_final_norm)
    weights = {n: given[n] for n in TWIN_WEIGHTS}
    shared = {n: given[n] for n in SHARED_INPUTS}
    per_example = {n: given[n] for n in ['x']}
    grad_fn = _jax.value_and_grad(_loss, argnums=(0, 1))

    def one_microbatch(ex, loss_target):
        ex = dict(ex)
        diff = ex.pop(TWIN_DIFF_INPUT)
        return grad_fn(weights, diff, {**shared, **ex}, loss_target)

    if N_MICROBATCH == 1:
        loss, (grad_w, grad_x) = one_microbatch(per_example, given["loss_target"])
    else:
        def body(carry, xs):
            loss_sum, grad_sum = carry
            l_k, (gw_k, gx_k) = one_microbatch(xs[0], xs[1])
            with _jax.named_scope("update"):
                return (loss_sum + l_k, _jax.tree.map(_jnp.add, grad_sum, gw_k)), gx_k

        init = (_jnp.zeros((), _jnp.float32), _jax.tree.map(_jnp.zeros_like, weights))
        (loss, grad_w), grad_x = _jax.lax.scan(body, init, (per_example, given["loss_target"]))
    with _jax.named_scope("update"):
        delta_w, new_m, new_v = {}, {}, {}
        for n in TWIN_WEIGHTS:
            delta_w[n], new_m[n], new_v[n] = _adamw(weights[n], grad_w[n], given["m_" + n], given["v_" + n])
    return (loss, grad_x, *[grad_w[n] for n in TWIN_WEIGHTS], *[delta_w[n] for n in TWIN_WEIGHTS],
            *[new_m[n] for n in TWIN_WEIGHTS], *[new_v[n] for n in TWIN_WEIGHTS])
```

```python
import functools

import jax
import jax.numpy as jnp
from jax import lax
from jax.experimental import pallas as pl
from jax.experimental.pallas import tpu as pltpu

F32 = jnp.float32
BF = jnp.bfloat16
MESH = pl.DeviceIdType.MESH
N_DEV = 8
N_CHIP = 4

HEADS = 8
HEAD_DIM = 128
GDN_WIDTH = HEADS * HEAD_DIM
CHUNK = 64
CONV_WIDTH = 4
LRU_WIDTH = 1024
LRU_BLOCKS = 8
LRU_BLOCK_DIM = 128
LRU_C = 8.0
RMS_EPS = 1e-6
L2_EPS = 1e-6
ADAM_LR = 0.001
ADAM_B1 = 0.9
ADAM_B2 = 0.999
ADAM_EPS = 1e-08
ADAM_WD = 0.01
ADAM_STEP = 10

LANE = 128
VMEM_LIMIT = 56 * 1024 * 1024


def _cparams(sem=None):
    return pltpu.CompilerParams(dimension_semantics=sem, vmem_limit_bytes=VMEM_LIMIT)


def _tile(n, pref, unit=LANE):
    if n <= pref:
        return n
    best = None
    for t in range(unit, pref + 1, unit):
        if n % t == 0:
            best = t
    assert best is not None, (n, pref, unit)
    return best


_NN = (((1,), (0,)), ((), ()))
_NT = (((1,), (1,)), ((), ()))
_TN = (((0,), (0,)), ((), ()))


def _dg(a, b, dims):
    return lax.dot_general(a.astype(BF), b.astype(BF), dims, preferred_element_type=F32)


def _p_nn(a, b):
    return _dg(a, b, _NN)


def _p_nt(a, b):
    return _dg(a, b, _NT)


def _p_tn(a, b):
    return _dg(a, b, _TN)


@jax.custom_vjp
def _d_nn(a, b):
    return _dg(a, b, _NN)


@jax.custom_vjp
def _d_nt(a, b):
    return _dg(a, b, _NT)


@jax.custom_vjp
def _d_tn(a, b):
    return _dg(a, b, _TN)


def _save(f):
    return lambda a, b: (f(a, b), (a, b))


_d_nn.defvjp(_save(_d_nn), lambda r, g: (_d_nt(g, r[1]), _d_tn(r[0], g)))
_d_nt.defvjp(_save(_d_nt), lambda r, g: (_d_nn(g, r[1]), _d_tn(g, r[0])))
_d_tn.defvjp(_save(_d_tn), lambda r, g: (_d_nt(r[1], g), _d_nn(r[0], g)))


class _MM:
    def __init__(self, nn, nt, tn):
        self.nn, self.nt, self.tn = nn, nt, tn


_PLAIN = _MM(_p_nn, _p_nt, _p_tn)
_DIFF = _MM(_d_nn, _d_nt, _d_tn)


def _sigmoid(x):
    return 1.0 / (1.0 + jnp.exp(-x))


def _silu(x):
    return x * _sigmoid(x)


def _softplus(x):
    return jnp.maximum(x, 0.0) + jnp.log(1.0 + jnp.exp(-jnp.maximum(x, -x)))


def _gelu(x):
    return 0.5 * x * (1.0 + jnp.tanh(0.7978845608028654 * (x + 0.044715 * (x * x * x))))


def _expm1(x):
    s = x * (1.0 + x * (1.0 / 2.0) * (1.0 + x * (1.0 / 3.0) * (1.0 + x * (1.0 / 4.0) * (
        1.0 + x * (1.0 / 5.0) * (1.0 + x * (1.0 / 6.0) * (1.0 + x * (1.0 / 7.0)))))))
    return jnp.where(jnp.maximum(x, -x) < 0.3, s, jnp.exp(x) - 1.0)


def _rms(x, gain):
    ms = jnp.mean(x * x, axis=-1, keepdims=True)
    return x * lax.rsqrt(ms + RMS_EPS) * gain


def _l2n(x):
    return x * lax.rsqrt(jnp.sum(x * x, axis=-1, keepdims=True) + L2_EPS)


def _shift_down(x, r):
    if r == 0:
        return x
    row = lax.broadcasted_iota(jnp.int32, x.shape, 0)
    return jnp.where(row >= r, pltpu.roll(x, r, 0), 0.0)


def _shift_up(x, r):
    if r == 0:
        return x
    n = x.shape[0]
    row = lax.broadcasted_iota(jnp.int32, x.shape, 0)
    return jnp.where(row < n - r, pltpu.roll(x, n - r, 0), 0.0)


def _conv(x, w_ref):
    y = None
    for j in range(CONV_WIDTH):
        t = _shift_down(x, CONV_WIDTH - 1 - j) * w_ref[j:j + 1, :]
        y = t if y is None else y + t
    return y


def _conv_bwd(x, dy, w_ref):
    dx = None
    dws = []
    for j in range(CONV_WIDTH):
        r = CONV_WIDTH - 1 - j
        t = _shift_up(dy, r) * w_ref[j:j + 1, :]
        dx = t if dx is None else dx + t
        dws.append(jnp.sum(dy * _shift_down(x, r), axis=0, keepdims=True))
    return dx, dws


def _scan_tile(a, b, reverse):
    n = a.shape[0]
    row = lax.broadcasted_iota(jnp.int32, a.shape, 0)
    s = 1
    while s < n:
        if reverse:
            keep = row < n - s
            a_sh = jnp.where(keep, pltpu.roll(a, n - s, 0), 1.0)
            b_sh = jnp.where(keep, pltpu.roll(b, n - s, 0), 0.0)
        else:
            keep = row >= s
            a_sh = jnp.where(keep, pltpu.roll(a, s, 0), 1.0)
            b_sh = jnp.where(keep, pltpu.roll(b, s, 0), 0.0)
        b = a * b_sh + b
        a = a * a_sh
        s *= 2
    return a, b


def _dg3(a, b, dims):
    a_hi = a.astype(BF)
    b_hi = b.astype(BF)
    a_lo = a - a_hi.astype(F32)
    b_lo = b - b_hi.astype(F32)
    return _dg(a_hi, b_hi, dims) + (_dg(a_hi, b_lo, dims) + _dg(a_lo, b_hi, dims))


@jax.custom_vjp
def _tri_inv(a):
    n = a.shape[0]
    p = _dg3(a, a, _NN)
    r = p
    e = 2
    while 2 * e < n:
        p = _dg3(p, p, _NN)
        r = r + p + _dg3(r, p, _NN)
        e *= 2
    row = lax.broadcasted_iota(jnp.int32, a.shape, 0)
    col = lax.broadcasted_iota(jnp.int32, a.shape, 1)
    eye = jnp.where(row == col, 1.0, 0.0)
    return eye - a + r - _dg3(a, r, _NN)


def _tri_inv_fwd(a):
    t = _tri_inv(a)
    return t, t


def _tri_inv_bwd(t, g):
    return (-_dg3(_dg3(t, g, _TN), t, _NT),)


_tri_inv.defvjp(_tri_inv_fwd, _tri_inv_bwd)


def _gdn_chunk(q, k, v, beta, gcol, grow, state, mm):
    c = q.shape[0]
    row = lax.broadcasted_iota(jnp.int32, (c, c), 0)
    col = lax.broadcasted_iota(jnp.int32, (c, c), 1)
    causal = row >= col
    strict = row > col
    qn = _l2n(q) * (HEAD_DIM ** -0.5)
    kn = _l2n(k)
    dec = jnp.where(causal, jnp.exp(jnp.where(causal, gcol - grow, 0.0)), 0.0)
    kb = kn * beta
    vb = v * beta
    a = jnp.where(strict, mm.nt(kb, kn) * dec, 0.0)
    t = _tri_inv(a)
    eg = jnp.exp(gcol)
    u = mm.nn(t, vb)
    w = mm.nn(t, kb * eg)
    p = jnp.where(causal, mm.nt(qn, kn) * dec, 0.0)
    vn = u - mm.nn(w, state)
    o = mm.nn(qn * eg, state) + mm.nn(p, vn)
    last = lax.broadcasted_iota(jnp.int32, (c, 1), 0) == c - 1
    gl = jnp.sum(jnp.where(last, gcol, 0.0), axis=0, keepdims=True)
    kd = kn * jnp.exp(gl - gcol)
    new_state = state * jnp.exp(gl) + mm.tn(kd, vn)
    return o, new_state


def _lru_gates(xc, wa, ba, wx, bx, lam, mm):
    r = _sigmoid(mm.nn(xc, wa) + ba)
    i = _sigmoid(mm.nn(xc, wx) + bx)
    log_a = -LRU_C * r * _softplus(-lam)
    a = jnp.exp(log_a)
    bterm = jnp.sqrt(-_expm1(2.0 * log_a)) * (i * xc)
    return a, bterm


def _matmul(a, b, mode, name, out_dtypes, epi=None, extras=(), tm=1024, tn=1024, tk=512):
    if mode == "nn":
        (m, k), (k2, n) = a.shape, b.shape
    elif mode == "nt":
        (m, k), (n, k2) = a.shape, b.shape
    else:
        (k, m), (k2, n) = a.shape, b.shape
    assert k == k2, (a.shape, b.shape, mode)
    tm, tn, tk = _tile(m, tm), _tile(n, tn), _tile(k, tk)
    nk = k // tk
    dims = {"nn": _NN, "nt": _NT, "tn": _TN}[mode]
    if mode == "nn":
        a_spec = pl.BlockSpec((tm, tk), lambda i, j, kk: (i, kk))
        b_spec = pl.BlockSpec((tk, tn), lambda i, j, kk: (kk, j))
    elif mode == "nt":
        a_spec = pl.BlockSpec((tm, tk), lambda i, j, kk: (i, kk))
        b_spec = pl.BlockSpec((tn, tk), lambda i, j, kk: (j, kk))
    else:
        a_spec = pl.BlockSpec((tk, tm), lambda i, j, kk: (kk, i))
        b_spec = pl.BlockSpec((tk, tn), lambda i, j, kk: (kk, j))
    o_spec = pl.BlockSpec((tm, tn), lambda i, j, kk: (i, j))
    n_ex, n_out = len(extras), len(out_dtypes)
    if epi is None:
        epi = lambda acc: (acc,)

    def body(*refs):
        a_ref, b_ref = refs[0], refs[1]
        ex_refs = refs[2:2 + n_ex]
        out_refs = refs[2 + n_ex:2 + n_ex + n_out]
        acc_ref = refs[-1]
        kk = pl.program_id(2)

        @pl.when(kk == 0)
        def _():
            acc_ref[...] = jnp.zeros_like(acc_ref)

        acc_ref[...] += _dg(a_ref[...], b_ref[...], dims)

        @pl.when(kk == nk - 1)
        def _():
            res = epi(acc_ref[...], *[r[...] for r in ex_refs])
            for r, v in zip(out_refs, res):
                r[...] = v.astype(r.dtype)

    outs = pl.pallas_call(
        body, name=name, grid=(m // tm, n // tn, nk),
        in_specs=[a_spec, b_spec] + [o_spec] * n_ex,
        out_specs=[o_spec] * n_out,
        out_shape=[jax.ShapeDtypeStruct((m, n), d) for d in out_dtypes],
        scratch_shapes=[pltpu.VMEM((tm, tn), F32)],
        compiler_params=_cparams(("parallel", "parallel", "arbitrary")),
    )(a, b, *extras)
    return outs[0] if n_out == 1 else outs


def _rowwise(fn, name, rows, tr, row_ins, params, row_outs, acc_outs=(), ncol=1):
    tr = _tile(rows, tr, 8)
    n_in, n_par, n_ro, n_acc = len(row_ins), len(params), len(row_outs), len(acc_outs)
    in_specs = []
    for _, width, base, per_col in row_ins:
        if per_col:
            in_specs.append(pl.BlockSpec((tr, width), lambda i, j, base=base: (i, base + j)))
        else:
            in_specs.append(pl.BlockSpec((tr, width), lambda i, j, base=base: (i, base)))
    for p in params:
        in_specs.append(pl.BlockSpec(p.shape, lambda i, j: (0, 0)))
    out_specs, out_shape = [], []
    for total, dtype, width, per_col in row_outs:
        if per_col:
            out_specs.append(pl.BlockSpec((tr, width), lambda i, j: (i, j)))
        else:
            out_specs.append(pl.BlockSpec((tr, width), lambda i, j: (i, 0)))
        out_shape.append(jax.ShapeDtypeStruct((rows, total), dtype))
    for shape, dtype in acc_outs:
        out_specs.append(pl.BlockSpec(shape, lambda i, j: (0, 0)))
        out_shape.append(jax.ShapeDtypeStruct(shape, dtype))

    def body(*refs):
        ins = [r[...] for r in refs[:n_in + n_par]]
        ro = refs[n_in + n_par:n_in + n_par + n_ro]
        ao = refs[n_in + n_par + n_ro:]
        res = fn(*ins)
        for r, v in zip(ro, res[:n_ro]):
            r[...] = v.astype(r.dtype)
        first = jnp.logical_and(pl.program_id(0) == 0, pl.program_id(1) == 0)

        @pl.when(first)
        def _():
            for r in ao:
                r[...] = jnp.zeros_like(r)

        for r, v in zip(ao, res[n_ro:]):
            r[...] += v.astype(r.dtype)

    sem = ("arbitrary", "arbitrary") if n_acc else ("parallel", "parallel")
    outs = pl.pallas_call(
        body, name=name, grid=(rows // tr, ncol), in_specs=in_specs, out_specs=out_specs,
        out_shape=out_shape, compiler_params=_cparams(sem),
    )(*[r[0] for r in row_ins], *params)
    return outs


def _row(x):
    return x.reshape(1, -1)


def _rms_fwd(x, gain, name):
    s, d = x.shape
    return _rowwise(lambda xt, g: (_rms(xt, g),), name, s, 512, [(x, d, 0, False)], [_row(gain)],
                    [(d, BF, d, False)])[0]


def _rms_bwd(x, gain, dh, dres, name):
    s, d = x.shape

    def fn(xt, dht, drt, g):
        _, vjp = jax.vjp(_rms, xt, g)
        dx, dg = vjp(dht)
        return drt + dx, dg

    return _rowwise(fn, name, s, 256, [(x, d, 0, False), (dh, d, 0, False), (dres, d, 0, False)], [_row(gain)],
                    [(d, F32, d, False)], [((1, d), F32)])


def _loss_head(x, gain, target, name):
    s, d = x.shape

    def fn(xt, tt, g):
        y, vjp = jax.vjp(_rms, xt, g)
        err = y - tt
        dx, dg = vjp(err * (1.0 / d))
        part = 0.5 * jnp.sum(jnp.sum(err * err, axis=1, keepdims=True), axis=0, keepdims=True) * (1.0 / d)
        return dx, jnp.broadcast_to(part, (1, LANE)), dg

    return _rowwise(fn, name, s, 256, [(x, d, 0, False), (target, d, 0, False)], [_row(gain)],
                    [(d, F32, d, False)], [((1, LANE), F32), ((1, d), F32)])


def _gdn_post_fn(o, z, g):
    return _rms(o, g) * _silu(z)


def _gdn_post_fwd(o, proj, gain, z_base, name):
    s = o.shape[0]
    return _rowwise(lambda ot, zt, g: (_gdn_post_fn(ot, zt, g),), name, s, 1024,
                    [(o, HEAD_DIM, 0, True), (proj, HEAD_DIM, z_base, True)], [_row(gain)],
                    [(GDN_WIDTH, BF, HEAD_DIM, True)], ncol=HEADS)[0]


def _gdn_post_bwd(o, proj, gain, dog, z_base, name):
    s = o.shape[0]

    def fn(ot, zt, dt, g):
        _, vjp = jax.vjp(_gdn_post_fn, ot, zt, g)
        return vjp(dt)

    return _rowwise(fn, name, s, 1024,
                    [(o, HEAD_DIM, 0, True), (proj, HEAD_DIM, z_base, True), (dog, HEAD_DIM, 0, True)],
                    [_row(gain)], [(GDN_WIDTH, F32, HEAD_DIM, True), (GDN_WIDTH, BF, HEAD_DIM, True)],
                    [((1, HEAD_DIM), F32)], ncol=HEADS)


def _merge_fn(gg, gl, yg, yl):
    return _sigmoid(gg) * yg + _sigmoid(gl) * yl


def _merge_fwd(proj, yg, yl, gg_base, name):
    s, d = yg.shape
    return _rowwise(lambda a, b, c, e: (_merge_fn(a, b, c, e),), name, s, 256,
                    [(proj, d, gg_base, False), (proj, d, gg_base + 1, False), (yg, d, 0, False), (yl, d, 0, False)],
                    [], [(d, BF, d, False)])[0]


def _merge_bwd(proj, yg, yl, dmerged, gg_base, name):
    s, d = yg.shape

    def fn(a, b, c, e, dm):
        _, vjp = jax.vjp(_merge_fn, a, b, c, e)
        return vjp(dm)

    return _rowwise(fn, name, s, 128,
                    [(proj, d, gg_base, False), (proj, d, gg_base + 1, False), (yg, d, 0, False), (yl, d, 0, False),
                     (dmerged, d, 0, False)], [], [(d, BF, d, False)] * 4)


def _conv_silu_fwd(proj, w, name):
    s = proj.shape[0]
    cols = w.shape[1]
    cw = _tile(cols, LANE)

    def body(x_ref, w_ref, o_ref):
        o_ref[...] = _silu(_conv(x_ref[...], w_ref))

    return pl.pallas_call(
        body, name=name, grid=(cols // cw,),
        in_specs=[pl.BlockSpec((s, cw), lambda j: (0, j)), pl.BlockSpec((CONV_WIDTH, cw), lambda j: (0, j))],
        out_specs=pl.BlockSpec((s, cw), lambda j: (0, j)),
        out_shape=jax.ShapeDtypeStruct((s, cols), F32), compiler_params=_cparams(("parallel",)),
    )(proj, w)


def _conv_silu_bwd(proj, w, dout, name):
    s = proj.shape[0]
    cols = w.shape[1]
    cw = _tile(cols, LANE)

    def body(x_ref, w_ref, d_ref, dx_ref, dw_ref):
        x = x_ref[...]
        y = _conv(x, w_ref)
        sg = _sigmoid(y)
        dy = d_ref[...] * (sg * (1.0 + y * (1.0 - sg)))
        dx, dws = _conv_bwd(x, dy, w_ref)
        dx_ref[...] = dx.astype(dx_ref.dtype)
        for j in range(CONV_WIDTH):
            dw_ref[j:j + 1, :] = dws[j]

    return pl.pallas_call(
        body, name=name, grid=(cols // cw,),
        in_specs=[pl.BlockSpec((s, cw), lambda j: (0, j)), pl.BlockSpec((CONV_WIDTH, cw), lambda j: (0, j)),
                  pl.BlockSpec((s, cw), lambda j: (0, j))],
        out_specs=[pl.BlockSpec((s, cw), lambda j: (0, j)), pl.BlockSpec((CONV_WIDTH, cw), lambda j: (0, j))],
        out_shape=[jax.ShapeDtypeStruct((s, cols), BF), jax.ShapeDtypeStruct((CONV_WIDTH, cols), F32)],
        compiler_params=_cparams(("parallel",)),
    )(proj, w, dout)


def _cumsum_mask(tr, transpose):
    row = lax.broadcasted_iota(jnp.int32, (tr, tr), 0)
    col = lax.broadcasted_iota(jnp.int32, (tr, tr), 1)
    same = (row // CHUNK) == (col // CHUNK)
    tri = (row <= col) if transpose else (row >= col)
    return jnp.where(jnp.logical_and(same, tri), 1.0, 0.0)


def _lane_pad(v):
    return jnp.pad(v.reshape(1, -1), ((0, 0), (0, LANE - v.shape[0])))


def _gdn_gates_fwd(proj, a_log, dt_bias, ab_base, name):
    s = proj.shape[0]

    def fn(ab, al, dt):
        g = -jnp.exp(al) * _softplus(ab + dt)
        beta = _sigmoid(pltpu.roll(ab, LANE - HEADS, 1))
        gc = jnp.dot(_cumsum_mask(ab.shape[0], False), g, precision=lax.Precision.HIGHEST,
                     preferred_element_type=F32)
        return gc, beta

    return _rowwise(fn, name, s, 512, [(proj, LANE, ab_base, False)], [_lane_pad(a_log), _lane_pad(dt_bias)],
                    [(LANE, F32, LANE, False)] * 2)


def _gdn_gates_bwd(proj, a_log, dt_bias, dgc_col, dgc_row, dbeta, ab_base, name):
    s = proj.shape[0]

    def fn(ab, dc, dr, db, al, dt):
        lane = lax.broadcasted_iota(jnp.int32, ab.shape, 1)
        dg = jnp.dot(_cumsum_mask(ab.shape[0], True), dc + dr, precision=lax.Precision.HIGHEST,
                     preferred_element_type=F32)
        ea = jnp.exp(al)
        pre = ab + dt
        g = -ea * _softplus(pre)
        da = jnp.where(lane < HEADS, dg * (-ea) * _sigmoid(pre), 0.0)
        dal = jnp.sum(jnp.where(lane < HEADS, dg * g, 0.0), axis=0, keepdims=True)
        ddt = jnp.sum(da, axis=0, keepdims=True)
        beta = _sigmoid(pltpu.roll(ab, LANE - HEADS, 1))
        dbl = jnp.where(lane < HEADS, db * beta * (1.0 - beta), 0.0)
        dab = da + pltpu.roll(dbl, HEADS, 1)
        return dab, dal, ddt

    return _rowwise(fn, name, s, 512,
                    [(proj, LANE, ab_base, False), (dgc_col, LANE, 0, False), (dgc_row, LANE, 0, False),
                     (dbeta, LANE, 0, False)], [_lane_pad(a_log), _lane_pad(dt_bias)],
                    [(LANE, BF, LANE, False)], [((1, LANE), F32), ((1, LANE), F32)])


def _head_col(blk, h):
    lane = lax.broadcasted_iota(jnp.int32, blk.shape, 1)
    return jnp.sum(jnp.where(lane == h, blk, 0.0), axis=1, keepdims=True)


def _gdn_chunk_fwd(qkv, gc, beta, grow, name):
    s = qkv.shape[0]
    nc = s // CHUNK

    def body(q_ref, k_ref, v_ref, gc_ref, b_ref, gr_ref, o_ref, st_ref, state):
        c, h = pl.program_id(0), pl.program_id(1)

        @pl.when(c == 0)
        def _():
            state[h] = jnp.zeros((HEAD_DIM, HEAD_DIM), F32)

        st = state[h]
        st_ref[0, 0] = st
        o, new = _gdn_chunk(q_ref[...], k_ref[...], v_ref[...], _head_col(b_ref[...], h), _head_col(gc_ref[...], h),
                            gr_ref[0, pl.ds(h, 1), :], st, _PLAIN)
        o_ref[...] = o
        state[h] = new

    blk = lambda off: pl.BlockSpec((CHUNK, HEAD_DIM), lambda c, h, off=off: (c, off + h))
    return pl.pallas_call(
        body, name=name, grid=(nc, HEADS),
        in_specs=[blk(0), blk(HEADS), blk(2 * HEADS),
                  pl.BlockSpec((CHUNK, LANE), lambda c, h: (c, 0)), pl.BlockSpec((CHUNK, LANE), lambda c, h: (c, 0)),
                  pl.BlockSpec((1, HEADS, CHUNK), lambda c, h: (c, 0, 0))],
        out_specs=[pl.BlockSpec((CHUNK, HEAD_DIM), lambda c, h: (c, h)),
                   pl.BlockSpec((1, 1, HEAD_DIM, HEAD_DIM), lambda c, h: (c, h, 0, 0))],
        out_shape=[jax.ShapeDtypeStruct((s, GDN_WIDTH), F32),
                   jax.ShapeDtypeStruct((nc, HEADS, HEAD_DIM, HEAD_DIM), F32)],
        scratch_shapes=[pltpu.VMEM((HEADS, HEAD_DIM, HEAD_DIM), F32)],
        compiler_params=_cparams(("arbitrary", "arbitrary")),
    )(qkv, qkv, qkv, gc, beta, grow)


def _gdn_chunk_bwd(qkv, gc, beta, grow, states, do, name):
    s = qkv.shape[0]
    nc = s // CHUNK

    def body(q_ref, k_ref, v_ref, gc_ref, b_ref, gr_ref, st_ref, do_ref,
             dq_ref, dk_ref, dv_ref, dgc_ref, db_ref, dgr_ref, dstate):
        c, h = pl.program_id(0), pl.program_id(1)

        @pl.when(c == 0)
        def _():
            dstate[h] = jnp.zeros((HEAD_DIM, HEAD_DIM), F32)

        f = functools.partial(_gdn_chunk, mm=_DIFF)
        _, vjp = jax.vjp(f, q_ref[...], k_ref[...], v_ref[...], _head_col(b_ref[...], h), _head_col(gc_ref[...], h),
                         gr_ref[0, pl.ds(h, 1), :], st_ref[0, 0])
        dq, dk, dv, dbeta, dgcol, dgrow, dst = vjp((do_ref[...], dstate[h]))
        dq_ref[...] = dq
        dk_ref[...] = dk
        dv_ref[...] = dv
        dstate[h] = dst
        lane = lax.broadcasted_iota(jnp.int32, (CHUNK, LANE), 1)

        @pl.when(h == 0)
        def _():
            dgc_ref[...] = jnp.zeros_like(dgc_ref)
            db_ref[...] = jnp.zeros_like(db_ref)

        dgc_ref[...] += jnp.where(lane == h, dgcol, 0.0)
        db_ref[...] += jnp.where(lane == h, dbeta, 0.0)
        dgr_ref[0, pl.ds(h, 1), :] = dgrow

    rev = lambda c: nc - 1 - c
    blk = lambda off: pl.BlockSpec((CHUNK, HEAD_DIM), lambda c, h, off=off: (rev(c), off + h))
    lanes = pl.BlockSpec((CHUNK, LANE), lambda c, h: (rev(c), 0))
    rows = pl.BlockSpec((1, HEADS, CHUNK), lambda c, h: (rev(c), 0, 0))
    return pl.pallas_call(
        body, name=name, grid=(nc, HEADS),
        in_specs=[blk(0), blk(HEADS), blk(2 * HEADS), lanes, lanes, rows,
                  pl.BlockSpec((1, 1, HEAD_DIM, HEAD_DIM), lambda c, h: (rev(c), h, 0, 0)),
                  pl.BlockSpec((CHUNK, HEAD_DIM), lambda c, h: (rev(c), h))],
        out_specs=[pl.BlockSpec((CHUNK, HEAD_DIM), lambda c, h: (rev(c), h))] * 3 + [lanes, lanes, rows],
        out_shape=[jax.ShapeDtypeStruct((s, GDN_WIDTH), F32)] * 3
        + [jax.ShapeDtypeStruct((s, LANE), F32)] * 2 + [jax.ShapeDtypeStruct((nc, HEADS, CHUNK), F32)],
        scratch_shapes=[pltpu.VMEM((HEADS, HEAD_DIM, HEAD_DIM), F32)],
        compiler_params=_cparams(("arbitrary", "arbitrary")),
    )(qkv, qkv, qkv, gc, beta, grow, states, do)


def _lru_specs(s, xb_base):
    col = lambda base: pl.BlockSpec((s, LRU_BLOCK_DIM), lambda n, base=base: (0, base + n))
    vec = pl.BlockSpec((1, LRU_BLOCK_DIM), lambda n: (0, n))
    mat = pl.BlockSpec((None, LRU_BLOCK_DIM, LRU_BLOCK_DIM), lambda n: (n, 0, 0))
    cw = pl.BlockSpec((CONV_WIDTH, LRU_BLOCK_DIM), lambda n: (0, n))
    return col, vec, mat, cw


def _lru_fwd(proj, cw, cb, wa, ba, wx, bx, lam, xb_base, name):
    s = proj.shape[0]
    tt = _tile(s, 256, 8)
    col, vec, mat, cws = _lru_specs(s, xb_base)

    def body(xb_ref, yb_ref, cw_ref, cb_ref, wa_ref, ba_ref, wx_ref, bx_ref, lam_ref, o_ref, h_ref, xc_s):
        xc_s[...] = _conv(xb_ref[...], cw_ref) + cb_ref[...]
        par = (wa_ref[...], ba_ref[...], wx_ref[...], bx_ref[...], lam_ref[...])

        def step(t, carry):
            sl = pl.ds(pl.multiple_of(t * tt, tt), tt)
            a, bt = _lru_gates(xc_s[sl, :], *par, _PLAIN)
            aa, bb = _scan_tile(a, bt, False)
            h = aa * carry + bb
            h_ref[sl, :] = h
            o_ref[sl, :] = (h * _gelu(yb_ref[sl, :])).astype(o_ref.dtype)
            return h_ref[pl.ds(t * tt + tt - 1, 1), :]

        lax.fori_loop(0, s // tt, step, jnp.zeros((1, LRU_BLOCK_DIM), F32))

    nb = LRU_BLOCKS
    return pl.pallas_call(
        body, name=name, grid=(LRU_BLOCKS,),
        in_specs=[col(xb_base), col(xb_base + nb), cws, vec, mat, vec, mat, vec, vec],
        out_specs=[col(0), col(0)],
        out_shape=[jax.ShapeDtypeStruct((s, LRU_WIDTH), BF), jax.ShapeDtypeStruct((s, LRU_WIDTH), F32)],
        scratch_shapes=[pltpu.VMEM((s, LRU_BLOCK_DIM), F32)],
        compiler_params=_cparams(("parallel",)),
    )(proj, proj, cw, _row(cb), wa, _row(ba), wx, _row(bx), _row(lam))


def _lru_bwd(proj, hs, dol, cw, cb, wa, ba, wx, bx, lam, xb_base, name):
    s = proj.shape[0]
    tt = _tile(s, 256, 8)
    nt = s // tt
    col, vec, mat, cws = _lru_specs(s, xb_base)

    def body(xb_ref, yb_ref, h_ref, d_ref, cw_ref, cb_ref, wa_ref, ba_ref, wx_ref, bx_ref, lam_ref,
             dxb_ref, dyb_ref, dcw_ref, dcb_ref, dwa_ref, dba_ref, dwx_ref, dbx_ref, dlam_ref,
             xc_s, a_s, dh_s, dxc_s):
        xc_s[...] = _conv(xb_ref[...], cw_ref) + cb_ref[...]
        par = (wa_ref[...], ba_ref[...], wx_ref[...], bx_ref[...], lam_ref[...])
        row = lax.broadcasted_iota(jnp.int32, (tt, LRU_BLOCK_DIM), 0)
        tile = lambda t: pl.ds(pl.multiple_of(t * tt, tt), tt)

        def prep(t, carry):
            sl = tile(t)
            a_s[sl, :] = _lru_gates(xc_s[sl, :], *par, _PLAIN)[0]
            d = d_ref[sl, :]
            gy, vjp_y = jax.vjp(_gelu, yb_ref[sl, :])
            dyb_ref[sl, :] = vjp_y(d * h_ref[sl, :])[0].astype(dyb_ref.dtype)
            dh_s[sl, :] = d * gy
            return carry

        lax.fori_loop(0, nt, prep, 0)

        def rscan(i, carry):
            dh_next, a_next = carry
            t = nt - 1 - i
            sl = tile(t)
            a_sh = jnp.where(row == tt - 1, a_next, pltpu.roll(a_s[sl, :], tt - 1, 0))
            aa, bb = _scan_tile(a_sh, dh_s[sl, :], True)
            dh_s[sl, :] = aa * dh_next + bb
            first = pl.ds(t * tt, 1)
            return dh_s[first, :], a_s[first, :]

        zero = jnp.zeros((1, LRU_BLOCK_DIM), F32)
        lax.fori_loop(0, nt, rscan, (zero, zero))

        def gates_vjp(t, acc):
            sl = tile(t)
            _, vjp_g = jax.vjp(functools.partial(_lru_gates, mm=_DIFF), xc_s[sl, :], *par)
            dh = dh_s[sl, :]
            before = jnp.where(t > 0, h_ref[pl.ds(jnp.maximum(t * tt - 1, 0), 1), :], 0.0)
            h_prev = jnp.where(row == 0, before, pltpu.roll(h_ref[sl, :], 1, 0))
            dxc, *dpar = vjp_g((dh * h_prev, dh))
            dxc_s[sl, :] = dxc
            return tuple(x + y for x, y in zip(acc, dpar))

        dwa, dba, dwx, dbx, dlam = lax.fori_loop(0, nt, gates_vjp, tuple(jnp.zeros_like(p) for p in par))
        dwa_ref[...] = dwa
        dba_ref[...] = dba
        dwx_ref[...] = dwx
        dbx_ref[...] = dbx
        dlam_ref[...] = dlam
        dxc = dxc_s[...]
        dcb_ref[...] = jnp.sum(dxc, axis=0, keepdims=True)
        dxb, dws = _conv_bwd(xb_ref[...], dxc, cw_ref)
        dxb_ref[...] = dxb.astype(dxb_ref.dtype)
        for j in range(CONV_WIDTH):
            dcw_ref[j:j + 1, :] = dws[j]

    nb = LRU_BLOCKS
    w = LRU_WIDTH
    return pl.pallas_call(
        body, name=name, grid=(LRU_BLOCKS,),
        in_specs=[col(xb_base), col(xb_base + nb), col(0), col(0), cws, vec, mat, vec, mat, vec, vec],
        out_specs=[col(0), col(0), cws, vec, mat, vec, mat, vec, vec],
        out_shape=[jax.ShapeDtypeStruct((s, w), BF), jax.ShapeDtypeStruct((s, w), BF),
                   jax.ShapeDtypeStruct((CONV_WIDTH, w), F32), jax.ShapeDtypeStruct((1, w), F32),
                   jax.ShapeDtypeStruct(wa.shape, F32), jax.ShapeDtypeStruct((1, w), F32),
                   jax.ShapeDtypeStruct(wx.shape, F32), jax.ShapeDtypeStruct((1, w), F32),
                   jax.ShapeDtypeStruct((1, w), F32)],
        scratch_shapes=[pltpu.VMEM((s, LRU_BLOCK_DIM), F32)] * 4,
        compiler_params=_cparams(("parallel",)),
    )(proj, proj, hs, dol, cw, _row(cb), wa, _row(ba), wx, _row(bx), _row(lam))


def _place():
    return lax.axis_index("x"), lax.axis_index("y"), lax.axis_index("c")


def _all_gather(blocks, name):
    n = len(blocks)
    per = N_DEV - 1

    def body(*refs):
        ins, outs = refs[:n], refs[n:2 * n]
        send_sems, recv_sems, local_sems = refs[2 * n:]
        x, y, c = _place()
        me, sibling = (x, y, c), (x, y, 1 - c)
        chips = [(1 - x, y), (x, 1 - y), (1 - x, 1 - y)]

        def copy(a, k, block, to, src=None):
            dst = outs[a].at[4 * block[0] + 2 * block[1] + block[2]]
            return pltpu.make_async_remote_copy(
                src_ref=dst if src is None else src, dst_ref=dst, send_sem=send_sems.at[a * per + k],
                recv_sem=recv_sems.at[a * per + k], device_id=to, device_id_type=MESH)

        started = []
        for a in range(n):
            mine = pltpu.make_async_copy(ins[a], outs[a].at[4 * x + 2 * y + c], local_sems.at[a])
            mine.start()
            started.append(mine)
        sends = []
        for a in range(n):
            first = [copy(a, 0, me, sibling, src=ins[a])]
            first += [copy(a, 1 + j, me, (*chip, c), src=ins[a]) for j, chip in enumerate(chips)]
            for cp in first:
                cp.start()
            sends += first
        for a in range(n):
            for j, chip in enumerate(chips):
                copy(a, 1 + j, (*chip, c), me).wait_recv()
                passed = copy(a, 4 + j, (*chip, c), sibling)
                passed.start()
                sends.append(passed)
        for a in range(n):
            copy(a, 0, sibling, me).wait_recv()
            for j, chip in enumerate(chips):
                copy(a, 4 + j, (*chip, 1 - c), me).wait_recv()
        for cp in sends:
            cp.wait_send()
        for cp in started:
            cp.wait()

    any_spec = pl.BlockSpec(memory_space=pl.ANY)
    return pl.pallas_call(
        body, name=name, in_specs=[any_spec] * n, out_specs=[any_spec] * n,
        out_shape=[jax.ShapeDtypeStruct((N_DEV,) + b.shape, b.dtype) for b in blocks],
        scratch_shapes=[pltpu.SemaphoreType.DMA((n * per,)), pltpu.SemaphoreType.DMA((n * per,)),
                        pltpu.SemaphoreType.DMA((n,))],
    )(*blocks)


def _sibling_exchange(stacks, name):
    n = len(stacks)

    def body(*refs):
        ins, outs = refs[:n], refs[n:2 * n]
        send_sems, recv_sems = refs[2 * n:]
        x, y, c = _place()
        copies = []
        for a in range(n):
            for j in range(N_CHIP):
                cp = pltpu.make_async_remote_copy(
                    src_ref=ins[a].at[2 * j + (1 - c)], dst_ref=outs[a].at[j], send_sem=send_sems.at[a * N_CHIP + j],
                    recv_sem=recv_sems.at[a * N_CHIP + j], device_id=(x, y, 1 - c), device_id_type=MESH)
                cp.start()
                copies.append(cp)
        for cp in copies:
            cp.wait()

    any_spec = pl.BlockSpec(memory_space=pl.ANY)
    return pl.pallas_call(
        body, name=name, in_specs=[any_spec] * n, out_specs=[any_spec] * n,
        out_shape=[jax.ShapeDtypeStruct((N_CHIP,) + g.shape[1:], g.dtype) for g in stacks],
        scratch_shapes=[pltpu.SemaphoreType.DMA((n * N_CHIP,)), pltpu.SemaphoreType.DMA((n * N_CHIP,))],
    )(*stacks)


def _chip_exchange(parts, name):
    n = len(parts)

    def body(*refs):
        ins, outs = refs[:n], refs[n:2 * n]
        send_sems, recv_sems, local_sems = refs[2 * n:]
        x, y, c = _place()
        my = 2 * x + y
        for a in range(n):
            pltpu.make_async_copy(ins[a].at[my], outs[a].at[my], local_sems.at[a]).start()
        for a in range(n):
            for j in range(N_CHIP):
                @pl.when(j != my)
                def _(a=a, j=j):
                    pltpu.make_async_remote_copy(
                        src_ref=ins[a].at[j], dst_ref=outs[a].at[my], send_sem=send_sems.at[a * N_CHIP + j],
                        recv_sem=recv_sems.at[a * N_CHIP + my], device_id=(j // 2, j % 2, c),
                        device_id_type=MESH).start()
        for a in range(n):
            for j in range(N_CHIP):
                @pl.when(j != my)
                def _(a=a, j=j):
                    pltpu.make_async_remote_copy(
                        src_ref=ins[a].at[j], dst_ref=outs[a].at[j], send_sem=send_sems.at[a * N_CHIP + j],
                        recv_sem=recv_sems.at[a * N_CHIP + j], device_id=(j // 2, j % 2, c),
                        device_id_type=MESH).wait()
        for a in range(n):
            pltpu.make_async_copy(ins[a].at[my], outs[a].at[my], local_sems.at[a]).wait()

    any_spec = pl.BlockSpec(memory_space=pl.ANY)
    return pl.pallas_call(
        body, name=name, in_specs=[any_spec] * n, out_specs=[any_spec] * n,
        out_shape=[jax.ShapeDtypeStruct(p.shape, p.dtype) for p in parts],
        scratch_shapes=[pltpu.SemaphoreType.DMA((n * N_CHIP,)), pltpu.SemaphoreType.DMA((n * N_CHIP,)),
                        pltpu.SemaphoreType.DMA((n,))],
    )(*parts)


def _sibling_add(stack, recv, core, name):
    _, r, c = stack.shape
    tr = _tile(r, 256, 8)

    def body(core_ref, g_ref, p_ref, o_ref):
        o_ref[...] = (g_ref[...].astype(F32) + p_ref[...].astype(F32)).astype(o_ref.dtype)

    return pl.pallas_call(
        body, name=name,
        grid_spec=pltpu.PrefetchScalarGridSpec(
            num_scalar_prefetch=1, grid=(N_CHIP, r // tr),
            in_specs=[pl.BlockSpec((1, tr, c), lambda j, i, core_ref: (2 * j + core_ref[0], i, 0)),
                      pl.BlockSpec((1, tr, c), lambda j, i, core_ref: (j, i, 0))],
            out_specs=pl.BlockSpec((1, tr, c), lambda j, i, core_ref: (j, i, 0))),
        out_shape=jax.ShapeDtypeStruct((N_CHIP, r, c), stack.dtype),
        compiler_params=_cparams(("parallel", "parallel")),
    )(core, stack, recv)


def _adamw_math(w, g, m, v):
    m = ADAM_B1 * m + (1.0 - ADAM_B1) * g
    v = ADAM_B2 * v + (1.0 - ADAM_B2) * (g * g)
    m_hat = m / (1.0 - ADAM_B1 ** ADAM_STEP)
    v_hat = v / (1.0 - ADAM_B2 ** ADAM_STEP)
    delta = -ADAM_LR * (m_hat / (jnp.sqrt(v_hat) + ADAM_EPS) + ADAM_WD * w)
    return delta, m, v


def _adamw_sharded(parts, w, m, v, layer, prev, name):
    depth, r, c = w.shape
    tr = _tile(r, 128, 8)
    n_prev = 0 if prev is None else 4

    def body(*refs):
        q_ref, w_ref, m_ref, v_ref = refs[:4]
        g_ref, d_ref, nm_ref, nv_ref = refs[4 + n_prev:]
        g = q_ref[0].astype(F32)
        for j in range(1, N_CHIP):
            g = g + q_ref[j].astype(F32)
        delta, nm, nv = _adamw_math(w_ref[...], g, m_ref[...], v_ref[...])
        g_ref[...] = g
        d_ref[...] = delta
        nm_ref[...] = nm
        nv_ref[...] = nv

    lay = pl.BlockSpec((None, tr, c), lambda i: (layer, i, 0))
    any_spec = pl.BlockSpec(memory_space=pl.ANY)
    return pl.pallas_call(
        body, name=name, grid=(r // tr,),
        in_specs=[pl.BlockSpec((N_CHIP, tr, c), lambda i: (0, i, 0)), lay, lay, lay] + [any_spec] * n_prev,
        out_specs=[lay] * 4, out_shape=[jax.ShapeDtypeStruct(w.shape, F32)] * 4,
        input_output_aliases={4 + i: i for i in range(n_prev)},
        compiler_params=_cparams(("parallel",)),
    )(parts, w, m, v, *(prev or ()))


def _sum_devices(stack, name):
    _, r, c = stack.shape
    tr = _tile(r, 512, 8)

    def body(s_ref, o_ref):
        g = s_ref[0]
        for j in range(1, N_DEV):
            g = g + s_ref[j]
        o_ref[...] = g

    return pl.pallas_call(
        body, name=name, grid=(r // tr,), in_specs=[pl.BlockSpec((N_DEV, tr, c), lambda i: (0, i, 0))],
        out_specs=pl.BlockSpec((tr, c), lambda i: (i, 0)), out_shape=jax.ShapeDtypeStruct((r, c), F32),
        compiler_params=_cparams(("parallel",)),
    )(stack)


def _adamw_packed(w, g, m, v, name):
    r, c = w.shape
    tr = _tile(r, 512, 8)

    def body(w_ref, g_ref, m_ref, v_ref, d_ref, nm_ref, nv_ref):
        delta, nm, nv = _adamw_math(w_ref[...], g_ref[...], m_ref[...], v_ref[...])
        d_ref[...] = delta
        nm_ref[...] = nm
        nv_ref[...] = nv

    spec = pl.BlockSpec((tr, c), lambda i: (i, 0))
    return pl.pallas_call(
        body, name=name, grid=(r // tr,), in_specs=[spec] * 4, out_specs=[spec] * 3,
        out_shape=[jax.ShapeDtypeStruct((r, c), F32)] * 3, compiler_params=_cparams(("parallel",)),
    )(w, g, m, v)


def _pack(arrays):
    unit = 8 * LANE
    flat = []
    for a in arrays:
        f = a.reshape(-1)
        flat.append(jnp.pad(f, (0, (-f.shape[0]) % unit)))
    return jnp.concatenate(flat).reshape(-1, LANE)


def _unpack(packed, like):
    unit = 8 * LANE
    flat = packed.reshape(-1)
    out, off = [], 0
    for a in like:
        n = 1
        for d in a.shape:
            n *= d
        out.append(flat[off:off + n].reshape(a.shape))
        off += n + (-n) % unit
    return out


class _Layout:
    def __init__(self, d):
        self.d = d
        self.n_front = 4 * GDN_WIDTH
        self.n_ab = 2 * HEADS
        self.n_back = 2 * LRU_WIDTH + 2 * d
        self.cols = self.n_front + self.n_ab + self.n_back
        self.pcols = self.n_front + self.n_back + LANE
        self.z_base = 3 * GDN_WIDTH // HEAD_DIM
        self.xb_base = self.n_front // LRU_BLOCK_DIM
        self.gg_off = self.n_front + 2 * LRU_WIDTH
        assert self.gg_off % d == 0
        self.gg_base = self.gg_off // d
        self.ab_base = (self.n_front + self.n_back) // LANE

    def permute(self, w):
        f, ab = self.n_front, self.n_ab
        pad = jnp.zeros((w.shape[0], LANE - ab), w.dtype)
        return jnp.concatenate([w[:, :f], w[:, f + ab:], w[:, f:f + ab], pad], axis=1)

    def unpermute(self, w):
        f, ab, b = self.n_front, self.n_ab, self.n_back
        return jnp.concatenate([w[:, :f], w[:, f + b:f + b + ab], w[:, f:f + b]], axis=1)


def _cols_from_stack(stack):
    nd, k, n = stack.shape
    return stack.transpose(1, 0, 2).reshape(k, nd * n)


def _stack_from_cols(w):
    k, n = w.shape
    return w.reshape(k, N_DEV, n // N_DEV).transpose(1, 0, 2)


def _layer_fwd(x, p, lay, tag):
    s, d = x.shape
    nc = s // CHUNK
    h = _rms_fwd(x, p["attn_norm"], f"rms1_fwd{tag}")
    proj = _matmul(h, p["w_in"], "nn", f"in_proj{tag}", [F32], tn=1152)
    qkv = _conv_silu_fwd(proj, p["gdn_conv_w"], f"gdn_conv_fwd{tag}")
    gc, beta = _gdn_gates_fwd(proj, p["gdn_a_log"], p["gdn_dt_bias"], lay.ab_base, f"gdn_gates_fwd{tag}")
    grow = gc[:, :HEADS].reshape(nc, CHUNK, HEADS).transpose(0, 2, 1)
    o, states = _gdn_chunk_fwd(qkv, gc, beta, grow, f"gdn_chunk_fwd{tag}")
    og = _gdn_post_fwd(o, proj, p["gdn_norm"], lay.z_base, f"gdn_post_fwd{tag}")
    ol, hs = _lru_fwd(proj, p["lru_conv_w"], p["lru_conv_b"], p["lru_w_a"], p["lru_b_a"], p["lru_w_x"], p["lru_b_x"],
                      p["lru_lambda"], lay.xb_base, f"lru_fwd{tag}")
    yg = _matmul(og, p["w_branch_gdn"], "nn", f"branch_gdn{tag}", [F32])
    yl = _matmul(ol, p["w_branch_lru"], "nn", f"branch_lru{tag}", [F32])
    merged = _merge_fwd(proj, yg, yl, lay.gg_base, f"merge_fwd{tag}")
    x1 = _matmul(merged, p["w_out"], "nn", f"out_proj{tag}", [F32], epi=lambda acc, r: (acc + r,), extras=[x])
    h2 = _rms_fwd(x1, p["mlp_norm"], f"rms2_fwd{tag}")
    u, act = _matmul(h2, p["w_up"], "nn", f"mlp_up{tag}", [F32, BF],
                     epi=lambda acc: (acc, jnp.square(jnp.maximum(acc, 0.0))))
    x2 = _matmul(act, p["w_down"], "nn", f"mlp_down{tag}", [F32], epi=lambda acc, r: (acc + r,), extras=[x1])
    saved = dict(x=x, h=h, proj=proj, qkv=qkv, gc=gc, beta=beta, grow=grow, o=o, states=states, og=og, ol=ol, hs=hs,
                 yg=yg, yl=yl, merged=merged, x1=x1, h2=h2, u=u, act=act)
    return x2, saved


def _layer_bwd(dx2, p, sv, lay, tag):
    s, d = dx2.shape
    nc = s // CHUNK
    big, small = {}, {}
    du = _matmul(dx2, p["w_down"], "nt", f"d_act{tag}", [BF],
                 epi=lambda acc, u: (acc * (2.0 * jnp.maximum(u, 0.0)),), extras=[sv["u"]])
    big["w_down"] = _matmul(sv["act"], dx2, "tn", f"dw_down{tag}", [BF]).reshape(N_DEV, -1, d)
    big["w_up"] = _stack_from_cols(_matmul(sv["h2"], du, "tn", f"dw_up{tag}", [BF]))
    dh2 = _matmul(du, p["w_up"], "nt", f"d_h2{tag}", [F32])
    dx1, g = _rms_bwd(sv["x1"], p["mlp_norm"], dh2, dx2, f"rms2_bwd{tag}")
    small["mlp_norm"] = g.reshape(-1)
    dmerged = _matmul(dx1, p["w_out"], "nt", f"d_merged{tag}", [F32])
    big["w_out"] = _matmul(sv["merged"], dx1, "tn", f"dw_out{tag}", [BF]).reshape(N_DEV, -1, d)
    dgg, dgl, dyg, dyl = _merge_bwd(sv["proj"], sv["yg"], sv["yl"], dmerged, lay.gg_base, f"merge_bwd{tag}")
    big["w_branch_gdn"] = _stack_from_cols(_matmul(sv["og"], dyg, "tn", f"dw_bg{tag}", [BF]))
    big["w_branch_lru"] = _stack_from_cols(_matmul(sv["ol"], dyl, "tn", f"dw_bl{tag}", [BF]))
    dog = _matmul(dyg, p["w_branch_gdn"], "nt", f"d_og{tag}", [F32])
    dol = _matmul(dyl, p["w_branch_lru"], "nt", f"d_ol{tag}", [F32])
    (dxb, dyb, dcw, dcb, dwa, dba, dwx, dbx, dlam) = _lru_bwd(
        sv["proj"], sv["hs"], dol, p["lru_conv_w"], p["lru_conv_b"], p["lru_w_a"], p["lru_b_a"], p["lru_w_x"],
        p["lru_b_x"], p["lru_lambda"], lay.xb_base, f"lru_bwd{tag}")
    small.update(lru_conv_w=dcw, lru_conv_b=dcb.reshape(-1), lru_w_a=dwa, lru_b_a=dba.reshape(-1), lru_w_x=dwx,
                 lru_b_x=dbx.reshape(-1), lru_lambda=dlam.reshape(-1))
    do, dz, g = _gdn_post_bwd(sv["o"], sv["proj"], p["gdn_norm"], dog, lay.z_base, f"gdn_post_bwd{tag}")
    small["gdn_norm"] = g.reshape(-1)
    dq, dk, dv, dgc_col, dbeta, dgrow = _gdn_chunk_bwd(sv["qkv"], sv["gc"], sv["beta"], sv["grow"], sv["states"], do,
                                                       f"gdn_chunk_bwd{tag}")
    dqkv_c = jnp.concatenate([dq, dk, dv], axis=1)
    dqkv, dgcw = _conv_silu_bwd(sv["proj"], p["gdn_conv_w"], dqkv_c, f"gdn_conv_bwd{tag}")
    small["gdn_conv_w"] = dgcw
    dgc_row = jnp.pad(dgrow.transpose(0, 2, 1).reshape(s, HEADS), ((0, 0), (0, LANE - HEADS)))
    dab, dal, ddt = _gdn_gates_bwd(sv["proj"], p["gdn_a_log"], p["gdn_dt_bias"], dgc_col, dgc_row, dbeta, lay.ab_base,
                                   f"gdn_gates_bwd{tag}")
    small["gdn_a_log"] = dal[0, :HEADS]
    small["gdn_dt_bias"] = ddt[0, :HEADS]
    dproj = jnp.concatenate([dqkv, dz, dxb, dyb, dgg, dgl, dab], axis=1)
    dwin = _matmul(sv["h"], dproj, "tn", f"dw_in{tag}", [BF], tn=1152)
    big["w_in"] = _stack_from_cols(lay.unpermute(dwin))
    dh = _matmul(dproj, p["w_in"], "nt", f"d_h{tag}", [F32], tk=1152)
    dx, g = _rms_bwd(sv["x"], p["attn_norm"], dh, dx1, f"rms1_bwd{tag}")
    small["attn_norm"] = g.reshape(-1)
    return dx, big, small


_BIG = ("w_in", "w_branch_gdn", "w_branch_lru", "w_out", "w_up", "w_down")
_ROW_SHARDED = ("w_out", "w_down")
_CONV = ("gdn_conv_w", "lru_conv_w")
_SMALL = ("attn_norm", "gdn_a_log", "gdn_dt_bias", "gdn_norm", "lru_conv_b", "lru_w_a", "lru_b_a", "lru_w_x",
          "lru_b_x", "lru_lambda", "mlp_norm")
_ORDER = ("attn_norm", "w_in", "gdn_conv_w", "gdn_a_log", "gdn_dt_bias", "gdn_norm", "lru_conv_w", "lru_conv_b",
          "lru_w_a", "lru_b_a", "lru_w_x", "lru_b_x", "lru_lambda", "w_branch_gdn", "w_branch_lru", "w_out",
          "mlp_norm", "w_up", "w_down", "final_norm")


def kernel(x, attn_norm, w_in, gdn_conv_w, gdn_a_log, gdn_dt_bias, gdn_norm, lru_conv_w, lru_conv_b, lru_w_a, lru_b_a, lru_w_x, lru_b_x, lru_lambda, w_branch_gdn, w_branch_lru, w_out, mlp_norm, w_up, w_down, final_norm, loss_target, m_attn_norm, m_w_in, m_gdn_conv_w, m_gdn_a_log, m_gdn_dt_bias, m_gdn_norm, m_lru_conv_w, m_lru_conv_b, m_lru_w_a, m_lru_b_a, m_lru_w_x, m_lru_b_x, m_lru_lambda, m_w_branch_gdn, m_w_branch_lru, m_w_out, m_mlp_norm, m_w_up, m_w_down, m_final_norm, v_attn_norm, v_w_in, v_gdn_conv_w, v_gdn_a_log, v_gdn_dt_bias, v_gdn_norm, v_lru_conv_w, v_lru_conv_b, v_lru_w_a, v_lru_b_a, v_lru_w_x, v_lru_b_x, v_lru_lambda, v_w_branch_gdn, v_w_branch_lru, v_w_out, v_mlp_norm, v_w_up, v_w_down, v_final_norm):
    w = dict(attn_norm=attn_norm, w_in=w_in, gdn_conv_w=gdn_conv_w, gdn_a_log=gdn_a_log, gdn_dt_bias=gdn_dt_bias, gdn_norm=gdn_norm, lru_conv_w=lru_conv_w, lru_conv_b=lru_conv_b, lru_w_a=lru_w_a, lru_b_a=lru_b_a, lru_w_x=lru_w_x, lru_b_x=lru_b_x, lru_lambda=lru_lambda, w_branch_gdn=w_branch_gdn, w_branch_lru=w_branch_lru, w_out=w_out, mlp_norm=mlp_norm, w_up=w_up, w_down=w_down, final_norm=final_norm)
    m = dict(attn_norm=m_attn_norm, w_in=m_w_in, gdn_conv_w=m_gdn_conv_w, gdn_a_log=m_gdn_a_log, gdn_dt_bias=m_gdn_dt_bias, gdn_norm=m_gdn_norm, lru_conv_w=m_lru_conv_w, lru_conv_b=m_lru_conv_b, lru_w_a=m_lru_w_a, lru_b_a=m_lru_b_a, lru_w_x=m_lru_w_x, lru_b_x=m_lru_b_x, lru_lambda=m_lru_lambda, w_branch_gdn=m_w_branch_gdn, w_branch_lru=m_w_branch_lru, w_out=m_w_out, mlp_norm=m_mlp_norm, w_up=m_w_up, w_down=m_w_down, final_norm=m_final_norm)
    v = dict(attn_norm=v_attn_norm, w_in=v_w_in, gdn_conv_w=v_gdn_conv_w, gdn_a_log=v_gdn_a_log, gdn_dt_bias=v_gdn_dt_bias, gdn_norm=v_gdn_norm, lru_conv_w=v_lru_conv_w, lru_conv_b=v_lru_conv_b, lru_w_a=v_lru_w_a, lru_b_a=v_lru_b_a, lru_w_x=v_lru_w_x, lru_b_x=v_lru_b_x, lru_lambda=v_lru_lambda, w_branch_gdn=v_w_branch_gdn, w_branch_lru=v_w_branch_lru, w_out=v_w_out, mlp_norm=v_mlp_norm, w_up=v_w_up, w_down=v_w_down, final_norm=v_final_norm)
    depth = w_in.shape[0]
    x = x[0]
    target = loss_target[0]
    s, d = x.shape
    lay = _Layout(d)
    assert lay.cols == w_in.shape[2] * N_DEV
    cx, cy, cc = _place()
    me = 4 * cx + 2 * cy + cc

    params = []
    for l in range(depth):
        shards = [w[n][l].astype(BF) for n in _BIG] + [w[n][l] for n in _CONV]
        gathered = _all_gather(shards, f"gather_weights_l{l}")
        p = {n: w[n][l] for n in _SMALL}
        for n, g in zip(_BIG + _CONV, gathered):
            full = g.reshape(-1, g.shape[-1]) if n in _ROW_SHARDED else _cols_from_stack(g)
            p[n] = lay.permute(full) if n == "w_in" else full
        params.append(p)

    saved = []
    xl = x
    for l in range(depth):
        xl, sv = _layer_fwd(xl, params[l], lay, f"_l{l}")
        saved.append(sv)
    dx, loss_row, dfinal = _loss_head(xl, final_norm, target, "loss_head")
    loss = lax.psum(loss_row[0, 0], ("x", "y", "c"))

    bigs, smalls = [None] * depth, [None] * depth
    for l in reversed(range(depth)):
        dx, bigs[l], smalls[l] = _layer_bwd(dx, params[l], saved[l], lay, f"_l{l}")

    core = cc.astype(jnp.int32).reshape(1)
    out = {n: None for n in _BIG}
    for l in reversed(range(depth)):
        stacks = [bigs[l][n] for n in _BIG]
        from_sibling = _sibling_exchange(stacks, f"grads_to_sibling_l{l}")
        chip_sums = [_sibling_add(g, r, core, f"chip_sum_{n}_l{l}") for n, g, r in zip(_BIG, stacks, from_sibling)]
        summed = _chip_exchange(chip_sums, f"grads_to_chips_l{l}")
        for n, q in zip(_BIG, summed):
            out[n] = _adamw_sharded(q, w[n], m[n], v[n], l, out[n], f"adamw_{n}_l{l}")

    small_names = _SMALL + _CONV
    part = _pack([jnp.stack([smalls[l][n] for l in range(depth)]) for n in small_names] + [dfinal.reshape(-1)])
    total = _sum_devices(_all_gather([part], "gather_small_grads")[0], "sum_small_grads")
    like = [jnp.stack([smalls[l][n] for l in range(depth)]) for n in small_names] + [final_norm]
    grads = dict(zip(small_names + ("final_norm",), _unpack(total, like)))
    for n in _CONV:
        blk = w[n].shape[2]
        grads[n] = lax.dynamic_slice_in_dim(grads[n], me * blk, blk, axis=2)
    names = small_names + ("final_norm",)
    gp = _pack([grads[n] for n in names])
    upd = _adamw_packed(_pack([w[n] for n in names]), gp, _pack([m[n] for n in names]), _pack([v[n] for n in names]),
                        "adamw_small")
    like = [w[n] for n in names]
    unp = [_unpack(u, like) for u in upd]
    for i, n in enumerate(names):
        out[n] = (grads[n], unp[0][i], unp[1][i], unp[2][i])

    res = [loss, dx.reshape(1, s, d)]
    for k in range(4):
        res += [out[n][k] for n in _ORDER]
    return tuple(res)
```

```python
import functools

import jax
import jax.numpy as jnp
from jax import lax
from jax.experimental import pallas as pl
from jax.experimental.pallas import tpu as pltpu

F32 = jnp.float32
BF = jnp.bfloat16
MESH = pl.DeviceIdType.MESH
N_DEV = 8
N_CHIP = 4

HEADS = 8
HEAD_DIM = 128
GDN_WIDTH = HEADS * HEAD_DIM
CHUNK = 64
CONV_WIDTH = 4
LRU_WIDTH = 1024
LRU_BLOCKS = 8
LRU_BLOCK_DIM = 128
LRU_C = 8.0
RMS_EPS = 1e-6
L2_EPS = 1e-6
ADAM_LR = 0.001
ADAM_B1 = 0.9
ADAM_B2 = 0.999
ADAM_EPS = 1e-08
ADAM_WD = 0.01
ADAM_STEP = 10

LANE = 128
VMEM_LIMIT = 56 * 1024 * 1024


def _cparams(sem=None):
    return pltpu.CompilerParams(dimension_semantics=sem, vmem_limit_bytes=VMEM_LIMIT)


def _tile(n, pref, unit=LANE):
    if n <= pref:
        return n
    best = None
    for t in range(unit, pref + 1, unit):
        if n % t == 0:
            best = t
    assert best is not None, (n, pref, unit)
    return best


_NN = (((1,), (0,)), ((), ()))
_NT = (((1,), (1,)), ((), ()))
_TN = (((0,), (0,)), ((), ()))


def _dg(a, b, dims):
    return lax.dot_general(a.astype(BF), b.astype(BF), dims, preferred_element_type=F32)


class _MM:
    def __init__(self, nn, nt, tn):
        self.nn, self.nt, self.tn = nn, nt, tn


def _make_mm(nn, nt, tn):
    def p_nn(a, b):
        return _dg(a, b, nn)

    def p_nt(a, b):
        return _dg(a, b, nt)

    def p_tn(a, b):
        return _dg(a, b, tn)

    d_nn, d_nt, d_tn = jax.custom_vjp(p_nn), jax.custom_vjp(p_nt), jax.custom_vjp(p_tn)

    def save(f):
        return lambda a, b: (f(a, b), (a, b))

    d_nn.defvjp(save(d_nn), lambda r, g: (d_nt(g, r[1]), d_tn(r[0], g)))
    d_nt.defvjp(save(d_nt), lambda r, g: (d_nn(g, r[1]), d_tn(g, r[0])))
    d_tn.defvjp(save(d_tn), lambda r, g: (d_nt(r[1], g), d_nn(r[0], g)))
    return _MM(p_nn, p_nt, p_tn), _MM(d_nn, d_nt, d_tn)


_BNN = (((2,), (1,)), ((0,), (0,)))
_BNT = (((2,), (2,)), ((0,), (0,)))
_BTN = (((1,), (1,)), ((0,), (0,)))

_PLAIN, _DIFF = _make_mm(_NN, _NT, _TN)
_PLAIN_B, _DIFF_B = _make_mm(_BNN, _BNT, _BTN)


def _sigmoid(x):
    return 1.0 / (1.0 + jnp.exp(-x))


def _silu(x):
    return x * _sigmoid(x)


def _softplus(x):
    return jnp.maximum(x, 0.0) + jnp.log(1.0 + jnp.exp(-jnp.maximum(x, -x)))


def _gelu(x):
    return 0.5 * x * (1.0 + jnp.tanh(0.7978845608028654 * (x + 0.044715 * (x * x * x))))


def _expm1(x):
    s = x * (1.0 + x * (1.0 / 2.0) * (1.0 + x * (1.0 / 3.0) * (1.0 + x * (1.0 / 4.0) * (
        1.0 + x * (1.0 / 5.0) * (1.0 + x * (1.0 / 6.0) * (1.0 + x * (1.0 / 7.0)))))))
    return jnp.where(jnp.maximum(x, -x) < 0.3, s, jnp.exp(x) - 1.0)


def _rms(x, gain):
    ms = jnp.mean(x * x, axis=-1, keepdims=True)
    return x * lax.rsqrt(ms + RMS_EPS) * gain


def _l2n(x):
    return x * lax.rsqrt(jnp.sum(x * x, axis=-1, keepdims=True) + L2_EPS)


def _shift_down(x, r):
    if r == 0:
        return x
    row = lax.broadcasted_iota(jnp.int32, x.shape, 0)
    return jnp.where(row >= r, pltpu.roll(x, r, 0), 0.0)


def _shift_up(x, r):
    if r == 0:
        return x
    n = x.shape[0]
    row = lax.broadcasted_iota(jnp.int32, x.shape, 0)
    return jnp.where(row < n - r, pltpu.roll(x, n - r, 0), 0.0)


def _conv(x, w_ref):
    y = None
    for j in range(CONV_WIDTH):
        t = _shift_down(x, CONV_WIDTH - 1 - j) * w_ref[j:j + 1, :]
        y = t if y is None else y + t
    return y


def _conv_bwd(x, dy, w_ref):
    dx = None
    dws = []
    for j in range(CONV_WIDTH):
        r = CONV_WIDTH - 1 - j
        t = _shift_up(dy, r) * w_ref[j:j + 1, :]
        dx = t if dx is None else dx + t
        dws.append(jnp.sum(dy * _shift_down(x, r), axis=0, keepdims=True))
    return dx, dws


def _scan_tile(a, b, reverse):
    n = a.shape[0]
    row = lax.broadcasted_iota(jnp.int32, a.shape, 0)
    s = 1
    while s < n:
        if reverse:
            keep = row < n - s
            a_sh = jnp.where(keep, pltpu.roll(a, n - s, 0), 1.0)
            b_sh = jnp.where(keep, pltpu.roll(b, n - s, 0), 0.0)
        else:
            keep = row >= s
            a_sh = jnp.where(keep, pltpu.roll(a, s, 0), 1.0)
            b_sh = jnp.where(keep, pltpu.roll(b, s, 0), 0.0)
        b = a * b_sh + b
        a = a * a_sh
        s *= 2
    return a, b


def _dg3(a, b, dims):
    a_hi = a.astype(BF)
    b_hi = b.astype(BF)
    a_lo = a - a_hi.astype(F32)
    b_lo = b - b_hi.astype(F32)
    return _dg(a_hi, b_hi, dims) + (_dg(a_hi, b_lo, dims) + _dg(a_lo, b_hi, dims))


@jax.custom_vjp
def _tri_inv(a):
    n = a.shape[1]
    p = _dg3(a, a, _BNN)
    r = p
    e = 2
    while 2 * e < n:
        p = _dg3(p, p, _BNN)
        r = r + p + _dg3(r, p, _BNN)
        e *= 2
    row = lax.broadcasted_iota(jnp.int32, (1, n, n), 1)
    col = lax.broadcasted_iota(jnp.int32, (1, n, n), 2)
    eye = jnp.where(row == col, 1.0, 0.0)
    return eye - a + r - _dg3(a, r, _BNN)


def _tri_inv_fwd(a):
    t = _tri_inv(a)
    return t, t


def _tri_inv_bwd(t, g):
    return (-_dg3(_dg3(t, g, _BTN), t, _BNT),)


_tri_inv.defvjp(_tri_inv_fwd, _tri_inv_bwd)


def _gdn_chunk(q, k, v, beta, gcol, grow, state, mm):
    c = q.shape[1]
    row = lax.broadcasted_iota(jnp.int32, (1, c, c), 1)
    col = lax.broadcasted_iota(jnp.int32, (1, c, c), 2)
    causal = row >= col
    strict = row > col
    qn = _l2n(q) * (HEAD_DIM ** -0.5)
    kn = _l2n(k)
    dec = jnp.where(causal, jnp.exp(jnp.where(causal, gcol - grow, 0.0)), 0.0)
    kb = kn * beta
    vb = v * beta
    a = jnp.where(strict, mm.nt(kb, kn) * dec, 0.0)
    t = _tri_inv(a)
    eg = jnp.exp(gcol)
    u = mm.nn(t, vb)
    w = mm.nn(t, kb * eg)
    p = jnp.where(causal, mm.nt(qn, kn) * dec, 0.0)
    vn = u - mm.nn(w, state)
    o = mm.nn(qn * eg, state) + mm.nn(p, vn)
    last = lax.broadcasted_iota(jnp.int32, (1, c, 1), 1) == c - 1
    gl = jnp.sum(jnp.where(last, gcol, 0.0), axis=1, keepdims=True)
    kd = kn * jnp.exp(gl - gcol)
    new_state = state * jnp.exp(gl) + mm.tn(kd, vn)
    return o, new_state


def _lru_gates(xc, wa, ba, wx, bx, lam, mm):
    r = _sigmoid(mm.nn(xc, wa) + ba)
    i = _sigmoid(mm.nn(xc, wx) + bx)
    log_a = -LRU_C * r * _softplus(-lam)
    a = jnp.exp(log_a)
    bterm = jnp.sqrt(-_expm1(2.0 * log_a)) * (i * xc)
    return a, bterm


def _matmul(a, b, mode, name, out_dtypes, epi=None, extras=(), tm=1024, tn=1024, tk=512):
    if mode == "nn":
        (m, k), (k2, n) = a.shape, b.shape
    elif mode == "nt":
        (m, k), (n, k2) = a.shape, b.shape
    else:
        (k, m), (k2, n) = a.shape, b.shape
    assert k == k2, (a.shape, b.shape, mode)
    tm, tn, tk = _tile(m, tm), _tile(n, tn), _tile(k, tk)
    nk = k // tk
    dims = {"nn": _NN, "nt": _NT, "tn": _TN}[mode]
    if mode == "nn":
        a_spec = pl.BlockSpec((tm, tk), lambda i, j, kk: (i, kk))
        b_spec = pl.BlockSpec((tk, tn), lambda i, j, kk: (kk, j))
    elif mode == "nt":
        a_spec = pl.BlockSpec((tm, tk), lambda i, j, kk: (i, kk))
        b_spec = pl.BlockSpec((tn, tk), lambda i, j, kk: (j, kk))
    else:
        a_spec = pl.BlockSpec((tk, tm), lambda i, j, kk: (kk, i))
        b_spec = pl.BlockSpec((tk, tn), lambda i, j, kk: (kk, j))
    o_spec = pl.BlockSpec((tm, tn), lambda i, j, kk: (i, j))
    n_ex, n_out = len(extras), len(out_dtypes)
    if epi is None:
        epi = lambda acc: (acc,)

    def body(*refs):
        a_ref, b_ref = refs[0], refs[1]
        ex_refs = refs[2:2 + n_ex]
        out_refs = refs[2 + n_ex:2 + n_ex + n_out]
        acc_ref = refs[-1]
        kk = pl.program_id(2)

        @pl.when(kk == 0)
        def _():
            acc_ref[...] = jnp.zeros_like(acc_ref)

        acc_ref[...] += _dg(a_ref[...], b_ref[...], dims)

        @pl.when(kk == nk - 1)
        def _():
            res = epi(acc_ref[...], *[r[...] for r in ex_refs])
            for r, v in zip(out_refs, res):
                r[...] = v.astype(r.dtype)

    outs = pl.pallas_call(
        body, name=name, grid=(m // tm, n // tn, nk),
        in_specs=[a_spec, b_spec] + [o_spec] * n_ex,
        out_specs=[o_spec] * n_out,
        out_shape=[jax.ShapeDtypeStruct((m, n), d) for d in out_dtypes],
        scratch_shapes=[pltpu.VMEM((tm, tn), F32)],
        compiler_params=_cparams(("parallel", "parallel", "arbitrary")),
    )(a, b, *extras)
    return outs[0] if n_out == 1 else outs


def _rowwise(fn, name, rows, tr, row_ins, params, row_outs, acc_outs=(), ncol=1):
    tr = _tile(rows, tr, 8)
    n_in, n_par, n_ro, n_acc = len(row_ins), len(params), len(row_outs), len(acc_outs)
    in_specs = []
    for _, width, base, per_col in row_ins:
        if per_col:
            in_specs.append(pl.BlockSpec((tr, width), lambda i, j, base=base: (i, base + j)))
        else:
            in_specs.append(pl.BlockSpec((tr, width), lambda i, j, base=base: (i, base)))
    for p in params:
        in_specs.append(pl.BlockSpec(p.shape, lambda i, j: (0, 0)))
    out_specs, out_shape = [], []
    for total, dtype, width, per_col in row_outs:
        if per_col:
            out_specs.append(pl.BlockSpec((tr, width), lambda i, j: (i, j)))
        else:
            out_specs.append(pl.BlockSpec((tr, width), lambda i, j: (i, 0)))
        out_shape.append(jax.ShapeDtypeStruct((rows, total), dtype))
    for shape, dtype in acc_outs:
        out_specs.append(pl.BlockSpec(shape, lambda i, j: (0, 0)))
        out_shape.append(jax.ShapeDtypeStruct(shape, dtype))

    def body(*refs):
        ins = [r[...] for r in refs[:n_in + n_par]]
        ro = refs[n_in + n_par:n_in + n_par + n_ro]
        ao = refs[n_in + n_par + n_ro:]
        res = fn(*ins)
        for r, v in zip(ro, res[:n_ro]):
            r[...] = v.astype(r.dtype)
        first = jnp.logical_and(pl.program_id(0) == 0, pl.program_id(1) == 0)

        @pl.when(first)
        def _():
            for r in ao:
                r[...] = jnp.zeros_like(r)

        for r, v in zip(ao, res[n_ro:]):
            r[...] += v.astype(r.dtype)

    sem = ("arbitrary", "arbitrary") if n_acc else ("parallel", "parallel")
    outs = pl.pallas_call(
        body, name=name, grid=(rows // tr, ncol), in_specs=in_specs, out_specs=out_specs,
        out_shape=out_shape, compiler_params=_cparams(sem),
    )(*[r[0] for r in row_ins], *params)
    return outs


def _row(x):
    return x.reshape(1, -1)


def _rms_fwd(x, gain, name):
    s, d = x.shape
    return _rowwise(lambda xt, g: (_rms(xt, g),), name, s, 512, [(x, d, 0, False)], [_row(gain)],
                    [(d, BF, d, False)])[0]


def _rms_bwd(x, gain, dh, dres, name):
    s, d = x.shape

    def fn(xt, dht, drt, g):
        _, vjp = jax.vjp(_rms, xt, g)
        dx, dg = vjp(dht)
        return drt + dx, dg

    return _rowwise(fn, name, s, 256, [(x, d, 0, False), (dh, d, 0, False), (dres, d, 0, False)], [_row(gain)],
                    [(d, F32, d, False)], [((1, d), F32)])


def _loss_head(x, gain, target, name):
    s, d = x.shape

    def fn(xt, tt, g):
        y, vjp = jax.vjp(_rms, xt, g)
        err = y - tt
        dx, dg = vjp(err * (1.0 / d))
        part = 0.5 * jnp.sum(jnp.sum(err * err, axis=1, keepdims=True), axis=0, keepdims=True) * (1.0 / d)
        return dx, jnp.broadcast_to(part, (1, LANE)), dg

    return _rowwise(fn, name, s, 256, [(x, d, 0, False), (target, d, 0, False)], [_row(gain)],
                    [(d, F32, d, False)], [((1, LANE), F32), ((1, d), F32)])


def _gdn_post_fn(o, z, g):
    return _rms(o, g) * _silu(z)


def _gdn_post_fwd(o, proj, gain, z_base, name):
    s = o.shape[0]
    return _rowwise(lambda ot, zt, g: (_gdn_post_fn(ot, zt, g),), name, s, 1024,
                    [(o, HEAD_DIM, 0, True), (proj, HEAD_DIM, z_base, True)], [_row(gain)],
                    [(GDN_WIDTH, BF, HEAD_DIM, True)], ncol=HEADS)[0]


def _gdn_post_bwd(o, proj, gain, dog, z_base, name):
    s = o.shape[0]

    def fn(ot, zt, dt, g):
        _, vjp = jax.vjp(_gdn_post_fn, ot, zt, g)
        return vjp(dt)

    return _rowwise(fn, name, s, 1024,
                    [(o, HEAD_DIM, 0, True), (proj, HEAD_DIM, z_base, True), (dog, HEAD_DIM, 0, True)],
                    [_row(gain)], [(GDN_WIDTH, F32, HEAD_DIM, True), (GDN_WIDTH, BF, HEAD_DIM, True)],
                    [((1, HEAD_DIM), F32)], ncol=HEADS)


def _merge_fn(gg, gl, yg, yl):
    return _sigmoid(gg) * yg + _sigmoid(gl) * yl


def _merge_fwd(proj, yg, yl, gg_base, name):
    s, d = yg.shape
    return _rowwise(lambda a, b, c, e: (_merge_fn(a, b, c, e),), name, s, 256,
                    [(proj, d, gg_base, False), (proj, d, gg_base + 1, False), (yg, d, 0, False), (yl, d, 0, False)],
                    [], [(d, BF, d, False)])[0]


def _merge_bwd(proj, yg, yl, dmerged, gg_base, name):
    s, d = yg.shape

    def fn(a, b, c, e, dm):
        _, vjp = jax.vjp(_merge_fn, a, b, c, e)
        return vjp(dm)

    return _rowwise(fn, name, s, 128,
                    [(proj, d, gg_base, False), (proj, d, gg_base + 1, False), (yg, d, 0, False), (yl, d, 0, False),
                     (dmerged, d, 0, False)], [], [(d, BF, d, False)] * 4)


def _conv_silu_fwd(proj, w, name):
    s = proj.shape[0]
    cols = w.shape[1]
    cw = _tile(cols, LANE)

    def body(x_ref, w_ref, o_ref):
        o_ref[...] = _silu(_conv(x_ref[...], w_ref))

    return pl.pallas_call(
        body, name=name, grid=(cols // cw,),
        in_specs=[pl.BlockSpec((s, cw), lambda j: (0, j)), pl.BlockSpec((CONV_WIDTH, cw), lambda j: (0, j))],
        out_specs=pl.BlockSpec((s, cw), lambda j: (0, j)),
        out_shape=jax.ShapeDtypeStruct((s, cols), F32), compiler_params=_cparams(("parallel",)),
    )(proj, w)


def _conv_silu_bwd(proj, w, dout, name):
    s = proj.shape[0]
    cols = w.shape[1]
    cw = _tile(cols, LANE)

    def body(x_ref, w_ref, d_ref, dx_ref, dw_ref):
        x = x_ref[...]
        y = _conv(x, w_ref)
        sg = _sigmoid(y)
        dy = d_ref[...] * (sg * (1.0 + y * (1.0 - sg)))
        dx, dws = _conv_bwd(x, dy, w_ref)
        dx_ref[...] = dx.astype(dx_ref.dtype)
        for j in range(CONV_WIDTH):
            dw_ref[j:j + 1, :] = dws[j]

    return pl.pallas_call(
        body, name=name, grid=(cols // cw,),
        in_specs=[pl.BlockSpec((s, cw), lambda j: (0, j)), pl.BlockSpec((CONV_WIDTH, cw), lambda j: (0, j)),
                  pl.BlockSpec((s, cw), lambda j: (0, j))],
        out_specs=[pl.BlockSpec((s, cw), lambda j: (0, j)), pl.BlockSpec((CONV_WIDTH, cw), lambda j: (0, j))],
        out_shape=[jax.ShapeDtypeStruct((s, cols), BF), jax.ShapeDtypeStruct((CONV_WIDTH, cols), F32)],
        compiler_params=_cparams(("parallel",)),
    )(proj, w, dout)


def _cumsum_mask(tr, transpose):
    row = lax.broadcasted_iota(jnp.int32, (tr, tr), 0)
    col = lax.broadcasted_iota(jnp.int32, (tr, tr), 1)
    same = (row // CHUNK) == (col // CHUNK)
    tri = (row <= col) if transpose else (row >= col)
    return jnp.where(jnp.logical_and(same, tri), 1.0, 0.0)


def _lane_pad(v):
    return jnp.pad(v.reshape(1, -1), ((0, 0), (0, LANE - v.shape[0])))


def _gdn_gates_fwd(proj, a_log, dt_bias, ab_base, name):
    s = proj.shape[0]

    def fn(ab, al, dt):
        g = -jnp.exp(al) * _softplus(ab + dt)
        beta = _sigmoid(pltpu.roll(ab, LANE - HEADS, 1))
        gc = jnp.dot(_cumsum_mask(ab.shape[0], False), g, precision=lax.Precision.HIGHEST,
                     preferred_element_type=F32)
        return gc, beta

    return _rowwise(fn, name, s, 512, [(proj, LANE, ab_base, False)], [_lane_pad(a_log), _lane_pad(dt_bias)],
                    [(LANE, F32, LANE, False)] * 2)


def _gdn_gates_bwd(proj, a_log, dt_bias, dgc_col, dgc_row, dbeta, ab_base, name):
    s = proj.shape[0]

    def fn(ab, dc, dr, db, al, dt):
        lane = lax.broadcasted_iota(jnp.int32, ab.shape, 1)
        dg = jnp.dot(_cumsum_mask(ab.shape[0], True), dc + dr, precision=lax.Precision.HIGHEST,
                     preferred_element_type=F32)
        ea = jnp.exp(al)
        pre = ab + dt
        g = -ea * _softplus(pre)
        da = jnp.where(lane < HEADS, dg * (-ea) * _sigmoid(pre), 0.0)
        dal = jnp.sum(jnp.where(lane < HEADS, dg * g, 0.0), axis=0, keepdims=True)
        ddt = jnp.sum(da, axis=0, keepdims=True)
        beta = _sigmoid(pltpu.roll(ab, LANE - HEADS, 1))
        dbl = jnp.where(lane < HEADS, db * beta * (1.0 - beta), 0.0)
        dab = da + pltpu.roll(dbl, HEADS, 1)
        return dab, dal, ddt

    return _rowwise(fn, name, s, 512,
                    [(proj, LANE, ab_base, False), (dgc_col, LANE, 0, False), (dgc_row, LANE, 0, False),
                     (dbeta, LANE, 0, False)], [_lane_pad(a_log), _lane_pad(dt_bias)],
                    [(LANE, BF, LANE, False)], [((1, LANE), F32), ((1, LANE), F32)])


def _head_col(blk, h):
    lane = lax.broadcasted_iota(jnp.int32, blk.shape, 1)
    return jnp.sum(jnp.where(lane == h, blk, 0.0), axis=1, keepdims=True)


def _chunk_operands(qkv_ref, gc_ref, b_ref, gr_ref):
    heads = range(HEADS)
    q, k, v = (jnp.stack([qkv_ref[:, (j * HEADS + h) * HEAD_DIM:(j * HEADS + h + 1) * HEAD_DIM] for h in heads])
               for j in range(3))
    gcs, betas = gc_ref[...], b_ref[...]
    beta = jnp.stack([_head_col(betas, h) for h in heads])
    gcol = jnp.stack([_head_col(gcs, h) for h in heads])
    grow = jnp.stack([gr_ref[0, h:h + 1, :] for h in heads])
    return q, k, v, beta, gcol, grow


def _gdn_chunk_fwd(qkv, gc, beta, grow, name):
    s = qkv.shape[0]
    nc = s // CHUNK

    def body(qkv_ref, gc_ref, b_ref, gr_ref, o_ref, st_ref, state):
        @pl.when(pl.program_id(0) == 0)
        def _():
            state[...] = jnp.zeros_like(state)

        st = state[...]
        st_ref[0] = st
        o, new = _gdn_chunk(*_chunk_operands(qkv_ref, gc_ref, b_ref, gr_ref), st, _PLAIN_B)
        for h in range(HEADS):
            o_ref[:, h * HEAD_DIM:(h + 1) * HEAD_DIM] = o[h]
        state[...] = new

    return pl.pallas_call(
        body, name=name, grid=(nc,),
        in_specs=[pl.BlockSpec((CHUNK, 3 * GDN_WIDTH), lambda c: (c, 0)),
                  pl.BlockSpec((CHUNK, LANE), lambda c: (c, 0)), pl.BlockSpec((CHUNK, LANE), lambda c: (c, 0)),
                  pl.BlockSpec((1, HEADS, CHUNK), lambda c: (c, 0, 0))],
        out_specs=[pl.BlockSpec((CHUNK, GDN_WIDTH), lambda c: (c, 0)),
                   pl.BlockSpec((1, HEADS, HEAD_DIM, HEAD_DIM), lambda c: (c, 0, 0, 0))],
        out_shape=[jax.ShapeDtypeStruct((s, GDN_WIDTH), F32),
                   jax.ShapeDtypeStruct((nc, HEADS, HEAD_DIM, HEAD_DIM), F32)],
        scratch_shapes=[pltpu.VMEM((HEADS, HEAD_DIM, HEAD_DIM), F32)],
        compiler_params=_cparams(("arbitrary",)),
    )(qkv, gc, beta, grow)


def _gdn_chunk_bwd(qkv, gc, beta, grow, states, do, name):
    s = qkv.shape[0]
    nc = s // CHUNK

    def body(qkv_ref, gc_ref, b_ref, gr_ref, st_ref, do_ref, dqkv_ref, dgc_ref, db_ref, dgr_ref, dstate):
        @pl.when(pl.program_id(0) == 0)
        def _():
            dstate[...] = jnp.zeros_like(dstate)

        lane = lax.broadcasted_iota(jnp.int32, (CHUNK, LANE), 1)
        _, vjp = jax.vjp(functools.partial(_gdn_chunk, mm=_DIFF_B), *_chunk_operands(qkv_ref, gc_ref, b_ref, gr_ref),
                         st_ref[0])
        do = jnp.stack([do_ref[:, h * HEAD_DIM:(h + 1) * HEAD_DIM] for h in range(HEADS)])
        dq, dk, dv, dbeta, dgcol, dgrow, dst = vjp((do, dstate[...]))
        dstate[...] = dst
        dgc = jnp.zeros((CHUNK, LANE), F32)
        db = jnp.zeros((CHUNK, LANE), F32)
        for h in range(HEADS):
            for j, dx in enumerate((dq, dk, dv)):
                dqkv_ref[:, (j * HEADS + h) * HEAD_DIM:(j * HEADS + h + 1) * HEAD_DIM] = dx[h]
            dgc = dgc + jnp.where(lane == h, dgcol[h], 0.0)
            db = db + jnp.where(lane == h, dbeta[h], 0.0)
            dgr_ref[0, h:h + 1, :] = dgrow[h]
        dgc_ref[...] = dgc
        db_ref[...] = db

    rev = lambda c: nc - 1 - c
    wide = pl.BlockSpec((CHUNK, 3 * GDN_WIDTH), lambda c: (rev(c), 0))
    lanes = pl.BlockSpec((CHUNK, LANE), lambda c: (rev(c), 0))
    rows = pl.BlockSpec((1, HEADS, CHUNK), lambda c: (rev(c), 0, 0))
    return pl.pallas_call(
        body, name=name, grid=(nc,),
        in_specs=[wide, lanes, lanes, rows,
                  pl.BlockSpec((1, HEADS, HEAD_DIM, HEAD_DIM), lambda c: (rev(c), 0, 0, 0)),
                  pl.BlockSpec((CHUNK, GDN_WIDTH), lambda c: (rev(c), 0))],
        out_specs=[wide, lanes, lanes, rows],
        out_shape=[jax.ShapeDtypeStruct((s, 3 * GDN_WIDTH), F32)]
        + [jax.ShapeDtypeStruct((s, LANE), F32)] * 2 + [jax.ShapeDtypeStruct((nc, HEADS, CHUNK), F32)],
        scratch_shapes=[pltpu.VMEM((HEADS, HEAD_DIM, HEAD_DIM), F32)],
        compiler_params=_cparams(("arbitrary",)),
    )(qkv, gc, beta, grow, states, do)


def _lru_specs(s, xb_base):
    col = lambda base: pl.BlockSpec((s, LRU_BLOCK_DIM), lambda n, base=base: (0, base + n))
    vec = pl.BlockSpec((1, LRU_BLOCK_DIM), lambda n: (0, n))
    mat = pl.BlockSpec((None, LRU_BLOCK_DIM, LRU_BLOCK_DIM), lambda n: (n, 0, 0))
    cw = pl.BlockSpec((CONV_WIDTH, LRU_BLOCK_DIM), lambda n: (0, n))
    return col, vec, mat, cw


def _lru_fwd(proj, cw, cb, wa, ba, wx, bx, lam, xb_base, name):
    s = proj.shape[0]
    tt = _tile(s, 256, 8)
    col, vec, mat, cws = _lru_specs(s, xb_base)

    def body(xb_ref, yb_ref, cw_ref, cb_ref, wa_ref, ba_ref, wx_ref, bx_ref, lam_ref, o_ref, h_ref, xc_s):
        xc_s[...] = _conv(xb_ref[...], cw_ref) + cb_ref[...]
        par = (wa_ref[...], ba_ref[...], wx_ref[...], bx_ref[...], lam_ref[...])

        def step(t, carry):
            sl = pl.ds(pl.multiple_of(t * tt, tt), tt)
            a, bt = _lru_gates(xc_s[sl, :], *par, _PLAIN)
            aa, bb = _scan_tile(a, bt, False)
            h = aa * carry + bb
            h_ref[sl, :] = h
            o_ref[sl, :] = (h * _gelu(yb_ref[sl, :])).astype(o_ref.dtype)
            return h_ref[pl.ds(t * tt + tt - 1, 1), :]

        lax.fori_loop(0, s // tt, step, jnp.zeros((1, LRU_BLOCK_DIM), F32))

    nb = LRU_BLOCKS
    return pl.pallas_call(
        body, name=name, grid=(LRU_BLOCKS,),
        in_specs=[col(xb_base), col(xb_base + nb), cws, vec, mat, vec, mat, vec, vec],
        out_specs=[col(0), col(0)],
        out_shape=[jax.ShapeDtypeStruct((s, LRU_WIDTH), BF), jax.ShapeDtypeStruct((s, LRU_WIDTH), F32)],
        scratch_shapes=[pltpu.VMEM((s, LRU_BLOCK_DIM), F32)],
        compiler_params=_cparams(("parallel",)),
    )(proj, proj, cw, _row(cb), wa, _row(ba), wx, _row(bx), _row(lam))


def _lru_bwd(proj, hs, dol, cw, cb, wa, ba, wx, bx, lam, xb_base, name):
    s = proj.shape[0]
    tt = _tile(s, 256, 8)
    nt = s // tt
    col, vec, mat, cws = _lru_specs(s, xb_base)

    def body(xb_ref, yb_ref, h_ref, d_ref, cw_ref, cb_ref, wa_ref, ba_ref, wx_ref, bx_ref, lam_ref,
             dxb_ref, dyb_ref, dcw_ref, dcb_ref, dwa_ref, dba_ref, dwx_ref, dbx_ref, dlam_ref,
             xc_s, a_s, dh_s, dxc_s):
        xc_s[...] = _conv(xb_ref[...], cw_ref) + cb_ref[...]
        par = (wa_ref[...], ba_ref[...], wx_ref[...], bx_ref[...], lam_ref[...])
        row = lax.broadcasted_iota(jnp.int32, (tt, LRU_BLOCK_DIM), 0)
        tile = lambda t: pl.ds(pl.multiple_of(t * tt, tt), tt)

        def prep(t, carry):
            sl = tile(t)
            a_s[sl, :] = _lru_gates(xc_s[sl, :], *par, _PLAIN)[0]
            d = d_ref[sl, :]
            gy, vjp_y = jax.vjp(_gelu, yb_ref[sl, :])
            dyb_ref[sl, :] = vjp_y(d * h_ref[sl, :])[0].astype(dyb_ref.dtype)
            dh_s[sl, :] = d * gy
            return carry

        lax.fori_loop(0, nt, prep, 0)

        def rscan(i, carry):
            dh_next, a_next = carry
            t = nt - 1 - i
            sl = tile(t)
            a_sh = jnp.where(row == tt - 1, a_next, pltpu.roll(a_s[sl, :], tt - 1, 0))
            aa, bb = _scan_tile(a_sh, dh_s[sl, :], True)
            dh_s[sl, :] = aa * dh_next + bb
            first = pl.ds(t * tt, 1)
            return dh_s[first, :], a_s[first, :]

        zero = jnp.zeros((1, LRU_BLOCK_DIM), F32)
        lax.fori_loop(0, nt, rscan, (zero, zero))

        def gates_vjp(t, acc):
            sl = tile(t)
            _, vjp_g = jax.vjp(functools.partial(_lru_gates, mm=_DIFF), xc_s[sl, :], *par)
            dh = dh_s[sl, :]
            before = jnp.where(t > 0, h_ref[pl.ds(jnp.maximum(t * tt - 1, 0), 1), :], 0.0)
            h_prev = jnp.where(row == 0, before, pltpu.roll(h_ref[sl, :], 1, 0))
            dxc, *dpar = vjp_g((dh * h_prev, dh))
            dxc_s[sl, :] = dxc
            return tuple(x + y for x, y in zip(acc, dpar))

        dwa, dba, dwx, dbx, dlam = lax.fori_loop(0, nt, gates_vjp, tuple(jnp.zeros_like(p) for p in par))
        dwa_ref[...] = dwa
        dba_ref[...] = dba
        dwx_ref[...] = dwx
        dbx_ref[...] = dbx
        dlam_ref[...] = dlam
        dxc = dxc_s[...]
        dcb_ref[...] = jnp.sum(dxc, axis=0, keepdims=True)
        dxb, dws = _conv_bwd(xb_ref[...], dxc, cw_ref)
        dxb_ref[...] = dxb.astype(dxb_ref.dtype)
        for j in range(CONV_WIDTH):
            dcw_ref[j:j + 1, :] = dws[j]

    nb = LRU_BLOCKS
    w = LRU_WIDTH
    return pl.pallas_call(
        body, name=name, grid=(LRU_BLOCKS,),
        in_specs=[col(xb_base), col(xb_base + nb), col(0), col(0), cws, vec, mat, vec, mat, vec, vec],
        out_specs=[col(0), col(0), cws, vec, mat, vec, mat, vec, vec],
        out_shape=[jax.ShapeDtypeStruct((s, w), BF), jax.ShapeDtypeStruct((s, w), BF),
                   jax.ShapeDtypeStruct((CONV_WIDTH, w), F32), jax.ShapeDtypeStruct((1, w), F32),
                   jax.ShapeDtypeStruct(wa.shape, F32), jax.ShapeDtypeStruct((1, w), F32),
                   jax.ShapeDtypeStruct(wx.shape, F32), jax.ShapeDtypeStruct((1, w), F32),
                   jax.ShapeDtypeStruct((1, w), F32)],
        scratch_shapes=[pltpu.VMEM((s, LRU_BLOCK_DIM), F32)] * 4,
        compiler_params=_cparams(("parallel",)),
    )(proj, proj, hs, dol, cw, _row(cb), wa, _row(ba), wx, _row(bx), _row(lam))


def _place():
    return lax.axis_index("x"), lax.axis_index("y"), lax.axis_index("c")


def _all_gather(blocks, name):
    n = len(blocks)
    per = N_DEV - 1

    def body(*refs):
        ins, outs = refs[:n], refs[n:2 * n]
        send_sems, recv_sems, local_sems = refs[2 * n:]
        x, y, c = _place()
        me, sibling = (x, y, c), (x, y, 1 - c)
        chips = [(1 - x, y), (x, 1 - y), (1 - x, 1 - y)]

        def copy(a, k, block, to, src=None):
            dst = outs[a].at[4 * block[0] + 2 * block[1] + block[2]]
            return pltpu.make_async_remote_copy(
                src_ref=dst if src is None else src, dst_ref=dst, send_sem=send_sems.at[a * per + k],
                recv_sem=recv_sems.at[a * per + k], device_id=to, device_id_type=MESH)

        started = []
        for a in range(n):
            mine = pltpu.make_async_copy(ins[a], outs[a].at[4 * x + 2 * y + c], local_sems.at[a])
            mine.start()
            started.append(mine)
        sends = []
        for a in range(n):
            first = [copy(a, 0, me, sibling, src=ins[a])]
            first += [copy(a, 1 + j, me, (*chip, c), src=ins[a]) for j, chip in enumerate(chips)]
            for cp in first:
                cp.start()
            sends += first
        for a in range(n):
            for j, chip in enumerate(chips):
                copy(a, 1 + j, (*chip, c), me).wait_recv()
                passed = copy(a, 4 + j, (*chip, c), sibling)
                passed.start()
                sends.append(passed)
        for a in range(n):
            copy(a, 0, sibling, me).wait_recv()
            for j, chip in enumerate(chips):
                copy(a, 4 + j, (*chip, 1 - c), me).wait_recv()
        for cp in sends:
            cp.wait_send()
        for cp in started:
            cp.wait()

    any_spec = pl.BlockSpec(memory_space=pl.ANY)
    return pl.pallas_call(
        body, name=name, in_specs=[any_spec] * n, out_specs=[any_spec] * n,
        out_shape=[jax.ShapeDtypeStruct((N_DEV,) + b.shape, b.dtype) for b in blocks],
        scratch_shapes=[pltpu.SemaphoreType.DMA((n * per,)), pltpu.SemaphoreType.DMA((n * per,)),
                        pltpu.SemaphoreType.DMA((n,))],
    )(*blocks)


def _sibling_exchange(stacks, name):
    n = len(stacks)

    def body(*refs):
        ins, outs = refs[:n], refs[n:2 * n]
        send_sems, recv_sems = refs[2 * n:]
        x, y, c = _place()
        copies = []
        for a in range(n):
            for j in range(N_CHIP):
                cp = pltpu.make_async_remote_copy(
                    src_ref=ins[a].at[2 * j + (1 - c)], dst_ref=outs[a].at[j], send_sem=send_sems.at[a * N_CHIP + j],
                    recv_sem=recv_sems.at[a * N_CHIP + j], device_id=(x, y, 1 - c), device_id_type=MESH)
                cp.start()
                copies.append(cp)
        for cp in copies:
            cp.wait()

    any_spec = pl.BlockSpec(memory_space=pl.ANY)
    return pl.pallas_call(
        body, name=name, in_specs=[any_spec] * n, out_specs=[any_spec] * n,
        out_shape=[jax.ShapeDtypeStruct((N_CHIP,) + g.shape[1:], g.dtype) for g in stacks],
        scratch_shapes=[pltpu.SemaphoreType.DMA((n * N_CHIP,)), pltpu.SemaphoreType.DMA((n * N_CHIP,))],
    )(*stacks)


def _chip_exchange(parts, name):
    n = len(parts)

    def body(*refs):
        ins, outs = refs[:n], refs[n:2 * n]
        send_sems, recv_sems, local_sems = refs[2 * n:]
        x, y, c = _place()
        my = 2 * x + y
        for a in range(n):
            pltpu.make_async_copy(ins[a].at[my], outs[a].at[my], local_sems.at[a]).start()
        for a in range(n):
            for j in range(N_CHIP):
                @pl.when(j != my)
                def _(a=a, j=j):
                    pltpu.make_async_remote_copy(
                        src_ref=ins[a].at[j], dst_ref=outs[a].at[my], send_sem=send_sems.at[a * N_CHIP + j],
                        recv_sem=recv_sems.at[a * N_CHIP + my], device_id=(j // 2, j % 2, c),
                        device_id_type=MESH).start()
        for a in range(n):
            for j in range(N_CHIP):
                @pl.when(j != my)
                def _(a=a, j=j):
                    pltpu.make_async_remote_copy(
                        src_ref=ins[a].at[j], dst_ref=outs[a].at[j], send_sem=send_sems.at[a * N_CHIP + j],
                        recv_sem=recv_sems.at[a * N_CHIP + j], device_id=(j // 2, j % 2, c),
                        device_id_type=MESH).wait()
        for a in range(n):
            pltpu.make_async_copy(ins[a].at[my], outs[a].at[my], local_sems.at[a]).wait()

    any_spec = pl.BlockSpec(memory_space=pl.ANY)
    return pl.pallas_call(
        body, name=name, in_specs=[any_spec] * n, out_specs=[any_spec] * n,
        out_shape=[jax.ShapeDtypeStruct(p.shape, p.dtype) for p in parts],
        scratch_shapes=[pltpu.SemaphoreType.DMA((n * N_CHIP,)), pltpu.SemaphoreType.DMA((n * N_CHIP,)),
                        pltpu.SemaphoreType.DMA((n,))],
    )(*parts)


def _sibling_add(stack, recv, core, name):
    _, r, c = stack.shape
    tr = _tile(r, 256, 8)

    def body(core_ref, g_ref, p_ref, o_ref):
        o_ref[...] = (g_ref[...].astype(F32) + p_ref[...].astype(F32)).astype(o_ref.dtype)

    return pl.pallas_call(
        body, name=name,
        grid_spec=pltpu.PrefetchScalarGridSpec(
            num_scalar_prefetch=1, grid=(N_CHIP, r // tr),
            in_specs=[pl.BlockSpec((1, tr, c), lambda j, i, core_ref: (2 * j + core_ref[0], i, 0)),
                      pl.BlockSpec((1, tr, c), lambda j, i, core_ref: (j, i, 0))],
            out_specs=pl.BlockSpec((1, tr, c), lambda j, i, core_ref: (j, i, 0))),
        out_shape=jax.ShapeDtypeStruct((N_CHIP, r, c), stack.dtype),
        compiler_params=_cparams(("parallel", "parallel")),
    )(core, stack, recv)


def _adamw_math(w, g, m, v):
    m = ADAM_B1 * m + (1.0 - ADAM_B1) * g
    v = ADAM_B2 * v + (1.0 - ADAM_B2) * (g * g)
    m_hat = m / (1.0 - ADAM_B1 ** ADAM_STEP)
    v_hat = v / (1.0 - ADAM_B2 ** ADAM_STEP)
    delta = -ADAM_LR * (m_hat / (jnp.sqrt(v_hat) + ADAM_EPS) + ADAM_WD * w)
    return delta, m, v


def _adamw_sharded(parts, w, m, v, layer, prev, name):
    depth, r, c = w.shape
    tr = _tile(r, 128, 8)
    n_prev = 0 if prev is None else 4

    def body(*refs):
        q_ref, w_ref, m_ref, v_ref = refs[:4]
        g_ref, d_ref, nm_ref, nv_ref = refs[4 + n_prev:]
        g = q_ref[0].astype(F32)
        for j in range(1, N_CHIP):
            g = g + q_ref[j].astype(F32)
        delta, nm, nv = _adamw_math(w_ref[...], g, m_ref[...], v_ref[...])
        g_ref[...] = g
        d_ref[...] = delta
        nm_ref[...] = nm
        nv_ref[...] = nv

    lay = pl.BlockSpec((None, tr, c), lambda i: (layer, i, 0))
    any_spec = pl.BlockSpec(memory_space=pl.ANY)
    return pl.pallas_call(
        body, name=name, grid=(r // tr,),
        in_specs=[pl.BlockSpec((N_CHIP, tr, c), lambda i: (0, i, 0)), lay, lay, lay] + [any_spec] * n_prev,
        out_specs=[lay] * 4, out_shape=[jax.ShapeDtypeStruct(w.shape, F32)] * 4,
        input_output_aliases={4 + i: i for i in range(n_prev)},
        compiler_params=_cparams(("parallel",)),
    )(parts, w, m, v, *(prev or ()))


def _sum_devices(stack, name):
    _, r, c = stack.shape
    tr = _tile(r, 512, 8)

    def body(s_ref, o_ref):
        g = s_ref[0]
        for j in range(1, N_DEV):
            g = g + s_ref[j]
        o_ref[...] = g

    return pl.pallas_call(
        body, name=name, grid=(r // tr,), in_specs=[pl.BlockSpec((N_DEV, tr, c), lambda i: (0, i, 0))],
        out_specs=pl.BlockSpec((tr, c), lambda i: (i, 0)), out_shape=jax.ShapeDtypeStruct((r, c), F32),
        compiler_params=_cparams(("parallel",)),
    )(stack)


def _adamw_packed(w, g, m, v, name):
    r, c = w.shape
    tr = _tile(r, 512, 8)

    def body(w_ref, g_ref, m_ref, v_ref, d_ref, nm_ref, nv_ref):
        delta, nm, nv = _adamw_math(w_ref[...], g_ref[...], m_ref[...], v_ref[...])
        d_ref[...] = delta
        nm_ref[...] = nm
        nv_ref[...] = nv

    spec = pl.BlockSpec((tr, c), lambda i: (i, 0))
    return pl.pallas_call(
        body, name=name, grid=(r // tr,), in_specs=[spec] * 4, out_specs=[spec] * 3,
        out_shape=[jax.ShapeDtypeStruct((r, c), F32)] * 3, compiler_params=_cparams(("parallel",)),
    )(w, g, m, v)


def _pack(arrays):
    unit = 8 * LANE
    flat = []
    for a in arrays:
        f = a.reshape(-1)
        flat.append(jnp.pad(f, (0, (-f.shape[0]) % unit)))
    return jnp.concatenate(flat).reshape(-1, LANE)


def _unpack(packed, like):
    unit = 8 * LANE
    flat = packed.reshape(-1)
    out, off = [], 0
    for a in like:
        n = 1
        for d in a.shape:
            n *= d
        out.append(flat[off:off + n].reshape(a.shape))
        off += n + (-n) % unit
    return out


class _Layout:
    def __init__(self, d):
        self.d = d
        self.n_front = 4 * GDN_WIDTH
        self.n_ab = 2 * HEADS
        self.n_back = 2 * LRU_WIDTH + 2 * d
        self.cols = self.n_front + self.n_ab + self.n_back
        self.pcols = self.n_front + self.n_back + LANE
        self.z_base = 3 * GDN_WIDTH // HEAD_DIM
        self.xb_base = self.n_front // LRU_BLOCK_DIM
        self.gg_off = self.n_front + 2 * LRU_WIDTH
        assert self.gg_off % d == 0
        self.gg_base = self.gg_off // d
        self.ab_base = (self.n_front + self.n_back) // LANE

    def permute(self, w):
        f, ab = self.n_front, self.n_ab
        pad = jnp.zeros((w.shape[0], LANE - ab), w.dtype)
        return jnp.concatenate([w[:, :f], w[:, f + ab:], w[:, f:f + ab], pad], axis=1)

    def unpermute(self, w):
        f, ab, b = self.n_front, self.n_ab, self.n_back
        return jnp.concatenate([w[:, :f], w[:, f + b:f + b + ab], w[:, f:f + b]], axis=1)


def _cols_from_stack(stack):
    nd, k, n = stack.shape
    return stack.transpose(1, 0, 2).reshape(k, nd * n)


def _stack_from_cols(w):
    k, n = w.shape
    return w.reshape(k, N_DEV, n // N_DEV).transpose(1, 0, 2)


def _layer_fwd(x, p, lay, tag):
    s, d = x.shape
    nc = s // CHUNK
    h = _rms_fwd(x, p["attn_norm"], f"rms1_fwd{tag}")
    proj = _matmul(h, p["w_in"], "nn", f"in_proj{tag}", [F32], tn=1152)
    qkv = _conv_silu_fwd(proj, p["gdn_conv_w"], f"gdn_conv_fwd{tag}")
    gc, beta = _gdn_gates_fwd(proj, p["gdn_a_log"], p["gdn_dt_bias"], lay.ab_base, f"gdn_gates_fwd{tag}")
    grow = gc[:, :HEADS].reshape(nc, CHUNK, HEADS).transpose(0, 2, 1)
    o, states = _gdn_chunk_fwd(qkv, gc, beta, grow, f"gdn_chunk_fwd{tag}")
    og = _gdn_post_fwd(o, proj, p["gdn_norm"], lay.z_base, f"gdn_post_fwd{tag}")
    ol, hs = _lru_fwd(proj, p["lru_conv_w"], p["lru_conv_b"], p["lru_w_a"], p["lru_b_a"], p["lru_w_x"], p["lru_b_x"],
                      p["lru_lambda"], lay.xb_base, f"lru_fwd{tag}")
    yg = _matmul(og, p["w_branch_gdn"], "nn", f"branch_gdn{tag}", [F32])
    yl = _matmul(ol, p["w_branch_lru"], "nn", f"branch_lru{tag}", [F32])
    merged = _merge_fwd(proj, yg, yl, lay.gg_base, f"merge_fwd{tag}")
    x1 = _matmul(merged, p["w_out"], "nn", f"out_proj{tag}", [F32], epi=lambda acc, r: (acc + r,), extras=[x])
    h2 = _rms_fwd(x1, p["mlp_norm"], f"rms2_fwd{tag}")
    u, act = _matmul(h2, p["w_up"], "nn", f"mlp_up{tag}", [F32, BF],
                     epi=lambda acc: (acc, jnp.square(jnp.maximum(acc, 0.0))))
    x2 = _matmul(act, p["w_down"], "nn", f"mlp_down{tag}", [F32], epi=lambda acc, r: (acc + r,), extras=[x1])
    saved = dict(x=x, h=h, proj=proj, qkv=qkv, gc=gc, beta=beta, grow=grow, o=o, states=states, og=og, ol=ol, hs=hs,
                 yg=yg, yl=yl, merged=merged, x1=x1, h2=h2, u=u, act=act)
    return x2, saved


def _layer_bwd(dx2, p, sv, lay, tag):
    s, d = dx2.shape
    nc = s // CHUNK
    big, small = {}, {}
    du = _matmul(dx2, p["w_down"], "nt", f"d_act{tag}", [BF],
                 epi=lambda acc, u: (acc * (2.0 * jnp.maximum(u, 0.0)),), extras=[sv["u"]])
    big["w_down"] = _matmul(sv["act"], dx2, "tn", f"dw_down{tag}", [BF]).reshape(N_DEV, -1, d)
    big["w_up"] = _stack_from_cols(_matmul(sv["h2"], du, "tn", f"dw_up{tag}", [BF]))
    dh2 = _matmul(du, p["w_up"], "nt", f"d_h2{tag}", [F32])
    dx1, g = _rms_bwd(sv["x1"], p["mlp_norm"], dh2, dx2, f"rms2_bwd{tag}")
    small["mlp_norm"] = g.reshape(-1)
    dmerged = _matmul(dx1, p["w_out"], "nt", f"d_merged{tag}", [F32])
    big["w_out"] = _matmul(sv["merged"], dx1, "tn", f"dw_out{tag}", [BF]).reshape(N_DEV, -1, d)
    dgg, dgl, dyg, dyl = _merge_bwd(sv["proj"], sv["yg"], sv["yl"], dmerged, lay.gg_base, f"merge_bwd{tag}")
    big["w_branch_gdn"] = _stack_from_cols(_matmul(sv["og"], dyg, "tn", f"dw_bg{tag}", [BF]))
    big["w_branch_lru"] = _stack_from_cols(_matmul(sv["ol"], dyl, "tn", f"dw_bl{tag}", [BF]))
    dog = _matmul(dyg, p["w_branch_gdn"], "nt", f"d_og{tag}", [F32])
    dol = _matmul(dyl, p["w_branch_lru"], "nt", f"d_ol{tag}", [F32])
    (dxb, dyb, dcw, dcb, dwa, dba, dwx, dbx, dlam) = _lru_bwd(
        sv["proj"], sv["hs"], dol, p["lru_conv_w"], p["lru_conv_b"], p["lru_w_a"], p["lru_b_a"], p["lru_w_x"],
        p["lru_b_x"], p["lru_lambda"], lay.xb_base, f"lru_bwd{tag}")
    small.update(lru_conv_w=dcw, lru_conv_b=dcb.reshape(-1), lru_w_a=dwa, lru_b_a=dba.reshape(-1), lru_w_x=dwx,
                 lru_b_x=dbx.reshape(-1), lru_lambda=dlam.reshape(-1))
    do, dz, g = _gdn_post_bwd(sv["o"], sv["proj"], p["gdn_norm"], dog, lay.z_base, f"gdn_post_bwd{tag}")
    small["gdn_norm"] = g.reshape(-1)
    dqkv_c, dgc_col, dbeta, dgrow = _gdn_chunk_bwd(sv["qkv"], sv["gc"], sv["beta"], sv["grow"], sv["states"], do,
                                                   f"gdn_chunk_bwd{tag}")
    dqkv, dgcw = _conv_silu_bwd(sv["proj"], p["gdn_conv_w"], dqkv_c, f"gdn_conv_bwd{tag}")
    small["gdn_conv_w"] = dgcw
    dgc_row = jnp.pad(dgrow.transpose(0, 2, 1).reshape(s, HEADS), ((0, 0), (0, LANE - HEADS)))
    dab, dal, ddt = _gdn_gates_bwd(sv["proj"], p["gdn_a_log"], p["gdn_dt_bias"], dgc_col, dgc_row, dbeta, lay.ab_base,
                                   f"gdn_gates_bwd{tag}")
    small["gdn_a_log"] = dal[0, :HEADS]
    small["gdn_dt_bias"] = ddt[0, :HEADS]
    dproj = jnp.concatenate([dqkv, dz, dxb, dyb, dgg, dgl, dab], axis=1)
    dwin = _matmul(sv["h"], dproj, "tn", f"dw_in{tag}", [BF], tn=1152)
    big["w_in"] = _stack_from_cols(lay.unpermute(dwin))
    dh = _matmul(dproj, p["w_in"], "nt", f"d_h{tag}", [F32], tk=1152)
    dx, g = _rms_bwd(sv["x"], p["attn_norm"], dh, dx1, f"rms1_bwd{tag}")
    small["attn_norm"] = g.reshape(-1)
    return dx, big, small


_BIG = ("w_in", "w_branch_gdn", "w_branch_lru", "w_out", "w_up", "w_down")
_ROW_SHARDED = ("w_out", "w_down")
_CONV = ("gdn_conv_w", "lru_conv_w")
_SMALL = ("attn_norm", "gdn_a_log", "gdn_dt_bias", "gdn_norm", "lru_conv_b", "lru_w_a", "lru_b_a", "lru_w_x",
          "lru_b_x", "lru_lambda", "mlp_norm")
_ORDER = ("attn_norm", "w_in", "gdn_conv_w", "gdn_a_log", "gdn_dt_bias", "gdn_norm", "lru_conv_w", "lru_conv_b",
          "lru_w_a", "lru_b_a", "lru_w_x", "lru_b_x", "lru_lambda", "w_branch_gdn", "w_branch_lru", "w_out",
          "mlp_norm", "w_up", "w_down", "final_norm")


def kernel(x, attn_norm, w_in, gdn_conv_w, gdn_a_log, gdn_dt_bias, gdn_norm, lru_conv_w, lru_conv_b, lru_w_a, lru_b_a, lru_w_x, lru_b_x, lru_lambda, w_branch_gdn, w_branch_lru, w_out, mlp_norm, w_up, w_down, final_norm, loss_target, m_attn_norm, m_w_in, m_gdn_conv_w, m_gdn_a_log, m_gdn_dt_bias, m_gdn_norm, m_lru_conv_w, m_lru_conv_b, m_lru_w_a, m_lru_b_a, m_lru_w_x, m_lru_b_x, m_lru_lambda, m_w_branch_gdn, m_w_branch_lru, m_w_out, m_mlp_norm, m_w_up, m_w_down, m_final_norm, v_attn_norm, v_w_in, v_gdn_conv_w, v_gdn_a_log, v_gdn_dt_bias, v_gdn_norm, v_lru_conv_w, v_lru_conv_b, v_lru_w_a, v_lru_b_a, v_lru_w_x, v_lru_b_x, v_lru_lambda, v_w_branch_gdn, v_w_branch_lru, v_w_out, v_mlp_norm, v_w_up, v_w_down, v_final_norm):
    w = dict(attn_norm=attn_norm, w_in=w_in, gdn_conv_w=gdn_conv_w, gdn_a_log=gdn_a_log, gdn_dt_bias=gdn_dt_bias, gdn_norm=gdn_norm, lru_conv_w=lru_conv_w, lru_conv_b=lru_conv_b, lru_w_a=lru_w_a, lru_b_a=lru_b_a, lru_w_x=lru_w_x, lru_b_x=lru_b_x, lru_lambda=lru_lambda, w_branch_gdn=w_branch_gdn, w_branch_lru=w_branch_lru, w_out=w_out, mlp_norm=mlp_norm, w_up=w_up, w_down=w_down, final_norm=final_norm)
    m = dict(attn_norm=m_attn_norm, w_in=m_w_in, gdn_conv_w=m_gdn_conv_w, gdn_a_log=m_gdn_a_log, gdn_dt_bias=m_gdn_dt_bias, gdn_norm=m_gdn_norm, lru_conv_w=m_lru_conv_w, lru_conv_b=m_lru_conv_b, lru_w_a=m_lru_w_a, lru_b_a=m_lru_b_a, lru_w_x=m_lru_w_x, lru_b_x=m_lru_b_x, lru_lambda=m_lru_lambda, w_branch_gdn=m_w_branch_gdn, w_branch_lru=m_w_branch_lru, w_out=m_w_out, mlp_norm=m_mlp_norm, w_up=m_w_up, w_down=m_w_down, final_norm=m_final_norm)
    v = dict(attn_norm=v_attn_norm, w_in=v_w_in, gdn_conv_w=v_gdn_conv_w, gdn_a_log=v_gdn_a_log, gdn_dt_bias=v_gdn_dt_bias, gdn_norm=v_gdn_norm, lru_conv_w=v_lru_conv_w, lru_conv_b=v_lru_conv_b, lru_w_a=v_lru_w_a, lru_b_a=v_lru_b_a, lru_w_x=v_lru_w_x, lru_b_x=v_lru_b_x, lru_lambda=v_lru_lambda, w_branch_gdn=v_w_branch_gdn, w_branch_lru=v_w_branch_lru, w_out=v_w_out, mlp_norm=v_mlp_norm, w_up=v_w_up, w_down=v_w_down, final_norm=v_final_norm)
    depth = w_in.shape[0]
    x = x[0]
    target = loss_target[0]
    s, d = x.shape
    lay = _Layout(d)
    assert lay.cols == w_in.shape[2] * N_DEV
    cx, cy, cc = _place()
    me = 4 * cx + 2 * cy + cc

    params = []
    for l in range(depth):
        shards = [w[n][l].astype(BF) for n in _BIG] + [w[n][l] for n in _CONV]
        gathered = _all_gather(shards, f"gather_weights_l{l}")
        p = {n: w[n][l] for n in _SMALL}
        for n, g in zip(_BIG + _CONV, gathered):
            full = g.reshape(-1, g.shape[-1]) if n in _ROW_SHARDED else _cols_from_stack(g)
            p[n] = lay.permute(full) if n == "w_in" else full
        params.append(p)

    saved = []
    xl = x
    for l in range(depth):
        xl, sv = _layer_fwd(xl, params[l], lay, f"_l{l}")
        saved.append(sv)
    dx, loss_row, dfinal = _loss_head(xl, final_norm, target, "loss_head")
    loss = lax.psum(loss_row[0, 0], ("x", "y", "c"))

    bigs, smalls = [None] * depth, [None] * depth
    for l in reversed(range(depth)):
        dx, bigs[l], smalls[l] = _layer_bwd(dx, params[l], saved[l], lay, f"_l{l}")

    core = cc.astype(jnp.int32).reshape(1)
    out = {n: None for n in _BIG}
    for l in reversed(range(depth)):
        stacks = [bigs[l][n] for n in _BIG]
        from_sibling = _sibling_exchange(stacks, f"grads_to_sibling_l{l}")
        chip_sums = [_sibling_add(g, r, core, f"chip_sum_{n}_l{l}") for n, g, r in zip(_BIG, stacks, from_sibling)]
        summed = _chip_exchange(chip_sums, f"grads_to_chips_l{l}")
        for n, q in zip(_BIG, summed):
            out[n] = _adamw_sharded(q, w[n], m[n], v[n], l, out[n], f"adamw_{n}_l{l}")

    small_names = _SMALL + _CONV
    part = _pack([jnp.stack([smalls[l][n] for l in range(depth)]) for n in small_names] + [dfinal.reshape(-1)])
    total = _sum_devices(_all_gather([part], "gather_small_grads")[0], "sum_small_grads")
    like = [jnp.stack([smalls[l][n] for l in range(depth)]) for n in small_names] + [final_norm]
    grads = dict(zip(small_names + ("final_norm",), _unpack(total, like)))
    for n in _CONV:
        blk = w[n].shape[2]
        grads[n] = lax.dynamic_slice_in_dim(grads[n], me * blk, blk, axis=2)
    names = small_names + ("final_norm",)
    gp = _pack([grads[n] for n in names])
    upd = _adamw_packed(_pack([w[n] for n in names]), gp, _pack([m[n] for n in names]), _pack([v[n] for n in names]),
                        "adamw_small")
    like = [w[n] for n in names]
    unp = [_unpack(u, like) for u in upd]
    for i, n in enumerate(names):
        out[n] = (grads[n], unp[0][i], unp[1][i], unp[2][i])

    res = [loss, dx.reshape(1, s, d)]
    for k in range(4):
        res += [out[n][k] for n in _ORDER]
    return tuple(res)
```

```python
import functools

import jax
import jax.numpy as jnp
from jax import lax
from jax.experimental import pallas as pl
from jax.experimental.pallas import tpu as pltpu

F32 = jnp.float32
BF = jnp.bfloat16
MESH = pl.DeviceIdType.MESH
N_DEV = 8
N_CHIP = 4

HEADS = 8
HEAD_DIM = 128
GDN_WIDTH = HEADS * HEAD_DIM
CHUNK = 64
CONV_WIDTH = 4
LRU_WIDTH = 1024
LRU_BLOCKS = 8
LRU_BLOCK_DIM = 128
LRU_C = 8.0
RMS_EPS = 1e-6
L2_EPS = 1e-6
ADAM_LR = 0.001
ADAM_B1 = 0.9
ADAM_B2 = 0.999
ADAM_EPS = 1e-08
ADAM_WD = 0.01
ADAM_STEP = 10

LANE = 128
VMEM_LIMIT = 56 * 1024 * 1024


def _cparams(sem=None):
    return pltpu.CompilerParams(dimension_semantics=sem, vmem_limit_bytes=VMEM_LIMIT)


def _tile(n, pref, unit=LANE):
    if n <= pref:
        return n
    best = None
    for t in range(unit, pref + 1, unit):
        if n % t == 0:
            best = t
    assert best is not None, (n, pref, unit)
    return best


_NN = (((1,), (0,)), ((), ()))
_NT = (((1,), (1,)), ((), ()))
_TN = (((0,), (0,)), ((), ()))


def _dg(a, b, dims):
    return lax.dot_general(a.astype(BF), b.astype(BF), dims, preferred_element_type=F32)


class _MM:
    def __init__(self, nn, nt, tn):
        self.nn, self.nt, self.tn = nn, nt, tn


def _make_mm(nn, nt, tn):
    def p_nn(a, b):
        return _dg(a, b, nn)

    def p_nt(a, b):
        return _dg(a, b, nt)

    def p_tn(a, b):
        return _dg(a, b, tn)

    d_nn, d_nt, d_tn = jax.custom_vjp(p_nn), jax.custom_vjp(p_nt), jax.custom_vjp(p_tn)

    def save(f):
        return lambda a, b: (f(a, b), (a, b))

    d_nn.defvjp(save(d_nn), lambda r, g: (d_nt(g, r[1]), d_tn(r[0], g)))
    d_nt.defvjp(save(d_nt), lambda r, g: (d_nn(g, r[1]), d_tn(g, r[0])))
    d_tn.defvjp(save(d_tn), lambda r, g: (d_nt(r[1], g), d_nn(r[0], g)))
    return _MM(p_nn, p_nt, p_tn), _MM(d_nn, d_nt, d_tn)


_BNN = (((2,), (1,)), ((0,), (0,)))
_BNT = (((2,), (2,)), ((0,), (0,)))
_BTN = (((1,), (1,)), ((0,), (0,)))

_PLAIN, _DIFF = _make_mm(_NN, _NT, _TN)
_PLAIN_B, _DIFF_B = _make_mm(_BNN, _BNT, _BTN)


def _sigmoid(x):
    return 1.0 / (1.0 + jnp.exp(-x))


def _silu(x):
    return x * _sigmoid(x)


def _softplus(x):
    return jnp.maximum(x, 0.0) + jnp.log(1.0 + jnp.exp(-jnp.maximum(x, -x)))


def _gelu(x):
    return 0.5 * x * (1.0 + jnp.tanh(0.7978845608028654 * (x + 0.044715 * (x * x * x))))


def _expm1(x):
    s = x * (1.0 + x * (1.0 / 2.0) * (1.0 + x * (1.0 / 3.0) * (1.0 + x * (1.0 / 4.0) * (
        1.0 + x * (1.0 / 5.0) * (1.0 + x * (1.0 / 6.0) * (1.0 + x * (1.0 / 7.0)))))))
    return jnp.where(jnp.maximum(x, -x) < 0.3, s, jnp.exp(x) - 1.0)


def _rms(x, gain):
    ms = jnp.mean(x * x, axis=-1, keepdims=True)
    return x * lax.rsqrt(ms + RMS_EPS) * gain


def _l2n(x):
    return x * lax.rsqrt(jnp.sum(x * x, axis=-1, keepdims=True) + L2_EPS)


def _shift_down(x, r):
    if r == 0:
        return x
    row = lax.broadcasted_iota(jnp.int32, x.shape, 0)
    return jnp.where(row >= r, pltpu.roll(x, r, 0), 0.0)


def _shift_up(x, r):
    if r == 0:
        return x
    n = x.shape[0]
    row = lax.broadcasted_iota(jnp.int32, x.shape, 0)
    return jnp.where(row < n - r, pltpu.roll(x, n - r, 0), 0.0)


def _conv(x, w_ref):
    y = None
    for j in range(CONV_WIDTH):
        t = _shift_down(x, CONV_WIDTH - 1 - j) * w_ref[j:j + 1, :]
        y = t if y is None else y + t
    return y


def _conv_bwd(x, dy, w_ref):
    dx = None
    dws = []
    for j in range(CONV_WIDTH):
        r = CONV_WIDTH - 1 - j
        t = _shift_up(dy, r) * w_ref[j:j + 1, :]
        dx = t if dx is None else dx + t
        dws.append(jnp.sum(dy * _shift_down(x, r), axis=0, keepdims=True))
    return dx, dws


def _scan_tile(a, b, reverse):
    n = a.shape[0]
    row = lax.broadcasted_iota(jnp.int32, a.shape, 0)
    s = 1
    while s < n:
        if reverse:
            keep = row < n - s
            a_sh = jnp.where(keep, pltpu.roll(a, n - s, 0), 1.0)
            b_sh = jnp.where(keep, pltpu.roll(b, n - s, 0), 0.0)
        else:
            keep = row >= s
            a_sh = jnp.where(keep, pltpu.roll(a, s, 0), 1.0)
            b_sh = jnp.where(keep, pltpu.roll(b, s, 0), 0.0)
        b = a * b_sh + b
        a = a * a_sh
        s *= 2
    return a, b


def _dg3(a, b, dims):
    a_hi = a.astype(BF)
    b_hi = b.astype(BF)
    a_lo = a - a_hi.astype(F32)
    b_lo = b - b_hi.astype(F32)
    return _dg(a_hi, b_hi, dims) + (_dg(a_hi, b_lo, dims) + _dg(a_lo, b_hi, dims))


@jax.custom_vjp
def _tri_inv(a):
    n = a.shape[1]
    p = _dg3(a, a, _BNN)
    r = p
    e = 2
    while 2 * e < n:
        p = _dg3(p, p, _BNN)
        r = r + p + _dg3(r, p, _BNN)
        e *= 2
    row = lax.broadcasted_iota(jnp.int32, (1, n, n), 1)
    col = lax.broadcasted_iota(jnp.int32, (1, n, n), 2)
    eye = jnp.where(row == col, 1.0, 0.0)
    return eye - a + r - _dg3(a, r, _BNN)


def _tri_inv_fwd(a):
    t = _tri_inv(a)
    return t, t


def _tri_inv_bwd(t, g):
    return (-_dg3(_dg3(t, g, _BTN), t, _BNT),)


_tri_inv.defvjp(_tri_inv_fwd, _tri_inv_bwd)


def _gdn_chunk(q, k, v, beta, gcol, grow, state, mm):
    c = q.shape[1]
    row = lax.broadcasted_iota(jnp.int32, (1, c, c), 1)
    col = lax.broadcasted_iota(jnp.int32, (1, c, c), 2)
    causal = row >= col
    strict = row > col
    qn = _l2n(q) * (HEAD_DIM ** -0.5)
    kn = _l2n(k)
    dec = jnp.where(causal, jnp.exp(jnp.where(causal, gcol - grow, 0.0)), 0.0)
    kb = kn * beta
    vb = v * beta
    a = jnp.where(strict, mm.nt(kb, kn) * dec, 0.0)
    t = _tri_inv(a)
    eg = jnp.exp(gcol)
    u = mm.nn(t, vb)
    w = mm.nn(t, kb * eg)
    p = jnp.where(causal, mm.nt(qn, kn) * dec, 0.0)
    vn = u - mm.nn(w, state)
    o = mm.nn(qn * eg, state) + mm.nn(p, vn)
    last = lax.broadcasted_iota(jnp.int32, (1, c, 1), 1) == c - 1
    gl = jnp.sum(jnp.where(last, gcol, 0.0), axis=1, keepdims=True)
    kd = kn * jnp.exp(gl - gcol)
    new_state = state * jnp.exp(gl) + mm.tn(kd, vn)
    return o, new_state


def _lru_gates(xc, wa, ba, wx, bx, lam, mm):
    r = _sigmoid(mm.nn(xc, wa) + ba)
    i = _sigmoid(mm.nn(xc, wx) + bx)
    log_a = -LRU_C * r * _softplus(-lam)
    a = jnp.exp(log_a)
    bterm = jnp.sqrt(-_expm1(2.0 * log_a)) * (i * xc)
    return a, bterm


def _matmul(a, b, mode, name, out_dtypes, epi=None, extras=(), tm=1024, tn=1024, tk=512):
    if mode == "nn":
        (m, k), (k2, n) = a.shape, b.shape
    elif mode == "nt":
        (m, k), (n, k2) = a.shape, b.shape
    else:
        (k, m), (k2, n) = a.shape, b.shape
    assert k == k2, (a.shape, b.shape, mode)
    tm, tn, tk = _tile(m, tm), _tile(n, tn), _tile(k, tk)
    nk = k // tk
    dims = {"nn": _NN, "nt": _NT, "tn": _TN}[mode]
    if mode == "nn":
        a_spec = pl.BlockSpec((tm, tk), lambda i, j, kk: (i, kk))
        b_spec = pl.BlockSpec((tk, tn), lambda i, j, kk: (kk, j))
    elif mode == "nt":
        a_spec = pl.BlockSpec((tm, tk), lambda i, j, kk: (i, kk))
        b_spec = pl.BlockSpec((tn, tk), lambda i, j, kk: (j, kk))
    else:
        a_spec = pl.BlockSpec((tk, tm), lambda i, j, kk: (kk, i))
        b_spec = pl.BlockSpec((tk, tn), lambda i, j, kk: (kk, j))
    o_spec = pl.BlockSpec((tm, tn), lambda i, j, kk: (i, j))
    n_ex, n_out = len(extras), len(out_dtypes)
    if epi is None:
        epi = lambda acc: (acc,)

    def body(*refs):
        a_ref, b_ref = refs[0], refs[1]
        ex_refs = refs[2:2 + n_ex]
        out_refs = refs[2 + n_ex:2 + n_ex + n_out]
        acc_ref = refs[-1]
        kk = pl.program_id(2)

        @pl.when(kk == 0)
        def _():
            acc_ref[...] = jnp.zeros_like(acc_ref)

        acc_ref[...] += _dg(a_ref[...], b_ref[...], dims)

        @pl.when(kk == nk - 1)
        def _():
            res = epi(acc_ref[...], *[r[...] for r in ex_refs])
            for r, v in zip(out_refs, res):
                r[...] = v.astype(r.dtype)

    outs = pl.pallas_call(
        body, name=name, grid=(m // tm, n // tn, nk),
        in_specs=[a_spec, b_spec] + [o_spec] * n_ex,
        out_specs=[o_spec] * n_out,
        out_shape=[jax.ShapeDtypeStruct((m, n), d) for d in out_dtypes],
        scratch_shapes=[pltpu.VMEM((tm, tn), F32)],
        compiler_params=_cparams(("parallel", "parallel", "arbitrary")),
    )(a, b, *extras)
    return outs[0] if n_out == 1 else outs


def _rowwise(fn, name, rows, tr, row_ins, params, row_outs, acc_outs=(), ncol=1):
    tr = _tile(rows, tr, 8)
    n_in, n_par, n_ro, n_acc = len(row_ins), len(params), len(row_outs), len(acc_outs)
    in_specs = []
    for _, width, base, per_col in row_ins:
        if per_col:
            in_specs.append(pl.BlockSpec((tr, width), lambda i, j, base=base: (i, base + j)))
        else:
            in_specs.append(pl.BlockSpec((tr, width), lambda i, j, base=base: (i, base)))
    for p in params:
        in_specs.append(pl.BlockSpec(p.shape, lambda i, j: (0, 0)))
    out_specs, out_shape = [], []
    for total, dtype, width, per_col in row_outs:
        if per_col:
            out_specs.append(pl.BlockSpec((tr, width), lambda i, j: (i, j)))
        else:
            out_specs.append(pl.BlockSpec((tr, width), lambda i, j: (i, 0)))
        out_shape.append(jax.ShapeDtypeStruct((rows, total), dtype))
    for shape, dtype in acc_outs:
        out_specs.append(pl.BlockSpec(shape, lambda i, j: (0, 0)))
        out_shape.append(jax.ShapeDtypeStruct(shape, dtype))

    def body(*refs):
        ins = [r[...] for r in refs[:n_in + n_par]]
        ro = refs[n_in + n_par:n_in + n_par + n_ro]
        ao = refs[n_in + n_par + n_ro:]
        res = fn(*ins)
        for r, v in zip(ro, res[:n_ro]):
            r[...] = v.astype(r.dtype)
        first = jnp.logical_and(pl.program_id(0) == 0, pl.program_id(1) == 0)

        @pl.when(first)
        def _():
            for r in ao:
                r[...] = jnp.zeros_like(r)

        for r, v in zip(ao, res[n_ro:]):
            r[...] += v.astype(r.dtype)

    sem = ("arbitrary", "arbitrary") if n_acc else ("parallel", "parallel")
    outs = pl.pallas_call(
        body, name=name, grid=(rows // tr, ncol), in_specs=in_specs, out_specs=out_specs,
        out_shape=out_shape, compiler_params=_cparams(sem),
    )(*[r[0] for r in row_ins], *params)
    return outs


def _row(x):
    return x.reshape(1, -1)


def _rms_fwd(x, gain, name):
    s, d = x.shape
    return _rowwise(lambda xt, g: (_rms(xt, g),), name, s, 512, [(x, d, 0, False)], [_row(gain)],
                    [(d, BF, d, False)])[0]


def _rms_bwd(x, gain, dh, dres, name):
    s, d = x.shape

    def fn(xt, dht, drt, g):
        _, vjp = jax.vjp(_rms, xt, g)
        dx, dg = vjp(dht)
        return drt + dx, dg

    return _rowwise(fn, name, s, 256, [(x, d, 0, False), (dh, d, 0, False), (dres, d, 0, False)], [_row(gain)],
                    [(d, F32, d, False)], [((1, d), F32)])


def _loss_head(x, gain, target, name):
    s, d = x.shape

    def fn(xt, tt, g):
        y, vjp = jax.vjp(_rms, xt, g)
        err = y - tt
        dx, dg = vjp(err * (1.0 / d))
        part = 0.5 * jnp.sum(jnp.sum(err * err, axis=1, keepdims=True), axis=0, keepdims=True) * (1.0 / d)
        return dx, jnp.broadcast_to(part, (1, LANE)), dg

    return _rowwise(fn, name, s, 256, [(x, d, 0, False), (target, d, 0, False)], [_row(gain)],
                    [(d, F32, d, False)], [((1, LANE), F32), ((1, d), F32)])


def _gdn_post_fn(o, z, g):
    return _rms(o, g) * _silu(z)


def _gdn_post_fwd(o, proj, gain, z_base, name):
    s = o.shape[0]
    return _rowwise(lambda ot, zt, g: (_gdn_post_fn(ot, zt, g),), name, s, 1024,
                    [(o, HEAD_DIM, 0, True), (proj, HEAD_DIM, z_base, True)], [_row(gain)],
                    [(GDN_WIDTH, BF, HEAD_DIM, True)], ncol=HEADS)[0]


def _gdn_post_bwd(o, proj, gain, dog, z_base, name):
    s = o.shape[0]

    def fn(ot, zt, dt, g):
        _, vjp = jax.vjp(_gdn_post_fn, ot, zt, g)
        return vjp(dt)

    return _rowwise(fn, name, s, 1024,
                    [(o, HEAD_DIM, 0, True), (proj, HEAD_DIM, z_base, True), (dog, HEAD_DIM, 0, True)],
                    [_row(gain)], [(GDN_WIDTH, F32, HEAD_DIM, True), (GDN_WIDTH, BF, HEAD_DIM, True)],
                    [((1, HEAD_DIM), F32)], ncol=HEADS)


def _merge_fn(gg, gl, yg, yl):
    return _sigmoid(gg) * yg + _sigmoid(gl) * yl


def _merge_fwd(proj, yg, yl, gg_base, name):
    s, d = yg.shape
    return _rowwise(lambda a, b, c, e: (_merge_fn(a, b, c, e),), name, s, 256,
                    [(proj, d, gg_base, False), (proj, d, gg_base + 1, False), (yg, d, 0, False), (yl, d, 0, False)],
                    [], [(d, BF, d, False)])[0]


def _merge_bwd(proj, yg, yl, dmerged, gg_base, name):
    s, d = yg.shape

    def fn(a, b, c, e, dm):
        _, vjp = jax.vjp(_merge_fn, a, b, c, e)
        return vjp(dm)

    return _rowwise(fn, name, s, 128,
                    [(proj, d, gg_base, False), (proj, d, gg_base + 1, False), (yg, d, 0, False), (yl, d, 0, False),
                     (dmerged, d, 0, False)], [], [(d, BF, d, False)] * 4)


def _conv_silu_fwd(proj, w, name):
    s = proj.shape[0]
    cols = w.shape[1]
    cw = _tile(cols, LANE)

    def body(x_ref, w_ref, o_ref):
        o_ref[...] = _silu(_conv(x_ref[...], w_ref))

    return pl.pallas_call(
        body, name=name, grid=(cols // cw,),
        in_specs=[pl.BlockSpec((s, cw), lambda j: (0, j)), pl.BlockSpec((CONV_WIDTH, cw), lambda j: (0, j))],
        out_specs=pl.BlockSpec((s, cw), lambda j: (0, j)),
        out_shape=jax.ShapeDtypeStruct((s, cols), F32), compiler_params=_cparams(("parallel",)),
    )(proj, w)


def _conv_silu_bwd(proj, w, dout, name):
    s = proj.shape[0]
    cols = w.shape[1]
    cw = _tile(cols, LANE)

    def body(x_ref, w_ref, d_ref, dx_ref, dw_ref):
        x = x_ref[...]
        y = _conv(x, w_ref)
        sg = _sigmoid(y)
        dy = d_ref[...] * (sg * (1.0 + y * (1.0 - sg)))
        dx, dws = _conv_bwd(x, dy, w_ref)
        dx_ref[...] = dx.astype(dx_ref.dtype)
        for j in range(CONV_WIDTH):
            dw_ref[j:j + 1, :] = dws[j]

    return pl.pallas_call(
        body, name=name, grid=(cols // cw,),
        in_specs=[pl.BlockSpec((s, cw), lambda j: (0, j)), pl.BlockSpec((CONV_WIDTH, cw), lambda j: (0, j)),
                  pl.BlockSpec((s, cw), lambda j: (0, j))],
        out_specs=[pl.BlockSpec((s, cw), lambda j: (0, j)), pl.BlockSpec((CONV_WIDTH, cw), lambda j: (0, j))],
        out_shape=[jax.ShapeDtypeStruct((s, cols), BF), jax.ShapeDtypeStruct((CONV_WIDTH, cols), F32)],
        compiler_params=_cparams(("parallel",)),
    )(proj, w, dout)


def _cumsum_mask(tr, transpose):
    row = lax.broadcasted_iota(jnp.int32, (tr, tr), 0)
    col = lax.broadcasted_iota(jnp.int32, (tr, tr), 1)
    same = (row // CHUNK) == (col // CHUNK)
    tri = (row <= col) if transpose else (row >= col)
    return jnp.where(jnp.logical_and(same, tri), 1.0, 0.0)


def _lane_pad(v):
    return jnp.pad(v.reshape(1, -1), ((0, 0), (0, LANE - v.shape[0])))


def _gdn_gates_fwd(proj, a_log, dt_bias, ab_base, name):
    s = proj.shape[0]

    def fn(ab, al, dt):
        g = -jnp.exp(al) * _softplus(ab + dt)
        beta = _sigmoid(pltpu.roll(ab, LANE - HEADS, 1))
        gc = jnp.dot(_cumsum_mask(ab.shape[0], False), g, precision=lax.Precision.HIGHEST,
                     preferred_element_type=F32)
        return gc, beta

    return _rowwise(fn, name, s, 512, [(proj, LANE, ab_base, False)], [_lane_pad(a_log), _lane_pad(dt_bias)],
                    [(LANE, F32, LANE, False)] * 2)


def _gdn_gates_bwd(proj, a_log, dt_bias, dgc_col, dgc_row, dbeta, ab_base, name):
    s = proj.shape[0]

    def fn(ab, dc, dr, db, al, dt):
        lane = lax.broadcasted_iota(jnp.int32, ab.shape, 1)
        dg = jnp.dot(_cumsum_mask(ab.shape[0], True), dc + dr, precision=lax.Precision.HIGHEST,
                     preferred_element_type=F32)
        ea = jnp.exp(al)
        pre = ab + dt
        g = -ea * _softplus(pre)
        da = jnp.where(lane < HEADS, dg * (-ea) * _sigmoid(pre), 0.0)
        dal = jnp.sum(jnp.where(lane < HEADS, dg * g, 0.0), axis=0, keepdims=True)
        ddt = jnp.sum(da, axis=0, keepdims=True)
        beta = _sigmoid(pltpu.roll(ab, LANE - HEADS, 1))
        dbl = jnp.where(lane < HEADS, db * beta * (1.0 - beta), 0.0)
        dab = da + pltpu.roll(dbl, HEADS, 1)
        return dab, dal, ddt

    return _rowwise(fn, name, s, 512,
                    [(proj, LANE, ab_base, False), (dgc_col, LANE, 0, False), (dgc_row, LANE, 0, False),
                     (dbeta, LANE, 0, False)], [_lane_pad(a_log), _lane_pad(dt_bias)],
                    [(LANE, BF, LANE, False)], [((1, LANE), F32), ((1, LANE), F32)])


def _head_col(blk, h):
    lane = lax.broadcasted_iota(jnp.int32, blk.shape, 1)
    return jnp.sum(jnp.where(lane == h, blk, 0.0), axis=1, keepdims=True)


def _chunk_operands(qkv_ref, gc_ref, b_ref, gr_ref):
    heads = range(HEADS)
    q, k, v = (jnp.stack([qkv_ref[:, (j * HEADS + h) * HEAD_DIM:(j * HEADS + h + 1) * HEAD_DIM] for h in heads])
               for j in range(3))
    gcs, betas = gc_ref[...], b_ref[...]
    beta = jnp.stack([_head_col(betas, h) for h in heads])
    gcol = jnp.stack([_head_col(gcs, h) for h in heads])
    grow = jnp.stack([gr_ref[0, h:h + 1, :] for h in heads])
    return q, k, v, beta, gcol, grow


def _gdn_chunk_fwd(qkv, gc, beta, grow, name):
    s = qkv.shape[0]
    nc = s // CHUNK

    def body(qkv_ref, gc_ref, b_ref, gr_ref, o_ref, st_ref, state):
        @pl.when(pl.program_id(0) == 0)
        def _():
            state[...] = jnp.zeros_like(state)

        st = state[...]
        st_ref[0] = st
        o, new = _gdn_chunk(*_chunk_operands(qkv_ref, gc_ref, b_ref, gr_ref), st, _PLAIN_B)
        for h in range(HEADS):
            o_ref[:, h * HEAD_DIM:(h + 1) * HEAD_DIM] = o[h]
        state[...] = new

    return pl.pallas_call(
        body, name=name, grid=(nc,),
        in_specs=[pl.BlockSpec((CHUNK, 3 * GDN_WIDTH), lambda c: (c, 0)),
                  pl.BlockSpec((CHUNK, LANE), lambda c: (c, 0)), pl.BlockSpec((CHUNK, LANE), lambda c: (c, 0)),
                  pl.BlockSpec((1, HEADS, CHUNK), lambda c: (c, 0, 0))],
        out_specs=[pl.BlockSpec((CHUNK, GDN_WIDTH), lambda c: (c, 0)),
                   pl.BlockSpec((1, HEADS, HEAD_DIM, HEAD_DIM), lambda c: (c, 0, 0, 0))],
        out_shape=[jax.ShapeDtypeStruct((s, GDN_WIDTH), F32),
                   jax.ShapeDtypeStruct((nc, HEADS, HEAD_DIM, HEAD_DIM), F32)],
        scratch_shapes=[pltpu.VMEM((HEADS, HEAD_DIM, HEAD_DIM), F32)],
        compiler_params=_cparams(("arbitrary",)),
    )(qkv, gc, beta, grow)


def _gdn_chunk_bwd(qkv, gc, beta, grow, states, do, name):
    s = qkv.shape[0]
    nc = s // CHUNK

    def body(qkv_ref, gc_ref, b_ref, gr_ref, st_ref, do_ref, dqkv_ref, dgc_ref, db_ref, dgr_ref, dstate):
        @pl.when(pl.program_id(0) == 0)
        def _():
            dstate[...] = jnp.zeros_like(dstate)

        lane = lax.broadcasted_iota(jnp.int32, (CHUNK, LANE), 1)
        _, vjp = jax.vjp(functools.partial(_gdn_chunk, mm=_DIFF_B), *_chunk_operands(qkv_ref, gc_ref, b_ref, gr_ref),
                         st_ref[0])
        do = jnp.stack([do_ref[:, h * HEAD_DIM:(h + 1) * HEAD_DIM] for h in range(HEADS)])
        dq, dk, dv, dbeta, dgcol, dgrow, dst = vjp((do, dstate[...]))
        dstate[...] = dst
        dgc = jnp.zeros((CHUNK, LANE), F32)
        db = jnp.zeros((CHUNK, LANE), F32)
        for h in range(HEADS):
            for j, dx in enumerate((dq, dk, dv)):
                dqkv_ref[:, (j * HEADS + h) * HEAD_DIM:(j * HEADS + h + 1) * HEAD_DIM] = dx[h]
            dgc = dgc + jnp.where(lane == h, dgcol[h], 0.0)
            db = db + jnp.where(lane == h, dbeta[h], 0.0)
            dgr_ref[0, h:h + 1, :] = dgrow[h]
        dgc_ref[...] = dgc
        db_ref[...] = db

    rev = lambda c: nc - 1 - c
    wide = pl.BlockSpec((CHUNK, 3 * GDN_WIDTH), lambda c: (rev(c), 0))
    lanes = pl.BlockSpec((CHUNK, LANE), lambda c: (rev(c), 0))
    rows = pl.BlockSpec((1, HEADS, CHUNK), lambda c: (rev(c), 0, 0))
    return pl.pallas_call(
        body, name=name, grid=(nc,),
        in_specs=[wide, lanes, lanes, rows,
                  pl.BlockSpec((1, HEADS, HEAD_DIM, HEAD_DIM), lambda c: (rev(c), 0, 0, 0)),
                  pl.BlockSpec((CHUNK, GDN_WIDTH), lambda c: (rev(c), 0))],
        out_specs=[wide, lanes, lanes, rows],
        out_shape=[jax.ShapeDtypeStruct((s, 3 * GDN_WIDTH), F32)]
        + [jax.ShapeDtypeStruct((s, LANE), F32)] * 2 + [jax.ShapeDtypeStruct((nc, HEADS, CHUNK), F32)],
        scratch_shapes=[pltpu.VMEM((HEADS, HEAD_DIM, HEAD_DIM), F32)],
        compiler_params=_cparams(("arbitrary",)),
    )(qkv, gc, beta, grow, states, do)


def _lru_specs(s, xb_base):
    col = lambda base: pl.BlockSpec((s, LRU_BLOCK_DIM), lambda n, base=base: (0, base + n))
    vec = pl.BlockSpec((1, LRU_BLOCK_DIM), lambda n: (0, n))
    mat = pl.BlockSpec((None, LRU_BLOCK_DIM, LRU_BLOCK_DIM), lambda n: (n, 0, 0))
    cw = pl.BlockSpec((CONV_WIDTH, LRU_BLOCK_DIM), lambda n: (0, n))
    return col, vec, mat, cw


def _lru_fwd(proj, cw, cb, wa, ba, wx, bx, lam, xb_base, name):
    s = proj.shape[0]
    tt = _tile(s, 256, 8)
    col, vec, mat, cws = _lru_specs(s, xb_base)

    def body(xb_ref, yb_ref, cw_ref, cb_ref, wa_ref, ba_ref, wx_ref, bx_ref, lam_ref, o_ref, h_ref, xc_s):
        xc_s[...] = _conv(xb_ref[...], cw_ref) + cb_ref[...]
        par = (wa_ref[...], ba_ref[...], wx_ref[...], bx_ref[...], lam_ref[...])

        def step(t, carry):
            sl = pl.ds(pl.multiple_of(t * tt, tt), tt)
            a, bt = _lru_gates(xc_s[sl, :], *par, _PLAIN)
            aa, bb = _scan_tile(a, bt, False)
            h = aa * carry + bb
            h_ref[sl, :] = h
            o_ref[sl, :] = (h * _gelu(yb_ref[sl, :])).astype(o_ref.dtype)
            return h_ref[pl.ds(t * tt + tt - 1, 1), :]

        lax.fori_loop(0, s // tt, step, jnp.zeros((1, LRU_BLOCK_DIM), F32))

    nb = LRU_BLOCKS
    return pl.pallas_call(
        body, name=name, grid=(LRU_BLOCKS,),
        in_specs=[col(xb_base), col(xb_base + nb), cws, vec, mat, vec, mat, vec, vec],
        out_specs=[col(0), col(0)],
        out_shape=[jax.ShapeDtypeStruct((s, LRU_WIDTH), BF), jax.ShapeDtypeStruct((s, LRU_WIDTH), F32)],
        scratch_shapes=[pltpu.VMEM((s, LRU_BLOCK_DIM), F32)],
        compiler_params=_cparams(("parallel",)),
    )(proj, proj, cw, _row(cb), wa, _row(ba), wx, _row(bx), _row(lam))


def _lru_bwd(proj, hs, dol, cw, cb, wa, ba, wx, bx, lam, xb_base, name):
    s = proj.shape[0]
    tt = _tile(s, 256, 8)
    nt = s // tt
    col, vec, mat, cws = _lru_specs(s, xb_base)

    def body(xb_ref, yb_ref, h_ref, d_ref, cw_ref, cb_ref, wa_ref, ba_ref, wx_ref, bx_ref, lam_ref,
             dxb_ref, dyb_ref, dcw_ref, dcb_ref, dwa_ref, dba_ref, dwx_ref, dbx_ref, dlam_ref,
             xc_s, a_s, dh_s, dxc_s):
        xc_s[...] = _conv(xb_ref[...], cw_ref) + cb_ref[...]
        par = (wa_ref[...], ba_ref[...], wx_ref[...], bx_ref[...], lam_ref[...])
        row = lax.broadcasted_iota(jnp.int32, (tt, LRU_BLOCK_DIM), 0)
        tile = lambda t: pl.ds(pl.multiple_of(t * tt, tt), tt)

        def prep(t, carry):
            sl = tile(t)
            a_s[sl, :] = _lru_gates(xc_s[sl, :], *par, _PLAIN)[0]
            d = d_ref[sl, :]
            gy, vjp_y = jax.vjp(_gelu, yb_ref[sl, :])
            dyb_ref[sl, :] = vjp_y(d * h_ref[sl, :])[0].astype(dyb_ref.dtype)
            dh_s[sl, :] = d * gy
            return carry

        lax.fori_loop(0, nt, prep, 0)

        def rscan(i, carry):
            dh_next, a_next = carry
            t = nt - 1 - i
            sl = tile(t)
            a_sh = jnp.where(row == tt - 1, a_next, pltpu.roll(a_s[sl, :], tt - 1, 0))
            aa, bb = _scan_tile(a_sh, dh_s[sl, :], True)
            dh_s[sl, :] = aa * dh_next + bb
            first = pl.ds(t * tt, 1)
            return dh_s[first, :], a_s[first, :]

        zero = jnp.zeros((1, LRU_BLOCK_DIM), F32)
        lax.fori_loop(0, nt, rscan, (zero, zero))

        def gates_vjp(t, acc):
            sl = tile(t)
            _, vjp_g = jax.vjp(functools.partial(_lru_gates, mm=_DIFF), xc_s[sl, :], *par)
            dh = dh_s[sl, :]
            before = jnp.where(t > 0, h_ref[pl.ds(jnp.maximum(t * tt - 1, 0), 1), :], 0.0)
            h_prev = jnp.where(row == 0, before, pltpu.roll(h_ref[sl, :], 1, 0))
            dxc, *dpar = vjp_g((dh * h_prev, dh))
            dxc_s[sl, :] = dxc
            return tuple(x + y for x, y in zip(acc, dpar))

        dwa, dba, dwx, dbx, dlam = lax.fori_loop(0, nt, gates_vjp, tuple(jnp.zeros_like(p) for p in par))
        dwa_ref[...] = dwa
        dba_ref[...] = dba
        dwx_ref[...] = dwx
        dbx_ref[...] = dbx
        dlam_ref[...] = dlam
        dxc = dxc_s[...]
        dcb_ref[...] = jnp.sum(dxc, axis=0, keepdims=True)
        dxb, dws = _conv_bwd(xb_ref[...], dxc, cw_ref)
        dxb_ref[...] = dxb.astype(dxb_ref.dtype)
        for j in range(CONV_WIDTH):
            dcw_ref[j:j + 1, :] = dws[j]

    nb = LRU_BLOCKS
    w = LRU_WIDTH
    return pl.pallas_call(
        body, name=name, grid=(LRU_BLOCKS,),
        in_specs=[col(xb_base), col(xb_base + nb), col(0), col(0), cws, vec, mat, vec, mat, vec, vec],
        out_specs=[col(0), col(0), cws, vec, mat, vec, mat, vec, vec],
        out_shape=[jax.ShapeDtypeStruct((s, w), BF), jax.ShapeDtypeStruct((s, w), BF),
                   jax.ShapeDtypeStruct((CONV_WIDTH, w), F32), jax.ShapeDtypeStruct((1, w), F32),
                   jax.ShapeDtypeStruct(wa.shape, F32), jax.ShapeDtypeStruct((1, w), F32),
                   jax.ShapeDtypeStruct(wx.shape, F32), jax.ShapeDtypeStruct((1, w), F32),
                   jax.ShapeDtypeStruct((1, w), F32)],
        scratch_shapes=[pltpu.VMEM((s, LRU_BLOCK_DIM), F32)] * 4,
        compiler_params=_cparams(("parallel",)),
    )(proj, proj, hs, dol, cw, _row(cb), wa, _row(ba), wx, _row(bx), _row(lam))


def _place():
    return lax.axis_index("x"), lax.axis_index("y"), lax.axis_index("c")


def _all_gather(blocks, name):
    n = len(blocks)
    per = N_DEV - 1

    def body(*refs):
        ins, outs = refs[:n], refs[n:2 * n]
        send_sems, recv_sems, local_sems = refs[2 * n:]
        x, y, c = _place()
        me, sibling = (x, y, c), (x, y, 1 - c)
        chips = [(1 - x, y), (x, 1 - y), (1 - x, 1 - y)]

        def copy(a, k, block, to, src=None):
            dst = outs[a].at[4 * block[0] + 2 * block[1] + block[2]]
            return pltpu.make_async_remote_copy(
                src_ref=dst if src is None else src, dst_ref=dst, send_sem=send_sems.at[a * per + k],
                recv_sem=recv_sems.at[a * per + k], device_id=to, device_id_type=MESH)

        started = []
        for a in range(n):
            mine = pltpu.make_async_copy(ins[a], outs[a].at[4 * x + 2 * y + c], local_sems.at[a])
            mine.start()
            started.append(mine)
        sends = []
        for a in range(n):
            first = [copy(a, 0, me, sibling, src=ins[a])]
            first += [copy(a, 1 + j, me, (*chip, c), src=ins[a]) for j, chip in enumerate(chips)]
            for cp in first:
                cp.start()
            sends += first
        for a in range(n):
            for j, chip in enumerate(chips):
                copy(a, 1 + j, (*chip, c), me).wait_recv()
                passed = copy(a, 4 + j, (*chip, c), sibling)
                passed.start()
                sends.append(passed)
        for a in range(n):
            copy(a, 0, sibling, me).wait_recv()
            for j, chip in enumerate(chips):
                copy(a, 4 + j, (*chip, 1 - c), me).wait_recv()
        for cp in sends:
            cp.wait_send()
        for cp in started:
            cp.wait()

    any_spec = pl.BlockSpec(memory_space=pl.ANY)
    return pl.pallas_call(
        body, name=name, in_specs=[any_spec] * n, out_specs=[any_spec] * n,
        out_shape=[jax.ShapeDtypeStruct((N_DEV,) + b.shape, b.dtype) for b in blocks],
        scratch_shapes=[pltpu.SemaphoreType.DMA((n * per,)), pltpu.SemaphoreType.DMA((n * per,)),
                        pltpu.SemaphoreType.DMA((n,))],
    )(*blocks)


_HBM = pl.BlockSpec(memory_space=pltpu.HBM)
_SEM = pl.BlockSpec(memory_space=pltpu.SEMAPHORE)
_ANY = pl.BlockSpec(memory_space=pl.ANY)
_EFFECT = pltpu.SideEffectType.DATAFLOW_SIDE_EFFECTING


def _peers(x, y, c):
    return [(x, y, 1 - c), (1 - x, y, c), (1 - x, y, 1 - c), (x, 1 - y, c), (x, 1 - y, 1 - c),
            (1 - x, 1 - y, c), (1 - x, 1 - y, 1 - c)]


class _InFlight:
    def __init__(self, send_sem, recv_sem, src, land):
        self.send_sem, self.recv_sem, self.src, self.land = send_sem, recv_sem, src, land


def _send_start(srcs, whole, after, name):
    n = len(srcs)
    lands = [lax.empty(((N_DEV,) + s.shape) if whole else s.shape, s.dtype) for s in srcs]

    def body(*refs):
        src_refs, land_refs = refs[:n], refs[n:2 * n]
        outs = refs[2 * n + 1:]
        send_sems, recv_sems, token = outs[:n], outs[n:2 * n], outs[-1]
        x, y, c = _place()
        me = 4 * x + 2 * y + c
        for a in range(n):
            for p in _peers(x, y, c):
                src = src_refs[a] if whole else src_refs[a].at[4 * p[0] + 2 * p[1] + p[2]]
                pltpu.make_async_remote_copy(src_ref=src, dst_ref=land_refs[a].at[me], send_sem=send_sems[a],
                                             recv_sem=recv_sems[a], device_id=p, device_id_type=MESH).start()
        token[...] = jnp.zeros_like(token)

    hbm = lambda a: pltpu.HBM(a.shape, a.dtype)
    outs = pl.pallas_call(
        body, name=name,
        out_shape=[pltpu.SemaphoreType.DMA(())] * (2 * n) + [hbm(s) for s in srcs] + [hbm(l) for l in lands]
        + [jax.ShapeDtypeStruct((8, LANE), F32)],
        in_specs=[_HBM] * (2 * n) + [_ANY],
        out_specs=[_SEM] * (2 * n) + [_HBM] * (2 * n) + [pl.BlockSpec(memory_space=pltpu.VMEM)],
        input_output_aliases={i: 2 * n + i for i in range(2 * n)},
        compiler_params=pltpu.CompilerParams(has_side_effects=_EFFECT),
    )(*[pltpu.with_memory_space_constraint(s, pltpu.HBM) for s in srcs],
      *[pltpu.with_memory_space_constraint(l, pltpu.HBM) for l in lands], after)
    flights = [_InFlight(outs[a], outs[n + a], outs[2 * n + a], outs[3 * n + a]) for a in range(n)]
    return flights, outs[-1]


def _send_wait(flights, after, name):
    n = len(flights)

    def body(*refs):
        x, y, c = _place()
        for a in range(n):
            land_ref, send_sem, recv_sem = refs[4 * a + 1], refs[4 * a + 2], refs[4 * a + 3]
            seven = land_ref.at[pl.ds(0, N_DEV - 1)]
            copy = pltpu.make_async_remote_copy(src_ref=seven, dst_ref=seven, send_sem=send_sem, recv_sem=recv_sem,
                                                device_id=(x, y, 1 - c), device_id_type=MESH)
            copy.wait_send()
            copy.wait_recv()

    ins, out_shape = [], []
    for f in flights:
        ins += [f.src, f.land, f.send_sem, f.recv_sem]
        out_shape += [pltpu.HBM(f.src.shape, f.src.dtype), pltpu.HBM(f.land.shape, f.land.dtype)]
    outs = pl.pallas_call(
        body, name=name, out_shape=out_shape,
        in_specs=[_HBM, _HBM, _SEM, _SEM] * n + [_ANY], out_specs=[_HBM] * (2 * n),
        input_output_aliases={4 * a + i: 2 * a + i for a in range(n) for i in range(2)},
        compiler_params=pltpu.CompilerParams(has_side_effects=_EFFECT),
    )(*ins, after)
    return [(outs[2 * a], outs[2 * a + 1]) for a in range(n)]


def _adamw_math(w, g, m, v):
    m = ADAM_B1 * m + (1.0 - ADAM_B1) * g
    v = ADAM_B2 * v + (1.0 - ADAM_B2) * (g * g)
    m_hat = m / (1.0 - ADAM_B1 ** ADAM_STEP)
    v_hat = v / (1.0 - ADAM_B2 ** ADAM_STEP)
    delta = -ADAM_LR * (m_hat / (jnp.sqrt(v_hat) + ADAM_EPS) + ADAM_WD * w)
    return delta, m, v


def _adamw_sharded(mine, landed, me, w, m, v, layer, prev, name):
    depth, r, c = w.shape
    tr = _tile(r, 128, 8)
    n_prev = 0 if prev is None else 4

    def body(*refs):
        me_ref, own_ref, land_ref, w_ref, m_ref, v_ref = refs[:6]
        g_ref, d_ref, nm_ref, nv_ref = refs[6 + n_prev:]
        g = None
        for k in range(N_DEV):
            part = jnp.where(me_ref[0] == k, own_ref[0], land_ref[k]).astype(F32)
            g = part if g is None else g + part
        delta, nm, nv = _adamw_math(w_ref[...], g, m_ref[...], v_ref[...])
        g_ref[...] = g
        d_ref[...] = delta
        nm_ref[...] = nm
        nv_ref[...] = nv

    lay = pl.BlockSpec((None, tr, c), lambda i, me_ref: (layer, i, 0))
    return pl.pallas_call(
        body, name=name,
        grid_spec=pltpu.PrefetchScalarGridSpec(
            num_scalar_prefetch=1, grid=(r // tr,),
            in_specs=[pl.BlockSpec((1, tr, c), lambda i, me_ref: (me_ref[0], i, 0)),
                      pl.BlockSpec((N_DEV, tr, c), lambda i, me_ref: (0, i, 0)), lay, lay, lay] + [_ANY] * n_prev,
            out_specs=[lay] * 4),
        out_shape=[jax.ShapeDtypeStruct(w.shape, F32)] * 4,
        input_output_aliases={6 + i: i for i in range(n_prev)},
        compiler_params=_cparams(("parallel",)),
    )(me, mine, landed, w, m, v, *(prev or ()))


def _sum_devices(stack, name):
    _, r, c = stack.shape
    tr = _tile(r, 512, 8)

    def body(s_ref, o_ref):
        g = s_ref[0]
        for j in range(1, N_DEV):
            g = g + s_ref[j]
        o_ref[...] = g

    return pl.pallas_call(
        body, name=name, grid=(r // tr,), in_specs=[pl.BlockSpec((N_DEV, tr, c), lambda i: (0, i, 0))],
        out_specs=pl.BlockSpec((tr, c), lambda i: (i, 0)), out_shape=jax.ShapeDtypeStruct((r, c), F32),
        compiler_params=_cparams(("parallel",)),
    )(stack)


def _adamw_packed(w, g, m, v, name):
    r, c = w.shape
    tr = _tile(r, 512, 8)

    def body(w_ref, g_ref, m_ref, v_ref, d_ref, nm_ref, nv_ref):
        delta, nm, nv = _adamw_math(w_ref[...], g_ref[...], m_ref[...], v_ref[...])
        d_ref[...] = delta
        nm_ref[...] = nm
        nv_ref[...] = nv

    spec = pl.BlockSpec((tr, c), lambda i: (i, 0))
    return pl.pallas_call(
        body, name=name, grid=(r // tr,), in_specs=[spec] * 4, out_specs=[spec] * 3,
        out_shape=[jax.ShapeDtypeStruct((r, c), F32)] * 3, compiler_params=_cparams(("parallel",)),
    )(w, g, m, v)


def _pack(arrays):
    unit = 8 * LANE
    flat = []
    for a in arrays:
        f = a.reshape(-1)
        flat.append(jnp.pad(f, (0, (-f.shape[0]) % unit)))
    return jnp.concatenate(flat).reshape(-1, LANE)


def _unpack(packed, like):
    unit = 8 * LANE
    flat = packed.reshape(-1)
    out, off = [], 0
    for a in like:
        n = 1
        for d in a.shape:
            n *= d
        out.append(flat[off:off + n].reshape(a.shape))
        off += n + (-n) % unit
    return out


class _Layout:
    def __init__(self, d):
        self.d = d
        self.n_front = 4 * GDN_WIDTH
        self.n_ab = 2 * HEADS
        self.n_back = 2 * LRU_WIDTH + 2 * d
        self.cols = self.n_front + self.n_ab + self.n_back
        self.pcols = self.n_front + self.n_back + LANE
        self.z_base = 3 * GDN_WIDTH // HEAD_DIM
        self.xb_base = self.n_front // LRU_BLOCK_DIM
        self.gg_off = self.n_front + 2 * LRU_WIDTH
        assert self.gg_off % d == 0
        self.gg_base = self.gg_off // d
        self.ab_base = (self.n_front + self.n_back) // LANE

    def permute(self, w):
        f, ab = self.n_front, self.n_ab
        pad = jnp.zeros((w.shape[0], LANE - ab), w.dtype)
        return jnp.concatenate([w[:, :f], w[:, f + ab:], w[:, f:f + ab], pad], axis=1)

    def unpermute(self, w):
        f, ab, b = self.n_front, self.n_ab, self.n_back
        return jnp.concatenate([w[:, :f], w[:, f + b:f + b + ab], w[:, f:f + b]], axis=1)


def _cols_from_stack(stack):
    nd, k, n = stack.shape
    return stack.transpose(1, 0, 2).reshape(k, nd * n)


def _stack_from_cols(w):
    k, n = w.shape
    return w.reshape(k, N_DEV, n // N_DEV).transpose(1, 0, 2)


def _after(v, *tokens):
    for t in tokens:
        v = v + t[0, 0]
    return v


def _layer_fwd(x, p, late, lay, tag):
    s, d = x.shape
    nc = s // CHUNK
    h = _rms_fwd(x, p["attn_norm"], f"rms1_fwd{tag}")
    proj = _matmul(h, p["w_in"], "nn", f"in_proj{tag}", [F32], tn=1152)
    qkv = _conv_silu_fwd(proj, p["gdn_conv_w"], f"gdn_conv_fwd{tag}")
    gc, beta = _gdn_gates_fwd(proj, p["gdn_a_log"], p["gdn_dt_bias"], lay.ab_base, f"gdn_gates_fwd{tag}")
    grow = gc[:, :HEADS].reshape(nc, CHUNK, HEADS).transpose(0, 2, 1)
    o, states = _gdn_chunk_fwd(qkv, gc, beta, grow, f"gdn_chunk_fwd{tag}")
    og = _gdn_post_fwd(o, proj, p["gdn_norm"], lay.z_base, f"gdn_post_fwd{tag}")
    ol, hs = _lru_fwd(proj, p["lru_conv_w"], p["lru_conv_b"], p["lru_w_a"], p["lru_b_a"], p["lru_w_x"], p["lru_b_x"],
                      p["lru_lambda"], lay.xb_base, f"lru_fwd{tag}")
    late(hs)
    yg = _matmul(og, p["w_branch_gdn"], "nn", f"branch_gdn{tag}", [F32])
    yl = _matmul(ol, p["w_branch_lru"], "nn", f"branch_lru{tag}", [F32])
    merged = _merge_fwd(proj, yg, yl, lay.gg_base, f"merge_fwd{tag}")
    x1 = _matmul(merged, p["w_out"], "nn", f"out_proj{tag}", [F32], epi=lambda acc, r: (acc + r,), extras=[x])
    h2 = _rms_fwd(x1, p["mlp_norm"], f"rms2_fwd{tag}")
    u, act = _matmul(h2, p["w_up"], "nn", f"mlp_up{tag}", [F32, BF],
                     epi=lambda acc: (acc, jnp.square(jnp.maximum(acc, 0.0))))
    x2 = _matmul(act, p["w_down"], "nn", f"mlp_down{tag}", [F32], epi=lambda acc, r: (acc + r,), extras=[x1])
    saved = dict(x=x, h=h, proj=proj, qkv=qkv, gc=gc, beta=beta, grow=grow, o=o, states=states, og=og, ol=ol, hs=hs,
                 yg=yg, yl=yl, merged=merged, x1=x1, h2=h2, u=u, act=act)
    return x2, saved


def _layer_bwd(dx2, p, sv, send, after_last_send, lay, tag):
    s, d = dx2.shape
    nc = s // CHUNK
    small = {}
    du = _matmul(dx2, p["w_down"], "nt", f"d_act{tag}", [BF],
                 epi=lambda acc, u: (acc * (2.0 * jnp.maximum(u, 0.0)),), extras=[sv["u"]])
    sent = send({"w_down": _matmul(sv["act"], dx2, "tn", f"dw_down{tag}", [BF]).reshape(N_DEV, -1, d),
                 "w_up": _stack_from_cols(_matmul(sv["h2"], du, "tn", f"dw_up{tag}", [BF]))})
    dh2 = _matmul(du, p["w_up"], "nt", f"d_h2{tag}", [F32])
    dx1, g = _rms_bwd(sv["x1"], _after(p["mlp_norm"], sent), dh2, dx2, f"rms2_bwd{tag}")
    small["mlp_norm"] = g.reshape(-1)
    dmerged = _matmul(dx1, p["w_out"], "nt", f"d_merged{tag}", [F32])
    dwout = _matmul(sv["merged"], dx1, "tn", f"dw_out{tag}", [BF]).reshape(N_DEV, -1, d)
    dgg, dgl, dyg, dyl = _merge_bwd(sv["proj"], sv["yg"], sv["yl"], dmerged, lay.gg_base, f"merge_bwd{tag}")
    sent = send({"w_out": dwout,
                 "w_branch_gdn": _stack_from_cols(_matmul(sv["og"], dyg, "tn", f"dw_bg{tag}", [BF])),
                 "w_branch_lru": _stack_from_cols(_matmul(sv["ol"], dyl, "tn", f"dw_bl{tag}", [BF]))})
    dog = _matmul(dyg, p["w_branch_gdn"], "nt", f"d_og{tag}", [F32])
    dol = _matmul(dyl, p["w_branch_lru"], "nt", f"d_ol{tag}", [F32])
    (dxb, dyb, dcw, dcb, dwa, dba, dwx, dbx, dlam) = _lru_bwd(
        sv["proj"], sv["hs"], dol, p["lru_conv_w"], p["lru_conv_b"], p["lru_w_a"], p["lru_b_a"], p["lru_w_x"],
        p["lru_b_x"], _after(p["lru_lambda"], sent), lay.xb_base, f"lru_bwd{tag}")
    small.update(lru_conv_w=dcw, lru_conv_b=dcb.reshape(-1), lru_w_a=dwa, lru_b_a=dba.reshape(-1), lru_w_x=dwx,
                 lru_b_x=dbx.reshape(-1), lru_lambda=dlam.reshape(-1))
    do, dz, g = _gdn_post_bwd(sv["o"], sv["proj"], p["gdn_norm"], dog, lay.z_base, f"gdn_post_bwd{tag}")
    small["gdn_norm"] = g.reshape(-1)
    dqkv_c, dgc_col, dbeta, dgrow = _gdn_chunk_bwd(sv["qkv"], sv["gc"], sv["beta"], sv["grow"], sv["states"], do,
                                                   f"gdn_chunk_bwd{tag}")
    dqkv, dgcw = _conv_silu_bwd(sv["proj"], p["gdn_conv_w"], dqkv_c, f"gdn_conv_bwd{tag}")
    small["gdn_conv_w"] = dgcw
    dgc_row = jnp.pad(dgrow.transpose(0, 2, 1).reshape(s, HEADS), ((0, 0), (0, LANE - HEADS)))
    dab, dal, ddt = _gdn_gates_bwd(sv["proj"], p["gdn_a_log"], p["gdn_dt_bias"], dgc_col, dgc_row, dbeta, lay.ab_base,
                                   f"gdn_gates_bwd{tag}")
    small["gdn_a_log"] = dal[0, :HEADS]
    small["gdn_dt_bias"] = ddt[0, :HEADS]
    dproj = jnp.concatenate([dqkv, dz, dxb, dyb, dgg, dgl, dab], axis=1)
    dwin = _matmul(sv["h"], dproj, "tn", f"dw_in{tag}", [BF], tn=1152)
    sent = send({"w_in": _stack_from_cols(lay.unpermute(dwin))})
    after_last_send(sent)
    dh = _matmul(dproj, p["w_in"], "nt", f"d_h{tag}", [F32], tk=1152)
    dx, g = _rms_bwd(sv["x"], _after(p["attn_norm"], sent), dh, dx1, f"rms1_bwd{tag}")
    small["attn_norm"] = g.reshape(-1)
    return dx, small


_BIG = ("w_in", "w_branch_gdn", "w_branch_lru", "w_out", "w_up", "w_down")
_ROW_SHARDED = ("w_out", "w_down")
_CONV = ("gdn_conv_w", "lru_conv_w")
_SMALL = ("attn_norm", "gdn_a_log", "gdn_dt_bias", "gdn_norm", "lru_conv_b", "lru_w_a", "lru_b_a", "lru_w_x",
          "lru_b_x", "lru_lambda", "mlp_norm")
_ORDER = ("attn_norm", "w_in", "gdn_conv_w", "gdn_a_log", "gdn_dt_bias", "gdn_norm", "lru_conv_w", "lru_conv_b",
          "lru_w_a", "lru_b_a", "lru_w_x", "lru_b_x", "lru_lambda", "w_branch_gdn", "w_branch_lru", "w_out",
          "mlp_norm", "w_up", "w_down", "final_norm")


def kernel(x, attn_norm, w_in, gdn_conv_w, gdn_a_log, gdn_dt_bias, gdn_norm, lru_conv_w, lru_conv_b, lru_w_a, lru_b_a, lru_w_x, lru_b_x, lru_lambda, w_branch_gdn, w_branch_lru, w_out, mlp_norm, w_up, w_down, final_norm, loss_target, m_attn_norm, m_w_in, m_gdn_conv_w, m_gdn_a_log, m_gdn_dt_bias, m_gdn_norm, m_lru_conv_w, m_lru_conv_b, m_lru_w_a, m_lru_b_a, m_lru_w_x, m_lru_b_x, m_lru_lambda, m_w_branch_gdn, m_w_branch_lru, m_w_out, m_mlp_norm, m_w_up, m_w_down, m_final_norm, v_attn_norm, v_w_in, v_gdn_conv_w, v_gdn_a_log, v_gdn_dt_bias, v_gdn_norm, v_lru_conv_w, v_lru_conv_b, v_lru_w_a, v_lru_b_a, v_lru_w_x, v_lru_b_x, v_lru_lambda, v_w_branch_gdn, v_w_branch_lru, v_w_out, v_mlp_norm, v_w_up, v_w_down, v_final_norm):
    w = dict(attn_norm=attn_norm, w_in=w_in, gdn_conv_w=gdn_conv_w, gdn_a_log=gdn_a_log, gdn_dt_bias=gdn_dt_bias, gdn_norm=gdn_norm, lru_conv_w=lru_conv_w, lru_conv_b=lru_conv_b, lru_w_a=lru_w_a, lru_b_a=lru_b_a, lru_w_x=lru_w_x, lru_b_x=lru_b_x, lru_lambda=lru_lambda, w_branch_gdn=w_branch_gdn, w_branch_lru=w_branch_lru, w_out=w_out, mlp_norm=mlp_norm, w_up=w_up, w_down=w_down, final_norm=final_norm)
    m = dict(attn_norm=m_attn_norm, w_in=m_w_in, gdn_conv_w=m_gdn_conv_w, gdn_a_log=m_gdn_a_log, gdn_dt_bias=m_gdn_dt_bias, gdn_norm=m_gdn_norm, lru_conv_w=m_lru_conv_w, lru_conv_b=m_lru_conv_b, lru_w_a=m_lru_w_a, lru_b_a=m_lru_b_a, lru_w_x=m_lru_w_x, lru_b_x=m_lru_b_x, lru_lambda=m_lru_lambda, w_branch_gdn=m_w_branch_gdn, w_branch_lru=m_w_branch_lru, w_out=m_w_out, mlp_norm=m_mlp_norm, w_up=m_w_up, w_down=m_w_down, final_norm=m_final_norm)
    v = dict(attn_norm=v_attn_norm, w_in=v_w_in, gdn_conv_w=v_gdn_conv_w, gdn_a_log=v_gdn_a_log, gdn_dt_bias=v_gdn_dt_bias, gdn_norm=v_gdn_norm, lru_conv_w=v_lru_conv_w, lru_conv_b=v_lru_conv_b, lru_w_a=v_lru_w_a, lru_b_a=v_lru_b_a, lru_w_x=v_lru_w_x, lru_b_x=v_lru_b_x, lru_lambda=v_lru_lambda, w_branch_gdn=v_w_branch_gdn, w_branch_lru=v_w_branch_lru, w_out=v_w_out, mlp_norm=v_mlp_norm, w_up=v_w_up, w_down=v_w_down, final_norm=v_final_norm)
    depth = w_in.shape[0]
    x = x[0]
    target = loss_target[0]
    s, d = x.shape
    lay = _Layout(d)
    assert lay.cols == w_in.shape[2] * N_DEV
    cx, cy, cc = _place()
    me = 4 * cx + 2 * cy + cc

    me_arr = me.astype(jnp.int32).reshape(1)

    early, late_names = ("w_in",) + _CONV, _BIG[1:]

    def whole(name, own, landed):
        g = lax.dynamic_update_slice_in_dim(landed, own[None], me, axis=0)
        full = g.reshape(-1, g.shape[-1]) if name in _ROW_SHARDED else _cols_from_stack(g)
        return lay.permute(full) if name == "w_in" else full

    def start_weights(l, after):
        shards = [w[n][l].astype(BF) if n in _BIG else w[n][l] for n in early + late_names]
        flights, token = _send_start(shards, True, after, f"weights_start_l{l}")
        return dict(zip(early + late_names, flights)), token

    def wait_weights(l, flights, names, p, after, which):
        got = _send_wait([flights[n] for n in names], after, f"weights_wait_{which}_l{l}")
        for n, (own, landed) in zip(names, got):
            p[n] = whole(n, own, landed)

    saved, params = [], []
    xl = x
    flights, token = start_weights(0, me_arr)
    after = token
    for l in range(depth):
        p = {n: w[n][l] for n in _SMALL}
        wait_weights(l, flights, early, p, after, "early")
        mine = flights
        if l + 1 < depth:
            flights, token = start_weights(l + 1, p["w_in"])
            p["attn_norm"] = _after(p["attn_norm"], token)
        late = functools.partial(wait_weights, l, mine, late_names, p, which="late")
        xl, sv = _layer_fwd(xl, p, late, lay, f"_l{l}")
        after = xl
        saved.append(sv)
        params.append(p)
    dx, loss_row, dfinal = _loss_head(xl, final_norm, target, "loss_head")
    loss = lax.psum(loss_row[0, 0], ("x", "y", "c"))

    out = {n: None for n in _BIG}
    pending = {l: {} for l in range(depth)}

    def sender(l):
        def send(stacks):
            names = tuple(stacks)
            fl, token = _send_start([stacks[n] for n in names], False, me_arr, f"grads_start_{'_'.join(names)}_l{l}")
            pending[l].update(zip(names, fl))
            return token
        return send

    def finish(l, names, after, which):
        got = _send_wait([pending[l][n] for n in names], after, f"grads_wait_{which}_l{l}")
        for n, (mine, landed) in zip(names, got):
            out[n] = _adamw_sharded(mine, landed, me_arr, w[n], m[n], v[n], l, out[n], f"adamw_{n}_l{l}")

    smalls = [None] * depth
    for l in reversed(range(depth)):
        if l + 1 < depth:
            hook = functools.partial(finish, l + 1, _BIG, which="all")
        else:
            hook = lambda after: None
        dx, smalls[l] = _layer_bwd(dx, params[l], saved[l], sender(l), hook, lay, f"_l{l}")

    small_names = _SMALL + _CONV
    part = _pack([jnp.stack([smalls[l][n] for l in range(depth)]) for n in small_names] + [dfinal.reshape(-1)])
    total = _sum_devices(_all_gather([part], "gather_small_grads")[0], "sum_small_grads")
    like = [jnp.stack([smalls[l][n] for l in range(depth)]) for n in small_names] + [final_norm]
    grads = dict(zip(small_names + ("final_norm",), _unpack(total, like)))
    for n in _CONV:
        blk = w[n].shape[2]
        grads[n] = lax.dynamic_slice_in_dim(grads[n], me * blk, blk, axis=2)
    names = small_names + ("final_norm",)
    gp = _pack([grads[n] for n in names])
    upd = _adamw_packed(_pack([w[n] for n in names]), gp, _pack([m[n] for n in names]), _pack([v[n] for n in names]),
                        "adamw_small")
    like = [w[n] for n in names]
    unp = [_unpack(u, like) for u in upd]
    for i, n in enumerate(names):
        out[n] = (grads[n], unp[0][i], unp[1][i], unp[2][i])
    finish(0, late_names, upd[0], "late")
    finish(0, ("w_in",), out[late_names[-1]][0], "w_in")

    res = [loss, dx.reshape(1, s, d)]
    for k in range(4):
        res += [out[n][k] for n in _ORDER]
    return tuple(res)
```

```python
import functools

import jax
import jax.numpy as jnp
from jax import lax
from jax.experimental import pallas as pl
from jax.experimental.pallas import tpu as pltpu

F32 = jnp.float32
BF = jnp.bfloat16
MESH = pl.DeviceIdType.MESH
N_DEV = 8
N_CHIP = 4

HEADS = 8
HEAD_DIM = 128
GDN_WIDTH = HEADS * HEAD_DIM
CHUNK = 64
CONV_WIDTH = 4
LRU_WIDTH = 1024
LRU_BLOCKS = 8
LRU_BLOCK_DIM = 128
LRU_C = 8.0
RMS_EPS = 1e-6
L2_EPS = 1e-6
ADAM_LR = 0.001
ADAM_B1 = 0.9
ADAM_B2 = 0.999
ADAM_EPS = 1e-08
ADAM_WD = 0.01
ADAM_STEP = 10

LANE = 128
VMEM_LIMIT = 56 * 1024 * 1024


def _cparams(sem=None):
    return pltpu.CompilerParams(dimension_semantics=sem, vmem_limit_bytes=VMEM_LIMIT)


def _tile(n, pref, unit=LANE):
    if n <= pref:
        return n
    best = None
    for t in range(unit, pref + 1, unit):
        if n % t == 0:
            best = t
    assert best is not None, (n, pref, unit)
    return best


_NN = (((1,), (0,)), ((), ()))
_NT = (((1,), (1,)), ((), ()))
_TN = (((0,), (0,)), ((), ()))


def _dg(a, b, dims):
    return lax.dot_general(a.astype(BF), b.astype(BF), dims, preferred_element_type=F32)


class _MM:
    def __init__(self, nn, nt, tn):
        self.nn, self.nt, self.tn = nn, nt, tn


def _make_mm(nn, nt, tn):
    def p_nn(a, b):
        return _dg(a, b, nn)

    def p_nt(a, b):
        return _dg(a, b, nt)

    def p_tn(a, b):
        return _dg(a, b, tn)

    d_nn, d_nt, d_tn = jax.custom_vjp(p_nn), jax.custom_vjp(p_nt), jax.custom_vjp(p_tn)

    def save(f):
        return lambda a, b: (f(a, b), (a, b))

    d_nn.defvjp(save(d_nn), lambda r, g: (d_nt(g, r[1]), d_tn(r[0], g)))
    d_nt.defvjp(save(d_nt), lambda r, g: (d_nn(g, r[1]), d_tn(g, r[0])))
    d_tn.defvjp(save(d_tn), lambda r, g: (d_nt(r[1], g), d_nn(r[0], g)))
    return _MM(p_nn, p_nt, p_tn), _MM(d_nn, d_nt, d_tn)


_BNN = (((2,), (1,)), ((0,), (0,)))
_BNT = (((2,), (2,)), ((0,), (0,)))
_BTN = (((1,), (1,)), ((0,), (0,)))

_PLAIN, _DIFF = _make_mm(_NN, _NT, _TN)
_PLAIN_B, _DIFF_B = _make_mm(_BNN, _BNT, _BTN)


def _sigmoid(x):
    return 1.0 / (1.0 + jnp.exp(-x))


def _silu(x):
    return x * _sigmoid(x)


def _softplus(x):
    return jnp.maximum(x, 0.0) + jnp.log(1.0 + jnp.exp(-jnp.maximum(x, -x)))


def _gelu(x):
    return 0.5 * x * (1.0 + jnp.tanh(0.7978845608028654 * (x + 0.044715 * (x * x * x))))


def _expm1(x):
    s = x * (1.0 + x * (1.0 / 2.0) * (1.0 + x * (1.0 / 3.0) * (1.0 + x * (1.0 / 4.0) * (
        1.0 + x * (1.0 / 5.0) * (1.0 + x * (1.0 / 6.0) * (1.0 + x * (1.0 / 7.0)))))))
    return jnp.where(jnp.maximum(x, -x) < 0.3, s, jnp.exp(x) - 1.0)


def _rms(x, gain):
    ms = jnp.mean(x * x, axis=-1, keepdims=True)
    return x * lax.rsqrt(ms + RMS_EPS) * gain


def _l2n(x):
    return x * lax.rsqrt(jnp.sum(x * x, axis=-1, keepdims=True) + L2_EPS)


def _shift_down(x, r):
    if r == 0:
        return x
    row = lax.broadcasted_iota(jnp.int32, x.shape, 0)
    return jnp.where(row >= r, pltpu.roll(x, r, 0), 0.0)


def _shift_up(x, r):
    if r == 0:
        return x
    n = x.shape[0]
    row = lax.broadcasted_iota(jnp.int32, x.shape, 0)
    return jnp.where(row < n - r, pltpu.roll(x, n - r, 0), 0.0)


def _conv(x, w_ref):
    y = None
    for j in range(CONV_WIDTH):
        t = _shift_down(x, CONV_WIDTH - 1 - j) * w_ref[j:j + 1, :]
        y = t if y is None else y + t
    return y


def _conv_bwd(x, dy, w_ref):
    dx = None
    dws = []
    for j in range(CONV_WIDTH):
        r = CONV_WIDTH - 1 - j
        t = _shift_up(dy, r) * w_ref[j:j + 1, :]
        dx = t if dx is None else dx + t
        dws.append(jnp.sum(dy * _shift_down(x, r), axis=0, keepdims=True))
    return dx, dws


def _scan_tile(a, b, reverse):
    n = a.shape[0]
    row = lax.broadcasted_iota(jnp.int32, a.shape, 0)
    s = 1
    while s < n:
        if reverse:
            keep = row < n - s
            a_sh = jnp.where(keep, pltpu.roll(a, n - s, 0), 1.0)
            b_sh = jnp.where(keep, pltpu.roll(b, n - s, 0), 0.0)
        else:
            keep = row >= s
            a_sh = jnp.where(keep, pltpu.roll(a, s, 0), 1.0)
            b_sh = jnp.where(keep, pltpu.roll(b, s, 0), 0.0)
        b = a * b_sh + b
        a = a * a_sh
        s *= 2
    return a, b


def _dg3(a, b, dims):
    a_hi = a.astype(BF)
    b_hi = b.astype(BF)
    a_lo = a - a_hi.astype(F32)
    b_lo = b - b_hi.astype(F32)
    return _dg(a_hi, b_hi, dims) + (_dg(a_hi, b_lo, dims) + _dg(a_lo, b_hi, dims))


@jax.custom_vjp
def _tri_inv(a):
    n = a.shape[1]
    p = _dg3(a, a, _BNN)
    r = p
    e = 2
    while 2 * e < n:
        p = _dg3(p, p, _BNN)
        r = r + p + _dg3(r, p, _BNN)
        e *= 2
    row = lax.broadcasted_iota(jnp.int32, (1, n, n), 1)
    col = lax.broadcasted_iota(jnp.int32, (1, n, n), 2)
    eye = jnp.where(row == col, 1.0, 0.0)
    return eye - a + r - _dg3(a, r, _BNN)


def _tri_inv_fwd(a):
    t = _tri_inv(a)
    return t, t


def _tri_inv_bwd(t, g):
    return (-_dg3(_dg3(t, g, _BTN), t, _BNT),)


_tri_inv.defvjp(_tri_inv_fwd, _tri_inv_bwd)


def _gdn_chunk(q, k, v, beta, gcol, grow, state, mm):
    c = q.shape[1]
    row = lax.broadcasted_iota(jnp.int32, (1, c, c), 1)
    col = lax.broadcasted_iota(jnp.int32, (1, c, c), 2)
    causal = row >= col
    strict = row > col
    qn = _l2n(q) * (HEAD_DIM ** -0.5)
    kn = _l2n(k)
    dec = jnp.where(causal, jnp.exp(jnp.where(causal, gcol - grow, 0.0)), 0.0)
    kb = kn * beta
    vb = v * beta
    a = jnp.where(strict, mm.nt(kb, kn) * dec, 0.0)
    t = _tri_inv(a)
    eg = jnp.exp(gcol)
    u = mm.nn(t, vb)
    w = mm.nn(t, kb * eg)
    p = jnp.where(causal, mm.nt(qn, kn) * dec, 0.0)
    vn = u - mm.nn(w, state)
    o = mm.nn(qn * eg, state) + mm.nn(p, vn)
    last = lax.broadcasted_iota(jnp.int32, (1, c, 1), 1) == c - 1
    gl = jnp.sum(jnp.where(last, gcol, 0.0), axis=1, keepdims=True)
    kd = kn * jnp.exp(gl - gcol)
    new_state = state * jnp.exp(gl) + mm.tn(kd, vn)
    return o, new_state


def _lru_gates(xc, wa, ba, wx, bx, lam, mm):
    r = _sigmoid(mm.nn(xc, wa) + ba)
    i = _sigmoid(mm.nn(xc, wx) + bx)
    log_a = -LRU_C * r * _softplus(-lam)
    a = jnp.exp(log_a)
    bterm = jnp.sqrt(-_expm1(2.0 * log_a)) * (i * xc)
    return a, bterm


def _matmul(a, b, mode, name, out_dtypes, epi=None, extras=(), tm=1024, tn=1024, tk=2048):
    if mode == "nn":
        (m, k), (k2, n) = a.shape, b.shape
    elif mode == "nt":
        (m, k), (n, k2) = a.shape, b.shape
    else:
        (k, m), (k2, n) = a.shape, b.shape
    assert k == k2, (a.shape, b.shape, mode)
    tm, tn, tk = _tile(m, tm), _tile(n, tn), _tile(k, tk)
    nk = k // tk
    dims = {"nn": _NN, "nt": _NT, "tn": _TN}[mode]
    if mode == "nn":
        a_spec = pl.BlockSpec((tm, tk), lambda i, j, kk: (i, kk))
        b_spec = pl.BlockSpec((tk, tn), lambda i, j, kk: (kk, j))
    elif mode == "nt":
        a_spec = pl.BlockSpec((tm, tk), lambda i, j, kk: (i, kk))
        b_spec = pl.BlockSpec((tn, tk), lambda i, j, kk: (j, kk))
    else:
        a_spec = pl.BlockSpec((tk, tm), lambda i, j, kk: (kk, i))
        b_spec = pl.BlockSpec((tk, tn), lambda i, j, kk: (kk, j))
    o_spec = pl.BlockSpec((tm, tn), lambda i, j, kk: (i, j))
    n_ex, n_out = len(extras), len(out_dtypes)
    if epi is None:
        epi = lambda acc: (acc,)

    def body(*refs):
        a_ref, b_ref = refs[0], refs[1]
        ex_refs = refs[2:2 + n_ex]
        out_refs = refs[2 + n_ex:2 + n_ex + n_out]
        kk = pl.program_id(2)
        part = _dg(a_ref[...], b_ref[...], dims)

        def finish(total):
            res = epi(total, *[r[...] for r in ex_refs])
            for r, v in zip(out_refs, res):
                r[...] = v.astype(r.dtype)

        if nk == 1:
            finish(part)
            return
        acc_ref = refs[-1]

        @pl.when(kk == 0)
        def _():
            acc_ref[...] = part

        @pl.when(jnp.logical_and(kk > 0, kk < nk - 1))
        def _():
            acc_ref[...] += part

        @pl.when(kk == nk - 1)
        def _():
            finish(acc_ref[...] + part)

    outs = pl.pallas_call(
        body, name=name, grid=(m // tm, n // tn, nk),
        in_specs=[a_spec, b_spec] + [o_spec] * n_ex,
        out_specs=[o_spec] * n_out,
        out_shape=[jax.ShapeDtypeStruct((m, n), d) for d in out_dtypes],
        scratch_shapes=[pltpu.VMEM((tm, tn), F32)] if nk > 1 else [],
        compiler_params=_cparams(("parallel", "parallel", "arbitrary")),
    )(a, b, *extras)
    return outs[0] if n_out == 1 else outs


def _rowwise(fn, name, rows, tr, row_ins, params, row_outs, acc_outs=(), ncol=1):
    tr = _tile(rows, tr, 8)
    n_in, n_par, n_ro, n_acc = len(row_ins), len(params), len(row_outs), len(acc_outs)
    in_specs = []
    for _, width, base, per_col in row_ins:
        if per_col:
            in_specs.append(pl.BlockSpec((tr, width), lambda i, j, base=base: (i, base + j)))
        else:
            in_specs.append(pl.BlockSpec((tr, width), lambda i, j, base=base: (i, base)))
    for p in params:
        in_specs.append(pl.BlockSpec(p.shape, lambda i, j: (0, 0)))
    out_specs, out_shape = [], []
    for total, dtype, width, per_col in row_outs:
        if per_col:
            out_specs.append(pl.BlockSpec((tr, width), lambda i, j: (i, j)))
        else:
            out_specs.append(pl.BlockSpec((tr, width), lambda i, j: (i, 0)))
        out_shape.append(jax.ShapeDtypeStruct((rows, total), dtype))
    for shape, dtype in acc_outs:
        out_specs.append(pl.BlockSpec(shape, lambda i, j: (0, 0)))
        out_shape.append(jax.ShapeDtypeStruct(shape, dtype))

    def body(*refs):
        ins = [r[...] for r in refs[:n_in + n_par]]
        ro = refs[n_in + n_par:n_in + n_par + n_ro]
        ao = refs[n_in + n_par + n_ro:]
        res = fn(*ins)
        for r, v in zip(ro, res[:n_ro]):
            r[...] = v.astype(r.dtype)
        first = jnp.logical_and(pl.program_id(0) == 0, pl.program_id(1) == 0)

        @pl.when(first)
        def _():
            for r in ao:
                r[...] = jnp.zeros_like(r)

        for r, v in zip(ao, res[n_ro:]):
            r[...] += v.astype(r.dtype)

    sem = ("arbitrary", "arbitrary") if n_acc else ("parallel", "parallel")
    outs = pl.pallas_call(
        body, name=name, grid=(rows // tr, ncol), in_specs=in_specs, out_specs=out_specs,
        out_shape=out_shape, compiler_params=_cparams(sem),
    )(*[r[0] for r in row_ins], *params)
    return outs


def _row(x):
    return x.reshape(1, -1)


def _rms_fwd(x, gain, name):
    s, d = x.shape
    return _rowwise(lambda xt, g: (_rms(xt, g),), name, s, 512, [(x, d, 0, False)], [_row(gain)],
                    [(d, BF, d, False)])[0]


def _rms_bwd(x, gain, dh, dres, name):
    s, d = x.shape

    def fn(xt, dht, drt, g):
        _, vjp = jax.vjp(_rms, xt, g)
        dx, dg = vjp(dht)
        return drt + dx, dg

    return _rowwise(fn, name, s, 256, [(x, d, 0, False), (dh, d, 0, False), (dres, d, 0, False)], [_row(gain)],
                    [(d, F32, d, False)], [((1, d), F32)])


def _loss_head(x, gain, target, name):
    s, d = x.shape

    def fn(xt, tt, g):
        y, vjp = jax.vjp(_rms, xt, g)
        err = y - tt
        dx, dg = vjp(err * (1.0 / d))
        part = 0.5 * jnp.sum(jnp.sum(err * err, axis=1, keepdims=True), axis=0, keepdims=True) * (1.0 / d)
        return dx, jnp.broadcast_to(part, (1, LANE)), dg

    return _rowwise(fn, name, s, 256, [(x, d, 0, False), (target, d, 0, False)], [_row(gain)],
                    [(d, F32, d, False)], [((1, LANE), F32), ((1, d), F32)])


def _gdn_post_fn(o, z, g):
    return _rms(o, g) * _silu(z)


def _gdn_post_fwd(o, proj, gain, z_base, name):
    s = o.shape[0]
    return _rowwise(lambda ot, zt, g: (_gdn_post_fn(ot, zt, g),), name, s, 1024,
                    [(o, HEAD_DIM, 0, True), (proj, HEAD_DIM, z_base, True)], [_row(gain)],
                    [(GDN_WIDTH, BF, HEAD_DIM, True)], ncol=HEADS)[0]


def _gdn_post_bwd(o, proj, gain, dog, z_base, name):
    s = o.shape[0]

    def fn(ot, zt, dt, g):
        _, vjp = jax.vjp(_gdn_post_fn, ot, zt, g)
        return vjp(dt)

    return _rowwise(fn, name, s, 1024,
                    [(o, HEAD_DIM, 0, True), (proj, HEAD_DIM, z_base, True), (dog, HEAD_DIM, 0, True)],
                    [_row(gain)], [(GDN_WIDTH, F32, HEAD_DIM, True), (GDN_WIDTH, BF, HEAD_DIM, True)],
                    [((1, HEAD_DIM), F32)], ncol=HEADS)


def _merge_fn(gg, gl, yg, yl):
    return _sigmoid(gg) * yg + _sigmoid(gl) * yl


def _merge_fwd(proj, yg, yl, gg_base, name):
    s, d = yg.shape
    return _rowwise(lambda a, b, c, e: (_merge_fn(a, b, c, e),), name, s, 256,
                    [(proj, d, gg_base, False), (proj, d, gg_base + 1, False), (yg, d, 0, False), (yl, d, 0, False)],
                    [], [(d, BF, d, False)])[0]


def _merge_bwd(proj, yg, yl, dmerged, gg_base, name):
    s, d = yg.shape

    def fn(a, b, c, e, dm):
        _, vjp = jax.vjp(_merge_fn, a, b, c, e)
        return vjp(dm)

    return _rowwise(fn, name, s, 128,
                    [(proj, d, gg_base, False), (proj, d, gg_base + 1, False), (yg, d, 0, False), (yl, d, 0, False),
                     (dmerged, d, 0, False)], [], [(d, BF, d, False)] * 4)


def _conv_silu_fwd(proj, w, name):
    s = proj.shape[0]
    cols = w.shape[1]
    cw = _tile(cols, LANE)

    def body(x_ref, w_ref, o_ref):
        o_ref[...] = _silu(_conv(x_ref[...], w_ref))

    return pl.pallas_call(
        body, name=name, grid=(cols // cw,),
        in_specs=[pl.BlockSpec((s, cw), lambda j: (0, j)), pl.BlockSpec((CONV_WIDTH, cw), lambda j: (0, j))],
        out_specs=pl.BlockSpec((s, cw), lambda j: (0, j)),
        out_shape=jax.ShapeDtypeStruct((s, cols), F32), compiler_params=_cparams(("parallel",)),
    )(proj, w)


def _conv_silu_bwd(proj, w, dout, name):
    s = proj.shape[0]
    cols = w.shape[1]
    cw = _tile(cols, LANE)

    def body(x_ref, w_ref, d_ref, dx_ref, dw_ref):
        x = x_ref[...]
        y = _conv(x, w_ref)
        sg = _sigmoid(y)
        dy = d_ref[...] * (sg * (1.0 + y * (1.0 - sg)))
        dx, dws = _conv_bwd(x, dy, w_ref)
        dx_ref[...] = dx.astype(dx_ref.dtype)
        for j in range(CONV_WIDTH):
            dw_ref[j:j + 1, :] = dws[j]

    return pl.pallas_call(
        body, name=name, grid=(cols // cw,),
        in_specs=[pl.BlockSpec((s, cw), lambda j: (0, j)), pl.BlockSpec((CONV_WIDTH, cw), lambda j: (0, j)),
                  pl.BlockSpec((s, cw), lambda j: (0, j))],
        out_specs=[pl.BlockSpec((s, cw), lambda j: (0, j)), pl.BlockSpec((CONV_WIDTH, cw), lambda j: (0, j))],
        out_shape=[jax.ShapeDtypeStruct((s, cols), BF), jax.ShapeDtypeStruct((CONV_WIDTH, cols), F32)],
        compiler_params=_cparams(("parallel",)),
    )(proj, w, dout)


def _cumsum_mask(tr, transpose):
    row = lax.broadcasted_iota(jnp.int32, (tr, tr), 0)
    col = lax.broadcasted_iota(jnp.int32, (tr, tr), 1)
    same = (row // CHUNK) == (col // CHUNK)
    tri = (row <= col) if transpose else (row >= col)
    return jnp.where(jnp.logical_and(same, tri), 1.0, 0.0)


def _lane_pad(v):
    return jnp.pad(v.reshape(1, -1), ((0, 0), (0, LANE - v.shape[0])))


def _gdn_gates_fwd(proj, a_log, dt_bias, ab_base, name):
    s = proj.shape[0]

    def fn(ab, al, dt):
        g = -jnp.exp(al) * _softplus(ab + dt)
        beta = _sigmoid(pltpu.roll(ab, LANE - HEADS, 1))
        gc = jnp.dot(_cumsum_mask(ab.shape[0], False), g, precision=lax.Precision.HIGHEST,
                     preferred_element_type=F32)
        return gc, beta

    return _rowwise(fn, name, s, 512, [(proj, LANE, ab_base, False)], [_lane_pad(a_log), _lane_pad(dt_bias)],
                    [(LANE, F32, LANE, False)] * 2)


def _gdn_gates_bwd(proj, a_log, dt_bias, dgc_col, dgc_row, dbeta, ab_base, name):
    s = proj.shape[0]

    def fn(ab, dc, dr, db, al, dt):
        lane = lax.broadcasted_iota(jnp.int32, ab.shape, 1)
        dg = jnp.dot(_cumsum_mask(ab.shape[0], True), dc + dr, precision=lax.Precision.HIGHEST,
                     preferred_element_type=F32)
        ea = jnp.exp(al)
        pre = ab + dt
        g = -ea * _softplus(pre)
        da = jnp.where(lane < HEADS, dg * (-ea) * _sigmoid(pre), 0.0)
        dal = jnp.sum(jnp.where(lane < HEADS, dg * g, 0.0), axis=0, keepdims=True)
        ddt = jnp.sum(da, axis=0, keepdims=True)
        beta = _sigmoid(pltpu.roll(ab, LANE - HEADS, 1))
        dbl = jnp.where(lane < HEADS, db * beta * (1.0 - beta), 0.0)
        dab = da + pltpu.roll(dbl, HEADS, 1)
        return dab, dal, ddt

    return _rowwise(fn, name, s, 512,
                    [(proj, LANE, ab_base, False), (dgc_col, LANE, 0, False), (dgc_row, LANE, 0, False),
                     (dbeta, LANE, 0, False)], [_lane_pad(a_log), _lane_pad(dt_bias)],
                    [(LANE, BF, LANE, False)], [((1, LANE), F32), ((1, LANE), F32)])


def _head_col(blk, h):
    lane = lax.broadcasted_iota(jnp.int32, blk.shape, 1)
    return jnp.sum(jnp.where(lane == h, blk, 0.0), axis=1, keepdims=True)


def _chunk_operands(qkv_ref, gc_ref, b_ref, gr_ref):
    heads = range(HEADS)
    q, k, v = (jnp.stack([qkv_ref[:, (j * HEADS + h) * HEAD_DIM:(j * HEADS + h + 1) * HEAD_DIM] for h in heads])
               for j in range(3))
    gcs, betas = gc_ref[...], b_ref[...]
    beta = jnp.stack([_head_col(betas, h) for h in heads])
    gcol = jnp.stack([_head_col(gcs, h) for h in heads])
    grow = jnp.stack([gr_ref[0, h:h + 1, :] for h in heads])
    return q, k, v, beta, gcol, grow


def _gdn_chunk_fwd(qkv, gc, beta, grow, name):
    s = qkv.shape[0]
    nc = s // CHUNK

    def body(qkv_ref, gc_ref, b_ref, gr_ref, o_ref, st_ref, state):
        @pl.when(pl.program_id(0) == 0)
        def _():
            state[...] = jnp.zeros_like(state)

        st = state[...]
        st_ref[0] = st
        o, new = _gdn_chunk(*_chunk_operands(qkv_ref, gc_ref, b_ref, gr_ref), st, _PLAIN_B)
        for h in range(HEADS):
            o_ref[:, h * HEAD_DIM:(h + 1) * HEAD_DIM] = o[h]
        state[...] = new

    return pl.pallas_call(
        body, name=name, grid=(nc,),
        in_specs=[pl.BlockSpec((CHUNK, 3 * GDN_WIDTH), lambda c: (c, 0)),
                  pl.BlockSpec((CHUNK, LANE), lambda c: (c, 0)), pl.BlockSpec((CHUNK, LANE), lambda c: (c, 0)),
                  pl.BlockSpec((1, HEADS, CHUNK), lambda c: (c, 0, 0))],
        out_specs=[pl.BlockSpec((CHUNK, GDN_WIDTH), lambda c: (c, 0)),
                   pl.BlockSpec((1, HEADS, HEAD_DIM, HEAD_DIM), lambda c: (c, 0, 0, 0))],
        out_shape=[jax.ShapeDtypeStruct((s, GDN_WIDTH), F32),
                   jax.ShapeDtypeStruct((nc, HEADS, HEAD_DIM, HEAD_DIM), F32)],
        scratch_shapes=[pltpu.VMEM((HEADS, HEAD_DIM, HEAD_DIM), F32)],
        compiler_params=_cparams(("arbitrary",)),
    )(qkv, gc, beta, grow)


def _gdn_chunk_bwd(qkv, gc, beta, grow, states, do, name):
    s = qkv.shape[0]
    nc = s // CHUNK

    def body(qkv_ref, gc_ref, b_ref, gr_ref, st_ref, do_ref, dqkv_ref, dgc_ref, db_ref, dgr_ref, dstate):
        @pl.when(pl.program_id(0) == 0)
        def _():
            dstate[...] = jnp.zeros_like(dstate)

        lane = lax.broadcasted_iota(jnp.int32, (CHUNK, LANE), 1)
        _, vjp = jax.vjp(functools.partial(_gdn_chunk, mm=_DIFF_B), *_chunk_operands(qkv_ref, gc_ref, b_ref, gr_ref),
                         st_ref[0])
        do = jnp.stack([do_ref[:, h * HEAD_DIM:(h + 1) * HEAD_DIM] for h in range(HEADS)])
        dq, dk, dv, dbeta, dgcol, dgrow, dst = vjp((do, dstate[...]))
        dstate[...] = dst
        dgc = jnp.zeros((CHUNK, LANE), F32)
        db = jnp.zeros((CHUNK, LANE), F32)
        for h in range(HEADS):
            for j, dx in enumerate((dq, dk, dv)):
                dqkv_ref[:, (j * HEADS + h) * HEAD_DIM:(j * HEADS + h + 1) * HEAD_DIM] = dx[h]
            dgc = dgc + jnp.where(lane == h, dgcol[h], 0.0)
            db = db + jnp.where(lane == h, dbeta[h], 0.0)
            dgr_ref[0, h:h + 1, :] = dgrow[h]
        dgc_ref[...] = dgc
        db_ref[...] = db

    rev = lambda c: nc - 1 - c
    wide = pl.BlockSpec((CHUNK, 3 * GDN_WIDTH), lambda c: (rev(c), 0))
    lanes = pl.BlockSpec((CHUNK, LANE), lambda c: (rev(c), 0))
    rows = pl.BlockSpec((1, HEADS, CHUNK), lambda c: (rev(c), 0, 0))
    return pl.pallas_call(
        body, name=name, grid=(nc,),
        in_specs=[wide, lanes, lanes, rows,
                  pl.BlockSpec((1, HEADS, HEAD_DIM, HEAD_DIM), lambda c: (rev(c), 0, 0, 0)),
                  pl.BlockSpec((CHUNK, GDN_WIDTH), lambda c: (rev(c), 0))],
        out_specs=[wide, lanes, lanes, rows],
        out_shape=[jax.ShapeDtypeStruct((s, 3 * GDN_WIDTH), F32)]
        + [jax.ShapeDtypeStruct((s, LANE), F32)] * 2 + [jax.ShapeDtypeStruct((nc, HEADS, CHUNK), F32)],
        scratch_shapes=[pltpu.VMEM((HEADS, HEAD_DIM, HEAD_DIM), F32)],
        compiler_params=_cparams(("arbitrary",)),
    )(qkv, gc, beta, grow, states, do)


def _lru_specs(s, xb_base):
    col = lambda base: pl.BlockSpec((s, LRU_BLOCK_DIM), lambda n, base=base: (0, base + n))
    vec = pl.BlockSpec((1, LRU_BLOCK_DIM), lambda n: (0, n))
    mat = pl.BlockSpec((None, LRU_BLOCK_DIM, LRU_BLOCK_DIM), lambda n: (n, 0, 0))
    cw = pl.BlockSpec((CONV_WIDTH, LRU_BLOCK_DIM), lambda n: (0, n))
    return col, vec, mat, cw


def _lru_fwd(proj, cw, cb, wa, ba, wx, bx, lam, xb_base, name):
    s = proj.shape[0]
    tt = _tile(s, 256, 8)
    col, vec, mat, cws = _lru_specs(s, xb_base)

    def body(xb_ref, yb_ref, cw_ref, cb_ref, wa_ref, ba_ref, wx_ref, bx_ref, lam_ref, o_ref, h_ref, xc_s):
        xc_s[...] = _conv(xb_ref[...], cw_ref) + cb_ref[...]
        par = (wa_ref[...], ba_ref[...], wx_ref[...], bx_ref[...], lam_ref[...])

        def step(t, carry):
            sl = pl.ds(pl.multiple_of(t * tt, tt), tt)
            a, bt = _lru_gates(xc_s[sl, :], *par, _PLAIN)
            aa, bb = _scan_tile(a, bt, False)
            h = aa * carry + bb
            h_ref[sl, :] = h
            o_ref[sl, :] = (h * _gelu(yb_ref[sl, :])).astype(o_ref.dtype)
            return h_ref[pl.ds(t * tt + tt - 1, 1), :]

        lax.fori_loop(0, s // tt, step, jnp.zeros((1, LRU_BLOCK_DIM), F32))

    nb = LRU_BLOCKS
    return pl.pallas_call(
        body, name=name, grid=(LRU_BLOCKS,),
        in_specs=[col(xb_base), col(xb_base + nb), cws, vec, mat, vec, mat, vec, vec],
        out_specs=[col(0), col(0)],
        out_shape=[jax.ShapeDtypeStruct((s, LRU_WIDTH), BF), jax.ShapeDtypeStruct((s, LRU_WIDTH), F32)],
        scratch_shapes=[pltpu.VMEM((s, LRU_BLOCK_DIM), F32)],
        compiler_params=_cparams(("parallel",)),
    )(proj, proj, cw, _row(cb), wa, _row(ba), wx, _row(bx), _row(lam))


def _lru_bwd(proj, hs, dol, cw, cb, wa, ba, wx, bx, lam, xb_base, name):
    s = proj.shape[0]
    tt = _tile(s, 256, 8)
    nt = s // tt
    col, vec, mat, cws = _lru_specs(s, xb_base)

    def body(xb_ref, yb_ref, h_ref, d_ref, cw_ref, cb_ref, wa_ref, ba_ref, wx_ref, bx_ref, lam_ref,
             dxb_ref, dyb_ref, dcw_ref, dcb_ref, dwa_ref, dba_ref, dwx_ref, dbx_ref, dlam_ref,
             xc_s, a_s, dh_s, dxc_s):
        xc_s[...] = _conv(xb_ref[...], cw_ref) + cb_ref[...]
        par = (wa_ref[...], ba_ref[...], wx_ref[...], bx_ref[...], lam_ref[...])
        row = lax.broadcasted_iota(jnp.int32, (tt, LRU_BLOCK_DIM), 0)
        tile = lambda t: pl.ds(pl.multiple_of(t * tt, tt), tt)

        def prep(t, carry):
            sl = tile(t)
            a_s[sl, :] = _lru_gates(xc_s[sl, :], *par, _PLAIN)[0]
            d = d_ref[sl, :]
            gy, vjp_y = jax.vjp(_gelu, yb_ref[sl, :])
            dyb_ref[sl, :] = vjp_y(d * h_ref[sl, :])[0].astype(dyb_ref.dtype)
            dh_s[sl, :] = d * gy
            return carry

        lax.fori_loop(0, nt, prep, 0)

        def rscan(i, carry):
            dh_next, a_next = carry
            t = nt - 1 - i
            sl = tile(t)
            a_sh = jnp.where(row == tt - 1, a_next, pltpu.roll(a_s[sl, :], tt - 1, 0))
            aa, bb = _scan_tile(a_sh, dh_s[sl, :], True)
            dh_s[sl, :] = aa * dh_next + bb
            first = pl.ds(t * tt, 1)
            return dh_s[first, :], a_s[first, :]

        zero = jnp.zeros((1, LRU_BLOCK_DIM), F32)
        lax.fori_loop(0, nt, rscan, (zero, zero))

        def gates_vjp(t, acc):
            sl = tile(t)
            _, vjp_g = jax.vjp(functools.partial(_lru_gates, mm=_DIFF), xc_s[sl, :], *par)
            dh = dh_s[sl, :]
            before = jnp.where(t > 0, h_ref[pl.ds(jnp.maximum(t * tt - 1, 0), 1), :], 0.0)
            h_prev = jnp.where(row == 0, before, pltpu.roll(h_ref[sl, :], 1, 0))
            dxc, *dpar = vjp_g((dh * h_prev, dh))
            dxc_s[sl, :] = dxc
            return tuple(x + y for x, y in zip(acc, dpar))

        dwa, dba, dwx, dbx, dlam = lax.fori_loop(0, nt, gates_vjp, tuple(jnp.zeros_like(p) for p in par))
        dwa_ref[...] = dwa
        dba_ref[...] = dba
        dwx_ref[...] = dwx
        dbx_ref[...] = dbx
        dlam_ref[...] = dlam
        dxc = dxc_s[...]
        dcb_ref[...] = jnp.sum(dxc, axis=0, keepdims=True)
        dxb, dws = _conv_bwd(xb_ref[...], dxc, cw_ref)
        dxb_ref[...] = dxb.astype(dxb_ref.dtype)
        for j in range(CONV_WIDTH):
            dcw_ref[j:j + 1, :] = dws[j]

    nb = LRU_BLOCKS
    w = LRU_WIDTH
    return pl.pallas_call(
        body, name=name, grid=(LRU_BLOCKS,),
        in_specs=[col(xb_base), col(xb_base + nb), col(0), col(0), cws, vec, mat, vec, mat, vec, vec],
        out_specs=[col(0), col(0), cws, vec, mat, vec, mat, vec, vec],
        out_shape=[jax.ShapeDtypeStruct((s, w), BF), jax.ShapeDtypeStruct((s, w), BF),
                   jax.ShapeDtypeStruct((CONV_WIDTH, w), F32), jax.ShapeDtypeStruct((1, w), F32),
                   jax.ShapeDtypeStruct(wa.shape, F32), jax.ShapeDtypeStruct((1, w), F32),
                   jax.ShapeDtypeStruct(wx.shape, F32), jax.ShapeDtypeStruct((1, w), F32),
                   jax.ShapeDtypeStruct((1, w), F32)],
        scratch_shapes=[pltpu.VMEM((s, LRU_BLOCK_DIM), F32)] * 4,
        compiler_params=_cparams(("parallel",)),
    )(proj, proj, hs, dol, cw, _row(cb), wa, _row(ba), wx, _row(bx), _row(lam))


def _place():
    return lax.axis_index("x"), lax.axis_index("y"), lax.axis_index("c")


def _all_gather(blocks, name):
    n = len(blocks)
    per = N_DEV - 1

    def body(*refs):
        ins, outs = refs[:n], refs[n:2 * n]
        send_sems, recv_sems, local_sems = refs[2 * n:]
        x, y, c = _place()
        me, sibling = (x, y, c), (x, y, 1 - c)
        chips = [(1 - x, y), (x, 1 - y), (1 - x, 1 - y)]

        def copy(a, k, block, to, src=None):
            dst = outs[a].at[4 * block[0] + 2 * block[1] + block[2]]
            return pltpu.make_async_remote_copy(
                src_ref=dst if src is None else src, dst_ref=dst, send_sem=send_sems.at[a * per + k],
                recv_sem=recv_sems.at[a * per + k], device_id=to, device_id_type=MESH)

        started = []
        for a in range(n):
            mine = pltpu.make_async_copy(ins[a], outs[a].at[4 * x + 2 * y + c], local_sems.at[a])
            mine.start()
            started.append(mine)
        sends = []
        for a in range(n):
            first = [copy(a, 0, me, sibling, src=ins[a])]
            first += [copy(a, 1 + j, me, (*chip, c), src=ins[a]) for j, chip in enumerate(chips)]
            for cp in first:
                cp.start()
            sends += first
        for a in range(n):
            for j, chip in enumerate(chips):
                copy(a, 1 + j, (*chip, c), me).wait_recv()
                passed = copy(a, 4 + j, (*chip, c), sibling)
                passed.start()
                sends.append(passed)
        for a in range(n):
            copy(a, 0, sibling, me).wait_recv()
            for j, chip in enumerate(chips):
                copy(a, 4 + j, (*chip, 1 - c), me).wait_recv()
        for cp in sends:
            cp.wait_send()
        for cp in started:
            cp.wait()

    any_spec = pl.BlockSpec(memory_space=pl.ANY)
    return pl.pallas_call(
        body, name=name, in_specs=[any_spec] * n, out_specs=[any_spec] * n,
        out_shape=[jax.ShapeDtypeStruct((N_DEV,) + b.shape, b.dtype) for b in blocks],
        scratch_shapes=[pltpu.SemaphoreType.DMA((n * per,)), pltpu.SemaphoreType.DMA((n * per,)),
                        pltpu.SemaphoreType.DMA((n,))],
    )(*blocks)


_HBM = pl.BlockSpec(memory_space=pltpu.HBM)
_SEM = pl.BlockSpec(memory_space=pltpu.SEMAPHORE)
_ANY = pl.BlockSpec(memory_space=pl.ANY)
_EFFECT = pltpu.SideEffectType.DATAFLOW_SIDE_EFFECTING


def _peers(x, y, c):
    return [(x, y, 1 - c), (1 - x, y, c), (1 - x, y, 1 - c), (x, 1 - y, c), (x, 1 - y, 1 - c),
            (1 - x, 1 - y, c), (1 - x, 1 - y, 1 - c)]


class _InFlight:
    def __init__(self, send_sem, recv_sem, src, land):
        self.send_sem, self.recv_sem, self.src, self.land = send_sem, recv_sem, src, land


def _send_start(srcs, whole, after, name):
    n = len(srcs)
    lands = [lax.empty(((N_DEV,) + s.shape) if whole else s.shape, s.dtype) for s in srcs]

    def body(*refs):
        src_refs, land_refs = refs[:n], refs[n:2 * n]
        outs = refs[2 * n + 1:]
        send_sems, recv_sems, token = outs[:n], outs[n:2 * n], outs[-1]
        x, y, c = _place()
        me = 4 * x + 2 * y + c
        for a in range(n):
            for p in _peers(x, y, c):
                src = src_refs[a] if whole else src_refs[a].at[4 * p[0] + 2 * p[1] + p[2]]
                pltpu.make_async_remote_copy(src_ref=src, dst_ref=land_refs[a].at[me], send_sem=send_sems[a],
                                             recv_sem=recv_sems[a], device_id=p, device_id_type=MESH).start()
        token[...] = jnp.zeros_like(token)

    hbm = lambda a: pltpu.HBM(a.shape, a.dtype)
    outs = pl.pallas_call(
        body, name=name,
        out_shape=[pltpu.SemaphoreType.DMA(())] * (2 * n) + [hbm(s) for s in srcs] + [hbm(l) for l in lands]
        + [jax.ShapeDtypeStruct((8, LANE), F32)],
        in_specs=[_HBM] * (2 * n) + [_ANY],
        out_specs=[_SEM] * (2 * n) + [_HBM] * (2 * n) + [pl.BlockSpec(memory_space=pltpu.VMEM)],
        input_output_aliases={i: 2 * n + i for i in range(2 * n)},
        compiler_params=pltpu.CompilerParams(has_side_effects=_EFFECT),
    )(*[pltpu.with_memory_space_constraint(s, pltpu.HBM) for s in srcs],
      *[pltpu.with_memory_space_constraint(l, pltpu.HBM) for l in lands], after)
    flights = [_InFlight(outs[a], outs[n + a], outs[2 * n + a], outs[3 * n + a]) for a in range(n)]
    return flights, outs[-1]


def _send_wait(flights, after, name):
    n = len(flights)

    def body(*refs):
        x, y, c = _place()
        for a in range(n):
            land_ref, send_sem, recv_sem = refs[4 * a + 1], refs[4 * a + 2], refs[4 * a + 3]
            seven = land_ref.at[pl.ds(0, N_DEV - 1)]
            copy = pltpu.make_async_remote_copy(src_ref=seven, dst_ref=seven, send_sem=send_sem, recv_sem=recv_sem,
                                                device_id=(x, y, 1 - c), device_id_type=MESH)
            copy.wait_send()
            copy.wait_recv()

    ins, out_shape = [], []
    for f in flights:
        ins += [f.src, f.land, f.send_sem, f.recv_sem]
        out_shape += [pltpu.HBM(f.src.shape, f.src.dtype), pltpu.HBM(f.land.shape, f.land.dtype)]
    outs = pl.pallas_call(
        body, name=name, out_shape=out_shape,
        in_specs=[_HBM, _HBM, _SEM, _SEM] * n + [_ANY], out_specs=[_HBM] * (2 * n),
        input_output_aliases={4 * a + i: 2 * a + i for a in range(n) for i in range(2)},
        compiler_params=pltpu.CompilerParams(has_side_effects=_EFFECT),
    )(*ins, after)
    return [(outs[2 * a], outs[2 * a + 1]) for a in range(n)]


def _adamw_math(w, g, m, v):
    m = ADAM_B1 * m + (1.0 - ADAM_B1) * g
    v = ADAM_B2 * v + (1.0 - ADAM_B2) * (g * g)
    m_hat = m / (1.0 - ADAM_B1 ** ADAM_STEP)
    v_hat = v / (1.0 - ADAM_B2 ** ADAM_STEP)
    delta = -ADAM_LR * (m_hat / (jnp.sqrt(v_hat) + ADAM_EPS) + ADAM_WD * w)
    return delta, m, v


def _adamw_sharded(mine, landed, me, w, m, v, layer, prev, name):
    depth, r, c = w.shape
    tr = _tile(r, 128, 8)
    n_prev = 0 if prev is None else 4

    def body(*refs):
        me_ref, own_ref, land_ref, w_ref, m_ref, v_ref = refs[:6]
        g_ref, d_ref, nm_ref, nv_ref = refs[6 + n_prev:]
        g = None
        for k in range(N_DEV):
            part = jnp.where(me_ref[0] == k, own_ref[0], land_ref[k]).astype(F32)
            g = part if g is None else g + part
        delta, nm, nv = _adamw_math(w_ref[...], g, m_ref[...], v_ref[...])
        g_ref[...] = g
        d_ref[...] = delta
        nm_ref[...] = nm
        nv_ref[...] = nv

    lay = pl.BlockSpec((None, tr, c), lambda i, me_ref: (layer, i, 0))
    return pl.pallas_call(
        body, name=name,
        grid_spec=pltpu.PrefetchScalarGridSpec(
            num_scalar_prefetch=1, grid=(r // tr,),
            in_specs=[pl.BlockSpec((1, tr, c), lambda i, me_ref: (me_ref[0], i, 0)),
                      pl.BlockSpec((N_DEV, tr, c), lambda i, me_ref: (0, i, 0)), lay, lay, lay] + [_ANY] * n_prev,
            out_specs=[lay] * 4),
        out_shape=[jax.ShapeDtypeStruct(w.shape, F32)] * 4,
        input_output_aliases={6 + i: i for i in range(n_prev)},
        compiler_params=_cparams(("parallel",)),
    )(me, mine, landed, w, m, v, *(prev or ()))


def _sum_devices(stack, name):
    _, r, c = stack.shape
    tr = _tile(r, 512, 8)

    def body(s_ref, o_ref):
        g = s_ref[0]
        for j in range(1, N_DEV):
            g = g + s_ref[j]
        o_ref[...] = g

    return pl.pallas_call(
        body, name=name, grid=(r // tr,), in_specs=[pl.BlockSpec((N_DEV, tr, c), lambda i: (0, i, 0))],
        out_specs=pl.BlockSpec((tr, c), lambda i: (i, 0)), out_shape=jax.ShapeDtypeStruct((r, c), F32),
        compiler_params=_cparams(("parallel",)),
    )(stack)


def _adamw_packed(w, g, m, v, name):
    r, c = w.shape
    tr = _tile(r, 512, 8)

    def body(w_ref, g_ref, m_ref, v_ref, d_ref, nm_ref, nv_ref):
        delta, nm, nv = _adamw_math(w_ref[...], g_ref[...], m_ref[...], v_ref[...])
        d_ref[...] = delta
        nm_ref[...] = nm
        nv_ref[...] = nv

    spec = pl.BlockSpec((tr, c), lambda i: (i, 0))
    return pl.pallas_call(
        body, name=name, grid=(r // tr,), in_specs=[spec] * 4, out_specs=[spec] * 3,
        out_shape=[jax.ShapeDtypeStruct((r, c), F32)] * 3, compiler_params=_cparams(("parallel",)),
    )(w, g, m, v)


def _pack(arrays):
    unit = 8 * LANE
    flat = []
    for a in arrays:
        f = a.reshape(-1)
        flat.append(jnp.pad(f, (0, (-f.shape[0]) % unit)))
    return jnp.concatenate(flat).reshape(-1, LANE)


def _unpack(packed, like):
    unit = 8 * LANE
    flat = packed.reshape(-1)
    out, off = [], 0
    for a in like:
        n = 1
        for d in a.shape:
            n *= d
        out.append(flat[off:off + n].reshape(a.shape))
        off += n + (-n) % unit
    return out


class _Layout:
    def __init__(self, d):
        self.d = d
        self.n_front = 4 * GDN_WIDTH
        self.n_ab = 2 * HEADS
        self.n_back = 2 * LRU_WIDTH + 2 * d
        self.cols = self.n_front + self.n_ab + self.n_back
        self.pcols = self.n_front + self.n_back + LANE
        self.z_base = 3 * GDN_WIDTH // HEAD_DIM
        self.xb_base = self.n_front // LRU_BLOCK_DIM
        self.gg_off = self.n_front + 2 * LRU_WIDTH
        assert self.gg_off % d == 0
        self.gg_base = self.gg_off // d
        self.ab_base = (self.n_front + self.n_back) // LANE

    def permute(self, w):
        f, ab = self.n_front, self.n_ab
        pad = jnp.zeros((w.shape[0], LANE - ab), w.dtype)
        return jnp.concatenate([w[:, :f], w[:, f + ab:], w[:, f:f + ab], pad], axis=1)

    def unpermute(self, w):
        f, ab, b = self.n_front, self.n_ab, self.n_back
        return jnp.concatenate([w[:, :f], w[:, f + b:f + b + ab], w[:, f:f + b]], axis=1)


def _cols_from_stack(stack):
    nd, k, n = stack.shape
    return stack.transpose(1, 0, 2).reshape(k, nd * n)


def _stack_from_cols(w):
    k, n = w.shape
    return w.reshape(k, N_DEV, n // N_DEV).transpose(1, 0, 2)


def _after(v, *tokens):
    for t in tokens:
        v = v + t[0, 0]
    return v


def _layer_fwd(x, p, late, lay, tag):
    s, d = x.shape
    nc = s // CHUNK
    h = _rms_fwd(x, p["attn_norm"], f"rms1_fwd{tag}")
    proj = _matmul(h, p["w_in"], "nn", f"in_proj{tag}", [F32], tn=1152)
    qkv = _conv_silu_fwd(proj, p["gdn_conv_w"], f"gdn_conv_fwd{tag}")
    gc, beta = _gdn_gates_fwd(proj, p["gdn_a_log"], p["gdn_dt_bias"], lay.ab_base, f"gdn_gates_fwd{tag}")
    grow = gc[:, :HEADS].reshape(nc, CHUNK, HEADS).transpose(0, 2, 1)
    o, states = _gdn_chunk_fwd(qkv, gc, beta, grow, f"gdn_chunk_fwd{tag}")
    og = _gdn_post_fwd(o, proj, p["gdn_norm"], lay.z_base, f"gdn_post_fwd{tag}")
    ol, hs = _lru_fwd(proj, p["lru_conv_w"], p["lru_conv_b"], p["lru_w_a"], p["lru_b_a"], p["lru_w_x"], p["lru_b_x"],
                      p["lru_lambda"], lay.xb_base, f"lru_fwd{tag}")
    late[0](hs)
    yg = _matmul(og, p["w_branch_gdn"], "nn", f"branch_gdn{tag}", [F32])
    yl = _matmul(ol, p["w_branch_lru"], "nn", f"branch_lru{tag}", [F32])
    merged = _merge_fwd(proj, yg, yl, lay.gg_base, f"merge_fwd{tag}")
    x1 = _matmul(merged, p["w_out"], "nn", f"out_proj{tag}", [F32], epi=lambda acc, r: (acc + r,), extras=[x])
    h2 = _rms_fwd(x1, p["mlp_norm"], f"rms2_fwd{tag}")
    late[1](h2)
    u, act = _matmul(h2, p["w_up"], "nn", f"mlp_up{tag}", [F32, BF],
                     epi=lambda acc: (acc, jnp.square(jnp.maximum(acc, 0.0))))
    x2 = _matmul(act, p["w_down"], "nn", f"mlp_down{tag}", [F32], epi=lambda acc, r: (acc + r,), extras=[x1])
    saved = dict(x=x, h=h, proj=proj, qkv=qkv, gc=gc, beta=beta, grow=grow, o=o, states=states, og=og, ol=ol, hs=hs,
                 yg=yg, yl=yl, merged=merged, x1=x1, h2=h2, u=u, act=act)
    return x2, saved


def _layer_bwd(dx2, p, sv, send, after_last_send, lay, tag):
    s, d = dx2.shape
    nc = s // CHUNK
    small = {}
    du = _matmul(dx2, p["w_down"], "nt", f"d_act{tag}", [BF],
                 epi=lambda acc, u: (acc * (2.0 * jnp.maximum(u, 0.0)),), extras=[sv["u"]])
    sent = send({"w_down": _matmul(sv["act"], dx2, "tn", f"dw_down{tag}", [BF]).reshape(N_DEV, -1, d),
                 "w_up": _stack_from_cols(_matmul(sv["h2"], du, "tn", f"dw_up{tag}", [BF]))})
    dh2 = _matmul(du, p["w_up"], "nt", f"d_h2{tag}", [F32])
    dx1, g = _rms_bwd(sv["x1"], _after(p["mlp_norm"], sent), dh2, dx2, f"rms2_bwd{tag}")
    small["mlp_norm"] = g.reshape(-1)
    dmerged = _matmul(dx1, p["w_out"], "nt", f"d_merged{tag}", [F32])
    dwout = _matmul(sv["merged"], dx1, "tn", f"dw_out{tag}", [BF]).reshape(N_DEV, -1, d)
    dgg, dgl, dyg, dyl = _merge_bwd(sv["proj"], sv["yg"], sv["yl"], dmerged, lay.gg_base, f"merge_bwd{tag}")
    sent = send({"w_out": dwout,
                 "w_branch_gdn": _stack_from_cols(_matmul(sv["og"], dyg, "tn", f"dw_bg{tag}", [BF])),
                 "w_branch_lru": _stack_from_cols(_matmul(sv["ol"], dyl, "tn", f"dw_bl{tag}", [BF]))})
    dog = _matmul(dyg, p["w_branch_gdn"], "nt", f"d_og{tag}", [F32])
    dol = _matmul(dyl, p["w_branch_lru"], "nt", f"d_ol{tag}", [F32])
    (dxb, dyb, dcw, dcb, dwa, dba, dwx, dbx, dlam) = _lru_bwd(
        sv["proj"], sv["hs"], dol, p["lru_conv_w"], p["lru_conv_b"], p["lru_w_a"], p["lru_b_a"], p["lru_w_x"],
        p["lru_b_x"], _after(p["lru_lambda"], sent), lay.xb_base, f"lru_bwd{tag}")
    small.update(lru_conv_w=dcw, lru_conv_b=dcb.reshape(-1), lru_w_a=dwa, lru_b_a=dba.reshape(-1), lru_w_x=dwx,
                 lru_b_x=dbx.reshape(-1), lru_lambda=dlam.reshape(-1))
    do, dz, g = _gdn_post_bwd(sv["o"], sv["proj"], p["gdn_norm"], dog, lay.z_base, f"gdn_post_bwd{tag}")
    small["gdn_norm"] = g.reshape(-1)
    dqkv_c, dgc_col, dbeta, dgrow = _gdn_chunk_bwd(sv["qkv"], sv["gc"], sv["beta"], sv["grow"], sv["states"], do,
                                                   f"gdn_chunk_bwd{tag}")
    dqkv, dgcw = _conv_silu_bwd(sv["proj"], p["gdn_conv_w"], dqkv_c, f"gdn_conv_bwd{tag}")
    small["gdn_conv_w"] = dgcw
    dgc_row = jnp.pad(dgrow.transpose(0, 2, 1).reshape(s, HEADS), ((0, 0), (0, LANE - HEADS)))
    dab, dal, ddt = _gdn_gates_bwd(sv["proj"], p["gdn_a_log"], p["gdn_dt_bias"], dgc_col, dgc_row, dbeta, lay.ab_base,
                                   f"gdn_gates_bwd{tag}")
    small["gdn_a_log"] = dal[0, :HEADS]
    small["gdn_dt_bias"] = ddt[0, :HEADS]
    dproj = jnp.concatenate([dqkv, dz, dxb, dyb, dgg, dgl, dab], axis=1)
    dwin = _matmul(sv["h"], dproj, "tn", f"dw_in{tag}", [BF], tn=1152)
    sent = send({"w_in": _stack_from_cols(lay.unpermute(dwin))})
    after_last_send(sent)
    dh = _matmul(dproj, p["w_in"], "nt", f"d_h{tag}", [F32], tk=3456)
    dx, g = _rms_bwd(sv["x"], _after(p["attn_norm"], sent), dh, dx1, f"rms1_bwd{tag}")
    small["attn_norm"] = g.reshape(-1)
    return dx, small


_BIG = ("w_in", "w_branch_gdn", "w_branch_lru", "w_out", "w_up", "w_down")
_ROW_SHARDED = ("w_out", "w_down")
_CONV = ("gdn_conv_w", "lru_conv_w")
_SMALL = ("attn_norm", "gdn_a_log", "gdn_dt_bias", "gdn_norm", "lru_conv_b", "lru_w_a", "lru_b_a", "lru_w_x",
          "lru_b_x", "lru_lambda", "mlp_norm")
_ORDER = ("attn_norm", "w_in", "gdn_conv_w", "gdn_a_log", "gdn_dt_bias", "gdn_norm", "lru_conv_w", "lru_conv_b",
          "lru_w_a", "lru_b_a", "lru_w_x", "lru_b_x", "lru_lambda", "w_branch_gdn", "w_branch_lru", "w_out",
          "mlp_norm", "w_up", "w_down", "final_norm")


def kernel(x, attn_norm, w_in, gdn_conv_w, gdn_a_log, gdn_dt_bias, gdn_norm, lru_conv_w, lru_conv_b, lru_w_a, lru_b_a, lru_w_x, lru_b_x, lru_lambda, w_branch_gdn, w_branch_lru, w_out, mlp_norm, w_up, w_down, final_norm, loss_target, m_attn_norm, m_w_in, m_gdn_conv_w, m_gdn_a_log, m_gdn_dt_bias, m_gdn_norm, m_lru_conv_w, m_lru_conv_b, m_lru_w_a, m_lru_b_a, m_lru_w_x, m_lru_b_x, m_lru_lambda, m_w_branch_gdn, m_w_branch_lru, m_w_out, m_mlp_norm, m_w_up, m_w_down, m_final_norm, v_attn_norm, v_w_in, v_gdn_conv_w, v_gdn_a_log, v_gdn_dt_bias, v_gdn_norm, v_lru_conv_w, v_lru_conv_b, v_lru_w_a, v_lru_b_a, v_lru_w_x, v_lru_b_x, v_lru_lambda, v_w_branch_gdn, v_w_branch_lru, v_w_out, v_mlp_norm, v_w_up, v_w_down, v_final_norm):
    w = dict(attn_norm=attn_norm, w_in=w_in, gdn_conv_w=gdn_conv_w, gdn_a_log=gdn_a_log, gdn_dt_bias=gdn_dt_bias, gdn_norm=gdn_norm, lru_conv_w=lru_conv_w, lru_conv_b=lru_conv_b, lru_w_a=lru_w_a, lru_b_a=lru_b_a, lru_w_x=lru_w_x, lru_b_x=lru_b_x, lru_lambda=lru_lambda, w_branch_gdn=w_branch_gdn, w_branch_lru=w_branch_lru, w_out=w_out, mlp_norm=mlp_norm, w_up=w_up, w_down=w_down, final_norm=final_norm)
    m = dict(attn_norm=m_attn_norm, w_in=m_w_in, gdn_conv_w=m_gdn_conv_w, gdn_a_log=m_gdn_a_log, gdn_dt_bias=m_gdn_dt_bias, gdn_norm=m_gdn_norm, lru_conv_w=m_lru_conv_w, lru_conv_b=m_lru_conv_b, lru_w_a=m_lru_w_a, lru_b_a=m_lru_b_a, lru_w_x=m_lru_w_x, lru_b_x=m_lru_b_x, lru_lambda=m_lru_lambda, w_branch_gdn=m_w_branch_gdn, w_branch_lru=m_w_branch_lru, w_out=m_w_out, mlp_norm=m_mlp_norm, w_up=m_w_up, w_down=m_w_down, final_norm=m_final_norm)
    v = dict(attn_norm=v_attn_norm, w_in=v_w_in, gdn_conv_w=v_gdn_conv_w, gdn_a_log=v_gdn_a_log, gdn_dt_bias=v_gdn_dt_bias, gdn_norm=v_gdn_norm, lru_conv_w=v_lru_conv_w, lru_conv_b=v_lru_conv_b, lru_w_a=v_lru_w_a, lru_b_a=v_lru_b_a, lru_w_x=v_lru_w_x, lru_b_x=v_lru_b_x, lru_lambda=v_lru_lambda, w_branch_gdn=v_w_branch_gdn, w_branch_lru=v_w_branch_lru, w_out=v_w_out, mlp_norm=v_mlp_norm, w_up=v_w_up, w_down=v_w_down, final_norm=v_final_norm)
    depth = w_in.shape[0]
    x = x[0]
    target = loss_target[0]
    s, d = x.shape
    lay = _Layout(d)
    assert lay.cols == w_in.shape[2] * N_DEV
    cx, cy, cc = _place()
    me = 4 * cx + 2 * cy + cc

    me_arr = me.astype(jnp.int32).reshape(1)

    early, mixer, mlp = ("w_in",) + _CONV, ("w_branch_gdn", "w_branch_lru", "w_out"), ("w_up", "w_down")

    def shard(n, l):
        return w[n][l].astype(BF) if n in _BIG else w[n][l]

    def joined(name, stack):
        full = stack.reshape(-1, stack.shape[-1]) if name in _ROW_SHARDED else _cols_from_stack(stack)
        return lay.permute(full) if name == "w_in" else full

    def start_weights(l, names, after):
        flights, token = _send_start([shard(n, l) for n in names], True, after, f"weights_start_l{l}")
        return dict(zip(names, flights)), token

    def wait_weights(l, flights, names, p, after, which):
        got = _send_wait([flights[n] for n in names], after, f"weights_wait_{which}_l{l}")
        for n, (own, landed) in zip(names, got):
            p[n] = joined(n, lax.dynamic_update_slice_in_dim(landed, own[None], me, axis=0))

    saved, params = [], []
    xl = x
    first = dict(zip(early, _all_gather([shard(n, 0) for n in early], "gather_first_weights")))
    flights, token = start_weights(0, mixer + mlp, first["w_in"])
    for l in range(depth):
        p = {n: w[n][l] for n in _SMALL}
        if l == 0:
            p.update({n: joined(n, g) for n, g in first.items()})
        else:
            wait_weights(l, flights, early, p, xl, "early")
        mine = flights
        if l + 1 < depth:
            flights, token = start_weights(l + 1, early + mixer + mlp, token)
        p["attn_norm"] = _after(p["attn_norm"], token)
        late = [functools.partial(wait_weights, l, mine, names, p, which=which)
                for names, which in ((mixer, "mixer"), (mlp, "mlp"))]
        xl, sv = _layer_fwd(xl, p, late, lay, f"_l{l}")
        saved.append(sv)
        params.append(p)
    dx, loss_row, dfinal = _loss_head(xl, final_norm, target, "loss_head")
    loss = lax.psum(loss_row[0, 0], ("x", "y", "c"))

    out = {n: None for n in _BIG}
    pending = {l: {} for l in range(depth)}

    def sender(l):
        def send(stacks):
            names = tuple(stacks)
            fl, token = _send_start([stacks[n] for n in names], False, me_arr, f"grads_start_{'_'.join(names)}_l{l}")
            pending[l].update(zip(names, fl))
            return token
        return send

    def finish(l, names, after, which):
        got = _send_wait([pending[l][n] for n in names], after, f"grads_wait_{which}_l{l}")
        for n, (mine, landed) in zip(names, got):
            out[n] = _adamw_sharded(mine, landed, me_arr, w[n], m[n], v[n], l, out[n], f"adamw_{n}_l{l}")

    smalls = [None] * depth
    for l in reversed(range(depth)):
        if l + 1 < depth:
            hook = functools.partial(finish, l + 1, _BIG, which="all")
        else:
            hook = lambda after: None
        dx, smalls[l] = _layer_bwd(dx, params[l], saved[l], sender(l), hook, lay, f"_l{l}")

    small_names = _SMALL + _CONV
    part = _pack([jnp.stack([smalls[l][n] for l in range(depth)]) for n in small_names] + [dfinal.reshape(-1)])
    small_flight, token = _send_start([part], True, me_arr, "small_grads_start")
    finish(0, mixer + mlp, token, "late")
    (part, landed), = _send_wait(small_flight, out[mlp[-1]][0], "small_grads_wait")
    total = _sum_devices(lax.dynamic_update_slice_in_dim(landed, part[None], me, axis=0), "sum_small_grads")
    like =[jnp.stack([smalls[l][n] for l in range(depth)]) for n in small_names] + [final_norm]
    grads = dict(zip(small_names + ("final_norm",), _unpack(total, like)))
    for n in _CONV:
        blk = w[n].shape[2]
        grads[n] = lax.dynamic_slice_in_dim(grads[n], me * blk, blk, axis=2)
    names = small_names + ("final_norm",)
    gp = _pack([grads[n] for n in names])
    upd = _adamw_packed(_pack([w[n] for n in names]), gp, _pack([m[n] for n in names]), _pack([v[n] for n in names]),
                        "adamw_small")
    like = [w[n] for n in names]
    unp = [_unpack(u, like) for u in upd]
    for i, n in enumerate(names):
        out[n] = (grads[n], unp[0][i], unp[1][i], unp[2][i])
    finish(0, ("w_in",), upd[0], "w_in")

    res = [loss, dx.reshape(1, s, d)]
    for k in range(4):
        res += [out[n][k] for n in _ORDER]
    return tuple(res)
```

```python
import functools

import jax
import jax.numpy as jnp
from jax import lax
from jax.experimental import pallas as pl
from jax.experimental.pallas import tpu as pltpu

F32 = jnp.float32
BF = jnp.bfloat16
MESH = pl.DeviceIdType.MESH
N_DEV = 8
N_CHIP = 4

HEADS = 8
HEAD_DIM = 128
GDN_WIDTH = HEADS * HEAD_DIM
CHUNK = 64
CONV_WIDTH = 4
LRU_WIDTH = 1024
LRU_BLOCKS = 8
LRU_BLOCK_DIM = 128
LRU_C = 8.0
RMS_EPS = 1e-6
L2_EPS = 1e-6
ADAM_LR = 0.001
ADAM_B1 = 0.9
ADAM_B2 = 0.999
ADAM_EPS = 1e-08
ADAM_WD = 0.01
ADAM_STEP = 10

LANE = 128
VMEM_LIMIT = 56 * 1024 * 1024


def _cparams(sem=None):
    return pltpu.CompilerParams(dimension_semantics=sem, vmem_limit_bytes=VMEM_LIMIT)


def _tile(n, pref, unit=LANE):
    if n <= pref:
        return n
    best = None
    for t in range(unit, pref + 1, unit):
        if n % t == 0:
            best = t
    assert best is not None, (n, pref, unit)
    return best


_NN = (((1,), (0,)), ((), ()))
_NT = (((1,), (1,)), ((), ()))
_TN = (((0,), (0,)), ((), ()))


def _dg(a, b, dims):
    return lax.dot_general(a.astype(BF), b.astype(BF), dims, preferred_element_type=F32)


class _MM:
    def __init__(self, nn, nt, tn):
        self.nn, self.nt, self.tn = nn, nt, tn


def _make_mm(nn, nt, tn):
    def p_nn(a, b):
        return _dg(a, b, nn)

    def p_nt(a, b):
        return _dg(a, b, nt)

    def p_tn(a, b):
        return _dg(a, b, tn)

    d_nn, d_nt, d_tn = jax.custom_vjp(p_nn), jax.custom_vjp(p_nt), jax.custom_vjp(p_tn)

    def save(f):
        return lambda a, b: (f(a, b), (a, b))

    d_nn.defvjp(save(d_nn), lambda r, g: (d_nt(g, r[1]), d_tn(r[0], g)))
    d_nt.defvjp(save(d_nt), lambda r, g: (d_nn(g, r[1]), d_tn(g, r[0])))
    d_tn.defvjp(save(d_tn), lambda r, g: (d_nt(r[1], g), d_nn(r[0], g)))
    return _MM(p_nn, p_nt, p_tn), _MM(d_nn, d_nt, d_tn)


_BNN = (((2,), (1,)), ((0,), (0,)))
_BNT = (((2,), (2,)), ((0,), (0,)))
_BTN = (((1,), (1,)), ((0,), (0,)))

_PLAIN, _DIFF = _make_mm(_NN, _NT, _TN)
_PLAIN_B, _DIFF_B = _make_mm(_BNN, _BNT, _BTN)


def _sigmoid(x):
    return 1.0 / (1.0 + jnp.exp(-x))


def _silu(x):
    return x * _sigmoid(x)


def _softplus(x):
    return jnp.maximum(x, 0.0) + jnp.log(1.0 + jnp.exp(-jnp.maximum(x, -x)))


def _gelu(x):
    return 0.5 * x * (1.0 + jnp.tanh(0.7978845608028654 * (x + 0.044715 * (x * x * x))))


def _expm1(x):
    s = x * (1.0 + x * (1.0 / 2.0) * (1.0 + x * (1.0 / 3.0) * (1.0 + x * (1.0 / 4.0) * (
        1.0 + x * (1.0 / 5.0) * (1.0 + x * (1.0 / 6.0) * (1.0 + x * (1.0 / 7.0)))))))
    return jnp.where(jnp.maximum(x, -x) < 0.3, s, jnp.exp(x) - 1.0)


def _rms(x, gain):
    ms = jnp.mean(x * x, axis=-1, keepdims=True)
    return x * lax.rsqrt(ms + RMS_EPS) * gain


def _l2n(x):
    return x * lax.rsqrt(jnp.sum(x * x, axis=-1, keepdims=True) + L2_EPS)


def _shift_down(x, r):
    if r == 0:
        return x
    row = lax.broadcasted_iota(jnp.int32, x.shape, 0)
    return jnp.where(row >= r, pltpu.roll(x, r, 0), 0.0)


def _shift_up(x, r):
    if r == 0:
        return x
    n = x.shape[0]
    row = lax.broadcasted_iota(jnp.int32, x.shape, 0)
    return jnp.where(row < n - r, pltpu.roll(x, n - r, 0), 0.0)


def _conv(x, w_ref):
    y = None
    for j in range(CONV_WIDTH):
        t = _shift_down(x, CONV_WIDTH - 1 - j) * w_ref[j:j + 1, :]
        y = t if y is None else y + t
    return y


def _conv_bwd(x, dy, w_ref):
    dx = None
    dws = []
    for j in range(CONV_WIDTH):
        r = CONV_WIDTH - 1 - j
        t = _shift_up(dy, r) * w_ref[j:j + 1, :]
        dx = t if dx is None else dx + t
        dws.append(jnp.sum(dy * _shift_down(x, r), axis=0, keepdims=True))
    return dx, dws


def _scan_tile(a, b, reverse):
    n = a.shape[0]
    row = lax.broadcasted_iota(jnp.int32, a.shape, 0)
    s = 1
    while s < n:
        if reverse:
            keep = row < n - s
            a_sh = jnp.where(keep, pltpu.roll(a, n - s, 0), 1.0)
            b_sh = jnp.where(keep, pltpu.roll(b, n - s, 0), 0.0)
        else:
            keep = row >= s
            a_sh = jnp.where(keep, pltpu.roll(a, s, 0), 1.0)
            b_sh = jnp.where(keep, pltpu.roll(b, s, 0), 0.0)
        b = a * b_sh + b
        a = a * a_sh
        s *= 2
    return a, b


def _dg3(a, b, dims):
    a_hi = a.astype(BF)
    b_hi = b.astype(BF)
    a_lo = a - a_hi.astype(F32)
    b_lo = b - b_hi.astype(F32)
    return _dg(a_hi, b_hi, dims) + (_dg(a_hi, b_lo, dims) + _dg(a_lo, b_hi, dims))


@jax.custom_vjp
def _tri_inv(a):
    n = a.shape[1]
    p = _dg3(a, a, _BNN)
    r = p
    e = 2
    while 2 * e < n:
        p = _dg3(p, p, _BNN)
        r = r + p + _dg3(r, p, _BNN)
        e *= 2
    row = lax.broadcasted_iota(jnp.int32, (1, n, n), 1)
    col = lax.broadcasted_iota(jnp.int32, (1, n, n), 2)
    eye = jnp.where(row == col, 1.0, 0.0)
    return eye - a + r - _dg3(a, r, _BNN)


def _tri_inv_fwd(a):
    t = _tri_inv(a)
    return t, t


def _tri_inv_bwd(t, g):
    return (-_dg3(_dg3(t, g, _BTN), t, _BNT),)


_tri_inv.defvjp(_tri_inv_fwd, _tri_inv_bwd)


def _gdn_chunk(q, k, v, beta, gcol, grow, state, mm):
    c = q.shape[1]
    row = lax.broadcasted_iota(jnp.int32, (1, c, c), 1)
    col = lax.broadcasted_iota(jnp.int32, (1, c, c), 2)
    causal = row >= col
    strict = row > col
    qn = _l2n(q) * (HEAD_DIM ** -0.5)
    kn = _l2n(k)
    dec = jnp.where(causal, jnp.exp(jnp.where(causal, gcol - grow, 0.0)), 0.0)
    kb = kn * beta
    vb = v * beta
    a = jnp.where(strict, mm.nt(kb, kn) * dec, 0.0)
    t = _tri_inv(a)
    eg = jnp.exp(gcol)
    u = mm.nn(t, vb)
    w = mm.nn(t, kb * eg)
    p = jnp.where(causal, mm.nt(qn, kn) * dec, 0.0)
    vn = u - mm.nn(w, state)
    o = mm.nn(qn * eg, state) + mm.nn(p, vn)
    last = lax.broadcasted_iota(jnp.int32, (1, c, 1), 1) == c - 1
    gl = jnp.sum(jnp.where(last, gcol, 0.0), axis=1, keepdims=True)
    kd = kn * jnp.exp(gl - gcol)
    new_state = state * jnp.exp(gl) + mm.tn(kd, vn)
    return o, new_state


def _lru_gates(xc, wa, ba, wx, bx, lam, mm):
    r = _sigmoid(mm.nn(xc, wa) + ba)
    i = _sigmoid(mm.nn(xc, wx) + bx)
    log_a = -LRU_C * r * _softplus(-lam)
    a = jnp.exp(log_a)
    bterm = jnp.sqrt(-_expm1(2.0 * log_a)) * (i * xc)
    return a, bterm


def _matmul(a, b, mode, name, out_dtypes, epi=None, extras=(), tm=1024, tn=1024, tk=2048, after=None):
    if mode == "nn":
        (m, k), (k2, n) = a.shape, b.shape
    elif mode == "nt":
        (m, k), (n, k2) = a.shape, b.shape
    else:
        (k, m), (k2, n) = a.shape, b.shape
    assert k == k2, (a.shape, b.shape, mode)
    tm, tn, tk = _tile(m, tm), _tile(n, tn), _tile(k, tk)
    nk = k // tk
    dims = {"nn": _NN, "nt": _NT, "tn": _TN}[mode]
    if mode == "nn":
        a_spec = pl.BlockSpec((tm, tk), lambda i, j, kk: (i, kk))
        b_spec = pl.BlockSpec((tk, tn), lambda i, j, kk: (kk, j))
    elif mode == "nt":
        a_spec = pl.BlockSpec((tm, tk), lambda i, j, kk: (i, kk))
        b_spec = pl.BlockSpec((tn, tk), lambda i, j, kk: (j, kk))
    else:
        a_spec = pl.BlockSpec((tk, tm), lambda i, j, kk: (kk, i))
        b_spec = pl.BlockSpec((tk, tn), lambda i, j, kk: (kk, j))
    o_spec = pl.BlockSpec((tm, tn), lambda i, j, kk: (i, j))
    n_ex, n_out = len(extras), len(out_dtypes)
    order = [] if after is None else [after]
    n_in = 2 + n_ex + len(order)
    if epi is None:
        epi = lambda acc: (acc,)

    def body(*refs):
        a_ref, b_ref = refs[0], refs[1]
        ex_refs = refs[2:2 + n_ex]
        out_refs = refs[n_in:n_in + n_out]
        kk = pl.program_id(2)
        part = _dg(a_ref[...], b_ref[...], dims)

        def finish(total):
            res = epi(total, *[r[...] for r in ex_refs])
            for r, v in zip(out_refs, res):
                r[...] = v.astype(r.dtype)

        if nk == 1:
            finish(part)
            return
        acc_ref = refs[-1]

        @pl.when(kk == 0)
        def _():
            acc_ref[...] = part

        @pl.when(jnp.logical_and(kk > 0, kk < nk - 1))
        def _():
            acc_ref[...] += part

        @pl.when(kk == nk - 1)
        def _():
            finish(acc_ref[...] + part)

    outs = pl.pallas_call(
        body, name=name, grid=(m // tm, n // tn, nk),
        in_specs=[a_spec, b_spec] + [o_spec] * n_ex + [pl.BlockSpec(memory_space=pl.ANY)] * len(order),
        out_specs=[o_spec] * n_out,
        out_shape=[jax.ShapeDtypeStruct((m, n), d) for d in out_dtypes],
        scratch_shapes=[pltpu.VMEM((tm, tn), F32)] if nk > 1 else [],
        compiler_params=_cparams(("parallel", "parallel", "arbitrary")),
    )(a, b, *extras, *order)
    return outs[0] if n_out == 1 else outs


def _rowwise(fn, name, rows, tr, row_ins, params, row_outs, acc_outs=(), ncol=1):
    tr = _tile(rows, tr, 8)
    n_in, n_par, n_ro, n_acc = len(row_ins), len(params), len(row_outs), len(acc_outs)
    in_specs = []
    for _, width, base, per_col in row_ins:
        if per_col:
            in_specs.append(pl.BlockSpec((tr, width), lambda i, j, base=base: (i, base + j)))
        else:
            in_specs.append(pl.BlockSpec((tr, width), lambda i, j, base=base: (i, base)))
    for p in params:
        in_specs.append(pl.BlockSpec(p.shape, lambda i, j: (0, 0)))
    out_specs, out_shape = [], []
    for total, dtype, width, per_col in row_outs:
        if per_col:
            out_specs.append(pl.BlockSpec((tr, width), lambda i, j: (i, j)))
        else:
            out_specs.append(pl.BlockSpec((tr, width), lambda i, j: (i, 0)))
        out_shape.append(jax.ShapeDtypeStruct((rows, total), dtype))
    for shape, dtype in acc_outs:
        out_specs.append(pl.BlockSpec(shape, lambda i, j: (0, 0)))
        out_shape.append(jax.ShapeDtypeStruct(shape, dtype))

    def body(*refs):
        ins = [r[...] for r in refs[:n_in + n_par]]
        ro = refs[n_in + n_par:n_in + n_par + n_ro]
        ao = refs[n_in + n_par + n_ro:]
        res = fn(*ins)
        for r, v in zip(ro, res[:n_ro]):
            r[...] = v.astype(r.dtype)
        first = jnp.logical_and(pl.program_id(0) == 0, pl.program_id(1) == 0)

        @pl.when(first)
        def _():
            for r in ao:
                r[...] = jnp.zeros_like(r)

        for r, v in zip(ao, res[n_ro:]):
            r[...] += v.astype(r.dtype)

    sem = ("arbitrary", "arbitrary") if n_acc else ("parallel", "parallel")
    outs = pl.pallas_call(
        body, name=name, grid=(rows // tr, ncol), in_specs=in_specs, out_specs=out_specs,
        out_shape=out_shape, compiler_params=_cparams(sem),
    )(*[r[0] for r in row_ins], *params)
    return outs


def _row(x):
    return x.reshape(1, -1)


def _rms_fwd(x, gain, name):
    s, d = x.shape
    return _rowwise(lambda xt, g: (_rms(xt, g),), name, s, 512, [(x, d, 0, False)], [_row(gain)],
                    [(d, BF, d, False)])[0]


def _rms_bwd(x, gain, dh, dres, name):
    s, d = x.shape

    def fn(xt, dht, drt, g):
        _, vjp = jax.vjp(_rms, xt, g)
        dx, dg = vjp(dht)
        return drt + dx, dg

    return _rowwise(fn, name, s, 256, [(x, d, 0, False), (dh, d, 0, False), (dres, d, 0, False)], [_row(gain)],
                    [(d, F32, d, False)], [((1, d), F32)])


def _loss_head(x, gain, target, name):
    s, d = x.shape

    def fn(xt, tt, g):
        y, vjp = jax.vjp(_rms, xt, g)
        err = y - tt
        dx, dg = vjp(err * (1.0 / d))
        part = 0.5 * jnp.sum(jnp.sum(err * err, axis=1, keepdims=True), axis=0, keepdims=True) * (1.0 / d)
        return dx, jnp.broadcast_to(part, (1, LANE)), dg

    return _rowwise(fn, name, s, 256, [(x, d, 0, False), (target, d, 0, False)], [_row(gain)],
                    [(d, F32, d, False)], [((1, LANE), F32), ((1, d), F32)])


def _gdn_post_fn(o, z, g):
    return _rms(o, g) * _silu(z)


def _gdn_post_fwd(o, proj, gain, z_base, name):
    s = o.shape[0]
    return _rowwise(lambda ot, zt, g: (_gdn_post_fn(ot, zt, g),), name, s, 1024,
                    [(o, HEAD_DIM, 0, True), (proj, HEAD_DIM, z_base, True)], [_row(gain)],
                    [(GDN_WIDTH, BF, HEAD_DIM, True)], ncol=HEADS)[0]


def _gdn_post_bwd(o, proj, gain, dog, z_base, name):
    s = o.shape[0]

    def fn(ot, zt, dt, g):
        _, vjp = jax.vjp(_gdn_post_fn, ot, zt, g)
        return vjp(dt)

    return _rowwise(fn, name, s, 1024,
                    [(o, HEAD_DIM, 0, True), (proj, HEAD_DIM, z_base, True), (dog, HEAD_DIM, 0, True)],
                    [_row(gain)], [(GDN_WIDTH, F32, HEAD_DIM, True), (GDN_WIDTH, BF, HEAD_DIM, True)],
                    [((1, HEAD_DIM), F32)], ncol=HEADS)


def _merge_fn(gg, gl, yg, yl):
    return _sigmoid(gg) * yg + _sigmoid(gl) * yl


def _merge_fwd(proj, yg, yl, gg_base, name):
    s, d = yg.shape
    return _rowwise(lambda a, b, c, e: (_merge_fn(a, b, c, e),), name, s, 256,
                    [(proj, d, gg_base, False), (proj, d, gg_base + 1, False), (yg, d, 0, False), (yl, d, 0, False)],
                    [], [(d, BF, d, False)])[0]


def _merge_bwd(proj, yg, yl, dmerged, gg_base, name):
    s, d = yg.shape

    def fn(a, b, c, e, dm):
        _, vjp = jax.vjp(_merge_fn, a, b, c, e)
        return vjp(dm)

    return _rowwise(fn, name, s, 128,
                    [(proj, d, gg_base, False), (proj, d, gg_base + 1, False), (yg, d, 0, False), (yl, d, 0, False),
                     (dmerged, d, 0, False)], [], [(d, BF, d, False)] * 4)


def _conv_silu_fwd(proj, w, name):
    s = proj.shape[0]
    cols = w.shape[1]
    cw = _tile(cols, LANE)

    def body(x_ref, w_ref, o_ref):
        o_ref[...] = _silu(_conv(x_ref[...], w_ref))

    return pl.pallas_call(
        body, name=name, grid=(cols // cw,),
        in_specs=[pl.BlockSpec((s, cw), lambda j: (0, j)), pl.BlockSpec((CONV_WIDTH, cw), lambda j: (0, j))],
        out_specs=pl.BlockSpec((s, cw), lambda j: (0, j)),
        out_shape=jax.ShapeDtypeStruct((s, cols), F32), compiler_params=_cparams(("parallel",)),
    )(proj, w)


def _conv_silu_bwd(proj, w, dout, name):
    s = proj.shape[0]
    cols = w.shape[1]
    cw = _tile(cols, LANE)

    def body(x_ref, w_ref, d_ref, dx_ref, dw_ref):
        x = x_ref[...]
        y = _conv(x, w_ref)
        sg = _sigmoid(y)
        dy = d_ref[...] * (sg * (1.0 + y * (1.0 - sg)))
        dx, dws = _conv_bwd(x, dy, w_ref)
        dx_ref[...] = dx.astype(dx_ref.dtype)
        for j in range(CONV_WIDTH):
            dw_ref[j:j + 1, :] = dws[j]

    return pl.pallas_call(
        body, name=name, grid=(cols // cw,),
        in_specs=[pl.BlockSpec((s, cw), lambda j: (0, j)), pl.BlockSpec((CONV_WIDTH, cw), lambda j: (0, j)),
                  pl.BlockSpec((s, cw), lambda j: (0, j))],
        out_specs=[pl.BlockSpec((s, cw), lambda j: (0, j)), pl.BlockSpec((CONV_WIDTH, cw), lambda j: (0, j))],
        out_shape=[jax.ShapeDtypeStruct((s, cols), BF), jax.ShapeDtypeStruct((CONV_WIDTH, cols), F32)],
        compiler_params=_cparams(("parallel",)),
    )(proj, w, dout)


def _cumsum_mask(tr, transpose):
    row = lax.broadcasted_iota(jnp.int32, (tr, tr), 0)
    col = lax.broadcasted_iota(jnp.int32, (tr, tr), 1)
    same = (row // CHUNK) == (col // CHUNK)
    tri = (row <= col) if transpose else (row >= col)
    return jnp.where(jnp.logical_and(same, tri), 1.0, 0.0)


def _lane_pad(v):
    return jnp.pad(v.reshape(1, -1), ((0, 0), (0, LANE - v.shape[0])))


def _gdn_gates_fwd(proj, a_log, dt_bias, ab_base, name):
    s = proj.shape[0]

    def fn(ab, al, dt):
        g = -jnp.exp(al) * _softplus(ab + dt)
        beta = _sigmoid(pltpu.roll(ab, LANE - HEADS, 1))
        gc = jnp.dot(_cumsum_mask(ab.shape[0], False), g, precision=lax.Precision.HIGHEST,
                     preferred_element_type=F32)
        return gc, beta

    return _rowwise(fn, name, s, 512, [(proj, LANE, ab_base, False)], [_lane_pad(a_log), _lane_pad(dt_bias)],
                    [(LANE, F32, LANE, False)] * 2)


def _gdn_gates_bwd(proj, a_log, dt_bias, dgc_col, dgc_row, dbeta, ab_base, name):
    s = proj.shape[0]

    def fn(ab, dc, dr, db, al, dt):
        lane = lax.broadcasted_iota(jnp.int32, ab.shape, 1)
        dg = jnp.dot(_cumsum_mask(ab.shape[0], True), dc + dr, precision=lax.Precision.HIGHEST,
                     preferred_element_type=F32)
        ea = jnp.exp(al)
        pre = ab + dt
        g = -ea * _softplus(pre)
        da = jnp.where(lane < HEADS, dg * (-ea) * _sigmoid(pre), 0.0)
        dal = jnp.sum(jnp.where(lane < HEADS, dg * g, 0.0), axis=0, keepdims=True)
        ddt = jnp.sum(da, axis=0, keepdims=True)
        beta = _sigmoid(pltpu.roll(ab, LANE - HEADS, 1))
        dbl = jnp.where(lane < HEADS, db * beta * (1.0 - beta), 0.0)
        dab = da + pltpu.roll(dbl, HEADS, 1)
        return dab, dal, ddt

    return _rowwise(fn, name, s, 512,
                    [(proj, LANE, ab_base, False), (dgc_col, LANE, 0, False), (dgc_row, LANE, 0, False),
                     (dbeta, LANE, 0, False)], [_lane_pad(a_log), _lane_pad(dt_bias)],
                    [(LANE, BF, LANE, False)], [((1, LANE), F32), ((1, LANE), F32)])


def _head_col(blk, h):
    lane = lax.broadcasted_iota(jnp.int32, blk.shape, 1)
    return jnp.sum(jnp.where(lane == h, blk, 0.0), axis=1, keepdims=True)


def _chunk_operands(qkv_ref, gc_ref, b_ref, gr_ref):
    heads = range(HEADS)
    q, k, v = (jnp.stack([qkv_ref[:, (j * HEADS + h) * HEAD_DIM:(j * HEADS + h + 1) * HEAD_DIM] for h in heads])
               for j in range(3))
    gcs, betas = gc_ref[...], b_ref[...]
    beta = jnp.stack([_head_col(betas, h) for h in heads])
    gcol = jnp.stack([_head_col(gcs, h) for h in heads])
    grow = jnp.stack([gr_ref[0, h:h + 1, :] for h in heads])
    return q, k, v, beta, gcol, grow


def _gdn_chunk_fwd(qkv, gc, beta, grow, name):
    s = qkv.shape[0]
    nc = s // CHUNK

    def body(qkv_ref, gc_ref, b_ref, gr_ref, o_ref, st_ref, state):
        @pl.when(pl.program_id(0) == 0)
        def _():
            state[...] = jnp.zeros_like(state)

        st = state[...]
        st_ref[0] = st
        o, new = _gdn_chunk(*_chunk_operands(qkv_ref, gc_ref, b_ref, gr_ref), st, _PLAIN_B)
        for h in range(HEADS):
            o_ref[:, h * HEAD_DIM:(h + 1) * HEAD_DIM] = o[h]
        state[...] = new

    return pl.pallas_call(
        body, name=name, grid=(nc,),
        in_specs=[pl.BlockSpec((CHUNK, 3 * GDN_WIDTH), lambda c: (c, 0)),
                  pl.BlockSpec((CHUNK, LANE), lambda c: (c, 0)), pl.BlockSpec((CHUNK, LANE), lambda c: (c, 0)),
                  pl.BlockSpec((1, HEADS, CHUNK), lambda c: (c, 0, 0))],
        out_specs=[pl.BlockSpec((CHUNK, GDN_WIDTH), lambda c: (c, 0)),
                   pl.BlockSpec((1, HEADS, HEAD_DIM, HEAD_DIM), lambda c: (c, 0, 0, 0))],
        out_shape=[jax.ShapeDtypeStruct((s, GDN_WIDTH), F32),
                   jax.ShapeDtypeStruct((nc, HEADS, HEAD_DIM, HEAD_DIM), F32)],
        scratch_shapes=[pltpu.VMEM((HEADS, HEAD_DIM, HEAD_DIM), F32)],
        compiler_params=_cparams(("arbitrary",)),
    )(qkv, gc, beta, grow)


def _gdn_chunk_bwd(qkv, gc, beta, grow, states, do, name):
    s = qkv.shape[0]
    nc = s // CHUNK

    def body(qkv_ref, gc_ref, b_ref, gr_ref, st_ref, do_ref, dqkv_ref, dgc_ref, db_ref, dgr_ref, dstate):
        @pl.when(pl.program_id(0) == 0)
        def _():
            dstate[...] = jnp.zeros_like(dstate)

        lane = lax.broadcasted_iota(jnp.int32, (CHUNK, LANE), 1)
        _, vjp = jax.vjp(functools.partial(_gdn_chunk, mm=_DIFF_B), *_chunk_operands(qkv_ref, gc_ref, b_ref, gr_ref),
                         st_ref[0])
        do = jnp.stack([do_ref[:, h * HEAD_DIM:(h + 1) * HEAD_DIM] for h in range(HEADS)])
        dq, dk, dv, dbeta, dgcol, dgrow, dst = vjp((do, dstate[...]))
        dstate[...] = dst
        dgc = jnp.zeros((CHUNK, LANE), F32)
        db = jnp.zeros((CHUNK, LANE), F32)
        for h in range(HEADS):
            for j, dx in enumerate((dq, dk, dv)):
                dqkv_ref[:, (j * HEADS + h) * HEAD_DIM:(j * HEADS + h + 1) * HEAD_DIM] = dx[h]
            dgc = dgc + jnp.where(lane == h, dgcol[h], 0.0)
            db = db + jnp.where(lane == h, dbeta[h], 0.0)
            dgr_ref[0, h:h + 1, :] = dgrow[h]
        dgc_ref[...] = dgc
        db_ref[...] = db

    rev = lambda c: nc - 1 - c
    wide = pl.BlockSpec((CHUNK, 3 * GDN_WIDTH), lambda c: (rev(c), 0))
    lanes = pl.BlockSpec((CHUNK, LANE), lambda c: (rev(c), 0))
    rows = pl.BlockSpec((1, HEADS, CHUNK), lambda c: (rev(c), 0, 0))
    return pl.pallas_call(
        body, name=name, grid=(nc,),
        in_specs=[wide, lanes, lanes, rows,
                  pl.BlockSpec((1, HEADS, HEAD_DIM, HEAD_DIM), lambda c: (rev(c), 0, 0, 0)),
                  pl.BlockSpec((CHUNK, GDN_WIDTH), lambda c: (rev(c), 0))],
        out_specs=[wide, lanes, lanes, rows],
        out_shape=[jax.ShapeDtypeStruct((s, 3 * GDN_WIDTH), F32)]
        + [jax.ShapeDtypeStruct((s, LANE), F32)] * 2 + [jax.ShapeDtypeStruct((nc, HEADS, CHUNK), F32)],
        scratch_shapes=[pltpu.VMEM((HEADS, HEAD_DIM, HEAD_DIM), F32)],
        compiler_params=_cparams(("arbitrary",)),
    )(qkv, gc, beta, grow, states, do)


def _lru_specs(s, xb_base):
    col = lambda base: pl.BlockSpec((s, LRU_BLOCK_DIM), lambda n, base=base: (0, base + n))
    vec = pl.BlockSpec((1, LRU_BLOCK_DIM), lambda n: (0, n))
    mat = pl.BlockSpec((None, LRU_BLOCK_DIM, LRU_BLOCK_DIM), lambda n: (n, 0, 0))
    cw = pl.BlockSpec((CONV_WIDTH, LRU_BLOCK_DIM), lambda n: (0, n))
    return col, vec, mat, cw


def _lru_fwd(proj, cw, cb, wa, ba, wx, bx, lam, xb_base, name):
    s = proj.shape[0]
    tt = _tile(s, 256, 8)
    col, vec, mat, cws = _lru_specs(s, xb_base)

    def body(xb_ref, yb_ref, cw_ref, cb_ref, wa_ref, ba_ref, wx_ref, bx_ref, lam_ref, o_ref, h_ref, xc_s):
        xc_s[...] = _conv(xb_ref[...], cw_ref) + cb_ref[...]
        par = (wa_ref[...], ba_ref[...], wx_ref[...], bx_ref[...], lam_ref[...])

        def step(t, carry):
            sl = pl.ds(pl.multiple_of(t * tt, tt), tt)
            a, bt = _lru_gates(xc_s[sl, :], *par, _PLAIN)
            aa, bb = _scan_tile(a, bt, False)
            h = aa * carry + bb
            h_ref[sl, :] = h
            o_ref[sl, :] = (h * _gelu(yb_ref[sl, :])).astype(o_ref.dtype)
            return h_ref[pl.ds(t * tt + tt - 1, 1), :]

        lax.fori_loop(0, s // tt, step, jnp.zeros((1, LRU_BLOCK_DIM), F32))

    nb = LRU_BLOCKS
    return pl.pallas_call(
        body, name=name, grid=(LRU_BLOCKS,),
        in_specs=[col(xb_base), col(xb_base + nb), cws, vec, mat, vec, mat, vec, vec],
        out_specs=[col(0), col(0)],
        out_shape=[jax.ShapeDtypeStruct((s, LRU_WIDTH), BF), jax.ShapeDtypeStruct((s, LRU_WIDTH), F32)],
        scratch_shapes=[pltpu.VMEM((s, LRU_BLOCK_DIM), F32)],
        compiler_params=_cparams(("parallel",)),
    )(proj, proj, cw, _row(cb), wa, _row(ba), wx, _row(bx), _row(lam))


def _lru_bwd(proj, hs, dol, cw, cb, wa, ba, wx, bx, lam, xb_base, name):
    s = proj.shape[0]
    tt = _tile(s, 256, 8)
    nt = s // tt
    col, vec, mat, cws = _lru_specs(s, xb_base)

    def body(xb_ref, yb_ref, h_ref, d_ref, cw_ref, cb_ref, wa_ref, ba_ref, wx_ref, bx_ref, lam_ref,
             dxb_ref, dyb_ref, dcw_ref, dcb_ref, dwa_ref, dba_ref, dwx_ref, dbx_ref, dlam_ref,
             xc_s, a_s, dh_s, dxc_s):
        xc_s[...] = _conv(xb_ref[...], cw_ref) + cb_ref[...]
        par = (wa_ref[...], ba_ref[...], wx_ref[...], bx_ref[...], lam_ref[...])
        row = lax.broadcasted_iota(jnp.int32, (tt, LRU_BLOCK_DIM), 0)
        tile = lambda t: pl.ds(pl.multiple_of(t * tt, tt), tt)

        def prep(t, carry):
            sl = tile(t)
            a_s[sl, :] = _lru_gates(xc_s[sl, :], *par, _PLAIN)[0]
            d = d_ref[sl, :]
            gy, vjp_y = jax.vjp(_gelu, yb_ref[sl, :])
            dyb_ref[sl, :] = vjp_y(d * h_ref[sl, :])[0].astype(dyb_ref.dtype)
            dh_s[sl, :] = d * gy
            return carry

        lax.fori_loop(0, nt, prep, 0)

        def rscan(i, carry):
            dh_next, a_next = carry
            t = nt - 1 - i
            sl = tile(t)
            a_sh = jnp.where(row == tt - 1, a_next, pltpu.roll(a_s[sl, :], tt - 1, 0))
            aa, bb = _scan_tile(a_sh, dh_s[sl, :], True)
            dh_s[sl, :] = aa * dh_next + bb
            first = pl.ds(t * tt, 1)
            return dh_s[first, :], a_s[first, :]

        zero = jnp.zeros((1, LRU_BLOCK_DIM), F32)
        lax.fori_loop(0, nt, rscan, (zero, zero))

        def gates_vjp(t, acc):
            sl = tile(t)
            _, vjp_g = jax.vjp(functools.partial(_lru_gates, mm=_DIFF), xc_s[sl, :], *par)
            dh = dh_s[sl, :]
            before = jnp.where(t > 0, h_ref[pl.ds(jnp.maximum(t * tt - 1, 0), 1), :], 0.0)
            h_prev = jnp.where(row == 0, before, pltpu.roll(h_ref[sl, :], 1, 0))
            dxc, *dpar = vjp_g((dh * h_prev, dh))
            dxc_s[sl, :] = dxc
            return tuple(x + y for x, y in zip(acc, dpar))

        dwa, dba, dwx, dbx, dlam = lax.fori_loop(0, nt, gates_vjp, tuple(jnp.zeros_like(p) for p in par))
        dwa_ref[...] = dwa
        dba_ref[...] = dba
        dwx_ref[...] = dwx
        dbx_ref[...] = dbx
        dlam_ref[...] = dlam
        dxc = dxc_s[...]
        dcb_ref[...] = jnp.sum(dxc, axis=0, keepdims=True)
        dxb, dws = _conv_bwd(xb_ref[...], dxc, cw_ref)
        dxb_ref[...] = dxb.astype(dxb_ref.dtype)
        for j in range(CONV_WIDTH):
            dcw_ref[j:j + 1, :] = dws[j]

    nb = LRU_BLOCKS
    w = LRU_WIDTH
    return pl.pallas_call(
        body, name=name, grid=(LRU_BLOCKS,),
        in_specs=[col(xb_base), col(xb_base + nb), col(0), col(0), cws, vec, mat, vec, mat, vec, vec],
        out_specs=[col(0), col(0), cws, vec, mat, vec, mat, vec, vec],
        out_shape=[jax.ShapeDtypeStruct((s, w), BF), jax.ShapeDtypeStruct((s, w), BF),
                   jax.ShapeDtypeStruct((CONV_WIDTH, w), F32), jax.ShapeDtypeStruct((1, w), F32),
                   jax.ShapeDtypeStruct(wa.shape, F32), jax.ShapeDtypeStruct((1, w), F32),
                   jax.ShapeDtypeStruct(wx.shape, F32), jax.ShapeDtypeStruct((1, w), F32),
                   jax.ShapeDtypeStruct((1, w), F32)],
        scratch_shapes=[pltpu.VMEM((s, LRU_BLOCK_DIM), F32)] * 4,
        compiler_params=_cparams(("parallel",)),
    )(proj, proj, hs, dol, cw, _row(cb), wa, _row(ba), wx, _row(bx), _row(lam))


def _place():
    return lax.axis_index("x"), lax.axis_index("y"), lax.axis_index("c")


def _all_gather(blocks, name):
    n = len(blocks)
    per = N_DEV - 1

    def body(*refs):
        ins, outs = refs[:n], refs[n:2 * n]
        send_sems, recv_sems, local_sems = refs[2 * n:]
        x, y, c = _place()
        me, sibling = (x, y, c), (x, y, 1 - c)
        chips = [(1 - x, y), (x, 1 - y), (1 - x, 1 - y)]

        def copy(a, k, block, to, src=None):
            dst = outs[a].at[4 * block[0] + 2 * block[1] + block[2]]
            return pltpu.make_async_remote_copy(
                src_ref=dst if src is None else src, dst_ref=dst, send_sem=send_sems.at[a * per + k],
                recv_sem=recv_sems.at[a * per + k], device_id=to, device_id_type=MESH)

        started = []
        for a in range(n):
            mine = pltpu.make_async_copy(ins[a], outs[a].at[4 * x + 2 * y + c], local_sems.at[a])
            mine.start()
            started.append(mine)
        sends = []
        for a in range(n):
            first = [copy(a, 0, me, sibling, src=ins[a])]
            first += [copy(a, 1 + j, me, (*chip, c), src=ins[a]) for j, chip in enumerate(chips)]
            for cp in first:
                cp.start()
            sends += first
        for a in range(n):
            for j, chip in enumerate(chips):
                copy(a, 1 + j, (*chip, c), me).wait_recv()
                passed = copy(a, 4 + j, (*chip, c), sibling)
                passed.start()
                sends.append(passed)
        for a in range(n):
            copy(a, 0, sibling, me).wait_recv()
            for j, chip in enumerate(chips):
                copy(a, 4 + j, (*chip, 1 - c), me).wait_recv()
        for cp in sends:
            cp.wait_send()
        for cp in started:
            cp.wait()

    any_spec = pl.BlockSpec(memory_space=pl.ANY)
    return pl.pallas_call(
        body, name=name, in_specs=[any_spec] * n, out_specs=[any_spec] * n,
        out_shape=[jax.ShapeDtypeStruct((N_DEV,) + b.shape, b.dtype) for b in blocks],
        scratch_shapes=[pltpu.SemaphoreType.DMA((n * per,)), pltpu.SemaphoreType.DMA((n * per,)),
                        pltpu.SemaphoreType.DMA((n,))],
    )(*blocks)


_HBM = pl.BlockSpec(memory_space=pltpu.HBM)
_SEM = pl.BlockSpec(memory_space=pltpu.SEMAPHORE)
_ANY = pl.BlockSpec(memory_space=pl.ANY)
_EFFECT = pltpu.SideEffectType.DATAFLOW_SIDE_EFFECTING


def _peers(x, y, c):
    return [(x, y, 1 - c), (1 - x, y, c), (1 - x, y, 1 - c), (x, 1 - y, c), (x, 1 - y, 1 - c),
            (1 - x, 1 - y, c), (1 - x, 1 - y, 1 - c)]


class _InFlight:
    def __init__(self, send_sem, recv_sem, src, land):
        self.send_sem, self.recv_sem, self.src, self.land = send_sem, recv_sem, src, land


def _send_start(srcs, whole, after, name):
    n = len(srcs)
    lands = [lax.empty(((N_DEV,) + s.shape) if whole else s.shape, s.dtype) for s in srcs]

    def body(*refs):
        src_refs, land_refs = refs[:n], refs[n:2 * n]
        outs = refs[2 * n + 1:]
        send_sems, recv_sems, token = outs[:n], outs[n:2 * n], outs[-1]
        x, y, c = _place()
        me = 4 * x + 2 * y + c
        for a in range(n):
            for p in _peers(x, y, c):
                src = src_refs[a] if whole else src_refs[a].at[4 * p[0] + 2 * p[1] + p[2]]
                pltpu.make_async_remote_copy(src_ref=src, dst_ref=land_refs[a].at[me], send_sem=send_sems[a],
                                             recv_sem=recv_sems[a], device_id=p, device_id_type=MESH).start()
        token[...] = jnp.zeros_like(token)

    hbm = lambda a: pltpu.HBM(a.shape, a.dtype)
    outs = pl.pallas_call(
        body, name=name,
        out_shape=[pltpu.SemaphoreType.DMA(())] * (2 * n) + [hbm(s) for s in srcs] + [hbm(l) for l in lands]
        + [jax.ShapeDtypeStruct((8, LANE), F32)],
        in_specs=[_HBM] * (2 * n) + [_ANY],
        out_specs=[_SEM] * (2 * n) + [_HBM] * (2 * n) + [pl.BlockSpec(memory_space=pltpu.VMEM)],
        input_output_aliases={i: 2 * n + i for i in range(2 * n)},
        compiler_params=pltpu.CompilerParams(has_side_effects=_EFFECT),
    )(*[pltpu.with_memory_space_constraint(s, pltpu.HBM) for s in srcs],
      *[pltpu.with_memory_space_constraint(l, pltpu.HBM) for l in lands], after)
    flights = [_InFlight(outs[a], outs[n + a], outs[2 * n + a], outs[3 * n + a]) for a in range(n)]
    return flights, outs[-1]


def _send_wait(flights, after, name):
    n = len(flights)

    def body(*refs):
        x, y, c = _place()
        for a in range(n):
            land_ref, send_sem, recv_sem = refs[4 * a + 1], refs[4 * a + 2], refs[4 * a + 3]
            seven = land_ref.at[pl.ds(0, N_DEV - 1)]
            copy = pltpu.make_async_remote_copy(src_ref=seven, dst_ref=seven, send_sem=send_sem, recv_sem=recv_sem,
                                                device_id=(x, y, 1 - c), device_id_type=MESH)
            copy.wait_send()
            copy.wait_recv()

    ins, out_shape = [], []
    for f in flights:
        ins += [f.src, f.land, f.send_sem, f.recv_sem]
        out_shape += [pltpu.HBM(f.src.shape, f.src.dtype), pltpu.HBM(f.land.shape, f.land.dtype)]
    outs = pl.pallas_call(
        body, name=name, out_shape=out_shape,
        in_specs=[_HBM, _HBM, _SEM, _SEM] * n + [_ANY], out_specs=[_HBM] * (2 * n),
        input_output_aliases={4 * a + i: 2 * a + i for a in range(n) for i in range(2)},
        compiler_params=pltpu.CompilerParams(has_side_effects=_EFFECT),
    )(*ins, after)
    return [(outs[2 * a], outs[2 * a + 1]) for a in range(n)]


def _adamw_math(w, g, m, v):
    m = ADAM_B1 * m + (1.0 - ADAM_B1) * g
    v = ADAM_B2 * v + (1.0 - ADAM_B2) * (g * g)
    m_hat = m / (1.0 - ADAM_B1 ** ADAM_STEP)
    v_hat = v / (1.0 - ADAM_B2 ** ADAM_STEP)
    delta = -ADAM_LR * (m_hat / (jnp.sqrt(v_hat) + ADAM_EPS) + ADAM_WD * w)
    return delta, m, v


def _adamw_sharded(mine, landed, me, w, m, v, layer, prev, after, name):
    depth, r, c = w.shape
    tr = _tile(r, 128, 8)
    n_prev = 0 if prev is None else 4

    def body(*refs):
        me_ref, own_ref, land_ref, w_ref, m_ref, v_ref = refs[:6]
        g_ref, d_ref, nm_ref, nv_ref = refs[7 + n_prev:]
        g = None
        for k in range(N_DEV):
            part = jnp.where(me_ref[0] == k, own_ref[0], land_ref[k]).astype(F32)
            g = part if g is None else g + part
        delta, nm, nv = _adamw_math(w_ref[...], g, m_ref[...], v_ref[...])
        g_ref[...] = g
        d_ref[...] = delta
        nm_ref[...] = nm
        nv_ref[...] = nv

    lay = pl.BlockSpec((None, tr, c), lambda i, me_ref: (layer, i, 0))
    return pl.pallas_call(
        body, name=name,
        grid_spec=pltpu.PrefetchScalarGridSpec(
            num_scalar_prefetch=1, grid=(r // tr,),
            in_specs=[pl.BlockSpec((1, tr, c), lambda i, me_ref: (me_ref[0], i, 0)),
                      pl.BlockSpec((N_DEV, tr, c), lambda i, me_ref: (0, i, 0)), lay, lay, lay]
            + [_ANY] * (1 + n_prev),
            out_specs=[lay] * 4),
        out_shape=[jax.ShapeDtypeStruct(w.shape, F32)] * 4,
        input_output_aliases={7 + i: i for i in range(n_prev)},
        compiler_params=_cparams(("parallel",)),
    )(me, mine, landed, w, m, v, after, *(prev or ()))


def _sum_devices(stack, name):
    _, r, c = stack.shape
    tr = _tile(r, 512, 8)

    def body(s_ref, o_ref):
        g = s_ref[0]
        for j in range(1, N_DEV):
            g = g + s_ref[j]
        o_ref[...] = g

    return pl.pallas_call(
        body, name=name, grid=(r // tr,), in_specs=[pl.BlockSpec((N_DEV, tr, c), lambda i: (0, i, 0))],
        out_specs=pl.BlockSpec((tr, c), lambda i: (i, 0)), out_shape=jax.ShapeDtypeStruct((r, c), F32),
        compiler_params=_cparams(("parallel",)),
    )(stack)


def _adamw_packed(w, g, m, v, name):
    r, c = w.shape
    tr = _tile(r, 512, 8)

    def body(w_ref, g_ref, m_ref, v_ref, d_ref, nm_ref, nv_ref):
        delta, nm, nv = _adamw_math(w_ref[...], g_ref[...], m_ref[...], v_ref[...])
        d_ref[...] = delta
        nm_ref[...] = nm
        nv_ref[...] = nv

    spec = pl.BlockSpec((tr, c), lambda i: (i, 0))
    return pl.pallas_call(
        body, name=name, grid=(r // tr,), in_specs=[spec] * 4, out_specs=[spec] * 3,
        out_shape=[jax.ShapeDtypeStruct((r, c), F32)] * 3, compiler_params=_cparams(("parallel",)),
    )(w, g, m, v)


def _pack(arrays):
    unit = 8 * LANE
    flat = []
    for a in arrays:
        f = a.reshape(-1)
        flat.append(jnp.pad(f, (0, (-f.shape[0]) % unit)))
    return jnp.concatenate(flat).reshape(-1, LANE)


def _unpack(packed, like):
    unit = 8 * LANE
    flat = packed.reshape(-1)
    out, off = [], 0
    for a in like:
        n = 1
        for d in a.shape:
            n *= d
        out.append(flat[off:off + n].reshape(a.shape))
        off += n + (-n) % unit
    return out


class _Layout:
    def __init__(self, d):
        self.d = d
        self.n_front = 4 * GDN_WIDTH
        self.n_ab = 2 * HEADS
        self.n_back = 2 * LRU_WIDTH + 2 * d
        self.cols = self.n_front + self.n_ab + self.n_back
        self.pcols = self.n_front + self.n_back + LANE
        self.z_base = 3 * GDN_WIDTH // HEAD_DIM
        self.xb_base = self.n_front // LRU_BLOCK_DIM
        self.gg_off = self.n_front + 2 * LRU_WIDTH
        assert self.gg_off % d == 0
        self.gg_base = self.gg_off // d
        self.ab_base = (self.n_front + self.n_back) // LANE

    def permute(self, w):
        f, ab = self.n_front, self.n_ab
        pad = jnp.zeros((w.shape[0], LANE - ab), w.dtype)
        return jnp.concatenate([w[:, :f], w[:, f + ab:], w[:, f:f + ab], pad], axis=1)

    def unpermute(self, w):
        f, ab, b = self.n_front, self.n_ab, self.n_back
        return jnp.concatenate([w[:, :f], w[:, f + b:f + b + ab], w[:, f:f + b]], axis=1)


def _cols_from_stack(stack):
    nd, k, n = stack.shape
    return stack.transpose(1, 0, 2).reshape(k, nd * n)


def _stack_from_cols(w):
    k, n = w.shape
    return w.reshape(k, N_DEV, n // N_DEV).transpose(1, 0, 2)


def _after(v, *tokens):
    for t in tokens:
        v = v + t[0, 0]
    return v


def _layer_fwd(x, p, late, lay, tag):
    s, d = x.shape
    nc = s // CHUNK
    h = _rms_fwd(x, p["attn_norm"], f"rms1_fwd{tag}")
    proj = _matmul(h, p["w_in"], "nn", f"in_proj{tag}", [F32], tn=1152)
    qkv = _conv_silu_fwd(proj, p["gdn_conv_w"], f"gdn_conv_fwd{tag}")
    gc, beta = _gdn_gates_fwd(proj, p["gdn_a_log"], p["gdn_dt_bias"], lay.ab_base, f"gdn_gates_fwd{tag}")
    grow = gc[:, :HEADS].reshape(nc, CHUNK, HEADS).transpose(0, 2, 1)
    o, states = _gdn_chunk_fwd(qkv, gc, beta, grow, f"gdn_chunk_fwd{tag}")
    og = _gdn_post_fwd(o, proj, p["gdn_norm"], lay.z_base, f"gdn_post_fwd{tag}")
    ol, hs = _lru_fwd(proj, p["lru_conv_w"], p["lru_conv_b"], p["lru_w_a"], p["lru_b_a"], p["lru_w_x"], p["lru_b_x"],
                      p["lru_lambda"], lay.xb_base, f"lru_fwd{tag}")
    late[0](hs)
    yg = _matmul(og, p["w_branch_gdn"], "nn", f"branch_gdn{tag}", [F32])
    yl = _matmul(ol, p["w_branch_lru"], "nn", f"branch_lru{tag}", [F32])
    merged = _merge_fwd(proj, yg, yl, lay.gg_base, f"merge_fwd{tag}")
    x1 = _matmul(merged, p["w_out"], "nn", f"out_proj{tag}", [F32], epi=lambda acc, r: (acc + r,), extras=[x])
    h2 = _rms_fwd(x1, p["mlp_norm"], f"rms2_fwd{tag}")
    late[1](h2)
    u, act = _matmul(h2, p["w_up"], "nn", f"mlp_up{tag}", [F32, BF],
                     epi=lambda acc: (acc, jnp.square(jnp.maximum(acc, 0.0))))
    x2 = _matmul(act, p["w_down"], "nn", f"mlp_down{tag}", [F32], epi=lambda acc, r: (acc + r,), extras=[x1])
    saved = dict(x=x, h=h, proj=proj, qkv=qkv, gc=gc, beta=beta, grow=grow, o=o, states=states, og=og, ol=ol, hs=hs,
                 yg=yg, yl=yl, merged=merged, x1=x1, h2=h2, u=u, act=act)
    return x2, saved


def _layer_bwd(dx2, p, sv, send, send_small, after_last_send, after, lay, tag):
    s, d = dx2.shape
    nc = s // CHUNK
    small = {}
    du = _matmul(dx2, p["w_down"], "nt", f"d_act{tag}", [BF],
                 epi=lambda acc, u: (acc * (2.0 * jnp.maximum(u, 0.0)),), extras=[sv["u"]], after=after)
    sent = send({"w_down": _matmul(sv["act"], dx2, "tn", f"dw_down{tag}", [BF]).reshape(N_DEV, -1, d),
                 "w_up": _stack_from_cols(_matmul(sv["h2"], du, "tn", f"dw_up{tag}", [BF]))})
    dh2 = _matmul(du, p["w_up"], "nt", f"d_h2{tag}", [F32])
    dx1, g = _rms_bwd(sv["x1"], _after(p["mlp_norm"], sent), dh2, dx2, f"rms2_bwd{tag}")
    small["mlp_norm"] = g.reshape(-1)
    dmerged = _matmul(dx1, p["w_out"], "nt", f"d_merged{tag}", [F32])
    dwout = _matmul(sv["merged"], dx1, "tn", f"dw_out{tag}", [BF]).reshape(N_DEV, -1, d)
    dgg, dgl, dyg, dyl = _merge_bwd(sv["proj"], sv["yg"], sv["yl"], dmerged, lay.gg_base, f"merge_bwd{tag}")
    sent = send({"w_out": dwout,
                 "w_branch_gdn": _stack_from_cols(_matmul(sv["og"], dyg, "tn", f"dw_bg{tag}", [BF])),
                 "w_branch_lru": _stack_from_cols(_matmul(sv["ol"], dyl, "tn", f"dw_bl{tag}", [BF]))})
    dog = _matmul(dyg, p["w_branch_gdn"], "nt", f"d_og{tag}", [F32])
    dol = _matmul(dyl, p["w_branch_lru"], "nt", f"d_ol{tag}", [F32])
    (dxb, dyb, dcw, dcb, dwa, dba, dwx, dbx, dlam) = _lru_bwd(
        sv["proj"], sv["hs"], dol, p["lru_conv_w"], p["lru_conv_b"], p["lru_w_a"], p["lru_b_a"], p["lru_w_x"],
        p["lru_b_x"], _after(p["lru_lambda"], sent), lay.xb_base, f"lru_bwd{tag}")
    small.update(lru_conv_w=dcw, lru_conv_b=dcb.reshape(-1), lru_w_a=dwa, lru_b_a=dba.reshape(-1), lru_w_x=dwx,
                 lru_b_x=dbx.reshape(-1), lru_lambda=dlam.reshape(-1))
    do, dz, g = _gdn_post_bwd(sv["o"], sv["proj"], p["gdn_norm"], dog, lay.z_base, f"gdn_post_bwd{tag}")
    small["gdn_norm"] = g.reshape(-1)
    dqkv_c, dgc_col, dbeta, dgrow = _gdn_chunk_bwd(sv["qkv"], sv["gc"], sv["beta"], sv["grow"], sv["states"], do,
                                                   f"gdn_chunk_bwd{tag}")
    dqkv, dgcw = _conv_silu_bwd(sv["proj"], p["gdn_conv_w"], dqkv_c, f"gdn_conv_bwd{tag}")
    small["gdn_conv_w"] = dgcw
    dgc_row = jnp.pad(dgrow.transpose(0, 2, 1).reshape(s, HEADS), ((0, 0), (0, LANE - HEADS)))
    dab, dal, ddt = _gdn_gates_bwd(sv["proj"], p["gdn_a_log"], p["gdn_dt_bias"], dgc_col, dgc_row, dbeta, lay.ab_base,
                                   f"gdn_gates_bwd{tag}")
    small["gdn_a_log"] = dal[0, :HEADS]
    small["gdn_dt_bias"] = ddt[0, :HEADS]
    dproj = jnp.concatenate([dqkv, dz, dxb, dyb, dgg, dgl, dab], axis=1)
    dh = _matmul(dproj, p["w_in"], "nt", f"d_h{tag}", [F32], tk=3456)
    dx, g = _rms_bwd(sv["x"], p["attn_norm"], dh, dx1, f"rms1_bwd{tag}")
    small["attn_norm"] = g.reshape(-1)
    sent = send_small(small)
    dwin = _matmul(sv["h"], dproj, "tn", f"dw_in{tag}", [BF], tn=1152, after=sent)
    after_last_send(send({"w_in": _stack_from_cols(lay.unpermute(dwin))}))
    return dx


_BIG = ("w_in", "w_branch_gdn", "w_branch_lru", "w_out", "w_up", "w_down")
_ROW_SHARDED = ("w_out", "w_down")
_CONV = ("gdn_conv_w", "lru_conv_w")
_SMALL = ("attn_norm", "gdn_a_log", "gdn_dt_bias", "gdn_norm", "lru_conv_b", "lru_w_a", "lru_b_a", "lru_w_x",
          "lru_b_x", "lru_lambda", "mlp_norm")
_ORDER = ("attn_norm", "w_in", "gdn_conv_w", "gdn_a_log", "gdn_dt_bias", "gdn_norm", "lru_conv_w", "lru_conv_b",
          "lru_w_a", "lru_b_a", "lru_w_x", "lru_b_x", "lru_lambda", "w_branch_gdn", "w_branch_lru", "w_out",
          "mlp_norm", "w_up", "w_down", "final_norm")


def kernel(x, attn_norm, w_in, gdn_conv_w, gdn_a_log, gdn_dt_bias, gdn_norm, lru_conv_w, lru_conv_b, lru_w_a, lru_b_a, lru_w_x, lru_b_x, lru_lambda, w_branch_gdn, w_branch_lru, w_out, mlp_norm, w_up, w_down, final_norm, loss_target, m_attn_norm, m_w_in, m_gdn_conv_w, m_gdn_a_log, m_gdn_dt_bias, m_gdn_norm, m_lru_conv_w, m_lru_conv_b, m_lru_w_a, m_lru_b_a, m_lru_w_x, m_lru_b_x, m_lru_lambda, m_w_branch_gdn, m_w_branch_lru, m_w_out, m_mlp_norm, m_w_up, m_w_down, m_final_norm, v_attn_norm, v_w_in, v_gdn_conv_w, v_gdn_a_log, v_gdn_dt_bias, v_gdn_norm, v_lru_conv_w, v_lru_conv_b, v_lru_w_a, v_lru_b_a, v_lru_w_x, v_lru_b_x, v_lru_lambda, v_w_branch_gdn, v_w_branch_lru, v_w_out, v_mlp_norm, v_w_up, v_w_down, v_final_norm):
    w = dict(attn_norm=attn_norm, w_in=w_in, gdn_conv_w=gdn_conv_w, gdn_a_log=gdn_a_log, gdn_dt_bias=gdn_dt_bias, gdn_norm=gdn_norm, lru_conv_w=lru_conv_w, lru_conv_b=lru_conv_b, lru_w_a=lru_w_a, lru_b_a=lru_b_a, lru_w_x=lru_w_x, lru_b_x=lru_b_x, lru_lambda=lru_lambda, w_branch_gdn=w_branch_gdn, w_branch_lru=w_branch_lru, w_out=w_out, mlp_norm=mlp_norm, w_up=w_up, w_down=w_down, final_norm=final_norm)
    m = dict(attn_norm=m_attn_norm, w_in=m_w_in, gdn_conv_w=m_gdn_conv_w, gdn_a_log=m_gdn_a_log, gdn_dt_bias=m_gdn_dt_bias, gdn_norm=m_gdn_norm, lru_conv_w=m_lru_conv_w, lru_conv_b=m_lru_conv_b, lru_w_a=m_lru_w_a, lru_b_a=m_lru_b_a, lru_w_x=m_lru_w_x, lru_b_x=m_lru_b_x, lru_lambda=m_lru_lambda, w_branch_gdn=m_w_branch_gdn, w_branch_lru=m_w_branch_lru, w_out=m_w_out, mlp_norm=m_mlp_norm, w_up=m_w_up, w_down=m_w_down, final_norm=m_final_norm)
    v = dict(attn_norm=v_attn_norm, w_in=v_w_in, gdn_conv_w=v_gdn_conv_w, gdn_a_log=v_gdn_a_log, gdn_dt_bias=v_gdn_dt_bias, gdn_norm=v_gdn_norm, lru_conv_w=v_lru_conv_w, lru_conv_b=v_lru_conv_b, lru_w_a=v_lru_w_a, lru_b_a=v_lru_b_a, lru_w_x=v_lru_w_x, lru_b_x=v_lru_b_x, lru_lambda=v_lru_lambda, w_branch_gdn=v_w_branch_gdn, w_branch_lru=v_w_branch_lru, w_out=v_w_out, mlp_norm=v_mlp_norm, w_up=v_w_up, w_down=v_w_down, final_norm=v_final_norm)
    depth = w_in.shape[0]
    x = x[0]
    target = loss_target[0]
    s, d = x.shape
    lay = _Layout(d)
    assert lay.cols == w_in.shape[2] * N_DEV
    cx, cy, cc = _place()
    me = 4 * cx + 2 * cy + cc

    me_arr = me.astype(jnp.int32).reshape(1)

    early, mixer, mlp = ("w_in",) + _CONV, ("w_branch_gdn", "w_branch_lru", "w_out"), ("w_up", "w_down")

    def shard(n, l):
        return w[n][l].astype(BF) if n in _BIG else w[n][l]

    def joined(name, stack):
        full = stack.reshape(-1, stack.shape[-1]) if name in _ROW_SHARDED else _cols_from_stack(stack)
        return lay.permute(full) if name == "w_in" else full

    def start_weights(l, names, after):
        flights, token = _send_start([shard(n, l) for n in names], True, after, f"weights_start_l{l}")
        return dict(zip(names, flights)), token

    def wait_weights(l, flights, names, p, after, which):
        got = _send_wait([flights[n] for n in names], after, f"weights_wait_{which}_l{l}")
        for n, (own, landed) in zip(names, got):
            p[n] = joined(n, lax.dynamic_update_slice_in_dim(landed, own[None], me, axis=0))

    saved, params = [], []
    xl = x
    first = dict(zip(early, _all_gather([shard(n, 0) for n in early], "gather_first_weights")))
    flights, token = start_weights(0, mixer + mlp, first["w_in"])
    for l in range(depth):
        p = {n: w[n][l] for n in _SMALL}
        if l == 0:
            p.update({n: joined(n, g) for n, g in first.items()})
        else:
            wait_weights(l, flights, early, p, xl, "early")
        mine = flights
        if l + 1 < depth:
            flights, token = start_weights(l + 1, early + mixer + mlp, token)
        p["attn_norm"] = _after(p["attn_norm"], token)
        late = [functools.partial(wait_weights, l, mine, names, p, which=which)
                for names, which in ((mixer, "mixer"), (mlp, "mlp"))]
        xl, sv = _layer_fwd(xl, p, late, lay, f"_l{l}")
        saved.append(sv)
        params.append(p)
    dx, loss_row, dfinal = _loss_head(xl, final_norm, target, "loss_head")
    loss = lax.psum(loss_row[0, 0], ("x", "y", "c"))

    out = {n: None for n in _BIG}
    pending = {l: {} for l in range(depth)}

    def sender(l):
        def send(stacks):
            names = tuple(stacks)
            fl, token = _send_start([stacks[n] for n in names], False, me_arr, f"grads_start_{'_'.join(names)}_l{l}")
            pending[l].update(zip(names, fl))
            return token
        return send

    def finish(l, names, after, which):
        got = _send_wait([pending[l][n] for n in names], after, f"grads_wait_{which}_l{l}")
        for n, (mine, landed) in zip(names, got):
            out[n] = _adamw_sharded(mine, landed, me_arr, w[n], m[n], v[n], l, out[n], after, f"adamw_{n}_l{l}")
            after = out[n][0]
        return after

    small_names = _SMALL + _CONV
    small_sent = {}

    def small_sender(l):
        def send_small(small):
            arrays = [small[n] for n in small_names] + ([dfinal.reshape(-1)] if l == depth - 1 else [])
            flights, token = _send_start([_pack(arrays)], True, me_arr, f"small_grads_start_l{l}")
            small_sent[l] = (flights[0], arrays)
            return token
        return send_small

    last = [None]

    def hook_for(l):
        def hook(token):
            last[0] = finish(l + 1, _BIG, token, "all") if l + 1 < depth else token
        return hook

    for l in reversed(range(depth)):
        dx = _layer_bwd(dx, params[l], saved[l], sender(l), small_sender(l), hook_for(l), last[0], lay, f"_l{l}")

    last[0] = finish(0, mixer + mlp, last[0], "late")
    got = _send_wait([small_sent[l][0] for l in range(depth)], last[0], "small_grads_wait")
    per_layer = []
    for l, (part, landed) in enumerate(got):
        total = _sum_devices(lax.dynamic_update_slice_in_dim(landed, part[None], me, axis=0), f"sum_small_grads_l{l}")
        per_layer.append(_unpack(total, small_sent[l][1]))
    grads = {n: jnp.stack([per_layer[l][i] for l in range(depth)]) for i, n in enumerate(small_names)}
    grads["final_norm"] = per_layer[depth - 1][len(small_names)]
    for n in _CONV:
        blk = w[n].shape[2]
        grads[n] = lax.dynamic_slice_in_dim(grads[n], me * blk, blk, axis=2)
    names = small_names + ("final_norm",)
    gp = _pack([grads[n] for n in names])
    upd = _adamw_packed(_pack([w[n] for n in names]), gp, _pack([m[n] for n in names]), _pack([v[n] for n in names]),
                        "adamw_small")
    like = [w[n] for n in names]
    unp = [_unpack(u, like) for u in upd]
    for i, n in enumerate(names):
        out[n] = (grads[n], unp[0][i], unp[1][i], unp[2][i])
    finish(0, ("w_in",), upd[0], "w_in")

    res = [loss, dx.reshape(1, s, d)]
    for k in range(4):
        res += [out[n][k] for n in _ORDER]
    return tuple(res)
```

```python
import functools

import jax
import jax.numpy as jnp
from jax import lax
from jax.experimental import pallas as pl
from jax.experimental.pallas import tpu as pltpu

F32 = jnp.float32
BF = jnp.bfloat16
MESH = pl.DeviceIdType.MESH
N_DEV = 8
N_CHIP = 4

HEADS = 8
HEAD_DIM = 128
GDN_WIDTH = HEADS * HEAD_DIM
CHUNK = 64
CONV_WIDTH = 4
LRU_WIDTH = 1024
LRU_BLOCKS = 8
LRU_BLOCK_DIM = 128
LRU_C = 8.0
RMS_EPS = 1e-6
L2_EPS = 1e-6
ADAM_LR = 0.001
ADAM_B1 = 0.9
ADAM_B2 = 0.999
ADAM_EPS = 1e-08
ADAM_WD = 0.01
ADAM_STEP = 10

LANE = 128
VMEM_LIMIT = 56 * 1024 * 1024


def _cparams(sem=None):
    return pltpu.CompilerParams(dimension_semantics=sem, vmem_limit_bytes=VMEM_LIMIT)


def _tile(n, pref, unit=LANE):
    if n <= pref:
        return n
    best = None
    for t in range(unit, pref + 1, unit):
        if n % t == 0:
            best = t
    assert best is not None, (n, pref, unit)
    return best


_NN = (((1,), (0,)), ((), ()))
_NT = (((1,), (1,)), ((), ()))
_TN = (((0,), (0,)), ((), ()))


def _dg(a, b, dims):
    return lax.dot_general(a.astype(BF), b.astype(BF), dims, preferred_element_type=F32)


class _MM:
    def __init__(self, nn, nt, tn):
        self.nn, self.nt, self.tn = nn, nt, tn


def _make_mm(nn, nt, tn):
    def p_nn(a, b):
        return _dg(a, b, nn)

    def p_nt(a, b):
        return _dg(a, b, nt)

    def p_tn(a, b):
        return _dg(a, b, tn)

    d_nn, d_nt, d_tn = jax.custom_vjp(p_nn), jax.custom_vjp(p_nt), jax.custom_vjp(p_tn)

    def save(f):
        return lambda a, b: (f(a, b), (a, b))

    d_nn.defvjp(save(d_nn), lambda r, g: (d_nt(g, r[1]), d_tn(r[0], g)))
    d_nt.defvjp(save(d_nt), lambda r, g: (d_nn(g, r[1]), d_tn(g, r[0])))
    d_tn.defvjp(save(d_tn), lambda r, g: (d_nt(r[1], g), d_nn(r[0], g)))
    return _MM(p_nn, p_nt, p_tn), _MM(d_nn, d_nt, d_tn)


_BNN = (((2,), (1,)), ((0,), (0,)))
_BNT = (((2,), (2,)), ((0,), (0,)))
_BTN = (((1,), (1,)), ((0,), (0,)))

_PLAIN, _DIFF = _make_mm(_NN, _NT, _TN)
_PLAIN_B, _DIFF_B = _make_mm(_BNN, _BNT, _BTN)


def _sigmoid(x):
    return 1.0 / (1.0 + jnp.exp(-x))


def _silu(x):
    return x * _sigmoid(x)


def _softplus(x):
    return jnp.maximum(x, 0.0) + jnp.log(1.0 + jnp.exp(-jnp.maximum(x, -x)))


def _gelu(x):
    return 0.5 * x * (1.0 + jnp.tanh(0.7978845608028654 * (x + 0.044715 * (x * x * x))))


def _expm1(x):
    s = x * (1.0 + x * (1.0 / 2.0) * (1.0 + x * (1.0 / 3.0) * (1.0 + x * (1.0 / 4.0) * (
        1.0 + x * (1.0 / 5.0) * (1.0 + x * (1.0 / 6.0) * (1.0 + x * (1.0 / 7.0)))))))
    return jnp.where(jnp.maximum(x, -x) < 0.3, s, jnp.exp(x) - 1.0)


def _rms(x, gain):
    ms = jnp.mean(x * x, axis=-1, keepdims=True)
    return x * lax.rsqrt(ms + RMS_EPS) * gain


def _l2n(x):
    return x * lax.rsqrt(jnp.sum(x * x, axis=-1, keepdims=True) + L2_EPS)


def _shift_down(x, r):
    if r == 0:
        return x
    row = lax.broadcasted_iota(jnp.int32, x.shape, 0)
    return jnp.where(row >= r, pltpu.roll(x, r, 0), 0.0)


def _shift_up(x, r):
    if r == 0:
        return x
    n = x.shape[0]
    row = lax.broadcasted_iota(jnp.int32, x.shape, 0)
    return jnp.where(row < n - r, pltpu.roll(x, n - r, 0), 0.0)


def _conv(x, w_ref):
    y = None
    for j in range(CONV_WIDTH):
        t = _shift_down(x, CONV_WIDTH - 1 - j) * w_ref[j:j + 1, :]
        y = t if y is None else y + t
    return y


def _conv_bwd(x, dy, w_ref):
    dx = None
    dws = []
    for j in range(CONV_WIDTH):
        r = CONV_WIDTH - 1 - j
        t = _shift_up(dy, r) * w_ref[j:j + 1, :]
        dx = t if dx is None else dx + t
        dws.append(jnp.sum(dy * _shift_down(x, r), axis=0, keepdims=True))
    return dx, dws


def _scan_tile(a, b, reverse):
    n = a.shape[0]
    row = lax.broadcasted_iota(jnp.int32, a.shape, 0)
    s = 1
    while s < n:
        if reverse:
            keep = row < n - s
            a_sh = jnp.where(keep, pltpu.roll(a, n - s, 0), 1.0)
            b_sh = jnp.where(keep, pltpu.roll(b, n - s, 0), 0.0)
        else:
            keep = row >= s
            a_sh = jnp.where(keep, pltpu.roll(a, s, 0), 1.0)
            b_sh = jnp.where(keep, pltpu.roll(b, s, 0), 0.0)
        b = a * b_sh + b
        a = a * a_sh
        s *= 2
    return a, b


def _dg3(a, b, dims):
    a_hi = a.astype(BF)
    b_hi = b.astype(BF)
    a_lo = a - a_hi.astype(F32)
    b_lo = b - b_hi.astype(F32)
    return _dg(a_hi, b_hi, dims) + (_dg(a_hi, b_lo, dims) + _dg(a_lo, b_hi, dims))


@jax.custom_vjp
def _tri_inv(a):
    n = a.shape[1]
    p = _dg3(a, a, _BNN)
    r = p
    e = 2
    while 2 * e < n:
        p = _dg3(p, p, _BNN)
        r = r + p + _dg3(r, p, _BNN)
        e *= 2
    row = lax.broadcasted_iota(jnp.int32, (1, n, n), 1)
    col = lax.broadcasted_iota(jnp.int32, (1, n, n), 2)
    eye = jnp.where(row == col, 1.0, 0.0)
    return eye - a + r - _dg3(a, r, _BNN)


def _tri_inv_fwd(a):
    t = _tri_inv(a)
    return t, t


def _tri_inv_bwd(t, g):
    return (-_dg3(_dg3(t, g, _BTN), t, _BNT),)


_tri_inv.defvjp(_tri_inv_fwd, _tri_inv_bwd)


def _gdn_chunk(q, k, v, beta, gcol, grow, state, mm):
    c = q.shape[1]
    row = lax.broadcasted_iota(jnp.int32, (1, c, c), 1)
    col = lax.broadcasted_iota(jnp.int32, (1, c, c), 2)
    causal = row >= col
    strict = row > col
    qn = _l2n(q) * (HEAD_DIM ** -0.5)
    kn = _l2n(k)
    dec = jnp.where(causal, jnp.exp(jnp.where(causal, gcol - grow, 0.0)), 0.0)
    kb = kn * beta
    vb = v * beta
    a = jnp.where(strict, mm.nt(kb, kn) * dec, 0.0)
    t = _tri_inv(a)
    eg = jnp.exp(gcol)
    u = mm.nn(t, vb)
    w = mm.nn(t, kb * eg)
    p = jnp.where(causal, mm.nt(qn, kn) * dec, 0.0)
    vn = u - mm.nn(w, state)
    o = mm.nn(qn * eg, state) + mm.nn(p, vn)
    last = lax.broadcasted_iota(jnp.int32, (1, c, 1), 1) == c - 1
    gl = jnp.sum(jnp.where(last, gcol, 0.0), axis=1, keepdims=True)
    kd = kn * jnp.exp(gl - gcol)
    new_state = state * jnp.exp(gl) + mm.tn(kd, vn)
    return o, new_state


def _lru_gates(xc, wa, ba, wx, bx, lam, mm):
    r = _sigmoid(mm.nn(xc, wa) + ba)
    i = _sigmoid(mm.nn(xc, wx) + bx)
    log_a = -LRU_C * r * _softplus(-lam)
    a = jnp.exp(log_a)
    bterm = jnp.sqrt(-_expm1(2.0 * log_a)) * (i * xc)
    return a, bterm


def _matmul(a, b, mode, name, out_dtypes, epi=None, extras=(), tm=1024, tn=1024, tk=2048, after=None,
            shards=False):
    if shards:
        nd, rows, per = b.shape if mode != "tn" else (N_DEV, a.shape[1], b.shape[1] // N_DEV)
        b_shape = b.shape if mode == "tn" else (rows, nd * per)
    else:
        b_shape = b.shape
    if mode == "nn":
        (m, k), (k2, n) = a.shape, b_shape
    elif mode == "nt":
        (m, k), (n, k2) = a.shape, b_shape
    else:
        (k, m), (k2, n) = a.shape, b_shape
    assert k == k2, (a.shape, b.shape, mode)
    if shards:
        tn, tk = (min(tn, per), tk) if mode != "nt" else (tn, min(tk, per))
    tm, tn, tk = _tile(m, tm), _tile(n, tn), _tile(k, tk)
    nk = k // tk
    dims = {"nn": _NN, "nt": _NT, "tn": _TN}[mode]
    if mode == "nn":
        a_spec = pl.BlockSpec((tm, tk), lambda i, j, kk: (i, kk))
        b_spec = pl.BlockSpec((tk, tn), lambda i, j, kk: (kk, j))
        if shards:
            assert per % tn == 0
            b_spec = pl.BlockSpec((None, tk, tn), lambda i, j, kk: (j * tn // per, kk, j * tn % per // tn))
    elif mode == "nt":
        a_spec = pl.BlockSpec((tm, tk), lambda i, j, kk: (i, kk))
        b_spec = pl.BlockSpec((tn, tk), lambda i, j, kk: (j, kk))
        if shards:
            assert per % tk == 0
            b_spec = pl.BlockSpec((None, tn, tk), lambda i, j, kk: (kk * tk // per, j, kk * tk % per // tk))
    else:
        a_spec = pl.BlockSpec((tk, tm), lambda i, j, kk: (kk, i))
        b_spec = pl.BlockSpec((tk, tn), lambda i, j, kk: (kk, j))
    o_spec = pl.BlockSpec((tm, tn), lambda i, j, kk: (i, j))
    out_spec, out_dims = o_spec, (m, n)
    if shards and mode == "tn":
        assert per % tn == 0 and not extras
        out_spec = pl.BlockSpec((None, tm, tn), lambda i, j, kk: (j * tn // per, i, j * tn % per // tn))
        out_dims = (N_DEV, m, per)
    n_ex, n_out = len(extras), len(out_dtypes)
    order = [] if after is None else [after]
    n_in = 2 + n_ex + len(order)
    if epi is None:
        epi = lambda acc: (acc,)

    def body(*refs):
        a_ref, b_ref = refs[0], refs[1]
        ex_refs = refs[2:2 + n_ex]
        out_refs = refs[n_in:n_in + n_out]
        kk = pl.program_id(2)
        part = _dg(a_ref[...], b_ref[...], dims)

        def finish(total):
            res = epi(total, *[r[...] for r in ex_refs])
            for r, v in zip(out_refs, res):
                r[...] = v.astype(r.dtype)

        if nk == 1:
            finish(part)
            return
        acc_ref = refs[-1]

        @pl.when(kk == 0)
        def _():
            acc_ref[...] = part

        @pl.when(jnp.logical_and(kk > 0, kk < nk - 1))
        def _():
            acc_ref[...] += part

        @pl.when(kk == nk - 1)
        def _():
            finish(acc_ref[...] + part)

    outs = pl.pallas_call(
        body, name=name, grid=(m // tm, n // tn, nk),
        in_specs=[a_spec, b_spec] + [o_spec] * n_ex + [pl.BlockSpec(memory_space=pl.ANY)] * len(order),
        out_specs=[out_spec] * n_out,
        out_shape=[jax.ShapeDtypeStruct(out_dims, d) for d in out_dtypes],
        scratch_shapes=[pltpu.VMEM((tm, tn), F32)] if nk > 1 else [],
        compiler_params=_cparams(("parallel", "parallel", "arbitrary")),
    )(a, b, *extras, *order)
    return outs[0] if n_out == 1 else outs


def _rowwise(fn, name, rows, tr, row_ins, params, row_outs, acc_outs=(), ncol=1):
    tr = _tile(rows, tr, 8)
    n_in, n_par, n_ro, n_acc = len(row_ins), len(params), len(row_outs), len(acc_outs)
    in_specs = []
    for _, width, base, per_col in row_ins:
        if per_col:
            in_specs.append(pl.BlockSpec((tr, width), lambda i, j, base=base: (i, base + j)))
        else:
            in_specs.append(pl.BlockSpec((tr, width), lambda i, j, base=base: (i, base)))
    for p in params:
        in_specs.append(pl.BlockSpec(p.shape, lambda i, j: (0, 0)))
    out_specs, out_shape = [], []
    for total, dtype, width, per_col in row_outs:
        if per_col:
            out_specs.append(pl.BlockSpec((tr, width), lambda i, j: (i, j)))
        else:
            out_specs.append(pl.BlockSpec((tr, width), lambda i, j: (i, 0)))
        out_shape.append(jax.ShapeDtypeStruct((rows, total), dtype))
    for shape, dtype in acc_outs:
        out_specs.append(pl.BlockSpec(shape, lambda i, j: (0, 0)))
        out_shape.append(jax.ShapeDtypeStruct(shape, dtype))

    def body(*refs):
        ins = [r[...] for r in refs[:n_in + n_par]]
        ro = refs[n_in + n_par:n_in + n_par + n_ro]
        ao = refs[n_in + n_par + n_ro:]
        res = fn(*ins)
        for r, v in zip(ro, res[:n_ro]):
            r[...] = v.astype(r.dtype)
        first = jnp.logical_and(pl.program_id(0) == 0, pl.program_id(1) == 0)

        @pl.when(first)
        def _():
            for r in ao:
                r[...] = jnp.zeros_like(r)

        for r, v in zip(ao, res[n_ro:]):
            r[...] += v.astype(r.dtype)

    sem = ("arbitrary", "arbitrary") if n_acc else ("parallel", "parallel")
    outs = pl.pallas_call(
        body, name=name, grid=(rows // tr, ncol), in_specs=in_specs, out_specs=out_specs,
        out_shape=out_shape, compiler_params=_cparams(sem),
    )(*[r[0] for r in row_ins], *params)
    return outs


def _row(x):
    return x.reshape(1, -1)


def _rms_fwd(x, gain, name):
    s, d = x.shape
    return _rowwise(lambda xt, g: (_rms(xt, g),), name, s, 512, [(x, d, 0, False)], [_row(gain)],
                    [(d, BF, d, False)])[0]


def _rms_bwd(x, gain, dh, dres, name):
    s, d = x.shape

    def fn(xt, dht, drt, g):
        _, vjp = jax.vjp(_rms, xt, g)
        dx, dg = vjp(dht)
        return drt + dx, dg

    return _rowwise(fn, name, s, 256, [(x, d, 0, False), (dh, d, 0, False), (dres, d, 0, False)], [_row(gain)],
                    [(d, F32, d, False)], [((1, d), F32)])


def _loss_head(x, gain, target, name):
    s, d = x.shape

    def fn(xt, tt, g):
        y, vjp = jax.vjp(_rms, xt, g)
        err = y - tt
        dx, dg = vjp(err * (1.0 / d))
        part = 0.5 * jnp.sum(jnp.sum(err * err, axis=1, keepdims=True), axis=0, keepdims=True) * (1.0 / d)
        return dx, jnp.broadcast_to(part, (1, LANE)), dg

    return _rowwise(fn, name, s, 256, [(x, d, 0, False), (target, d, 0, False)], [_row(gain)],
                    [(d, F32, d, False)], [((1, LANE), F32), ((1, d), F32)])


def _gdn_post_fn(o, z, g):
    return _rms(o, g) * _silu(z)


def _gdn_post_fwd(o, proj, gain, z_base, name):
    s = o.shape[0]
    return _rowwise(lambda ot, zt, g: (_gdn_post_fn(ot, zt, g),), name, s, 1024,
                    [(o, HEAD_DIM, 0, True), (proj, HEAD_DIM, z_base, True)], [_row(gain)],
                    [(GDN_WIDTH, BF, HEAD_DIM, True)], ncol=HEADS)[0]


def _gdn_post_bwd(o, proj, gain, dog, z_base, name):
    s = o.shape[0]

    def fn(ot, zt, dt, g):
        _, vjp = jax.vjp(_gdn_post_fn, ot, zt, g)
        return vjp(dt)

    return _rowwise(fn, name, s, 1024,
                    [(o, HEAD_DIM, 0, True), (proj, HEAD_DIM, z_base, True), (dog, HEAD_DIM, 0, True)],
                    [_row(gain)], [(GDN_WIDTH, F32, HEAD_DIM, True), (GDN_WIDTH, BF, HEAD_DIM, True)],
                    [((1, HEAD_DIM), F32)], ncol=HEADS)


def _merge_fn(gg, gl, yg, yl):
    return _sigmoid(gg) * yg + _sigmoid(gl) * yl


def _merge_fwd(proj, yg, yl, gg_base, name):
    s, d = yg.shape
    return _rowwise(lambda a, b, c, e: (_merge_fn(a, b, c, e),), name, s, 256,
                    [(proj, d, gg_base, False), (proj, d, gg_base + 1, False), (yg, d, 0, False), (yl, d, 0, False)],
                    [], [(d, BF, d, False)])[0]


def _merge_bwd(proj, yg, yl, dmerged, gg_base, name):
    s, d = yg.shape

    def fn(a, b, c, e, dm):
        _, vjp = jax.vjp(_merge_fn, a, b, c, e)
        return vjp(dm)

    return _rowwise(fn, name, s, 128,
                    [(proj, d, gg_base, False), (proj, d, gg_base + 1, False), (yg, d, 0, False), (yl, d, 0, False),
                     (dmerged, d, 0, False)], [], [(d, BF, d, False)] * 4)


def _conv_silu_fwd(proj, w, name):
    s = proj.shape[0]
    cols = w.shape[1]
    cw = _tile(cols, LANE)

    def body(x_ref, w_ref, o_ref):
        o_ref[...] = _silu(_conv(x_ref[...], w_ref))

    return pl.pallas_call(
        body, name=name, grid=(cols // cw,),
        in_specs=[pl.BlockSpec((s, cw), lambda j: (0, j)), pl.BlockSpec((CONV_WIDTH, cw), lambda j: (0, j))],
        out_specs=pl.BlockSpec((s, cw), lambda j: (0, j)),
        out_shape=jax.ShapeDtypeStruct((s, cols), F32), compiler_params=_cparams(("parallel",)),
    )(proj, w)


def _conv_silu_bwd(proj, w, dout, name):
    s = proj.shape[0]
    cols = w.shape[1]
    cw = _tile(cols, LANE)

    def body(x_ref, w_ref, d_ref, dx_ref, dw_ref):
        x = x_ref[...]
        y = _conv(x, w_ref)
        sg = _sigmoid(y)
        dy = d_ref[...] * (sg * (1.0 + y * (1.0 - sg)))
        dx, dws = _conv_bwd(x, dy, w_ref)
        dx_ref[...] = dx.astype(dx_ref.dtype)
        for j in range(CONV_WIDTH):
            dw_ref[j:j + 1, :] = dws[j]

    return pl.pallas_call(
        body, name=name, grid=(cols // cw,),
        in_specs=[pl.BlockSpec((s, cw), lambda j: (0, j)), pl.BlockSpec((CONV_WIDTH, cw), lambda j: (0, j)),
                  pl.BlockSpec((s, cw), lambda j: (0, j))],
        out_specs=[pl.BlockSpec((s, cw), lambda j: (0, j)), pl.BlockSpec((CONV_WIDTH, cw), lambda j: (0, j))],
        out_shape=[jax.ShapeDtypeStruct((s, cols), BF), jax.ShapeDtypeStruct((CONV_WIDTH, cols), F32)],
        compiler_params=_cparams(("parallel",)),
    )(proj, w, dout)


def _cumsum_mask(tr, transpose):
    row = lax.broadcasted_iota(jnp.int32, (tr, tr), 0)
    col = lax.broadcasted_iota(jnp.int32, (tr, tr), 1)
    same = (row // CHUNK) == (col // CHUNK)
    tri = (row <= col) if transpose else (row >= col)
    return jnp.where(jnp.logical_and(same, tri), 1.0, 0.0)


def _lane_pad(v):
    return jnp.pad(v.reshape(1, -1), ((0, 0), (0, LANE - v.shape[0])))


def _gdn_gates_fwd(proj, a_log, dt_bias, ab_base, name):
    s = proj.shape[0]

    def fn(ab, al, dt):
        g = -jnp.exp(al) * _softplus(ab + dt)
        beta = _sigmoid(pltpu.roll(ab, LANE - HEADS, 1))
        gc = jnp.dot(_cumsum_mask(ab.shape[0], False), g, precision=lax.Precision.HIGHEST,
                     preferred_element_type=F32)
        return gc, beta

    return _rowwise(fn, name, s, 512, [(proj, LANE, ab_base, False)], [_lane_pad(a_log), _lane_pad(dt_bias)],
                    [(LANE, F32, LANE, False)] * 2)


def _gdn_gates_bwd(proj, a_log, dt_bias, dgc_col, dgc_row, dbeta, ab_base, name):
    s = proj.shape[0]

    def fn(ab, dc, dr, db, al, dt):
        lane = lax.broadcasted_iota(jnp.int32, ab.shape, 1)
        dg = jnp.dot(_cumsum_mask(ab.shape[0], True), dc + dr, precision=lax.Precision.HIGHEST,
                     preferred_element_type=F32)
        ea = jnp.exp(al)
        pre = ab + dt
        g = -ea * _softplus(pre)
        da = jnp.where(lane < HEADS, dg * (-ea) * _sigmoid(pre), 0.0)
        dal = jnp.sum(jnp.where(lane < HEADS, dg * g, 0.0), axis=0, keepdims=True)
        ddt = jnp.sum(da, axis=0, keepdims=True)
        beta = _sigmoid(pltpu.roll(ab, LANE - HEADS, 1))
        dbl = jnp.where(lane < HEADS, db * beta * (1.0 - beta), 0.0)
        dab = da + pltpu.roll(dbl, HEADS, 1)
        return dab, dal, ddt

    return _rowwise(fn, name, s, 512,
                    [(proj, LANE, ab_base, False), (dgc_col, LANE, 0, False), (dgc_row, LANE, 0, False),
                     (dbeta, LANE, 0, False)], [_lane_pad(a_log), _lane_pad(dt_bias)],
                    [(LANE, BF, LANE, False)], [((1, LANE), F32), ((1, LANE), F32)])


def _head_col(blk, h):
    lane = lax.broadcasted_iota(jnp.int32, blk.shape, 1)
    return jnp.sum(jnp.where(lane == h, blk, 0.0), axis=1, keepdims=True)


def _chunk_operands(qkv_ref, gc_ref, b_ref, gr_ref):
    heads = range(HEADS)
    q, k, v = (jnp.stack([qkv_ref[:, (j * HEADS + h) * HEAD_DIM:(j * HEADS + h + 1) * HEAD_DIM] for h in heads])
               for j in range(3))
    gcs, betas = gc_ref[...], b_ref[...]
    beta = jnp.stack([_head_col(betas, h) for h in heads])
    gcol = jnp.stack([_head_col(gcs, h) for h in heads])
    grow = jnp.stack([gr_ref[0, h:h + 1, :] for h in heads])
    return q, k, v, beta, gcol, grow


def _gdn_chunk_fwd(qkv, gc, beta, grow, name):
    s = qkv.shape[0]
    nc = s // CHUNK

    def body(qkv_ref, gc_ref, b_ref, gr_ref, o_ref, st_ref, state):
        @pl.when(pl.program_id(0) == 0)
        def _():
            state[...] = jnp.zeros_like(state)

        st = state[...]
        st_ref[0] = st
        o, new = _gdn_chunk(*_chunk_operands(qkv_ref, gc_ref, b_ref, gr_ref), st, _PLAIN_B)
        for h in range(HEADS):
            o_ref[:, h * HEAD_DIM:(h + 1) * HEAD_DIM] = o[h]
        state[...] = new

    return pl.pallas_call(
        body, name=name, grid=(nc,),
        in_specs=[pl.BlockSpec((CHUNK, 3 * GDN_WIDTH), lambda c: (c, 0)),
                  pl.BlockSpec((CHUNK, LANE), lambda c: (c, 0)), pl.BlockSpec((CHUNK, LANE), lambda c: (c, 0)),
                  pl.BlockSpec((1, HEADS, CHUNK), lambda c: (c, 0, 0))],
        out_specs=[pl.BlockSpec((CHUNK, GDN_WIDTH), lambda c: (c, 0)),
                   pl.BlockSpec((1, HEADS, HEAD_DIM, HEAD_DIM), lambda c: (c, 0, 0, 0))],
        out_shape=[jax.ShapeDtypeStruct((s, GDN_WIDTH), F32),
                   jax.ShapeDtypeStruct((nc, HEADS, HEAD_DIM, HEAD_DIM), F32)],
        scratch_shapes=[pltpu.VMEM((HEADS, HEAD_DIM, HEAD_DIM), F32)],
        compiler_params=_cparams(("arbitrary",)),
    )(qkv, gc, beta, grow)


def _gdn_chunk_bwd(qkv, gc, beta, grow, states, do, name):
    s = qkv.shape[0]
    nc = s // CHUNK

    def body(qkv_ref, gc_ref, b_ref, gr_ref, st_ref, do_ref, dqkv_ref, dgc_ref, db_ref, dgr_ref, dstate):
        @pl.when(pl.program_id(0) == 0)
        def _():
            dstate[...] = jnp.zeros_like(dstate)

        lane = lax.broadcasted_iota(jnp.int32, (CHUNK, LANE), 1)
        _, vjp = jax.vjp(functools.partial(_gdn_chunk, mm=_DIFF_B), *_chunk_operands(qkv_ref, gc_ref, b_ref, gr_ref),
                         st_ref[0])
        do = jnp.stack([do_ref[:, h * HEAD_DIM:(h + 1) * HEAD_DIM] for h in range(HEADS)])
        dq, dk, dv, dbeta, dgcol, dgrow, dst = vjp((do, dstate[...]))
        dstate[...] = dst
        dgc = jnp.zeros((CHUNK, LANE), F32)
        db = jnp.zeros((CHUNK, LANE), F32)
        for h in range(HEADS):
            for j, dx in enumerate((dq, dk, dv)):
                dqkv_ref[:, (j * HEADS + h) * HEAD_DIM:(j * HEADS + h + 1) * HEAD_DIM] = dx[h]
            dgc = dgc + jnp.where(lane == h, dgcol[h], 0.0)
            db = db + jnp.where(lane == h, dbeta[h], 0.0)
            dgr_ref[0, h:h + 1, :] = dgrow[h]
        dgc_ref[...] = dgc
        db_ref[...] = db

    rev = lambda c: nc - 1 - c
    wide = pl.BlockSpec((CHUNK, 3 * GDN_WIDTH), lambda c: (rev(c), 0))
    lanes = pl.BlockSpec((CHUNK, LANE), lambda c: (rev(c), 0))
    rows = pl.BlockSpec((1, HEADS, CHUNK), lambda c: (rev(c), 0, 0))
    return pl.pallas_call(
        body, name=name, grid=(nc,),
        in_specs=[wide, lanes, lanes, rows,
                  pl.BlockSpec((1, HEADS, HEAD_DIM, HEAD_DIM), lambda c: (rev(c), 0, 0, 0)),
                  pl.BlockSpec((CHUNK, GDN_WIDTH), lambda c: (rev(c), 0))],
        out_specs=[wide, lanes, lanes, rows],
        out_shape=[jax.ShapeDtypeStruct((s, 3 * GDN_WIDTH), F32)]
        + [jax.ShapeDtypeStruct((s, LANE), F32)] * 2 + [jax.ShapeDtypeStruct((nc, HEADS, CHUNK), F32)],
        scratch_shapes=[pltpu.VMEM((HEADS, HEAD_DIM, HEAD_DIM), F32)],
        compiler_params=_cparams(("arbitrary",)),
    )(qkv, gc, beta, grow, states, do)


def _lru_specs(s, xb_base):
    col = lambda base: pl.BlockSpec((s, LRU_BLOCK_DIM), lambda n, base=base: (0, base + n))
    vec = pl.BlockSpec((1, LRU_BLOCK_DIM), lambda n: (0, n))
    mat = pl.BlockSpec((None, LRU_BLOCK_DIM, LRU_BLOCK_DIM), lambda n: (n, 0, 0))
    cw = pl.BlockSpec((CONV_WIDTH, LRU_BLOCK_DIM), lambda n: (0, n))
    return col, vec, mat, cw


def _lru_fwd(proj, cw, cb, wa, ba, wx, bx, lam, xb_base, name):
    s = proj.shape[0]
    tt = _tile(s, 256, 8)
    col, vec, mat, cws = _lru_specs(s, xb_base)

    def body(xb_ref, yb_ref, cw_ref, cb_ref, wa_ref, ba_ref, wx_ref, bx_ref, lam_ref, o_ref, h_ref, xc_s):
        xc_s[...] = _conv(xb_ref[...], cw_ref) + cb_ref[...]
        par = (wa_ref[...], ba_ref[...], wx_ref[...], bx_ref[...], lam_ref[...])

        def step(t, carry):
            sl = pl.ds(pl.multiple_of(t * tt, tt), tt)
            a, bt = _lru_gates(xc_s[sl, :], *par, _PLAIN)
            aa, bb = _scan_tile(a, bt, False)
            h = aa * carry + bb
            h_ref[sl, :] = h
            o_ref[sl, :] = (h * _gelu(yb_ref[sl, :])).astype(o_ref.dtype)
            return h_ref[pl.ds(t * tt + tt - 1, 1), :]

        lax.fori_loop(0, s // tt, step, jnp.zeros((1, LRU_BLOCK_DIM), F32))

    nb = LRU_BLOCKS
    return pl.pallas_call(
        body, name=name, grid=(LRU_BLOCKS,),
        in_specs=[col(xb_base), col(xb_base + nb), cws, vec, mat, vec, mat, vec, vec],
        out_specs=[col(0), col(0)],
        out_shape=[jax.ShapeDtypeStruct((s, LRU_WIDTH), BF), jax.ShapeDtypeStruct((s, LRU_WIDTH), F32)],
        scratch_shapes=[pltpu.VMEM((s, LRU_BLOCK_DIM), F32)],
        compiler_params=_cparams(("parallel",)),
    )(proj, proj, cw, _row(cb), wa, _row(ba), wx, _row(bx), _row(lam))


def _lru_bwd(proj, hs, dol, cw, cb, wa, ba, wx, bx, lam, xb_base, name):
    s = proj.shape[0]
    tt = _tile(s, 256, 8)
    nt = s // tt
    col, vec, mat, cws = _lru_specs(s, xb_base)

    def body(xb_ref, yb_ref, h_ref, d_ref, cw_ref, cb_ref, wa_ref, ba_ref, wx_ref, bx_ref, lam_ref,
             dxb_ref, dyb_ref, dcw_ref, dcb_ref, dwa_ref, dba_ref, dwx_ref, dbx_ref, dlam_ref,
             xc_s, a_s, dh_s, dxc_s):
        xc_s[...] = _conv(xb_ref[...], cw_ref) + cb_ref[...]
        par = (wa_ref[...], ba_ref[...], wx_ref[...], bx_ref[...], lam_ref[...])
        row = lax.broadcasted_iota(jnp.int32, (tt, LRU_BLOCK_DIM), 0)
        tile = lambda t: pl.ds(pl.multiple_of(t * tt, tt), tt)

        def prep(t, carry):
            sl = tile(t)
            a_s[sl, :] = _lru_gates(xc_s[sl, :], *par, _PLAIN)[0]
            d = d_ref[sl, :]
            gy, vjp_y = jax.vjp(_gelu, yb_ref[sl, :])
            dyb_ref[sl, :] = vjp_y(d * h_ref[sl, :])[0].astype(dyb_ref.dtype)
            dh_s[sl, :] = d * gy
            return carry

        lax.fori_loop(0, nt, prep, 0)

        def rscan(i, carry):
            dh_next, a_next = carry
            t = nt - 1 - i
            sl = tile(t)
            a_sh = jnp.where(row == tt - 1, a_next, pltpu.roll(a_s[sl, :], tt - 1, 0))
            aa, bb = _scan_tile(a_sh, dh_s[sl, :], True)
            dh_s[sl, :] = aa * dh_next + bb
            first = pl.ds(t * tt, 1)
            return dh_s[first, :], a_s[first, :]

        zero = jnp.zeros((1, LRU_BLOCK_DIM), F32)
        lax.fori_loop(0, nt, rscan, (zero, zero))

        def gates_vjp(t, acc):
            sl = tile(t)
            _, vjp_g = jax.vjp(functools.partial(_lru_gates, mm=_DIFF), xc_s[sl, :], *par)
            dh = dh_s[sl, :]
            before = jnp.where(t > 0, h_ref[pl.ds(jnp.maximum(t * tt - 1, 0), 1), :], 0.0)
            h_prev = jnp.where(row == 0, before, pltpu.roll(h_ref[sl, :], 1, 0))
            dxc, *dpar = vjp_g((dh * h_prev, dh))
            dxc_s[sl, :] = dxc
            return tuple(x + y for x, y in zip(acc, dpar))

        dwa, dba, dwx, dbx, dlam = lax.fori_loop(0, nt, gates_vjp, tuple(jnp.zeros_like(p) for p in par))
        dwa_ref[...] = dwa
        dba_ref[...] = dba
        dwx_ref[...] = dwx
        dbx_ref[...] = dbx
        dlam_ref[...] = dlam
        dxc = dxc_s[...]
        dcb_ref[...] = jnp.sum(dxc, axis=0, keepdims=True)
        dxb, dws = _conv_bwd(xb_ref[...], dxc, cw_ref)
        dxb_ref[...] = dxb.astype(dxb_ref.dtype)
        for j in range(CONV_WIDTH):
            dcw_ref[j:j + 1, :] = dws[j]

    nb = LRU_BLOCKS
    w = LRU_WIDTH
    return pl.pallas_call(
        body, name=name, grid=(LRU_BLOCKS,),
        in_specs=[col(xb_base), col(xb_base + nb), col(0), col(0), cws, vec, mat, vec, mat, vec, vec],
        out_specs=[col(0), col(0), cws, vec, mat, vec, mat, vec, vec],
        out_shape=[jax.ShapeDtypeStruct((s, w), BF), jax.ShapeDtypeStruct((s, w), BF),
                   jax.ShapeDtypeStruct((CONV_WIDTH, w), F32), jax.ShapeDtypeStruct((1, w), F32),
                   jax.ShapeDtypeStruct(wa.shape, F32), jax.ShapeDtypeStruct((1, w), F32),
                   jax.ShapeDtypeStruct(wx.shape, F32), jax.ShapeDtypeStruct((1, w), F32),
                   jax.ShapeDtypeStruct((1, w), F32)],
        scratch_shapes=[pltpu.VMEM((s, LRU_BLOCK_DIM), F32)] * 4,
        compiler_params=_cparams(("parallel",)),
    )(proj, proj, hs, dol, cw, _row(cb), wa, _row(ba), wx, _row(bx), _row(lam))


def _place():
    return lax.axis_index("x"), lax.axis_index("y"), lax.axis_index("c")


def _all_gather(blocks, name):
    n = len(blocks)
    per = N_DEV - 1

    def body(*refs):
        ins, outs = refs[:n], refs[n:2 * n]
        send_sems, recv_sems, local_sems = refs[2 * n:]
        x, y, c = _place()
        me, sibling = (x, y, c), (x, y, 1 - c)
        chips = [(1 - x, y), (x, 1 - y), (1 - x, 1 - y)]

        def copy(a, k, block, to, src=None):
            dst = outs[a].at[4 * block[0] + 2 * block[1] + block[2]]
            return pltpu.make_async_remote_copy(
                src_ref=dst if src is None else src, dst_ref=dst, send_sem=send_sems.at[a * per + k],
                recv_sem=recv_sems.at[a * per + k], device_id=to, device_id_type=MESH)

        started = []
        for a in range(n):
            mine = pltpu.make_async_copy(ins[a], outs[a].at[4 * x + 2 * y + c], local_sems.at[a])
            mine.start()
            started.append(mine)
        sends = []
        for a in range(n):
            first = [copy(a, 0, me, sibling, src=ins[a])]
            first += [copy(a, 1 + j, me, (*chip, c), src=ins[a]) for j, chip in enumerate(chips)]
            for cp in first:
                cp.start()
            sends += first
        for a in range(n):
            for j, chip in enumerate(chips):
                copy(a, 1 + j, (*chip, c), me).wait_recv()
                passed = copy(a, 4 + j, (*chip, c), sibling)
                passed.start()
                sends.append(passed)
        for a in range(n):
            copy(a, 0, sibling, me).wait_recv()
            for j, chip in enumerate(chips):
                copy(a, 4 + j, (*chip, 1 - c), me).wait_recv()
        for cp in sends:
            cp.wait_send()
        for cp in started:
            cp.wait()

    any_spec = pl.BlockSpec(memory_space=pl.ANY)
    return pl.pallas_call(
        body, name=name, in_specs=[any_spec] * n, out_specs=[any_spec] * n,
        out_shape=[jax.ShapeDtypeStruct((N_DEV,) + b.shape, b.dtype) for b in blocks],
        scratch_shapes=[pltpu.SemaphoreType.DMA((n * per,)), pltpu.SemaphoreType.DMA((n * per,)),
                        pltpu.SemaphoreType.DMA((n,))],
    )(*blocks)


_HBM = pl.BlockSpec(memory_space=pltpu.HBM)
_SEM = pl.BlockSpec(memory_space=pltpu.SEMAPHORE)
_ANY = pl.BlockSpec(memory_space=pl.ANY)
_EFFECT = pltpu.SideEffectType.DATAFLOW_SIDE_EFFECTING


def _peers(x, y, c):
    return [(x, y, 1 - c), (1 - x, y, c), (1 - x, y, 1 - c), (x, 1 - y, c), (x, 1 - y, 1 - c),
            (1 - x, 1 - y, c), (1 - x, 1 - y, 1 - c)]


class _InFlight:
    def __init__(self, send_sem, recv_sem, src, land):
        self.send_sem, self.recv_sem, self.src, self.land = send_sem, recv_sem, src, land


def _send_start(srcs, whole, after, name):
    n = len(srcs)
    lands = [lax.empty(((N_DEV,) + s.shape) if whole else s.shape, s.dtype) for s in srcs]

    def body(*refs):
        src_refs, land_refs = refs[:n], refs[n:2 * n]
        outs = refs[2 * n + 1:]
        send_sems, recv_sems, token = outs[:n], outs[n:2 * n], outs[-1]
        x, y, c = _place()
        me = 4 * x + 2 * y + c
        for a in range(n):
            for p in _peers(x, y, c):
                src = src_refs[a] if whole else src_refs[a].at[4 * p[0] + 2 * p[1] + p[2]]
                pltpu.make_async_remote_copy(src_ref=src, dst_ref=land_refs[a].at[me], send_sem=send_sems[a],
                                             recv_sem=recv_sems[a], device_id=p, device_id_type=MESH).start()
        token[...] = jnp.zeros_like(token)

    hbm = lambda a: pltpu.HBM(a.shape, a.dtype)
    outs = pl.pallas_call(
        body, name=name,
        out_shape=[pltpu.SemaphoreType.DMA(())] * (2 * n) + [hbm(s) for s in srcs] + [hbm(l) for l in lands]
        + [jax.ShapeDtypeStruct((8, LANE), F32)],
        in_specs=[_HBM] * (2 * n) + [_ANY],
        out_specs=[_SEM] * (2 * n) + [_HBM] * (2 * n) + [pl.BlockSpec(memory_space=pltpu.VMEM)],
        input_output_aliases={i: 2 * n + i for i in range(2 * n)},
        compiler_params=pltpu.CompilerParams(has_side_effects=_EFFECT),
    )(*[pltpu.with_memory_space_constraint(s, pltpu.HBM) for s in srcs],
      *[pltpu.with_memory_space_constraint(l, pltpu.HBM) for l in lands], after)
    flights = [_InFlight(outs[a], outs[n + a], outs[2 * n + a], outs[3 * n + a]) for a in range(n)]
    return flights, outs[-1]


def _send_wait(flights, after, name):
    n = len(flights)

    def body(*refs):
        x, y, c = _place()
        for a in range(n):
            land_ref, send_sem, recv_sem = refs[4 * a + 1], refs[4 * a + 2], refs[4 * a + 3]
            seven = land_ref.at[pl.ds(0, N_DEV - 1)]
            copy = pltpu.make_async_remote_copy(src_ref=seven, dst_ref=seven, send_sem=send_sem, recv_sem=recv_sem,
                                                device_id=(x, y, 1 - c), device_id_type=MESH)
            copy.wait_send()
            copy.wait_recv()

    ins, out_shape = [], []
    for f in flights:
        ins += [f.src, f.land, f.send_sem, f.recv_sem]
        out_shape += [pltpu.HBM(f.src.shape, f.src.dtype), pltpu.HBM(f.land.shape, f.land.dtype)]
    outs = pl.pallas_call(
        body, name=name, out_shape=out_shape,
        in_specs=[_HBM, _HBM, _SEM, _SEM] * n + [_ANY], out_specs=[_HBM] * (2 * n),
        input_output_aliases={4 * a + i: 2 * a + i for a in range(n) for i in range(2)},
        compiler_params=pltpu.CompilerParams(has_side_effects=_EFFECT),
    )(*ins, after)
    return [(outs[2 * a], outs[2 * a + 1]) for a in range(n)]


def _adamw_math(w, g, m, v):
    m = ADAM_B1 * m + (1.0 - ADAM_B1) * g
    v = ADAM_B2 * v + (1.0 - ADAM_B2) * (g * g)
    m_hat = m / (1.0 - ADAM_B1 ** ADAM_STEP)
    v_hat = v / (1.0 - ADAM_B2 ** ADAM_STEP)
    delta = -ADAM_LR * (m_hat / (jnp.sqrt(v_hat) + ADAM_EPS) + ADAM_WD * w)
    return delta, m, v


def _adamw_sharded(mine, landed, me, w, m, v, layer, prev, after, name):
    depth, r, c = w.shape
    tr = _tile(r, 128, 8)
    n_prev = 0 if prev is None else 4

    def body(*refs):
        me_ref, own_ref, land_ref, w_ref, m_ref, v_ref = refs[:6]
        g_ref, d_ref, nm_ref, nv_ref = refs[7 + n_prev:]
        g = None
        for k in range(N_DEV):
            part = jnp.where(me_ref[0] == k, own_ref[0], land_ref[k]).astype(F32)
            g = part if g is None else g + part
        delta, nm, nv = _adamw_math(w_ref[...], g, m_ref[...], v_ref[...])
        g_ref[...] = g
        d_ref[...] = delta
        nm_ref[...] = nm
        nv_ref[...] = nv

    lay = pl.BlockSpec((None, tr, c), lambda i, me_ref: (layer, i, 0))
    return pl.pallas_call(
        body, name=name,
        grid_spec=pltpu.PrefetchScalarGridSpec(
            num_scalar_prefetch=1, grid=(r // tr,),
            in_specs=[pl.BlockSpec((1, tr, c), lambda i, me_ref: (me_ref[0], i, 0)),
                      pl.BlockSpec((N_DEV, tr, c), lambda i, me_ref: (0, i, 0)), lay, lay, lay]
            + [_ANY] * (1 + n_prev),
            out_specs=[lay] * 4),
        out_shape=[jax.ShapeDtypeStruct(w.shape, F32)] * 4,
        input_output_aliases={7 + i: i for i in range(n_prev)},
        compiler_params=_cparams(("parallel",)),
    )(me, mine, landed, w, m, v, after, *(prev or ()))


def _sum_devices(stack, name):
    _, r, c = stack.shape
    tr = _tile(r, 512, 8)

    def body(s_ref, o_ref):
        g = s_ref[0]
        for j in range(1, N_DEV):
            g = g + s_ref[j]
        o_ref[...] = g

    return pl.pallas_call(
        body, name=name, grid=(r // tr,), in_specs=[pl.BlockSpec((N_DEV, tr, c), lambda i: (0, i, 0))],
        out_specs=pl.BlockSpec((tr, c), lambda i: (i, 0)), out_shape=jax.ShapeDtypeStruct((r, c), F32),
        compiler_params=_cparams(("parallel",)),
    )(stack)


def _adamw_packed(w, g, m, v, name):
    r, c = w.shape
    tr = _tile(r, 512, 8)

    def body(w_ref, g_ref, m_ref, v_ref, d_ref, nm_ref, nv_ref):
        delta, nm, nv = _adamw_math(w_ref[...], g_ref[...], m_ref[...], v_ref[...])
        d_ref[...] = delta
        nm_ref[...] = nm
        nv_ref[...] = nv

    spec = pl.BlockSpec((tr, c), lambda i: (i, 0))
    return pl.pallas_call(
        body, name=name, grid=(r // tr,), in_specs=[spec] * 4, out_specs=[spec] * 3,
        out_shape=[jax.ShapeDtypeStruct((r, c), F32)] * 3, compiler_params=_cparams(("parallel",)),
    )(w, g, m, v)


def _pack(arrays):
    unit = 8 * LANE
    flat = []
    for a in arrays:
        f = a.reshape(-1)
        flat.append(jnp.pad(f, (0, (-f.shape[0]) % unit)))
    return jnp.concatenate(flat).reshape(-1, LANE)


def _unpack(packed, like):
    unit = 8 * LANE
    flat = packed.reshape(-1)
    out, off = [], 0
    for a in like:
        n = 1
        for d in a.shape:
            n *= d
        out.append(flat[off:off + n].reshape(a.shape))
        off += n + (-n) % unit
    return out


class _Layout:
    def __init__(self, d):
        self.d = d
        self.n_front = 4 * GDN_WIDTH
        self.n_ab = 2 * HEADS
        self.n_back = 2 * LRU_WIDTH + 2 * d
        self.cols = self.n_front + self.n_ab + self.n_back
        self.pcols = self.n_front + self.n_back + LANE
        self.z_base = 3 * GDN_WIDTH // HEAD_DIM
        self.xb_base = self.n_front // LRU_BLOCK_DIM
        self.gg_off = self.n_front + 2 * LRU_WIDTH
        assert self.gg_off % d == 0
        self.gg_base = self.gg_off // d
        self.ab_base = (self.n_front + self.n_back) // LANE

    def _ranges(self):
        f, ab = self.n_front, self.n_ab
        return [(0, f), (f + ab, self.cols), (f, f + ab)]

    def permuted_from_stack(self, stack):
        blk = stack.shape[2]
        pieces = []
        for lo, hi in self._ranges():
            while lo < hi:
                k = lo // blk
                end = min(hi, (k + 1) * blk)
                pieces.append(stack[k, :, lo - k * blk:end - k * blk])
                lo = end
        pieces.append(jnp.zeros((stack.shape[1], LANE - self.n_ab), stack.dtype))
        return jnp.concatenate(pieces, axis=1)

    def stack_from_permuted(self, w):
        blk = self.cols // N_DEV
        where = {}
        pos = 0
        for lo, hi in self._ranges():
            where[lo] = pos
            pos += hi - lo
        cuts = sorted(lo for lo, _ in self._ranges()) + [self.cols]
        blocks = []
        for k in range(N_DEV):
            lo, pieces = k * blk, []
            while lo < (k + 1) * blk:
                start = max(c for c in cuts if c <= lo)
                end = min((k + 1) * blk, min(c for c in cuts if c > lo))
                pieces.append(w[:, where[start] + lo - start:where[start] + end - start])
                lo = end
            blocks.append(pieces[0] if len(pieces) == 1 else jnp.concatenate(pieces, axis=1))
        return jnp.stack(blocks)


def _cols_from_stack(stack):
    nd, k, n = stack.shape
    return stack.transpose(1, 0, 2).reshape(k, nd * n)


def _stack_from_cols(w):
    k, n = w.shape
    return w.reshape(k, N_DEV, n // N_DEV).transpose(1, 0, 2)


def _after(v, *tokens):
    for t in tokens:
        v = v + t[0, 0]
    return v


def _layer_fwd(x, p, late, lay, tag):
    s, d = x.shape
    nc = s // CHUNK
    h = _rms_fwd(x, p["attn_norm"], f"rms1_fwd{tag}")
    proj = _matmul(h, p["w_in"], "nn", f"in_proj{tag}", [F32], tn=1152)
    qkv = _conv_silu_fwd(proj, p["gdn_conv_w"], f"gdn_conv_fwd{tag}")
    gc, beta = _gdn_gates_fwd(proj, p["gdn_a_log"], p["gdn_dt_bias"], lay.ab_base, f"gdn_gates_fwd{tag}")
    grow = gc[:, :HEADS].reshape(nc, CHUNK, HEADS).transpose(0, 2, 1)
    o, states = _gdn_chunk_fwd(qkv, gc, beta, grow, f"gdn_chunk_fwd{tag}")
    og = _gdn_post_fwd(o, proj, p["gdn_norm"], lay.z_base, f"gdn_post_fwd{tag}")
    ol, hs = _lru_fwd(proj, p["lru_conv_w"], p["lru_conv_b"], p["lru_w_a"], p["lru_b_a"], p["lru_w_x"], p["lru_b_x"],
                      p["lru_lambda"], lay.xb_base, f"lru_fwd{tag}")
    late[0](hs)
    yg = _matmul(og, p["w_branch_gdn"], "nn", f"branch_gdn{tag}", [F32])
    yl = _matmul(ol, p["w_branch_lru"], "nn", f"branch_lru{tag}", [F32])
    merged = _merge_fwd(proj, yg, yl, lay.gg_base, f"merge_fwd{tag}")
    x1 = _matmul(merged, p["w_out"], "nn", f"out_proj{tag}", [F32], epi=lambda acc, r: (acc + r,), extras=[x])
    h2 = _rms_fwd(x1, p["mlp_norm"], f"rms2_fwd{tag}")
    late[1](h2)
    u, act = _matmul(h2, p["w_up"], "nn", f"mlp_up{tag}", [F32, BF],
                     epi=lambda acc: (acc, jnp.square(jnp.maximum(acc, 0.0))), shards=True)
    x2 = _matmul(act, p["w_down"], "nn", f"mlp_down{tag}", [F32], epi=lambda acc, r: (acc + r,), extras=[x1])
    saved = dict(x=x, h=h, proj=proj, qkv=qkv, gc=gc, beta=beta, grow=grow, o=o, states=states, og=og, ol=ol, hs=hs,
                 yg=yg, yl=yl, merged=merged, x1=x1, h2=h2, u=u, act=act)
    return x2, saved


def _layer_bwd(dx2, p, sv, send, send_small, after_last_send, after, lay, tag):
    s, d = dx2.shape
    nc = s // CHUNK
    small = {}
    du = _matmul(dx2, p["w_down"], "nt", f"d_act{tag}", [BF],
                 epi=lambda acc, u: (acc * (2.0 * jnp.maximum(u, 0.0)),), extras=[sv["u"]], after=after)
    sent = send({"w_down": _matmul(sv["act"], dx2, "tn", f"dw_down{tag}", [BF]).reshape(N_DEV, -1, d),
                 "w_up": _matmul(sv["h2"], du, "tn", f"dw_up{tag}", [BF], shards=True)})
    dh2 = _matmul(du, p["w_up"], "nt", f"d_h2{tag}", [F32], shards=True)
    dx1, g = _rms_bwd(sv["x1"], _after(p["mlp_norm"], sent), dh2, dx2, f"rms2_bwd{tag}")
    small["mlp_norm"] = g.reshape(-1)
    dmerged = _matmul(dx1, p["w_out"], "nt", f"d_merged{tag}", [F32])
    dwout = _matmul(sv["merged"], dx1, "tn", f"dw_out{tag}", [BF]).reshape(N_DEV, -1, d)
    dgg, dgl, dyg, dyl = _merge_bwd(sv["proj"], sv["yg"], sv["yl"], dmerged, lay.gg_base, f"merge_bwd{tag}")
    sent = send({"w_out": dwout,
                 "w_branch_gdn": _stack_from_cols(_matmul(sv["og"], dyg, "tn", f"dw_bg{tag}", [BF])),
                 "w_branch_lru": _stack_from_cols(_matmul(sv["ol"], dyl, "tn", f"dw_bl{tag}", [BF]))})
    dog = _matmul(dyg, p["w_branch_gdn"], "nt", f"d_og{tag}", [F32])
    dol = _matmul(dyl, p["w_branch_lru"], "nt", f"d_ol{tag}", [F32])
    (dxb, dyb, dcw, dcb, dwa, dba, dwx, dbx, dlam) = _lru_bwd(
        sv["proj"], sv["hs"], dol, p["lru_conv_w"], p["lru_conv_b"], p["lru_w_a"], p["lru_b_a"], p["lru_w_x"],
        p["lru_b_x"], _after(p["lru_lambda"], sent), lay.xb_base, f"lru_bwd{tag}")
    small.update(lru_conv_w=dcw, lru_conv_b=dcb.reshape(-1), lru_w_a=dwa, lru_b_a=dba.reshape(-1), lru_w_x=dwx,
                 lru_b_x=dbx.reshape(-1), lru_lambda=dlam.reshape(-1))
    do, dz, g = _gdn_post_bwd(sv["o"], sv["proj"], p["gdn_norm"], dog, lay.z_base, f"gdn_post_bwd{tag}")
    small["gdn_norm"] = g.reshape(-1)
    dqkv_c, dgc_col, dbeta, dgrow = _gdn_chunk_bwd(sv["qkv"], sv["gc"], sv["beta"], sv["grow"], sv["states"], do,
                                                   f"gdn_chunk_bwd{tag}")
    dqkv, dgcw = _conv_silu_bwd(sv["proj"], p["gdn_conv_w"], dqkv_c, f"gdn_conv_bwd{tag}")
    small["gdn_conv_w"] = dgcw
    dgc_row = jnp.pad(dgrow.transpose(0, 2, 1).reshape(s, HEADS), ((0, 0), (0, LANE - HEADS)))
    dab, dal, ddt = _gdn_gates_bwd(sv["proj"], p["gdn_a_log"], p["gdn_dt_bias"], dgc_col, dgc_row, dbeta, lay.ab_base,
                                   f"gdn_gates_bwd{tag}")
    small["gdn_a_log"] = dal[0, :HEADS]
    small["gdn_dt_bias"] = ddt[0, :HEADS]
    dproj = jnp.concatenate([dqkv, dz, dxb, dyb, dgg, dgl, dab], axis=1)
    dh = _matmul(dproj, p["w_in"], "nt", f"d_h{tag}", [F32], tk=3456)
    dx, g = _rms_bwd(sv["x"], p["attn_norm"], dh, dx1, f"rms1_bwd{tag}")
    small["attn_norm"] = g.reshape(-1)
    sent = send_small(small)
    dwin = _matmul(sv["h"], dproj, "tn", f"dw_in{tag}", [BF], tn=1152, after=sent)
    after_last_send(send({"w_in": lay.stack_from_permuted(dwin)}))
    return dx


_BIG = ("w_in", "w_branch_gdn", "w_branch_lru", "w_out", "w_up", "w_down")
_ROW_SHARDED = ("w_out", "w_down")
_CONV = ("gdn_conv_w", "lru_conv_w")
_SMALL = ("attn_norm", "gdn_a_log", "gdn_dt_bias", "gdn_norm", "lru_conv_b", "lru_w_a", "lru_b_a", "lru_w_x",
          "lru_b_x", "lru_lambda", "mlp_norm")
_ORDER = ("attn_norm", "w_in", "gdn_conv_w", "gdn_a_log", "gdn_dt_bias", "gdn_norm", "lru_conv_w", "lru_conv_b",
          "lru_w_a", "lru_b_a", "lru_w_x", "lru_b_x", "lru_lambda", "w_branch_gdn", "w_branch_lru", "w_out",
          "mlp_norm", "w_up", "w_down", "final_norm")


def kernel(x, attn_norm, w_in, gdn_conv_w, gdn_a_log, gdn_dt_bias, gdn_norm, lru_conv_w, lru_conv_b, lru_w_a, lru_b_a, lru_w_x, lru_b_x, lru_lambda, w_branch_gdn, w_branch_lru, w_out, mlp_norm, w_up, w_down, final_norm, loss_target, m_attn_norm, m_w_in, m_gdn_conv_w, m_gdn_a_log, m_gdn_dt_bias, m_gdn_norm, m_lru_conv_w, m_lru_conv_b, m_lru_w_a, m_lru_b_a, m_lru_w_x, m_lru_b_x, m_lru_lambda, m_w_branch_gdn, m_w_branch_lru, m_w_out, m_mlp_norm, m_w_up, m_w_down, m_final_norm, v_attn_norm, v_w_in, v_gdn_conv_w, v_gdn_a_log, v_gdn_dt_bias, v_gdn_norm, v_lru_conv_w, v_lru_conv_b, v_lru_w_a, v_lru_b_a, v_lru_w_x, v_lru_b_x, v_lru_lambda, v_w_branch_gdn, v_w_branch_lru, v_w_out, v_mlp_norm, v_w_up, v_w_down, v_final_norm):
    w = dict(attn_norm=attn_norm, w_in=w_in, gdn_conv_w=gdn_conv_w, gdn_a_log=gdn_a_log, gdn_dt_bias=gdn_dt_bias, gdn_norm=gdn_norm, lru_conv_w=lru_conv_w, lru_conv_b=lru_conv_b, lru_w_a=lru_w_a, lru_b_a=lru_b_a, lru_w_x=lru_w_x, lru_b_x=lru_b_x, lru_lambda=lru_lambda, w_branch_gdn=w_branch_gdn, w_branch_lru=w_branch_lru, w_out=w_out, mlp_norm=mlp_norm, w_up=w_up, w_down=w_down, final_norm=final_norm)
    m = dict(attn_norm=m_attn_norm, w_in=m_w_in, gdn_conv_w=m_gdn_conv_w, gdn_a_log=m_gdn_a_log, gdn_dt_bias=m_gdn_dt_bias, gdn_norm=m_gdn_norm, lru_conv_w=m_lru_conv_w, lru_conv_b=m_lru_conv_b, lru_w_a=m_lru_w_a, lru_b_a=m_lru_b_a, lru_w_x=m_lru_w_x, lru_b_x=m_lru_b_x, lru_lambda=m_lru_lambda, w_branch_gdn=m_w_branch_gdn, w_branch_lru=m_w_branch_lru, w_out=m_w_out, mlp_norm=m_mlp_norm, w_up=m_w_up, w_down=m_w_down, final_norm=m_final_norm)
    v = dict(attn_norm=v_attn_norm, w_in=v_w_in, gdn_conv_w=v_gdn_conv_w, gdn_a_log=v_gdn_a_log, gdn_dt_bias=v_gdn_dt_bias, gdn_norm=v_gdn_norm, lru_conv_w=v_lru_conv_w, lru_conv_b=v_lru_conv_b, lru_w_a=v_lru_w_a, lru_b_a=v_lru_b_a, lru_w_x=v_lru_w_x, lru_b_x=v_lru_b_x, lru_lambda=v_lru_lambda, w_branch_gdn=v_w_branch_gdn, w_branch_lru=v_w_branch_lru, w_out=v_w_out, mlp_norm=v_mlp_norm, w_up=v_w_up, w_down=v_w_down, final_norm=v_final_norm)
    depth = w_in.shape[0]
    x = x[0]
    target = loss_target[0]
    s, d = x.shape
    lay = _Layout(d)
    assert lay.cols == w_in.shape[2] * N_DEV
    cx, cy, cc = _place()
    me = 4 * cx + 2 * cy + cc

    me_arr = me.astype(jnp.int32).reshape(1)

    early, mixer, mlp = ("w_in",) + _CONV, ("w_branch_gdn", "w_branch_lru", "w_out"), ("w_up", "w_down")

    def shard(n, l):
        return w[n][l].astype(BF) if n in _BIG else w[n][l]

    def joined(name, stack):
        if name == "w_in":
            return lay.permuted_from_stack(stack)
        if name == "w_up":
            return stack
        return stack.reshape(-1, stack.shape[-1]) if name in _ROW_SHARDED else _cols_from_stack(stack)

    def start_weights(l, names, after):
        flights, token = _send_start([shard(n, l) for n in names], True, after, f"weights_start_l{l}")
        return dict(zip(names, flights)), token

    def wait_weights(l, flights, names, p, after, which):
        got = _send_wait([flights[n] for n in names], after, f"weights_wait_{which}_l{l}")
        for n, (own, landed) in zip(names, got):
            p[n] = joined(n, lax.dynamic_update_slice_in_dim(landed, own[None], me, axis=0))

    saved, params = [], []
    xl = x
    first = dict(zip(early, _all_gather([shard(n, 0) for n in early], "gather_first_weights")))
    flights, token = start_weights(0, mixer + mlp, first["w_in"])
    for l in range(depth):
        p = {n: w[n][l] for n in _SMALL}
        if l == 0:
            p.update({n: joined(n, g) for n, g in first.items()})
        else:
            wait_weights(l, flights, early, p, xl, "early")
        mine = flights
        if l + 1 < depth:
            flights, token = start_weights(l + 1, early + mixer + mlp, token)
        p["attn_norm"] = _after(p["attn_norm"], token)
        late = [functools.partial(wait_weights, l, mine, names, p, which=which)
                for names, which in ((mixer, "mixer"), (mlp, "mlp"))]
        xl, sv = _layer_fwd(xl, p, late, lay, f"_l{l}")
        saved.append(sv)
        params.append(p)
    dx, loss_row, dfinal = _loss_head(xl, final_norm, target, "loss_head")
    loss = lax.psum(loss_row[0, 0], ("x", "y", "c"))

    out = {n: None for n in _BIG}
    pending = {l: {} for l in range(depth)}

    def sender(l):
        def send(stacks):
            names = tuple(stacks)
            fl, token = _send_start([stacks[n] for n in names], False, me_arr, f"grads_start_{'_'.join(names)}_l{l}")
            pending[l].update(zip(names, fl))
            return token
        return send

    def finish(l, names, after, which):
        got = _send_wait([pending[l][n] for n in names], after, f"grads_wait_{which}_l{l}")
        for n, (mine, landed) in zip(names, got):
            out[n] = _adamw_sharded(mine, landed, me_arr, w[n], m[n], v[n], l, out[n], after, f"adamw_{n}_l{l}")
            after = out[n][0]
        return after

    small_names = _SMALL + _CONV
    small_sent = {}

    def small_sender(l):
        def send_small(small):
            arrays = [small[n] for n in small_names] + ([dfinal.reshape(-1)] if l == depth - 1 else [])
            flights, token = _send_start([_pack(arrays)], True, me_arr, f"small_grads_start_l{l}")
            small_sent[l] = (flights[0], arrays)
            return token
        return send_small

    last = [None]

    def hook_for(l):
        def hook(token):
            last[0] = finish(l + 1, _BIG, token, "all") if l + 1 < depth else token
        return hook

    for l in reversed(range(depth)):
        dx = _layer_bwd(dx, params[l], saved[l], sender(l), small_sender(l), hook_for(l), last[0], lay, f"_l{l}")

    last[0] = finish(0, mixer + mlp, last[0], "late")
    got = _send_wait([small_sent[l][0] for l in range(depth)], last[0], "small_grads_wait")
    per_layer = []
    for l, (part, landed) in enumerate(got):
        total = _sum_devices(lax.dynamic_update_slice_in_dim(landed, part[None], me, axis=0), f"sum_small_grads_l{l}")
        per_layer.append(_unpack(total, small_sent[l][1]))
    grads = {n: jnp.stack([per_layer[l][i] for l in range(depth)]) for i, n in enumerate(small_names)}
    grads["final_norm"] = per_layer[depth - 1][len(small_names)]
    for n in _CONV:
        blk = w[n].shape[2]
        grads[n] = lax.dynamic_slice_in_dim(grads[n], me * blk, blk, axis=2)
    names = small_names + ("final_norm",)
    gp = _pack([grads[n] for n in names])
    upd = _adamw_packed(_pack([w[n] for n in names]), gp, _pack([m[n] for n in names]), _pack([v[n] for n in names]),
                        "adamw_small")
    like = [w[n] for n in names]
    unp = [_unpack(u, like) for u in upd]
    for i, n in enumerate(names):
        out[n] = (grads[n], unp[0][i], unp[1][i], unp[2][i])
    finish(0, ("w_in",), upd[0], "w_in")

    res = [loss, dx.reshape(1, s, d)]
    for k in range(4):
        res += [out[n][k] for n in _ORDER]
    return tuple(res)
```

```python
import functools

import jax
import jax.numpy as jnp
from jax import lax
from jax.experimental import pallas as pl
from jax.experimental.pallas import tpu as pltpu

F32 = jnp.float32
BF = jnp.bfloat16
MESH = pl.DeviceIdType.MESH
N_DEV = 8
N_CHIP = 4

HEADS = 8
HEAD_DIM = 128
GDN_WIDTH = HEADS * HEAD_DIM
CHUNK = 64
CONV_WIDTH = 4
LRU_WIDTH = 1024
LRU_BLOCKS = 8
LRU_BLOCK_DIM = 128
LRU_C = 8.0
RMS_EPS = 1e-6
L2_EPS = 1e-6
ADAM_LR = 0.001
ADAM_B1 = 0.9
ADAM_B2 = 0.999
ADAM_EPS = 1e-08
ADAM_WD = 0.01
ADAM_STEP = 10

LANE = 128
PACK_ROWS = 512
VMEM_LIMIT = 56 * 1024 * 1024


def _cparams(sem=None):
    return pltpu.CompilerParams(dimension_semantics=sem, vmem_limit_bytes=VMEM_LIMIT)


def _tile(n, pref, unit=LANE):
    if n <= pref:
        return n
    best = None
    for t in range(unit, pref + 1, unit):
        if n % t == 0:
            best = t
    assert best is not None, (n, pref, unit)
    return best


_NN = (((1,), (0,)), ((), ()))
_NT = (((1,), (1,)), ((), ()))
_TN = (((0,), (0,)), ((), ()))


def _dg(a, b, dims):
    return lax.dot_general(a.astype(BF), b.astype(BF), dims, preferred_element_type=F32)


class _MM:
    def __init__(self, nn, nt, tn):
        self.nn, self.nt, self.tn = nn, nt, tn


def _make_mm(nn, nt, tn):
    def p_nn(a, b):
        return _dg(a, b, nn)

    def p_nt(a, b):
        return _dg(a, b, nt)

    def p_tn(a, b):
        return _dg(a, b, tn)

    d_nn, d_nt, d_tn = jax.custom_vjp(p_nn), jax.custom_vjp(p_nt), jax.custom_vjp(p_tn)

    def save(f):
        return lambda a, b: (f(a, b), (a, b))

    d_nn.defvjp(save(d_nn), lambda r, g: (d_nt(g, r[1]), d_tn(r[0], g)))
    d_nt.defvjp(save(d_nt), lambda r, g: (d_nn(g, r[1]), d_tn(g, r[0])))
    d_tn.defvjp(save(d_tn), lambda r, g: (d_nt(r[1], g), d_nn(r[0], g)))
    return _MM(p_nn, p_nt, p_tn), _MM(d_nn, d_nt, d_tn)


_BNN = (((2,), (1,)), ((0,), (0,)))
_BNT = (((2,), (2,)), ((0,), (0,)))
_BTN = (((1,), (1,)), ((0,), (0,)))

_PLAIN, _DIFF = _make_mm(_NN, _NT, _TN)
_PLAIN_B, _DIFF_B = _make_mm(_BNN, _BNT, _BTN)


def _sigmoid(x):
    return 1.0 / (1.0 + jnp.exp(-x))


def _silu(x):
    return x * _sigmoid(x)


def _softplus(x):
    return jnp.maximum(x, 0.0) + jnp.log(1.0 + jnp.exp(-jnp.maximum(x, -x)))


def _gelu(x):
    return 0.5 * x * (1.0 + jnp.tanh(0.7978845608028654 * (x + 0.044715 * (x * x * x))))


def _expm1(x):
    s = x * (1.0 + x * (1.0 / 2.0) * (1.0 + x * (1.0 / 3.0) * (1.0 + x * (1.0 / 4.0) * (
        1.0 + x * (1.0 / 5.0) * (1.0 + x * (1.0 / 6.0) * (1.0 + x * (1.0 / 7.0)))))))
    return jnp.where(jnp.maximum(x, -x) < 0.3, s, jnp.exp(x) - 1.0)


def _rms(x, gain):
    ms = jnp.mean(x * x, axis=-1, keepdims=True)
    return x * lax.rsqrt(ms + RMS_EPS) * gain


def _l2n(x):
    return x * lax.rsqrt(jnp.sum(x * x, axis=-1, keepdims=True) + L2_EPS)


def _shift_down(x, r):
    if r == 0:
        return x
    row = lax.broadcasted_iota(jnp.int32, x.shape, 0)
    return jnp.where(row >= r, pltpu.roll(x, r, 0), 0.0)


def _shift_up(x, r):
    if r == 0:
        return x
    n = x.shape[0]
    row = lax.broadcasted_iota(jnp.int32, x.shape, 0)
    return jnp.where(row < n - r, pltpu.roll(x, n - r, 0), 0.0)


def _conv(x, w_ref):
    y = None
    for j in range(CONV_WIDTH):
        t = _shift_down(x, CONV_WIDTH - 1 - j) * w_ref[j:j + 1, :]
        y = t if y is None else y + t
    return y


def _conv_bwd(x, dy, w_ref):
    dx = None
    dws = []
    for j in range(CONV_WIDTH):
        r = CONV_WIDTH - 1 - j
        t = _shift_up(dy, r) * w_ref[j:j + 1, :]
        dx = t if dx is None else dx + t
        dws.append(jnp.sum(dy * _shift_down(x, r), axis=0, keepdims=True))
    return dx, dws


def _scan_tile(a, b, reverse):
    n = a.shape[0]
    row = lax.broadcasted_iota(jnp.int32, a.shape, 0)
    s = 1
    while s < n:
        if reverse:
            keep = row < n - s
            a_sh = jnp.where(keep, pltpu.roll(a, n - s, 0), 1.0)
            b_sh = jnp.where(keep, pltpu.roll(b, n - s, 0), 0.0)
        else:
            keep = row >= s
            a_sh = jnp.where(keep, pltpu.roll(a, s, 0), 1.0)
            b_sh = jnp.where(keep, pltpu.roll(b, s, 0), 0.0)
        b = a * b_sh + b
        a = a * a_sh
        s *= 2
    return a, b


def _dg3(a, b, dims):
    a_hi = a.astype(BF)
    b_hi = b.astype(BF)
    a_lo = a - a_hi.astype(F32)
    b_lo = b - b_hi.astype(F32)
    return _dg(a_hi, b_hi, dims) + (_dg(a_hi, b_lo, dims) + _dg(a_lo, b_hi, dims))


@jax.custom_vjp
def _tri_inv(a):
    n = a.shape[1]
    p = _dg3(a, a, _BNN)
    r = p
    e = 2
    while 2 * e < n:
        p = _dg3(p, p, _BNN)
        r = r + p + _dg3(r, p, _BNN)
        e *= 2
    row = lax.broadcasted_iota(jnp.int32, (1, n, n), 1)
    col = lax.broadcasted_iota(jnp.int32, (1, n, n), 2)
    eye = jnp.where(row == col, 1.0, 0.0)
    return eye - a + r - _dg3(a, r, _BNN)


def _tri_inv_fwd(a):
    t = _tri_inv(a)
    return t, t


def _tri_inv_bwd(t, g):
    return (-_dg3(_dg3(t, g, _BTN), t, _BNT),)


_tri_inv.defvjp(_tri_inv_fwd, _tri_inv_bwd)


@jax.custom_vjp
def _tri_inv_saved(a, t):
    return t


_tri_inv_saved.defvjp(lambda a, t: (t, t), lambda t, g: (_tri_inv_bwd(t, g)[0], jnp.zeros_like(t)))


def _gdn_chunk(q, k, v, beta, gcol, grow, state, mm, t_saved=None):
    c = q.shape[1]
    row = lax.broadcasted_iota(jnp.int32, (1, c, c), 1)
    col = lax.broadcasted_iota(jnp.int32, (1, c, c), 2)
    causal = row >= col
    strict = row > col
    qn = _l2n(q) * (HEAD_DIM ** -0.5)
    kn = _l2n(k)
    dec = jnp.where(causal, jnp.exp(jnp.where(causal, gcol - grow, 0.0)), 0.0)
    kb = kn * beta
    vb = v * beta
    a = jnp.where(strict, mm.nt(kb, kn) * dec, 0.0)
    t = _tri_inv(a) if t_saved is None else _tri_inv_saved(a, t_saved)
    eg = jnp.exp(gcol)
    u = mm.nn(t, vb)
    w = mm.nn(t, kb * eg)
    p = jnp.where(causal, mm.nt(qn, kn) * dec, 0.0)
    vn = u - mm.nn(w, state)
    o = mm.nn(qn * eg, state) + mm.nn(p, vn)
    last = lax.broadcasted_iota(jnp.int32, (1, c, 1), 1) == c - 1
    gl = jnp.sum(jnp.where(last, gcol, 0.0), axis=1, keepdims=True)
    kd = kn * jnp.exp(gl - gcol)
    new_state = state * jnp.exp(gl) + mm.tn(kd, vn)
    return o, new_state, t


def _lru_gates(xc, wa, ba, wx, bx, lam, mm):
    r = _sigmoid(mm.nn(xc, wa) + ba)
    i = _sigmoid(mm.nn(xc, wx) + bx)
    log_a = -LRU_C * r * _softplus(-lam)
    a = jnp.exp(log_a)
    bterm = jnp.sqrt(-_expm1(2.0 * log_a)) * (i * xc)
    return a, bterm


def _matmul(a, b, mode, name, out_dtypes, epi=None, extras=(), tm=1024, tn=1024, tk=2048, after=None,
            shards=False):
    if shards:
        nd, rows, per = b.shape if mode != "tn" else (N_DEV, a.shape[1], b.shape[1] // N_DEV)
        b_shape = b.shape if mode == "tn" else (rows, nd * per)
    else:
        b_shape = b.shape
    if mode == "nn":
        (m, k), (k2, n) = a.shape, b_shape
    elif mode == "nt":
        (m, k), (n, k2) = a.shape, b_shape
    else:
        (k, m), (k2, n) = a.shape, b_shape
    assert k == k2, (a.shape, b.shape, mode)
    if shards:
        tn, tk = (min(tn, per), tk) if mode != "nt" else (tn, min(tk, per))
    tm, tn, tk = _tile(m, tm), _tile(n, tn), _tile(k, tk)
    nk = k // tk
    dims = {"nn": _NN, "nt": _NT, "tn": _TN}[mode]
    if mode == "nn":
        a_spec = pl.BlockSpec((tm, tk), lambda i, j, kk: (i, kk))
        b_spec = pl.BlockSpec((tk, tn), lambda i, j, kk: (kk, j))
        if shards:
            assert per % tn == 0
            b_spec = pl.BlockSpec((None, tk, tn), lambda i, j, kk: (j * tn // per, kk, j * tn % per // tn))
    elif mode == "nt":
        a_spec = pl.BlockSpec((tm, tk), lambda i, j, kk: (i, kk))
        b_spec = pl.BlockSpec((tn, tk), lambda i, j, kk: (j, kk))
        if shards:
            assert per % tk == 0
            b_spec = pl.BlockSpec((None, tn, tk), lambda i, j, kk: (kk * tk // per, j, kk * tk % per // tk))
    else:
        a_spec = pl.BlockSpec((tk, tm), lambda i, j, kk: (kk, i))
        b_spec = pl.BlockSpec((tk, tn), lambda i, j, kk: (kk, j))
    o_spec = pl.BlockSpec((tm, tn), lambda i, j, kk: (i, j))
    out_spec, out_dims = o_spec, (m, n)
    if shards and mode == "tn":
        assert per % tn == 0 and not extras
        out_spec = pl.BlockSpec((None, tm, tn), lambda i, j, kk: (j * tn // per, i, j * tn % per // tn))
        out_dims = (N_DEV, m, per)
    n_ex, n_out = len(extras), len(out_dtypes)
    order = [] if after is None else [after]
    n_in = 2 + n_ex + len(order)
    if epi is None:
        epi = lambda acc: (acc,)

    def body(*refs):
        a_ref, b_ref = refs[0], refs[1]
        ex_refs = refs[2:2 + n_ex]
        out_refs = refs[n_in:n_in + n_out]
        kk = pl.program_id(2)
        part = _dg(a_ref[...], b_ref[...], dims)

        def finish(total):
            res = epi(total, *[r[...] for r in ex_refs])
            for r, v in zip(out_refs, res):
                r[...] = v.astype(r.dtype)

        if nk == 1:
            finish(part)
            return
        acc_ref = refs[-1]

        @pl.when(kk == 0)
        def _():
            acc_ref[...] = part

        @pl.when(jnp.logical_and(kk > 0, kk < nk - 1))
        def _():
            acc_ref[...] += part

        @pl.when(kk == nk - 1)
        def _():
            finish(acc_ref[...] + part)

    outs = pl.pallas_call(
        body, name=name, grid=(m // tm, n // tn, nk),
        in_specs=[a_spec, b_spec] + [o_spec] * n_ex + [pl.BlockSpec(memory_space=pl.ANY)] * len(order),
        out_specs=[out_spec] * n_out,
        out_shape=[jax.ShapeDtypeStruct(out_dims, d) for d in out_dtypes],
        scratch_shapes=[pltpu.VMEM((tm, tn), F32)] if nk > 1 else [],
        compiler_params=_cparams(("parallel", "parallel", "arbitrary")),
    )(a, b, *extras, *order)
    return outs[0] if n_out == 1 else outs


def _rowwise(fn, name, rows, tr, row_ins, params, row_outs, acc_outs=(), ncol=1):
    tr = _tile(rows, tr, 8)
    n_in, n_par, n_ro, n_acc = len(row_ins), len(params), len(row_outs), len(acc_outs)
    in_specs = []
    for _, width, base, per_col in row_ins:
        if per_col:
            in_specs.append(pl.BlockSpec((tr, width), lambda i, j, base=base: (i, base + j)))
        else:
            in_specs.append(pl.BlockSpec((tr, width), lambda i, j, base=base: (i, base)))
    for p in params:
        in_specs.append(pl.BlockSpec(p.shape, lambda i, j: (0, 0)))
    out_specs, out_shape = [], []
    for total, dtype, width, per_col in row_outs:
        if per_col:
            out_specs.append(pl.BlockSpec((tr, width), lambda i, j: (i, j)))
        else:
            out_specs.append(pl.BlockSpec((tr, width), lambda i, j: (i, 0)))
        out_shape.append(jax.ShapeDtypeStruct((rows, total), dtype))
    for shape, dtype in acc_outs:
        out_specs.append(pl.BlockSpec(shape, lambda i, j: (0, 0)))
        out_shape.append(jax.ShapeDtypeStruct(shape, dtype))

    def body(*refs):
        ins = [r[...] for r in refs[:n_in + n_par]]
        ro = refs[n_in + n_par:n_in + n_par + n_ro]
        ao = refs[n_in + n_par + n_ro:]
        res = fn(*ins)
        for r, v in zip(ro, res[:n_ro]):
            r[...] = v.astype(r.dtype)
        first = jnp.logical_and(pl.program_id(0) == 0, pl.program_id(1) == 0)

        @pl.when(first)
        def _():
            for r in ao:
                r[...] = jnp.zeros_like(r)

        for r, v in zip(ao, res[n_ro:]):
            r[...] += v.astype(r.dtype)

    sem = ("arbitrary", "arbitrary") if n_acc else ("parallel", "parallel")
    outs = pl.pallas_call(
        body, name=name, grid=(rows // tr, ncol), in_specs=in_specs, out_specs=out_specs,
        out_shape=out_shape, compiler_params=_cparams(sem),
    )(*[r[0] for r in row_ins], *params)
    return outs


def _row(x):
    return x.reshape(1, -1)


def _rms_fwd(x, gain, name):
    s, d = x.shape
    return _rowwise(lambda xt, g: (_rms(xt, g),), name, s, 512, [(x, d, 0, False)], [_row(gain)],
                    [(d, BF, d, False)])[0]


def _rms_bwd(x, gain, dh, dres, name):
    s, d = x.shape

    def fn(xt, dht, drt, g):
        _, vjp = jax.vjp(_rms, xt, g)
        dx, dg = vjp(dht)
        return drt + dx, dg

    return _rowwise(fn, name, s, 256, [(x, d, 0, False), (dh, d, 0, False), (dres, d, 0, False)], [_row(gain)],
                    [(d, F32, d, False)], [((1, d), F32)])


def _loss_head(x, gain, target, name):
    s, d = x.shape

    def fn(xt, tt, g):
        y, vjp = jax.vjp(_rms, xt, g)
        err = y - tt
        dx, dg = vjp(err * (1.0 / d))
        part = 0.5 * jnp.sum(jnp.sum(err * err, axis=1, keepdims=True), axis=0, keepdims=True) * (1.0 / d)
        return dx, jnp.broadcast_to(part, (1, LANE)), dg

    return _rowwise(fn, name, s, 256, [(x, d, 0, False), (target, d, 0, False)], [_row(gain)],
                    [(d, F32, d, False)], [((1, LANE), F32), ((1, d), F32)])


def _gdn_post_fn(o, z, g):
    return _rms(o, g) * _silu(z)


def _gdn_post_fwd(o, proj, gain, z_base, name):
    s = o.shape[0]
    return _rowwise(lambda ot, zt, g: (_gdn_post_fn(ot, zt, g),), name, s, 1024,
                    [(o, HEAD_DIM, 0, True), (proj, HEAD_DIM, z_base, True)], [_row(gain)],
                    [(GDN_WIDTH, BF, HEAD_DIM, True)], ncol=HEADS)[0]


def _gdn_post_bwd(o, proj, gain, dog, z_base, name):
    s = o.shape[0]

    def fn(ot, zt, dt, g):
        _, vjp = jax.vjp(_gdn_post_fn, ot, zt, g)
        return vjp(dt)

    return _rowwise(fn, name, s, 1024,
                    [(o, HEAD_DIM, 0, True), (proj, HEAD_DIM, z_base, True), (dog, HEAD_DIM, 0, True)],
                    [_row(gain)], [(GDN_WIDTH, F32, HEAD_DIM, True), (GDN_WIDTH, BF, HEAD_DIM, True)],
                    [((1, HEAD_DIM), F32)], ncol=HEADS)


def _merge_fn(gg, gl, yg, yl):
    return _sigmoid(gg) * yg + _sigmoid(gl) * yl


def _merge_fwd(proj, yg, yl, gg_base, name):
    s, d = yg.shape
    return _rowwise(lambda a, b, c, e: (_merge_fn(a, b, c, e),), name, s, 256,
                    [(proj, d, gg_base, False), (proj, d, gg_base + 1, False), (yg, d, 0, False), (yl, d, 0, False)],
                    [], [(d, BF, d, False)])[0]


def _merge_bwd(proj, yg, yl, dmerged, gg_base, name):
    s, d = yg.shape

    def fn(a, b, c, e, dm):
        _, vjp = jax.vjp(_merge_fn, a, b, c, e)
        return vjp(dm)

    return _rowwise(fn, name, s, 128,
                    [(proj, d, gg_base, False), (proj, d, gg_base + 1, False), (yg, d, 0, False), (yl, d, 0, False),
                     (dmerged, d, 0, False)], [], [(d, BF, d, False)] * 4)


def _conv_silu_fwd(proj, w, name):
    s = proj.shape[0]
    cols = w.shape[1]
    cw = _tile(cols, LANE)

    def body(x_ref, w_ref, o_ref):
        o_ref[...] = _silu(_conv(x_ref[...], w_ref))

    return pl.pallas_call(
        body, name=name, grid=(cols // cw,),
        in_specs=[pl.BlockSpec((s, cw), lambda j: (0, j)), pl.BlockSpec((CONV_WIDTH, cw), lambda j: (0, j))],
        out_specs=pl.BlockSpec((s, cw), lambda j: (0, j)),
        out_shape=jax.ShapeDtypeStruct((s, cols), F32), compiler_params=_cparams(("parallel",)),
    )(proj, w)


def _conv_silu_bwd(proj, w, dout, name):
    s = proj.shape[0]
    cols = w.shape[1]
    cw = _tile(cols, LANE)

    def body(x_ref, w_ref, d_ref, dx_ref, dw_ref):
        x = x_ref[...]
        y = _conv(x, w_ref)
        sg = _sigmoid(y)
        dy = d_ref[...] * (sg * (1.0 + y * (1.0 - sg)))
        dx, dws = _conv_bwd(x, dy, w_ref)
        dx_ref[...] = dx.astype(dx_ref.dtype)
        for j in range(CONV_WIDTH):
            dw_ref[j:j + 1, :] = dws[j]

    return pl.pallas_call(
        body, name=name, grid=(cols // cw,),
        in_specs=[pl.BlockSpec((s, cw), lambda j: (0, j)), pl.BlockSpec((CONV_WIDTH, cw), lambda j: (0, j)),
                  pl.BlockSpec((s, cw), lambda j: (0, j))],
        out_specs=[pl.BlockSpec((s, cw), lambda j: (0, j)), pl.BlockSpec((CONV_WIDTH, cw), lambda j: (0, j))],
        out_shape=[jax.ShapeDtypeStruct((s, cols), BF), jax.ShapeDtypeStruct((CONV_WIDTH, cols), F32)],
        compiler_params=_cparams(("parallel",)),
    )(proj, w, dout)


def _cumsum_mask(tr, transpose):
    row = lax.broadcasted_iota(jnp.int32, (tr, tr), 0)
    col = lax.broadcasted_iota(jnp.int32, (tr, tr), 1)
    same = (row // CHUNK) == (col // CHUNK)
    tri = (row <= col) if transpose else (row >= col)
    return jnp.where(jnp.logical_and(same, tri), 1.0, 0.0)


def _lane_pad(v):
    return jnp.pad(v.reshape(1, -1), ((0, 0), (0, LANE - v.shape[0])))


def _gdn_gates_fwd(proj, a_log, dt_bias, ab_base, name):
    s = proj.shape[0]

    def fn(ab, al, dt):
        g = -jnp.exp(al) * _softplus(ab + dt)
        beta = _sigmoid(pltpu.roll(ab, LANE - HEADS, 1))
        gc = jnp.dot(_cumsum_mask(ab.shape[0], False), g, precision=lax.Precision.HIGHEST,
                     preferred_element_type=F32)
        return gc, beta

    return _rowwise(fn, name, s, 512, [(proj, LANE, ab_base, False)], [_lane_pad(a_log), _lane_pad(dt_bias)],
                    [(LANE, F32, LANE, False)] * 2)


def _gdn_gates_bwd(proj, a_log, dt_bias, dgc_col, dgc_row, dbeta, ab_base, name):
    s = proj.shape[0]

    def fn(ab, dc, dr, db, al, dt):
        lane = lax.broadcasted_iota(jnp.int32, ab.shape, 1)
        dg = jnp.dot(_cumsum_mask(ab.shape[0], True), dc + dr, precision=lax.Precision.HIGHEST,
                     preferred_element_type=F32)
        ea = jnp.exp(al)
        pre = ab + dt
        g = -ea * _softplus(pre)
        da = jnp.where(lane < HEADS, dg * (-ea) * _sigmoid(pre), 0.0)
        dal = jnp.sum(jnp.where(lane < HEADS, dg * g, 0.0), axis=0, keepdims=True)
        ddt = jnp.sum(da, axis=0, keepdims=True)
        beta = _sigmoid(pltpu.roll(ab, LANE - HEADS, 1))
        dbl = jnp.where(lane < HEADS, db * beta * (1.0 - beta), 0.0)
        dab = da + pltpu.roll(dbl, HEADS, 1)
        return dab, dal, ddt

    return _rowwise(fn, name, s, 512,
                    [(proj, LANE, ab_base, False), (dgc_col, LANE, 0, False), (dgc_row, LANE, 0, False),
                     (dbeta, LANE, 0, False)], [_lane_pad(a_log), _lane_pad(dt_bias)],
                    [(LANE, BF, LANE, False)], [((1, LANE), F32), ((1, LANE), F32)])


def _head_col(blk, h):
    lane = lax.broadcasted_iota(jnp.int32, blk.shape, 1)
    return jnp.sum(jnp.where(lane == h, blk, 0.0), axis=1, keepdims=True)


def _chunk_operands(qkv_ref, gc_ref, b_ref, gr_ref):
    heads = range(HEADS)
    q, k, v = (jnp.stack([qkv_ref[:, (j * HEADS + h) * HEAD_DIM:(j * HEADS + h + 1) * HEAD_DIM] for h in heads])
               for j in range(3))
    gcs, betas = gc_ref[...], b_ref[...]
    beta = jnp.stack([_head_col(betas, h) for h in heads])
    gcol = jnp.stack([_head_col(gcs, h) for h in heads])
    grow = jnp.stack([gr_ref[0, h:h + 1, :] for h in heads])
    return q, k, v, beta, gcol, grow


def _gdn_chunk_fwd(qkv, gc, beta, grow, name):
    s = qkv.shape[0]
    nc = s // CHUNK

    def body(qkv_ref, gc_ref, b_ref, gr_ref, o_ref, st_ref, t_ref, state):
        @pl.when(pl.program_id(0) == 0)
        def _():
            state[...] = jnp.zeros_like(state)

        st = state[...]
        st_ref[0] = st
        o, new, t = _gdn_chunk(*_chunk_operands(qkv_ref, gc_ref, b_ref, gr_ref), st, _PLAIN_B)
        t_ref[0] = t
        for h in range(HEADS):
            o_ref[:, h * HEAD_DIM:(h + 1) * HEAD_DIM] = o[h]
        state[...] = new

    return pl.pallas_call(
        body, name=name, grid=(nc,),
        in_specs=[pl.BlockSpec((CHUNK, 3 * GDN_WIDTH), lambda c: (c, 0)),
                  pl.BlockSpec((CHUNK, LANE), lambda c: (c, 0)), pl.BlockSpec((CHUNK, LANE), lambda c: (c, 0)),
                  pl.BlockSpec((1, HEADS, CHUNK), lambda c: (c, 0, 0))],
        out_specs=[pl.BlockSpec((CHUNK, GDN_WIDTH), lambda c: (c, 0)),
                   pl.BlockSpec((1, HEADS, HEAD_DIM, HEAD_DIM), lambda c: (c, 0, 0, 0)),
                   pl.BlockSpec((1, HEADS, CHUNK, CHUNK), lambda c: (c, 0, 0, 0))],
        out_shape=[jax.ShapeDtypeStruct((s, GDN_WIDTH), F32),
                   jax.ShapeDtypeStruct((nc, HEADS, HEAD_DIM, HEAD_DIM), F32),
                   jax.ShapeDtypeStruct((nc, HEADS, CHUNK, CHUNK), F32)],
        scratch_shapes=[pltpu.VMEM((HEADS, HEAD_DIM, HEAD_DIM), F32)],
        compiler_params=_cparams(("arbitrary",)),
    )(qkv, gc, beta, grow)


def _gdn_chunk_bwd(qkv, gc, beta, grow, states, tinv, do, name):
    s = qkv.shape[0]
    nc = s // CHUNK

    def body(qkv_ref, gc_ref, b_ref, gr_ref, st_ref, t_ref, do_ref, dqkv_ref, dgc_ref, db_ref, dgr_ref, dstate):
        @pl.when(pl.program_id(0) == 0)
        def _():
            dstate[...] = jnp.zeros_like(dstate)

        lane = lax.broadcasted_iota(jnp.int32, (CHUNK, LANE), 1)
        chunk = lambda *args: _gdn_chunk(*args, mm=_DIFF_B, t_saved=t_ref[0])[:2]
        _, vjp = jax.vjp(chunk, *_chunk_operands(qkv_ref, gc_ref, b_ref, gr_ref), st_ref[0])
        do = jnp.stack([do_ref[:, h * HEAD_DIM:(h + 1) * HEAD_DIM] for h in range(HEADS)])
        dq, dk, dv, dbeta, dgcol, dgrow, dst = vjp((do, dstate[...]))
        dstate[...] = dst
        dgc = jnp.zeros((CHUNK, LANE), F32)
        db = jnp.zeros((CHUNK, LANE), F32)
        for h in range(HEADS):
            for j, dx in enumerate((dq, dk, dv)):
                dqkv_ref[:, (j * HEADS + h) * HEAD_DIM:(j * HEADS + h + 1) * HEAD_DIM] = dx[h]
            dgc = dgc + jnp.where(lane == h, dgcol[h], 0.0)
            db = db + jnp.where(lane == h, dbeta[h], 0.0)
            dgr_ref[0, h:h + 1, :] = dgrow[h]
        dgc_ref[...] = dgc
        db_ref[...] = db

    rev = lambda c: nc - 1 - c
    wide = pl.BlockSpec((CHUNK, 3 * GDN_WIDTH), lambda c: (rev(c), 0))
    lanes = pl.BlockSpec((CHUNK, LANE), lambda c: (rev(c), 0))
    rows = pl.BlockSpec((1, HEADS, CHUNK), lambda c: (rev(c), 0, 0))
    return pl.pallas_call(
        body, name=name, grid=(nc,),
        in_specs=[wide, lanes, lanes, rows,
                  pl.BlockSpec((1, HEADS, HEAD_DIM, HEAD_DIM), lambda c: (rev(c), 0, 0, 0)),
                  pl.BlockSpec((1, HEADS, CHUNK, CHUNK), lambda c: (rev(c), 0, 0, 0)),
                  pl.BlockSpec((CHUNK, GDN_WIDTH), lambda c: (rev(c), 0))],
        out_specs=[wide, lanes, lanes, rows],
        out_shape=[jax.ShapeDtypeStruct((s, 3 * GDN_WIDTH), F32)]
        + [jax.ShapeDtypeStruct((s, LANE), F32)] * 2 + [jax.ShapeDtypeStruct((nc, HEADS, CHUNK), F32)],
        scratch_shapes=[pltpu.VMEM((HEADS, HEAD_DIM, HEAD_DIM), F32)],
        compiler_params=_cparams(("arbitrary",)),
    )(qkv, gc, beta, grow, states, tinv, do)


def _lru_specs(s, xb_base):
    col = lambda base: pl.BlockSpec((s, LRU_BLOCK_DIM), lambda n, base=base: (0, base + n))
    vec = pl.BlockSpec((1, LRU_BLOCK_DIM), lambda n: (0, n))
    mat = pl.BlockSpec((None, LRU_BLOCK_DIM, LRU_BLOCK_DIM), lambda n: (n, 0, 0))
    cw = pl.BlockSpec((CONV_WIDTH, LRU_BLOCK_DIM), lambda n: (0, n))
    return col, vec, mat, cw


def _lru_fwd(proj, cw, cb, wa, ba, wx, bx, lam, xb_base, name):
    s = proj.shape[0]
    tt = _tile(s, 256, 8)
    col, vec, mat, cws = _lru_specs(s, xb_base)

    def body(xb_ref, yb_ref, cw_ref, cb_ref, wa_ref, ba_ref, wx_ref, bx_ref, lam_ref, o_ref, h_ref, a_ref, xc_s):
        xc_s[...] = _conv(xb_ref[...], cw_ref) + cb_ref[...]
        par = (wa_ref[...], ba_ref[...], wx_ref[...], bx_ref[...], lam_ref[...])

        def step(t, carry):
            sl = pl.ds(pl.multiple_of(t * tt, tt), tt)
            a, bt = _lru_gates(xc_s[sl, :], *par, _PLAIN)
            a_ref[sl, :] = a
            aa, bb = _scan_tile(a, bt, False)
            h = aa * carry + bb
            h_ref[sl, :] = h
            o_ref[sl, :] = (h * _gelu(yb_ref[sl, :])).astype(o_ref.dtype)
            return h_ref[pl.ds(t * tt + tt - 1, 1), :]

        lax.fori_loop(0, s // tt, step, jnp.zeros((1, LRU_BLOCK_DIM), F32))

    nb = LRU_BLOCKS
    return pl.pallas_call(
        body, name=name, grid=(LRU_BLOCKS,),
        in_specs=[col(xb_base), col(xb_base + nb), cws, vec, mat, vec, mat, vec, vec],
        out_specs=[col(0), col(0), col(0)],
        out_shape=[jax.ShapeDtypeStruct((s, LRU_WIDTH), BF), jax.ShapeDtypeStruct((s, LRU_WIDTH), F32),
                   jax.ShapeDtypeStruct((s, LRU_WIDTH), F32)],
        scratch_shapes=[pltpu.VMEM((s, LRU_BLOCK_DIM), F32)],
        compiler_params=_cparams(("parallel",)),
    )(proj, proj, cw, _row(cb), wa, _row(ba), wx, _row(bx), _row(lam))


def _lru_bwd(proj, hs, decay, dol, cw, cb, wa, ba, wx, bx, lam, xb_base, name):
    s = proj.shape[0]
    tt = _tile(s, 256, 8)
    nt = s // tt
    col, vec, mat, cws = _lru_specs(s, xb_base)

    def body(xb_ref, yb_ref, h_ref, a_ref, d_ref, cw_ref, cb_ref, wa_ref, ba_ref, wx_ref, bx_ref, lam_ref,
             dxb_ref, dyb_ref, dcw_ref, dcb_ref, dwa_ref, dba_ref, dwx_ref, dbx_ref, dlam_ref,
             xc_s, dh_s, dxc_s):
        xc_s[...] = _conv(xb_ref[...], cw_ref) + cb_ref[...]
        par = (wa_ref[...], ba_ref[...], wx_ref[...], bx_ref[...], lam_ref[...])
        row = lax.broadcasted_iota(jnp.int32, (tt, LRU_BLOCK_DIM), 0)
        tile = lambda t: pl.ds(pl.multiple_of(t * tt, tt), tt)

        def prep(t, carry):
            sl = tile(t)
            d = d_ref[sl, :]
            gy, vjp_y = jax.vjp(_gelu, yb_ref[sl, :])
            dyb_ref[sl, :] = vjp_y(d * h_ref[sl, :])[0].astype(dyb_ref.dtype)
            dh_s[sl, :] = d * gy
            return carry

        lax.fori_loop(0, nt, prep, 0)

        def rscan(i, carry):
            dh_next, a_next = carry
            t = nt - 1 - i
            sl = tile(t)
            a_sh = jnp.where(row == tt - 1, a_next, pltpu.roll(a_ref[sl, :], tt - 1, 0))
            aa, bb = _scan_tile(a_sh, dh_s[sl, :], True)
            dh_s[sl, :] = aa * dh_next + bb
            first = pl.ds(t * tt, 1)
            return dh_s[first, :], a_ref[first, :]

        zero = jnp.zeros((1, LRU_BLOCK_DIM), F32)
        lax.fori_loop(0, nt, rscan, (zero, zero))

        def gates_vjp(t, acc):
            sl = tile(t)
            _, vjp_g = jax.vjp(functools.partial(_lru_gates, mm=_DIFF), xc_s[sl, :], *par)
            dh = dh_s[sl, :]
            before = jnp.where(t > 0, h_ref[pl.ds(jnp.maximum(t * tt - 1, 0), 1), :], 0.0)
            h_prev = jnp.where(row == 0, before, pltpu.roll(h_ref[sl, :], 1, 0))
            dxc, *dpar = vjp_g((dh * h_prev, dh))
            dxc_s[sl, :] = dxc
            return tuple(x + y for x, y in zip(acc, dpar))

        dwa, dba, dwx, dbx, dlam = lax.fori_loop(0, nt, gates_vjp, tuple(jnp.zeros_like(p) for p in par))
        dwa_ref[...] = dwa
        dba_ref[...] = dba
        dwx_ref[...] = dwx
        dbx_ref[...] = dbx
        dlam_ref[...] = dlam
        dxc = dxc_s[...]
        dcb_ref[...] = jnp.sum(dxc, axis=0, keepdims=True)
        dxb, dws = _conv_bwd(xb_ref[...], dxc, cw_ref)
        dxb_ref[...] = dxb.astype(dxb_ref.dtype)
        for j in range(CONV_WIDTH):
            dcw_ref[j:j + 1, :] = dws[j]

    nb = LRU_BLOCKS
    w = LRU_WIDTH
    return pl.pallas_call(
        body, name=name, grid=(LRU_BLOCKS,),
        in_specs=[col(xb_base), col(xb_base + nb), col(0), col(0), col(0), cws, vec, mat, vec, mat, vec, vec],
        out_specs=[col(0), col(0), cws, vec, mat, vec, mat, vec, vec],
        out_shape=[jax.ShapeDtypeStruct((s, w), BF), jax.ShapeDtypeStruct((s, w), BF),
                   jax.ShapeDtypeStruct((CONV_WIDTH, w), F32), jax.ShapeDtypeStruct((1, w), F32),
                   jax.ShapeDtypeStruct(wa.shape, F32), jax.ShapeDtypeStruct((1, w), F32),
                   jax.ShapeDtypeStruct(wx.shape, F32), jax.ShapeDtypeStruct((1, w), F32),
                   jax.ShapeDtypeStruct((1, w), F32)],
        scratch_shapes=[pltpu.VMEM((s, LRU_BLOCK_DIM), F32)] * 3,
        compiler_params=_cparams(("parallel",)),
    )(proj, proj, hs, decay, dol, cw, _row(cb), wa, _row(ba), wx, _row(bx), _row(lam))


def _place():
    return lax.axis_index("x"), lax.axis_index("y"), lax.axis_index("c")


def _all_gather(blocks, name):
    n = len(blocks)
    per = N_DEV - 1

    def body(*refs):
        ins, outs = refs[:n], refs[n:2 * n]
        send_sems, recv_sems, local_sems = refs[2 * n:]
        x, y, c = _place()
        me, sibling = (x, y, c), (x, y, 1 - c)
        chips = [(1 - x, y), (x, 1 - y), (1 - x, 1 - y)]

        def copy(a, k, block, to, src=None):
            dst = outs[a].at[4 * block[0] + 2 * block[1] + block[2]]
            return pltpu.make_async_remote_copy(
                src_ref=dst if src is None else src, dst_ref=dst, send_sem=send_sems.at[a * per + k],
                recv_sem=recv_sems.at[a * per + k], device_id=to, device_id_type=MESH)

        started = []
        for a in range(n):
            mine = pltpu.make_async_copy(ins[a], outs[a].at[4 * x + 2 * y + c], local_sems.at[a])
            mine.start()
            started.append(mine)
        sends = []
        for a in range(n):
            first = [copy(a, 0, me, sibling, src=ins[a])]
            first += [copy(a, 1 + j, me, (*chip, c), src=ins[a]) for j, chip in enumerate(chips)]
            for cp in first:
                cp.start()
            sends += first
        for a in range(n):
            for j, chip in enumerate(chips):
                copy(a, 1 + j, (*chip, c), me).wait_recv()
                passed = copy(a, 4 + j, (*chip, c), sibling)
                passed.start()
                sends.append(passed)
        for a in range(n):
            copy(a, 0, sibling, me).wait_recv()
            for j, chip in enumerate(chips):
                copy(a, 4 + j, (*chip, 1 - c), me).wait_recv()
        for cp in sends:
            cp.wait_send()
        for cp in started:
            cp.wait()

    any_spec = pl.BlockSpec(memory_space=pl.ANY)
    return pl.pallas_call(
        body, name=name, in_specs=[any_spec] * n, out_specs=[any_spec] * n,
        out_shape=[jax.ShapeDtypeStruct((N_DEV,) + b.shape, b.dtype) for b in blocks],
        scratch_shapes=[pltpu.SemaphoreType.DMA((n * per,)), pltpu.SemaphoreType.DMA((n * per,)),
                        pltpu.SemaphoreType.DMA((n,))],
    )(*blocks)


_HBM = pl.BlockSpec(memory_space=pltpu.HBM)
_SEM = pl.BlockSpec(memory_space=pltpu.SEMAPHORE)
_ANY = pl.BlockSpec(memory_space=pl.ANY)
_EFFECT = pltpu.SideEffectType.DATAFLOW_SIDE_EFFECTING


def _peers(x, y, c):
    return [(x, y, 1 - c), (1 - x, y, c), (1 - x, y, 1 - c), (x, 1 - y, c), (x, 1 - y, 1 - c),
            (1 - x, 1 - y, c), (1 - x, 1 - y, 1 - c)]


class _InFlight:
    def __init__(self, send_sem, recv_sem, src, land):
        self.send_sem, self.recv_sem, self.src, self.land = send_sem, recv_sem, src, land


def _send_start(srcs, whole, after, name):
    n = len(srcs)
    lands = [lax.empty(((N_DEV,) + s.shape) if whole else s.shape, s.dtype) for s in srcs]

    def body(*refs):
        src_refs, land_refs = refs[:n], refs[n:2 * n]
        outs = refs[2 * n + 1:]
        send_sems, recv_sems, token = outs[:n], outs[n:2 * n], outs[-1]
        x, y, c = _place()
        me = 4 * x + 2 * y + c
        for a in range(n):
            for p in _peers(x, y, c):
                src = src_refs[a] if whole else src_refs[a].at[4 * p[0] + 2 * p[1] + p[2]]
                pltpu.make_async_remote_copy(src_ref=src, dst_ref=land_refs[a].at[me], send_sem=send_sems[a],
                                             recv_sem=recv_sems[a], device_id=p, device_id_type=MESH).start()
        token[...] = jnp.zeros_like(token)

    hbm = lambda a: pltpu.HBM(a.shape, a.dtype)
    outs = pl.pallas_call(
        body, name=name,
        out_shape=[pltpu.SemaphoreType.DMA(())] * (2 * n) + [hbm(s) for s in srcs] + [hbm(l) for l in lands]
        + [jax.ShapeDtypeStruct((8, LANE), F32)],
        in_specs=[_HBM] * (2 * n) + [_ANY],
        out_specs=[_SEM] * (2 * n) + [_HBM] * (2 * n) + [pl.BlockSpec(memory_space=pltpu.VMEM)],
        input_output_aliases={i: 2 * n + i for i in range(2 * n)},
        compiler_params=pltpu.CompilerParams(has_side_effects=_EFFECT),
    )(*[pltpu.with_memory_space_constraint(s, pltpu.HBM) for s in srcs],
      *[pltpu.with_memory_space_constraint(l, pltpu.HBM) for l in lands], after)
    flights = [_InFlight(outs[a], outs[n + a], outs[2 * n + a], outs[3 * n + a]) for a in range(n)]
    return flights, outs[-1]


def _send_wait(flights, after, name):
    n = len(flights)

    def body(*refs):
        x, y, c = _place()
        for a in range(n):
            land_ref, send_sem, recv_sem = refs[4 * a + 1], refs[4 * a + 2], refs[4 * a + 3]
            seven = land_ref.at[pl.ds(0, N_DEV - 1)]
            copy = pltpu.make_async_remote_copy(src_ref=seven, dst_ref=seven, send_sem=send_sem, recv_sem=recv_sem,
                                                device_id=(x, y, 1 - c), device_id_type=MESH)
            copy.wait_send()
            copy.wait_recv()

    ins, out_shape = [], []
    for f in flights:
        ins += [f.src, f.land, f.send_sem, f.recv_sem]
        out_shape += [pltpu.HBM(f.src.shape, f.src.dtype), pltpu.HBM(f.land.shape, f.land.dtype)]
    outs = pl.pallas_call(
        body, name=name, out_shape=out_shape,
        in_specs=[_HBM, _HBM, _SEM, _SEM] * n + [_ANY], out_specs=[_HBM] * (2 * n),
        input_output_aliases={4 * a + i: 2 * a + i for a in range(n) for i in range(2)},
        compiler_params=pltpu.CompilerParams(has_side_effects=_EFFECT),
    )(*ins, after)
    return [(outs[2 * a], outs[2 * a + 1]) for a in range(n)]


def _adamw_math(w, g, m, v):
    m = ADAM_B1 * m + (1.0 - ADAM_B1) * g
    v = ADAM_B2 * v + (1.0 - ADAM_B2) * (g * g)
    m_hat = m / (1.0 - ADAM_B1 ** ADAM_STEP)
    v_hat = v / (1.0 - ADAM_B2 ** ADAM_STEP)
    delta = -ADAM_LR * (m_hat / (jnp.sqrt(v_hat) + ADAM_EPS) + ADAM_WD * w)
    return delta, m, v


def _adamw_sharded(mine, landed, me, w, m, v, layer, prev, after, name):
    depth, r, c = w.shape
    tr = _tile(r, 128, 8)
    n_prev = 0 if prev is None else 4

    def body(*refs):
        me_ref, own_ref, land_ref, w_ref, m_ref, v_ref = refs[:6]
        g_ref, d_ref, nm_ref, nv_ref = refs[7 + n_prev:]
        g = None
        for k in range(N_DEV):
            part = jnp.where(me_ref[0] == k, own_ref[0], land_ref[k]).astype(F32)
            g = part if g is None else g + part
        delta, nm, nv = _adamw_math(w_ref[...], g, m_ref[...], v_ref[...])
        g_ref[...] = g
        d_ref[...] = delta
        nm_ref[...] = nm
        nv_ref[...] = nv

    lay = pl.BlockSpec((None, tr, c), lambda i, me_ref: (layer, i, 0))
    return pl.pallas_call(
        body, name=name,
        grid_spec=pltpu.PrefetchScalarGridSpec(
            num_scalar_prefetch=1, grid=(r // tr,),
            in_specs=[pl.BlockSpec((1, tr, c), lambda i, me_ref: (me_ref[0], i, 0)),
                      pl.BlockSpec((N_DEV, tr, c), lambda i, me_ref: (0, i, 0)), lay, lay, lay]
            + [_ANY] * (1 + n_prev),
            out_specs=[lay] * 4),
        out_shape=[jax.ShapeDtypeStruct(w.shape, F32)] * 4,
        input_output_aliases={7 + i: i for i in range(n_prev)},
        compiler_params=_cparams(("parallel",)),
    )(me, mine, landed, w, m, v, after, *(prev or ()))


def _sum_devices(stack, name):
    _, r, c = stack.shape
    tr = _tile(r, PACK_ROWS, 8)

    def body(s_ref, o_ref):
        g = s_ref[0]
        for j in range(1, N_DEV):
            g = g + s_ref[j]
        o_ref[...] = g

    return pl.pallas_call(
        body, name=name, grid=(r // tr,), in_specs=[pl.BlockSpec((N_DEV, tr, c), lambda i: (0, i, 0))],
        out_specs=pl.BlockSpec((tr, c), lambda i: (i, 0)), out_shape=jax.ShapeDtypeStruct((r, c), F32),
        compiler_params=_cparams(("parallel",)),
    )(stack)


def _adamw_packed(w, g, m, v, name):
    r, c = w.shape
    tr = _tile(r, PACK_ROWS, 8)

    def body(w_ref, g_ref, m_ref, v_ref, d_ref, nm_ref, nv_ref):
        delta, nm, nv = _adamw_math(w_ref[...], g_ref[...], m_ref[...], v_ref[...])
        d_ref[...] = delta
        nm_ref[...] = nm
        nv_ref[...] = nv

    spec = pl.BlockSpec((tr, c), lambda i: (i, 0))
    return pl.pallas_call(
        body, name=name, grid=(r // tr,), in_specs=[spec] * 4, out_specs=[spec] * 3,
        out_shape=[jax.ShapeDtypeStruct((r, c), F32)] * 3, compiler_params=_cparams(("parallel",)),
    )(w, g, m, v)


def _pack(arrays):
    unit = 8 * LANE
    flat = []
    for a in arrays:
        f = a.reshape(-1)
        flat.append(jnp.pad(f, (0, (-f.shape[0]) % unit)))
    total = sum(f.shape[0] for f in flat)
    flat.append(jnp.zeros(((-total) % (PACK_ROWS * LANE),), flat[0].dtype))
    return jnp.concatenate(flat).reshape(-1, LANE)


def _unpack(packed, like):
    unit = 8 * LANE
    flat = packed.reshape(-1)
    out, off = [], 0
    for a in like:
        n = 1
        for d in a.shape:
            n *= d
        out.append(flat[off:off + n].reshape(a.shape))
        off += n + (-n) % unit
    return out


class _Layout:
    def __init__(self, d):
        self.d = d
        self.n_front = 4 * GDN_WIDTH
        self.n_ab = 2 * HEADS
        self.n_back = 2 * LRU_WIDTH + 2 * d
        self.cols = self.n_front + self.n_ab + self.n_back
        self.pcols = self.n_front + self.n_back + LANE
        self.z_base = 3 * GDN_WIDTH // HEAD_DIM
        self.xb_base = self.n_front // LRU_BLOCK_DIM
        self.gg_off = self.n_front + 2 * LRU_WIDTH
        assert self.gg_off % d == 0
        self.gg_base = self.gg_off // d
        self.ab_base = (self.n_front + self.n_back) // LANE

    def _ranges(self):
        f, ab = self.n_front, self.n_ab
        return [(0, f), (f + ab, self.cols), (f, f + ab)]

    def permuted_from_stack(self, stack):
        blk = stack.shape[2]
        pieces = []
        for lo, hi in self._ranges():
            while lo < hi:
                k = lo // blk
                end = min(hi, (k + 1) * blk)
                pieces.append(stack[k, :, lo - k * blk:end - k * blk])
                lo = end
        pieces.append(jnp.zeros((stack.shape[1], LANE - self.n_ab), stack.dtype))
        return jnp.concatenate(pieces, axis=1)

    def stack_from_permuted(self, w):
        blk = self.cols // N_DEV
        where = {}
        pos = 0
        for lo, hi in self._ranges():
            where[lo] = pos
            pos += hi - lo
        cuts = sorted(lo for lo, _ in self._ranges()) + [self.cols]
        blocks = []
        for k in range(N_DEV):
            lo, pieces = k * blk, []
            while lo < (k + 1) * blk:
                start = max(c for c in cuts if c <= lo)
                end = min((k + 1) * blk, min(c for c in cuts if c > lo))
                pieces.append(w[:, where[start] + lo - start:where[start] + end - start])
                lo = end
            blocks.append(pieces[0] if len(pieces) == 1 else jnp.concatenate(pieces, axis=1))
        return jnp.stack(blocks)


def _cols_from_stack(stack):
    nd, k, n = stack.shape
    return stack.transpose(1, 0, 2).reshape(k, nd * n)


def _stack_from_cols(w):
    k, n = w.shape
    return w.reshape(k, N_DEV, n // N_DEV).transpose(1, 0, 2)


def _after(v, *tokens):
    for t in tokens:
        v = v + t[0, 0]
    return v


def _layer_fwd(x, p, late, lay, tag):
    s, d = x.shape
    nc = s // CHUNK
    h = _rms_fwd(x, p["attn_norm"], f"rms1_fwd{tag}")
    proj = _matmul(h, p["w_in"], "nn", f"in_proj{tag}", [F32], tn=1152)
    qkv = _conv_silu_fwd(proj, p["gdn_conv_w"], f"gdn_conv_fwd{tag}")
    gc, beta = _gdn_gates_fwd(proj, p["gdn_a_log"], p["gdn_dt_bias"], lay.ab_base, f"gdn_gates_fwd{tag}")
    grow = gc[:, :HEADS].reshape(nc, CHUNK, HEADS).transpose(0, 2, 1)
    o, states, tinv = _gdn_chunk_fwd(qkv, gc, beta, grow, f"gdn_chunk_fwd{tag}")
    og = _gdn_post_fwd(o, proj, p["gdn_norm"], lay.z_base, f"gdn_post_fwd{tag}")
    ol, hs, decay = _lru_fwd(proj, p["lru_conv_w"], p["lru_conv_b"], p["lru_w_a"], p["lru_b_a"], p["lru_w_x"], p["lru_b_x"],
                      p["lru_lambda"], lay.xb_base, f"lru_fwd{tag}")
    late[0](hs)
    yg = _matmul(og, p["w_branch_gdn"], "nn", f"branch_gdn{tag}", [F32])
    yl = _matmul(ol, p["w_branch_lru"], "nn", f"branch_lru{tag}", [F32])
    merged = _merge_fwd(proj, yg, yl, lay.gg_base, f"merge_fwd{tag}")
    x1 = _matmul(merged, p["w_out"], "nn", f"out_proj{tag}", [F32], epi=lambda acc, r: (acc + r,), extras=[x])
    h2 = _rms_fwd(x1, p["mlp_norm"], f"rms2_fwd{tag}")
    late[1](h2)
    u, act = _matmul(h2, p["w_up"], "nn", f"mlp_up{tag}", [F32, BF],
                     epi=lambda acc: (acc, jnp.square(jnp.maximum(acc, 0.0))), shards=True)
    x2 = _matmul(act, p["w_down"], "nn", f"mlp_down{tag}", [F32], epi=lambda acc, r: (acc + r,), extras=[x1])
    saved = dict(x=x, h=h, proj=proj, qkv=qkv, gc=gc, beta=beta, grow=grow, o=o, states=states, tinv=tinv, og=og, ol=ol, hs=hs, decay=decay,
                 yg=yg, yl=yl, merged=merged, x1=x1, h2=h2, u=u, act=act)
    return x2, saved


def _layer_bwd(dx2, p, sv, send, send_small, after_last_send, after, lay, tag):
    s, d = dx2.shape
    nc = s // CHUNK
    small = {}
    du = _matmul(dx2, p["w_down"], "nt", f"d_act{tag}", [BF],
                 epi=lambda acc, u: (acc * (2.0 * jnp.maximum(u, 0.0)),), extras=[sv["u"]], after=after)
    sent = send({"w_down": _matmul(sv["act"], dx2, "tn", f"dw_down{tag}", [BF]).reshape(N_DEV, -1, d),
                 "w_up": _matmul(sv["h2"], du, "tn", f"dw_up{tag}", [BF], shards=True)})
    dh2 = _matmul(du, p["w_up"], "nt", f"d_h2{tag}", [F32], shards=True)
    dx1, g = _rms_bwd(sv["x1"], _after(p["mlp_norm"], sent), dh2, dx2, f"rms2_bwd{tag}")
    small["mlp_norm"] = g.reshape(-1)
    dmerged = _matmul(dx1, p["w_out"], "nt", f"d_merged{tag}", [F32])
    dwout = _matmul(sv["merged"], dx1, "tn", f"dw_out{tag}", [BF]).reshape(N_DEV, -1, d)
    dgg, dgl, dyg, dyl = _merge_bwd(sv["proj"], sv["yg"], sv["yl"], dmerged, lay.gg_base, f"merge_bwd{tag}")
    sent = send({"w_out": dwout,
                 "w_branch_gdn": _stack_from_cols(_matmul(sv["og"], dyg, "tn", f"dw_bg{tag}", [BF])),
                 "w_branch_lru": _stack_from_cols(_matmul(sv["ol"], dyl, "tn", f"dw_bl{tag}", [BF]))})
    dog = _matmul(dyg, p["w_branch_gdn"], "nt", f"d_og{tag}", [F32])
    dol = _matmul(dyl, p["w_branch_lru"], "nt", f"d_ol{tag}", [F32])
    (dxb, dyb, dcw, dcb, dwa, dba, dwx, dbx, dlam) = _lru_bwd(
        sv["proj"], sv["hs"], sv["decay"], dol, p["lru_conv_w"], p["lru_conv_b"], p["lru_w_a"], p["lru_b_a"], p["lru_w_x"],
        p["lru_b_x"], _after(p["lru_lambda"], sent), lay.xb_base, f"lru_bwd{tag}")
    small.update(lru_conv_w=dcw, lru_conv_b=dcb.reshape(-1), lru_w_a=dwa, lru_b_a=dba.reshape(-1), lru_w_x=dwx,
                 lru_b_x=dbx.reshape(-1), lru_lambda=dlam.reshape(-1))
    do, dz, g = _gdn_post_bwd(sv["o"], sv["proj"], p["gdn_norm"], dog, lay.z_base, f"gdn_post_bwd{tag}")
    small["gdn_norm"] = g.reshape(-1)
    dqkv_c, dgc_col, dbeta, dgrow = _gdn_chunk_bwd(sv["qkv"], sv["gc"], sv["beta"], sv["grow"], sv["states"], sv["tinv"], do,
                                                   f"gdn_chunk_bwd{tag}")
    dqkv, dgcw = _conv_silu_bwd(sv["proj"], p["gdn_conv_w"], dqkv_c, f"gdn_conv_bwd{tag}")
    small["gdn_conv_w"] = dgcw
    dgc_row = jnp.pad(dgrow.transpose(0, 2, 1).reshape(s, HEADS), ((0, 0), (0, LANE - HEADS)))
    dab, dal, ddt = _gdn_gates_bwd(sv["proj"], p["gdn_a_log"], p["gdn_dt_bias"], dgc_col, dgc_row, dbeta, lay.ab_base,
                                   f"gdn_gates_bwd{tag}")
    small["gdn_a_log"] = dal[0, :HEADS]
    small["gdn_dt_bias"] = ddt[0, :HEADS]
    dproj = jnp.concatenate([dqkv, dz, dxb, dyb, dgg, dgl, dab], axis=1)
    dh = _matmul(dproj, p["w_in"], "nt", f"d_h{tag}", [F32], tk=3456)
    dx, g = _rms_bwd(sv["x"], p["attn_norm"], dh, dx1, f"rms1_bwd{tag}")
    small["attn_norm"] = g.reshape(-1)
    sent = send_small(small)
    dwin = _matmul(sv["h"], dproj, "tn", f"dw_in{tag}", [BF], tn=1152, after=sent)
    after_last_send(send({"w_in": lay.stack_from_permuted(dwin)}))
    return dx


_BIG = ("w_in", "w_branch_gdn", "w_branch_lru", "w_out", "w_up", "w_down")
_ROW_SHARDED = ("w_out", "w_down")
_CONV = ("gdn_conv_w", "lru_conv_w")
_SMALL = ("attn_norm", "gdn_a_log", "gdn_dt_bias", "gdn_norm", "lru_conv_b", "lru_w_a", "lru_b_a", "lru_w_x",
          "lru_b_x", "lru_lambda", "mlp_norm")
_ORDER = ("attn_norm", "w_in", "gdn_conv_w", "gdn_a_log", "gdn_dt_bias", "gdn_norm", "lru_conv_w", "lru_conv_b",
          "lru_w_a", "lru_b_a", "lru_w_x", "lru_b_x", "lru_lambda", "w_branch_gdn", "w_branch_lru", "w_out",
          "mlp_norm", "w_up", "w_down", "final_norm")


def kernel(x, attn_norm, w_in, gdn_conv_w, gdn_a_log, gdn_dt_bias, gdn_norm, lru_conv_w, lru_conv_b, lru_w_a, lru_b_a, lru_w_x, lru_b_x, lru_lambda, w_branch_gdn, w_branch_lru, w_out, mlp_norm, w_up, w_down, final_norm, loss_target, m_attn_norm, m_w_in, m_gdn_conv_w, m_gdn_a_log, m_gdn_dt_bias, m_gdn_norm, m_lru_conv_w, m_lru_conv_b, m_lru_w_a, m_lru_b_a, m_lru_w_x, m_lru_b_x, m_lru_lambda, m_w_branch_gdn, m_w_branch_lru, m_w_out, m_mlp_norm, m_w_up, m_w_down, m_final_norm, v_attn_norm, v_w_in, v_gdn_conv_w, v_gdn_a_log, v_gdn_dt_bias, v_gdn_norm, v_lru_conv_w, v_lru_conv_b, v_lru_w_a, v_lru_b_a, v_lru_w_x, v_lru_b_x, v_lru_lambda, v_w_branch_gdn, v_w_branch_lru, v_w_out, v_mlp_norm, v_w_up, v_w_down, v_final_norm):
    w = dict(attn_norm=attn_norm, w_in=w_in, gdn_conv_w=gdn_conv_w, gdn_a_log=gdn_a_log, gdn_dt_bias=gdn_dt_bias, gdn_norm=gdn_norm, lru_conv_w=lru_conv_w, lru_conv_b=lru_conv_b, lru_w_a=lru_w_a, lru_b_a=lru_b_a, lru_w_x=lru_w_x, lru_b_x=lru_b_x, lru_lambda=lru_lambda, w_branch_gdn=w_branch_gdn, w_branch_lru=w_branch_lru, w_out=w_out, mlp_norm=mlp_norm, w_up=w_up, w_down=w_down, final_norm=final_norm)
    m = dict(attn_norm=m_attn_norm, w_in=m_w_in, gdn_conv_w=m_gdn_conv_w, gdn_a_log=m_gdn_a_log, gdn_dt_bias=m_gdn_dt_bias, gdn_norm=m_gdn_norm, lru_conv_w=m_lru_conv_w, lru_conv_b=m_lru_conv_b, lru_w_a=m_lru_w_a, lru_b_a=m_lru_b_a, lru_w_x=m_lru_w_x, lru_b_x=m_lru_b_x, lru_lambda=m_lru_lambda, w_branch_gdn=m_w_branch_gdn, w_branch_lru=m_w_branch_lru, w_out=m_w_out, mlp_norm=m_mlp_norm, w_up=m_w_up, w_down=m_w_down, final_norm=m_final_norm)
    v = dict(attn_norm=v_attn_norm, w_in=v_w_in, gdn_conv_w=v_gdn_conv_w, gdn_a_log=v_gdn_a_log, gdn_dt_bias=v_gdn_dt_bias, gdn_norm=v_gdn_norm, lru_conv_w=v_lru_conv_w, lru_conv_b=v_lru_conv_b, lru_w_a=v_lru_w_a, lru_b_a=v_lru_b_a, lru_w_x=v_lru_w_x, lru_b_x=v_lru_b_x, lru_lambda=v_lru_lambda, w_branch_gdn=v_w_branch_gdn, w_branch_lru=v_w_branch_lru, w_out=v_w_out, mlp_norm=v_mlp_norm, w_up=v_w_up, w_down=v_w_down, final_norm=v_final_norm)
    depth = w_in.shape[0]
    x = x[0]
    target = loss_target[0]
    s, d = x.shape
    lay = _Layout(d)
    assert lay.cols == w_in.shape[2] * N_DEV
    cx, cy, cc = _place()
    me = 4 * cx + 2 * cy + cc

    me_arr = me.astype(jnp.int32).reshape(1)

    early, mixer, mlp = ("w_in",) + _CONV, ("w_branch_gdn", "w_branch_lru", "w_out"), ("w_up", "w_down")

    def shard(n, l):
        return w[n][l].astype(BF) if n in _BIG else w[n][l]

    def joined(name, stack):
        if name == "w_in":
            return lay.permuted_from_stack(stack)
        if name == "w_up":
            return stack
        return stack.reshape(-1, stack.shape[-1]) if name in _ROW_SHARDED else _cols_from_stack(stack)

    def start_weights(l, names, after):
        flights, token = _send_start([shard(n, l) for n in names], True, after, f"weights_start_l{l}")
        return dict(zip(names, flights)), token

    def wait_weights(l, flights, names, p, after, which):
        got = _send_wait([flights[n] for n in names], after, f"weights_wait_{which}_l{l}")
        for n, (own, landed) in zip(names, got):
            p[n] = joined(n, lax.dynamic_update_slice_in_dim(landed, own[None], me, axis=0))

    saved, params = [], []
    xl = x
    first = dict(zip(early, _all_gather([shard(n, 0) for n in early], "gather_first_weights")))
    flights, token = start_weights(0, mixer + mlp, first["w_in"])
    for l in range(depth):
        p = {n: w[n][l] for n in _SMALL}
        if l == 0:
            p.update({n: joined(n, g) for n, g in first.items()})
        else:
            wait_weights(l, flights, early, p, xl, "early")
        mine = flights
        if l + 1 < depth:
            flights, token = start_weights(l + 1, early + mixer + mlp, token)
        p["attn_norm"] = _after(p["attn_norm"], token)
        late = [functools.partial(wait_weights, l, mine, names, p, which=which)
                for names, which in ((mixer, "mixer"), (mlp, "mlp"))]
        xl, sv = _layer_fwd(xl, p, late, lay, f"_l{l}")
        saved.append(sv)
        params.append(p)
    dx, loss_row, dfinal = _loss_head(xl, final_norm, target, "loss_head")
    loss = lax.psum(loss_row[0, 0], ("x", "y", "c"))

    out = {n: None for n in _BIG}
    pending = {l: {} for l in range(depth)}

    def sender(l):
        def send(stacks):
            names = tuple(stacks)
            fl, token = _send_start([stacks[n] for n in names], False, me_arr, f"grads_start_{'_'.join(names)}_l{l}")
            pending[l].update(zip(names, fl))
            return token
        return send

    def finish(l, names, after, which):
        got = _send_wait([pending[l][n] for n in names], after, f"grads_wait_{which}_l{l}")
        for n, (mine, landed) in zip(names, got):
            out[n] = _adamw_sharded(mine, landed, me_arr, w[n], m[n], v[n], l, out[n], after, f"adamw_{n}_l{l}")
            after = out[n][0]
        return after

    small_names = _SMALL + _CONV
    small_sent = {}

    def small_sender(l):
        def send_small(small):
            arrays = [small[n] for n in small_names] + ([dfinal.reshape(-1)] if l == depth - 1 else [])
            flights, token = _send_start([_pack(arrays)], True, me_arr, f"small_grads_start_l{l}")
            small_sent[l] = (flights[0], arrays)
            return token
        return send_small

    last = [None]

    def hook_for(l):
        def hook(token):
            last[0] = finish(l + 1, _BIG, token, "all") if l + 1 < depth else token
        return hook

    for l in reversed(range(depth)):
        dx = _layer_bwd(dx, params[l], saved[l], sender(l), small_sender(l), hook_for(l), last[0], lay, f"_l{l}")

    last[0] = finish(0, mixer + mlp, last[0], "late")
    got = _send_wait([small_sent[l][0] for l in range(depth)], last[0], "small_grads_wait")
    per_layer = []
    for l, (part, landed) in enumerate(got):
        total = _sum_devices(lax.dynamic_update_slice_in_dim(landed, part[None], me, axis=0), f"sum_small_grads_l{l}")
        per_layer.append(_unpack(total, small_sent[l][1]))
    grads = {n: jnp.stack([per_layer[l][i] for l in range(depth)]) for i, n in enumerate(small_names)}
    grads["final_norm"] = per_layer[depth - 1][len(small_names)]
    for n in _CONV:
        blk = w[n].shape[2]
        grads[n] = lax.dynamic_slice_in_dim(grads[n], me * blk, blk, axis=2)
    names = small_names + ("final_norm",)
    gp = _pack([grads[n] for n in names])
    upd = _adamw_packed(_pack([w[n] for n in names]), gp, _pack([m[n] for n in names]), _pack([v[n] for n in names]),
                        "adamw_small")
    like = [w[n] for n in names]
    unp = [_unpack(u, like) for u in upd]
    for i, n in enumerate(names):
        out[n] = (grads[n], unp[0][i], unp[1][i], unp[2][i])
    finish(0, ("w_in",), upd[0], "w_in")

    res = [loss, dx.reshape(1, s, d)]
    for k in range(4):
        res += [out[n][k] for n in _ORDER]
    return tuple(res)
```

```python
import functools

import jax
import jax.numpy as jnp
from jax import lax
from jax.experimental import pallas as pl
from jax.experimental.pallas import tpu as pltpu

F32 = jnp.float32
BF = jnp.bfloat16
MESH = pl.DeviceIdType.MESH
N_DEV = 8
N_CHIP = 4

HEADS = 8
HEAD_DIM = 128
GDN_WIDTH = HEADS * HEAD_DIM
CHUNK = 64
CONV_WIDTH = 4
LRU_WIDTH = 1024
LRU_BLOCKS = 8
LRU_BLOCK_DIM = 128
LRU_C = 8.0
RMS_EPS = 1e-6
L2_EPS = 1e-6
ADAM_LR = 0.001
ADAM_B1 = 0.9
ADAM_B2 = 0.999
ADAM_EPS = 1e-08
ADAM_WD = 0.01
ADAM_STEP = 10

LANE = 128
PACK_ROWS = 512
VMEM_LIMIT = 56 * 1024 * 1024


def _cparams(sem=None):
    return pltpu.CompilerParams(dimension_semantics=sem, vmem_limit_bytes=VMEM_LIMIT)


def _tile(n, pref, unit=LANE):
    if n <= pref:
        return n
    best = None
    for t in range(unit, pref + 1, unit):
        if n % t == 0:
            best = t
    assert best is not None, (n, pref, unit)
    return best


_NN = (((1,), (0,)), ((), ()))
_NT = (((1,), (1,)), ((), ()))
_TN = (((0,), (0,)), ((), ()))


def _dg(a, b, dims):
    return lax.dot_general(a.astype(BF), b.astype(BF), dims, preferred_element_type=F32)


class _MM:
    def __init__(self, nn, nt, tn):
        self.nn, self.nt, self.tn = nn, nt, tn


def _make_mm(nn, nt, tn):
    def p_nn(a, b):
        return _dg(a, b, nn)

    def p_nt(a, b):
        return _dg(a, b, nt)

    def p_tn(a, b):
        return _dg(a, b, tn)

    d_nn, d_nt, d_tn = jax.custom_vjp(p_nn), jax.custom_vjp(p_nt), jax.custom_vjp(p_tn)

    def save(f):
        return lambda a, b: (f(a, b), (a, b))

    d_nn.defvjp(save(d_nn), lambda r, g: (d_nt(g, r[1]), d_tn(r[0], g)))
    d_nt.defvjp(save(d_nt), lambda r, g: (d_nn(g, r[1]), d_tn(g, r[0])))
    d_tn.defvjp(save(d_tn), lambda r, g: (d_nt(r[1], g), d_nn(r[0], g)))
    return _MM(p_nn, p_nt, p_tn), _MM(d_nn, d_nt, d_tn)


_BNN = (((2,), (1,)), ((0,), (0,)))
_BNT = (((2,), (2,)), ((0,), (0,)))
_BTN = (((1,), (1,)), ((0,), (0,)))

_PLAIN, _DIFF = _make_mm(_NN, _NT, _TN)
_PLAIN_B, _DIFF_B = _make_mm(_BNN, _BNT, _BTN)


def _sigmoid(x):
    return 1.0 / (1.0 + jnp.exp(-x))


def _silu(x):
    return x * _sigmoid(x)


def _softplus(x):
    return jnp.maximum(x, 0.0) + jnp.log(1.0 + jnp.exp(-jnp.maximum(x, -x)))


def _gelu(x):
    return 0.5 * x * (1.0 + jnp.tanh(0.7978845608028654 * (x + 0.044715 * (x * x * x))))


def _expm1(x):
    s = x * (1.0 + x * (1.0 / 2.0) * (1.0 + x * (1.0 / 3.0) * (1.0 + x * (1.0 / 4.0) * (
        1.0 + x * (1.0 / 5.0) * (1.0 + x * (1.0 / 6.0) * (1.0 + x * (1.0 / 7.0)))))))
    return jnp.where(jnp.maximum(x, -x) < 0.3, s, jnp.exp(x) - 1.0)


def _rms(x, gain):
    ms = jnp.mean(x * x, axis=-1, keepdims=True)
    return x * lax.rsqrt(ms + RMS_EPS) * gain


def _l2n(x):
    return x * lax.rsqrt(jnp.sum(x * x, axis=-1, keepdims=True) + L2_EPS)


def _shift_down(x, r):
    if r == 0:
        return x
    row = lax.broadcasted_iota(jnp.int32, x.shape, 0)
    return jnp.where(row >= r, pltpu.roll(x, r, 0), 0.0)


def _shift_up(x, r):
    if r == 0:
        return x
    n = x.shape[0]
    row = lax.broadcasted_iota(jnp.int32, x.shape, 0)
    return jnp.where(row < n - r, pltpu.roll(x, n - r, 0), 0.0)


def _conv(x, w_ref):
    y = None
    for j in range(CONV_WIDTH):
        t = _shift_down(x, CONV_WIDTH - 1 - j) * w_ref[j:j + 1, :]
        y = t if y is None else y + t
    return y


def _conv_bwd(x, dy, w_ref):
    dx = None
    dws = []
    for j in range(CONV_WIDTH):
        r = CONV_WIDTH - 1 - j
        t = _shift_up(dy, r) * w_ref[j:j + 1, :]
        dx = t if dx is None else dx + t
        dws.append(jnp.sum(dy * _shift_down(x, r), axis=0, keepdims=True))
    return dx, dws


def _scan_tile(a, b, reverse):
    n = a.shape[0]
    row = lax.broadcasted_iota(jnp.int32, a.shape, 0)
    s = 1
    while s < n:
        if reverse:
            keep = row < n - s
            a_sh = jnp.where(keep, pltpu.roll(a, n - s, 0), 1.0)
            b_sh = jnp.where(keep, pltpu.roll(b, n - s, 0), 0.0)
        else:
            keep = row >= s
            a_sh = jnp.where(keep, pltpu.roll(a, s, 0), 1.0)
            b_sh = jnp.where(keep, pltpu.roll(b, s, 0), 0.0)
        b = a * b_sh + b
        a = a * a_sh
        s *= 2
    return a, b


def _dg3(a, b, dims):
    a_hi = a.astype(BF)
    b_hi = b.astype(BF)
    a_lo = a - a_hi.astype(F32)
    b_lo = b - b_hi.astype(F32)
    return _dg(a_hi, b_hi, dims) + (_dg(a_hi, b_lo, dims) + _dg(a_lo, b_hi, dims))


@jax.custom_vjp
def _tri_inv(a):
    n = a.shape[1]
    p = _dg3(a, a, _BNN)
    r = p
    e = 2
    while 2 * e < n:
        p = _dg3(p, p, _BNN)
        r = r + p + _dg3(r, p, _BNN)
        e *= 2
    row = lax.broadcasted_iota(jnp.int32, (1, n, n), 1)
    col = lax.broadcasted_iota(jnp.int32, (1, n, n), 2)
    eye = jnp.where(row == col, 1.0, 0.0)
    return eye - a + r - _dg3(a, r, _BNN)


def _tri_inv_fwd(a):
    t = _tri_inv(a)
    return t, t


def _tri_inv_bwd(t, g):
    return (-_dg3(_dg3(t, g, _BTN), t, _BNT),)


_tri_inv.defvjp(_tri_inv_fwd, _tri_inv_bwd)


@jax.custom_vjp
def _tri_inv_saved(a, t):
    return t


_tri_inv_saved.defvjp(lambda a, t: (t, t), lambda t, g: (_tri_inv_bwd(t, g)[0], jnp.zeros_like(t)))


def _gdn_chunk(q, k, v, beta, gcol, grow, state, mm, t_saved=None):
    c = q.shape[1]
    row = lax.broadcasted_iota(jnp.int32, (1, c, c), 1)
    col = lax.broadcasted_iota(jnp.int32, (1, c, c), 2)
    causal = row >= col
    strict = row > col
    qn = _l2n(q) * (HEAD_DIM ** -0.5)
    kn = _l2n(k)
    dec = jnp.where(causal, jnp.exp(jnp.where(causal, gcol - grow, 0.0)), 0.0)
    kb = kn * beta
    vb = v * beta
    a = jnp.where(strict, mm.nt(kb, kn) * dec, 0.0)
    t = _tri_inv(a) if t_saved is None else _tri_inv_saved(a, t_saved)
    eg = jnp.exp(gcol)
    u = mm.nn(t, vb)
    w = mm.nn(t, kb * eg)
    p = jnp.where(causal, mm.nt(qn, kn) * dec, 0.0)
    vn = u - mm.nn(w, state)
    o = mm.nn(qn * eg, state) + mm.nn(p, vn)
    last = lax.broadcasted_iota(jnp.int32, (1, c, 1), 1) == c - 1
    gl = jnp.sum(jnp.where(last, gcol, 0.0), axis=1, keepdims=True)
    kd = kn * jnp.exp(gl - gcol)
    new_state = state * jnp.exp(gl) + mm.tn(kd, vn)
    return o, new_state, t


def _lru_gates(xc, wa, ba, wx, bx, lam, mm):
    r = _sigmoid(mm.nn(xc, wa) + ba)
    i = _sigmoid(mm.nn(xc, wx) + bx)
    log_a = -LRU_C * r * _softplus(-lam)
    a = jnp.exp(log_a)
    bterm = jnp.sqrt(-_expm1(2.0 * log_a)) * (i * xc)
    return a, bterm


def _matmul(a, b, mode, name, out_dtypes, epi=None, extras=(), tm=1024, tn=1024, tk=2048, after=None,
            shards=False):
    if shards:
        nd, rows, per = b.shape if mode != "tn" else (N_DEV, a.shape[1], b.shape[1] // N_DEV)
        b_shape = b.shape if mode == "tn" else (rows, nd * per)
    else:
        b_shape = b.shape
    if mode == "nn":
        (m, k), (k2, n) = a.shape, b_shape
    elif mode == "nt":
        (m, k), (n, k2) = a.shape, b_shape
    else:
        (k, m), (k2, n) = a.shape, b_shape
    assert k == k2, (a.shape, b.shape, mode)
    if shards:
        tn, tk = (min(tn, per), tk) if mode != "nt" else (tn, min(tk, per))
    tm, tn, tk = _tile(m, tm), _tile(n, tn), _tile(k, tk)
    nk = k // tk
    dims = {"nn": _NN, "nt": _NT, "tn": _TN}[mode]
    if mode == "nn":
        a_spec = pl.BlockSpec((tm, tk), lambda i, j, kk: (i, kk))
        b_spec = pl.BlockSpec((tk, tn), lambda i, j, kk: (kk, j))
        if shards:
            assert per % tn == 0
            b_spec = pl.BlockSpec((None, tk, tn), lambda i, j, kk: (j * tn // per, kk, j * tn % per // tn))
    elif mode == "nt":
        a_spec = pl.BlockSpec((tm, tk), lambda i, j, kk: (i, kk))
        b_spec = pl.BlockSpec((tn, tk), lambda i, j, kk: (j, kk))
        if shards:
            assert per % tk == 0
            b_spec = pl.BlockSpec((None, tn, tk), lambda i, j, kk: (kk * tk // per, j, kk * tk % per // tk))
    else:
        a_spec = pl.BlockSpec((tk, tm), lambda i, j, kk: (kk, i))
        b_spec = pl.BlockSpec((tk, tn), lambda i, j, kk: (kk, j))
    o_spec = pl.BlockSpec((tm, tn), lambda i, j, kk: (i, j))
    out_spec, out_dims = o_spec, (m, n)
    if shards and mode == "tn":
        assert per % tn == 0 and not extras
        out_spec = pl.BlockSpec((None, tm, tn), lambda i, j, kk: (j * tn // per, i, j * tn % per // tn))
        out_dims = (N_DEV, m, per)
    n_ex, n_out = len(extras), len(out_dtypes)
    order = [] if after is None else [after]
    n_in = 2 + n_ex + len(order)
    if epi is None:
        epi = lambda acc: (acc,)

    def body(*refs):
        a_ref, b_ref = refs[0], refs[1]
        ex_refs = refs[2:2 + n_ex]
        out_refs = refs[n_in:n_in + n_out]
        kk = pl.program_id(2)
        part = _dg(a_ref[...], b_ref[...], dims)

        def finish(total):
            res = epi(total, *[r[...] for r in ex_refs])
            for r, v in zip(out_refs, res):
                r[...] = v.astype(r.dtype)

        if nk == 1:
            finish(part)
            return
        acc_ref = refs[-1]

        @pl.when(kk == 0)
        def _():
            acc_ref[...] = part

        @pl.when(jnp.logical_and(kk > 0, kk < nk - 1))
        def _():
            acc_ref[...] += part

        @pl.when(kk == nk - 1)
        def _():
            finish(acc_ref[...] + part)

    outs = pl.pallas_call(
        body, name=name, grid=(m // tm, n // tn, nk),
        in_specs=[a_spec, b_spec] + [o_spec] * n_ex + [pl.BlockSpec(memory_space=pl.ANY)] * len(order),
        out_specs=[out_spec] * n_out,
        out_shape=[jax.ShapeDtypeStruct(out_dims, d) for d in out_dtypes],
        scratch_shapes=[pltpu.VMEM((tm, tn), F32)] if nk > 1 else [],
        compiler_params=_cparams(("parallel", "parallel", "arbitrary")),
    )(a, b, *extras, *order)
    return outs[0] if n_out == 1 else outs


def _rowwise(fn, name, rows, tr, row_ins, params, row_outs, acc_outs=(), ncol=1):
    tr = _tile(rows, tr, 8)
    n_in, n_par, n_ro, n_acc = len(row_ins), len(params), len(row_outs), len(acc_outs)
    in_specs = []
    for _, width, base, per_col in row_ins:
        if per_col:
            in_specs.append(pl.BlockSpec((tr, width), lambda i, j, base=base: (i, base + j)))
        else:
            in_specs.append(pl.BlockSpec((tr, width), lambda i, j, base=base: (i, base)))
    for p in params:
        in_specs.append(pl.BlockSpec(p.shape, lambda i, j: (0, 0)))
    out_specs, out_shape = [], []
    for total, dtype, width, per_col in row_outs:
        if per_col:
            out_specs.append(pl.BlockSpec((tr, width), lambda i, j: (i, j)))
        else:
            out_specs.append(pl.BlockSpec((tr, width), lambda i, j: (i, 0)))
        out_shape.append(jax.ShapeDtypeStruct((rows, total), dtype))
    for shape, dtype in acc_outs:
        out_specs.append(pl.BlockSpec(shape, lambda i, j: (0, 0)))
        out_shape.append(jax.ShapeDtypeStruct(shape, dtype))

    def body(*refs):
        ins = [r[...] for r in refs[:n_in + n_par]]
        ro = refs[n_in + n_par:n_in + n_par + n_ro]
        ao = refs[n_in + n_par + n_ro:]
        res = fn(*ins)
        for r, v in zip(ro, res[:n_ro]):
            r[...] = v.astype(r.dtype)
        first = jnp.logical_and(pl.program_id(0) == 0, pl.program_id(1) == 0)

        @pl.when(first)
        def _():
            for r in ao:
                r[...] = jnp.zeros_like(r)

        for r, v in zip(ao, res[n_ro:]):
            r[...] += v.astype(r.dtype)

    sem = ("arbitrary", "arbitrary") if n_acc else ("parallel", "parallel")
    outs = pl.pallas_call(
        body, name=name, grid=(rows // tr, ncol), in_specs=in_specs, out_specs=out_specs,
        out_shape=out_shape, compiler_params=_cparams(sem),
    )(*[r[0] for r in row_ins], *params)
    return outs


def _row(x):
    return x.reshape(1, -1)


def _rms_fwd(x, gain, name):
    s, d = x.shape
    return _rowwise(lambda xt, g: (_rms(xt, g),), name, s, 512, [(x, d, 0, False)], [_row(gain)],
                    [(d, BF, d, False)])[0]


def _rms_bwd(x, gain, dh, dres, name):
    s, d = x.shape

    def fn(xt, dht, drt, g):
        _, vjp = jax.vjp(_rms, xt, g)
        dx, dg = vjp(dht)
        return drt + dx, dg

    return _rowwise(fn, name, s, 256, [(x, d, 0, False), (dh, d, 0, False), (dres, d, 0, False)], [_row(gain)],
                    [(d, F32, d, False)], [((1, d), F32)])


def _loss_head(x, gain, target, name):
    s, d = x.shape

    def fn(xt, tt, g):
        y, vjp = jax.vjp(_rms, xt, g)
        err = y - tt
        dx, dg = vjp(err * (1.0 / d))
        part = 0.5 * jnp.sum(jnp.sum(err * err, axis=1, keepdims=True), axis=0, keepdims=True) * (1.0 / d)
        return dx, jnp.broadcast_to(part, (1, LANE)), dg

    return _rowwise(fn, name, s, 256, [(x, d, 0, False), (target, d, 0, False)], [_row(gain)],
                    [(d, F32, d, False)], [((1, LANE), F32), ((1, d), F32)])


def _gdn_post_fn(o, z, g):
    return _rms(o, g) * _silu(z)


def _gdn_post_fwd(o, proj, gain, z_base, name):
    s = o.shape[0]
    return _rowwise(lambda ot, zt, g: (_gdn_post_fn(ot, zt, g),), name, s, 1024,
                    [(o, HEAD_DIM, 0, True), (proj, HEAD_DIM, z_base, True)], [_row(gain)],
                    [(GDN_WIDTH, BF, HEAD_DIM, True)], ncol=HEADS)[0]


def _gdn_post_bwd(o, proj, gain, dog, z_base, name):
    s = o.shape[0]

    def fn(ot, zt, dt, g):
        _, vjp = jax.vjp(_gdn_post_fn, ot, zt, g)
        return vjp(dt)

    return _rowwise(fn, name, s, 1024,
                    [(o, HEAD_DIM, 0, True), (proj, HEAD_DIM, z_base, True), (dog, HEAD_DIM, 0, True)],
                    [_row(gain)], [(GDN_WIDTH, F32, HEAD_DIM, True), (GDN_WIDTH, BF, HEAD_DIM, True)],
                    [((1, HEAD_DIM), F32)], ncol=HEADS)


def _merge_fn(gg, gl, yg, yl):
    return _sigmoid(gg) * yg + _sigmoid(gl) * yl


def _merge_fwd(proj, yg, yl, gg_base, name):
    s, d = yg.shape
    return _rowwise(lambda a, b, c, e: (_merge_fn(a, b, c, e),), name, s, 256,
                    [(proj, d, gg_base, False), (proj, d, gg_base + 1, False), (yg, d, 0, False), (yl, d, 0, False)],
                    [], [(d, BF, d, False)])[0]


def _merge_bwd(proj, yg, yl, dmerged, gg_base, name):
    s, d = yg.shape

    def fn(a, b, c, e, dm):
        _, vjp = jax.vjp(_merge_fn, a, b, c, e)
        return vjp(dm)

    return _rowwise(fn, name, s, 128,
                    [(proj, d, gg_base, False), (proj, d, gg_base + 1, False), (yg, d, 0, False), (yl, d, 0, False),
                     (dmerged, d, 0, False)], [], [(d, BF, d, False)] * 4)


def _conv_silu_fwd(proj, w, name):
    s = proj.shape[0]
    cols = w.shape[1]
    cw = _tile(cols, LANE)

    def body(x_ref, w_ref, o_ref):
        o_ref[...] = _silu(_conv(x_ref[...], w_ref))

    return pl.pallas_call(
        body, name=name, grid=(cols // cw,),
        in_specs=[pl.BlockSpec((s, cw), lambda j: (0, j)), pl.BlockSpec((CONV_WIDTH, cw), lambda j: (0, j))],
        out_specs=pl.BlockSpec((s, cw), lambda j: (0, j)),
        out_shape=jax.ShapeDtypeStruct((s, cols), F32), compiler_params=_cparams(("parallel",)),
    )(proj, w)


def _conv_silu_bwd(proj, w, dout, name):
    s = proj.shape[0]
    cols = w.shape[1]
    cw = _tile(cols, LANE)

    def body(x_ref, w_ref, d_ref, dx_ref, dw_ref):
        x = x_ref[...]
        y = _conv(x, w_ref)
        sg = _sigmoid(y)
        dy = d_ref[...] * (sg * (1.0 + y * (1.0 - sg)))
        dx, dws = _conv_bwd(x, dy, w_ref)
        dx_ref[...] = dx.astype(dx_ref.dtype)
        for j in range(CONV_WIDTH):
            dw_ref[j:j + 1, :] = dws[j]

    return pl.pallas_call(
        body, name=name, grid=(cols // cw,),
        in_specs=[pl.BlockSpec((s, cw), lambda j: (0, j)), pl.BlockSpec((CONV_WIDTH, cw), lambda j: (0, j)),
                  pl.BlockSpec((s, cw), lambda j: (0, j))],
        out_specs=[pl.BlockSpec((s, cw), lambda j: (0, j)), pl.BlockSpec((CONV_WIDTH, cw), lambda j: (0, j))],
        out_shape=[jax.ShapeDtypeStruct((s, cols), BF), jax.ShapeDtypeStruct((CONV_WIDTH, cols), F32)],
        compiler_params=_cparams(("parallel",)),
    )(proj, w, dout)


def _cumsum_mask(tr, transpose):
    row = lax.broadcasted_iota(jnp.int32, (tr, tr), 0)
    col = lax.broadcasted_iota(jnp.int32, (tr, tr), 1)
    same = (row // CHUNK) == (col // CHUNK)
    tri = (row <= col) if transpose else (row >= col)
    return jnp.where(jnp.logical_and(same, tri), 1.0, 0.0)


def _lane_pad(v):
    return jnp.pad(v.reshape(1, -1), ((0, 0), (0, LANE - v.shape[0])))


def _gdn_gates_fwd(proj, a_log, dt_bias, ab_base, name):
    s = proj.shape[0]

    def fn(ab, al, dt):
        g = -jnp.exp(al) * _softplus(ab + dt)
        beta = _sigmoid(pltpu.roll(ab, LANE - HEADS, 1))
        gc = jnp.dot(_cumsum_mask(ab.shape[0], False), g, precision=lax.Precision.HIGHEST,
                     preferred_element_type=F32)
        return gc, beta

    return _rowwise(fn, name, s, 512, [(proj, LANE, ab_base, False)], [_lane_pad(a_log), _lane_pad(dt_bias)],
                    [(LANE, F32, LANE, False)] * 2)


def _gdn_gates_bwd(proj, a_log, dt_bias, dgc_col, dgc_row, dbeta, ab_base, name):
    s = proj.shape[0]

    def fn(ab, dc, dr, db, al, dt):
        lane = lax.broadcasted_iota(jnp.int32, ab.shape, 1)
        dg = jnp.dot(_cumsum_mask(ab.shape[0], True), dc + dr, precision=lax.Precision.HIGHEST,
                     preferred_element_type=F32)
        ea = jnp.exp(al)
        pre = ab + dt
        g = -ea * _softplus(pre)
        da = jnp.where(lane < HEADS, dg * (-ea) * _sigmoid(pre), 0.0)
        dal = jnp.sum(jnp.where(lane < HEADS, dg * g, 0.0), axis=0, keepdims=True)
        ddt = jnp.sum(da, axis=0, keepdims=True)
        beta = _sigmoid(pltpu.roll(ab, LANE - HEADS, 1))
        dbl = jnp.where(lane < HEADS, db * beta * (1.0 - beta), 0.0)
        dab = da + pltpu.roll(dbl, HEADS, 1)
        return dab, dal, ddt

    return _rowwise(fn, name, s, 512,
                    [(proj, LANE, ab_base, False), (dgc_col, LANE, 0, False), (dgc_row, LANE, 0, False),
                     (dbeta, LANE, 0, False)], [_lane_pad(a_log), _lane_pad(dt_bias)],
                    [(LANE, BF, LANE, False)], [((1, LANE), F32), ((1, LANE), F32)])


def _head_col(blk, h):
    lane = lax.broadcasted_iota(jnp.int32, blk.shape, 1)
    return jnp.sum(jnp.where(lane == h, blk, 0.0), axis=1, keepdims=True)


def _chunk_operands(qkv_ref, gc_ref, b_ref, gr_ref):
    heads = range(HEADS)
    q, k, v = (jnp.stack([qkv_ref[:, (j * HEADS + h) * HEAD_DIM:(j * HEADS + h + 1) * HEAD_DIM] for h in heads])
               for j in range(3))
    gcs, betas = gc_ref[...], b_ref[...]
    beta = jnp.stack([_head_col(betas, h) for h in heads])
    gcol = jnp.stack([_head_col(gcs, h) for h in heads])
    grow = jnp.stack([gr_ref[0, h:h + 1, :] for h in heads])
    return q, k, v, beta, gcol, grow


def _gdn_chunk_fwd(qkv, gc, beta, grow, name):
    s = qkv.shape[0]
    nc = s // CHUNK

    def body(qkv_ref, gc_ref, b_ref, gr_ref, o_ref, st_ref, t_ref, state):
        @pl.when(pl.program_id(0) == 0)
        def _():
            state[...] = jnp.zeros_like(state)

        st = state[...]
        st_ref[0] = st
        o, new, t = _gdn_chunk(*_chunk_operands(qkv_ref, gc_ref, b_ref, gr_ref), st, _PLAIN_B)
        t_ref[0] = t
        for h in range(HEADS):
            o_ref[:, h * HEAD_DIM:(h + 1) * HEAD_DIM] = o[h]
        state[...] = new

    return pl.pallas_call(
        body, name=name, grid=(nc,),
        in_specs=[pl.BlockSpec((CHUNK, 3 * GDN_WIDTH), lambda c: (c, 0)),
                  pl.BlockSpec((CHUNK, LANE), lambda c: (c, 0)), pl.BlockSpec((CHUNK, LANE), lambda c: (c, 0)),
                  pl.BlockSpec((1, HEADS, CHUNK), lambda c: (c, 0, 0))],
        out_specs=[pl.BlockSpec((CHUNK, GDN_WIDTH), lambda c: (c, 0)),
                   pl.BlockSpec((1, HEADS, HEAD_DIM, HEAD_DIM), lambda c: (c, 0, 0, 0)),
                   pl.BlockSpec((1, HEADS, CHUNK, CHUNK), lambda c: (c, 0, 0, 0))],
        out_shape=[jax.ShapeDtypeStruct((s, GDN_WIDTH), F32),
                   jax.ShapeDtypeStruct((nc, HEADS, HEAD_DIM, HEAD_DIM), F32),
                   jax.ShapeDtypeStruct((nc, HEADS, CHUNK, CHUNK), F32)],
        scratch_shapes=[pltpu.VMEM((HEADS, HEAD_DIM, HEAD_DIM), F32)],
        compiler_params=_cparams(("arbitrary",)),
    )(qkv, gc, beta, grow)


def _gdn_chunk_bwd(qkv, gc, beta, grow, states, tinv, do, name):
    s = qkv.shape[0]
    nc = s // CHUNK

    def body(qkv_ref, gc_ref, b_ref, gr_ref, st_ref, t_ref, do_ref, dqkv_ref, dgc_ref, db_ref, dgr_ref, dstate):
        @pl.when(pl.program_id(0) == 0)
        def _():
            dstate[...] = jnp.zeros_like(dstate)

        lane = lax.broadcasted_iota(jnp.int32, (CHUNK, LANE), 1)
        chunk = lambda *args: _gdn_chunk(*args, mm=_DIFF_B, t_saved=t_ref[0])[:2]
        _, vjp = jax.vjp(chunk, *_chunk_operands(qkv_ref, gc_ref, b_ref, gr_ref), st_ref[0])
        do = jnp.stack([do_ref[:, h * HEAD_DIM:(h + 1) * HEAD_DIM] for h in range(HEADS)])
        dq, dk, dv, dbeta, dgcol, dgrow, dst = vjp((do, dstate[...]))
        dstate[...] = dst
        dgc = jnp.zeros((CHUNK, LANE), F32)
        db = jnp.zeros((CHUNK, LANE), F32)
        for h in range(HEADS):
            for j, dx in enumerate((dq, dk, dv)):
                dqkv_ref[:, (j * HEADS + h) * HEAD_DIM:(j * HEADS + h + 1) * HEAD_DIM] = dx[h]
            dgc = dgc + jnp.where(lane == h, dgcol[h], 0.0)
            db = db + jnp.where(lane == h, dbeta[h], 0.0)
            dgr_ref[0, h:h + 1, :] = dgrow[h]
        dgc_ref[...] = dgc
        db_ref[...] = db

    rev = lambda c: nc - 1 - c
    wide = pl.BlockSpec((CHUNK, 3 * GDN_WIDTH), lambda c: (rev(c), 0))
    lanes = pl.BlockSpec((CHUNK, LANE), lambda c: (rev(c), 0))
    rows = pl.BlockSpec((1, HEADS, CHUNK), lambda c: (rev(c), 0, 0))
    return pl.pallas_call(
        body, name=name, grid=(nc,),
        in_specs=[wide, lanes, lanes, rows,
                  pl.BlockSpec((1, HEADS, HEAD_DIM, HEAD_DIM), lambda c: (rev(c), 0, 0, 0)),
                  pl.BlockSpec((1, HEADS, CHUNK, CHUNK), lambda c: (rev(c), 0, 0, 0)),
                  pl.BlockSpec((CHUNK, GDN_WIDTH), lambda c: (rev(c), 0))],
        out_specs=[wide, lanes, lanes, rows],
        out_shape=[jax.ShapeDtypeStruct((s, 3 * GDN_WIDTH), F32)]
        + [jax.ShapeDtypeStruct((s, LANE), F32)] * 2 + [jax.ShapeDtypeStruct((nc, HEADS, CHUNK), F32)],
        scratch_shapes=[pltpu.VMEM((HEADS, HEAD_DIM, HEAD_DIM), F32)],
        compiler_params=_cparams(("arbitrary",)),
    )(qkv, gc, beta, grow, states, tinv, do)


def _lru_specs(s, xb_base):
    col = lambda base: pl.BlockSpec((s, LRU_BLOCK_DIM), lambda n, base=base: (0, base + n))
    vec = pl.BlockSpec((1, LRU_BLOCK_DIM), lambda n: (0, n))
    mat = pl.BlockSpec((None, LRU_BLOCK_DIM, LRU_BLOCK_DIM), lambda n: (n, 0, 0))
    cw = pl.BlockSpec((CONV_WIDTH, LRU_BLOCK_DIM), lambda n: (0, n))
    return col, vec, mat, cw


def _lru_fwd(proj, cw, cb, wa, ba, wx, bx, lam, xb_base, name):
    s = proj.shape[0]
    tt = _tile(s, 256, 8)
    col, vec, mat, cws = _lru_specs(s, xb_base)

    def body(xb_ref, yb_ref, cw_ref, cb_ref, wa_ref, ba_ref, wx_ref, bx_ref, lam_ref, o_ref, h_ref, a_ref, xc_s):
        xc_s[...] = _conv(xb_ref[...], cw_ref) + cb_ref[...]
        par = (wa_ref[...], ba_ref[...], wx_ref[...], bx_ref[...], lam_ref[...])

        def step(t, carry):
            sl = pl.ds(pl.multiple_of(t * tt, tt), tt)
            a, bt = _lru_gates(xc_s[sl, :], *par, _PLAIN)
            a_ref[sl, :] = a
            aa, bb = _scan_tile(a, bt, False)
            h = aa * carry + bb
            h_ref[sl, :] = h
            o_ref[sl, :] = (h * _gelu(yb_ref[sl, :])).astype(o_ref.dtype)
            return h_ref[pl.ds(t * tt + tt - 1, 1), :]

        lax.fori_loop(0, s // tt, step, jnp.zeros((1, LRU_BLOCK_DIM), F32))

    nb = LRU_BLOCKS
    return pl.pallas_call(
        body, name=name, grid=(LRU_BLOCKS,),
        in_specs=[col(xb_base), col(xb_base + nb), cws, vec, mat, vec, mat, vec, vec],
        out_specs=[col(0), col(0), col(0)],
        out_shape=[jax.ShapeDtypeStruct((s, LRU_WIDTH), BF), jax.ShapeDtypeStruct((s, LRU_WIDTH), F32),
                   jax.ShapeDtypeStruct((s, LRU_WIDTH), F32)],
        scratch_shapes=[pltpu.VMEM((s, LRU_BLOCK_DIM), F32)],
        compiler_params=_cparams(("parallel",)),
    )(proj, proj, cw, _row(cb), wa, _row(ba), wx, _row(bx), _row(lam))


def _lru_bwd(proj, hs, decay, dol, cw, cb, wa, ba, wx, bx, lam, xb_base, name):
    s = proj.shape[0]
    tt = _tile(s, 256, 8)
    nt = s // tt
    col, vec, mat, cws = _lru_specs(s, xb_base)

    def body(xb_ref, yb_ref, h_ref, a_ref, d_ref, cw_ref, cb_ref, wa_ref, ba_ref, wx_ref, bx_ref, lam_ref,
             dxb_ref, dyb_ref, dcw_ref, dcb_ref, dwa_ref, dba_ref, dwx_ref, dbx_ref, dlam_ref,
             xc_s, dh_s, dxc_s):
        xc_s[...] = _conv(xb_ref[...], cw_ref) + cb_ref[...]
        par = (wa_ref[...], ba_ref[...], wx_ref[...], bx_ref[...], lam_ref[...])
        row = lax.broadcasted_iota(jnp.int32, (tt, LRU_BLOCK_DIM), 0)
        tile = lambda t: pl.ds(pl.multiple_of(t * tt, tt), tt)

        def prep(t, carry):
            sl = tile(t)
            d = d_ref[sl, :]
            gy, vjp_y = jax.vjp(_gelu, yb_ref[sl, :])
            dyb_ref[sl, :] = vjp_y(d * h_ref[sl, :])[0].astype(dyb_ref.dtype)
            dh_s[sl, :] = d * gy
            return carry

        lax.fori_loop(0, nt, prep, 0)

        def rscan(i, carry):
            dh_next, a_next = carry
            t = nt - 1 - i
            sl = tile(t)
            a_sh = jnp.where(row == tt - 1, a_next, pltpu.roll(a_ref[sl, :], tt - 1, 0))
            aa, bb = _scan_tile(a_sh, dh_s[sl, :], True)
            dh_s[sl, :] = aa * dh_next + bb
            first = pl.ds(t * tt, 1)
            return dh_s[first, :], a_ref[first, :]

        zero = jnp.zeros((1, LRU_BLOCK_DIM), F32)
        lax.fori_loop(0, nt, rscan, (zero, zero))

        def gates_vjp(t, acc):
            sl = tile(t)
            _, vjp_g = jax.vjp(functools.partial(_lru_gates, mm=_DIFF), xc_s[sl, :], *par)
            dh = dh_s[sl, :]
            before = jnp.where(t > 0, h_ref[pl.ds(jnp.maximum(t * tt - 1, 0), 1), :], 0.0)
            h_prev = jnp.where(row == 0, before, pltpu.roll(h_ref[sl, :], 1, 0))
            dxc, *dpar = vjp_g((dh * h_prev, dh))
            dxc_s[sl, :] = dxc
            return tuple(x + y for x, y in zip(acc, dpar))

        dwa, dba, dwx, dbx, dlam = lax.fori_loop(0, nt, gates_vjp, tuple(jnp.zeros_like(p) for p in par))
        dwa_ref[...] = dwa
        dba_ref[...] = dba
        dwx_ref[...] = dwx
        dbx_ref[...] = dbx
        dlam_ref[...] = dlam
        dxc = dxc_s[...]
        dcb_ref[...] = jnp.sum(dxc, axis=0, keepdims=True)
        dxb, dws = _conv_bwd(xb_ref[...], dxc, cw_ref)
        dxb_ref[...] = dxb.astype(dxb_ref.dtype)
        for j in range(CONV_WIDTH):
            dcw_ref[j:j + 1, :] = dws[j]

    nb = LRU_BLOCKS
    w = LRU_WIDTH
    return pl.pallas_call(
        body, name=name, grid=(LRU_BLOCKS,),
        in_specs=[col(xb_base), col(xb_base + nb), col(0), col(0), col(0), cws, vec, mat, vec, mat, vec, vec],
        out_specs=[col(0), col(0), cws, vec, mat, vec, mat, vec, vec],
        out_shape=[jax.ShapeDtypeStruct((s, w), BF), jax.ShapeDtypeStruct((s, w), BF),
                   jax.ShapeDtypeStruct((CONV_WIDTH, w), F32), jax.ShapeDtypeStruct((1, w), F32),
                   jax.ShapeDtypeStruct(wa.shape, F32), jax.ShapeDtypeStruct((1, w), F32),
                   jax.ShapeDtypeStruct(wx.shape, F32), jax.ShapeDtypeStruct((1, w), F32),
                   jax.ShapeDtypeStruct((1, w), F32)],
        scratch_shapes=[pltpu.VMEM((s, LRU_BLOCK_DIM), F32)] * 3,
        compiler_params=_cparams(("parallel",)),
    )(proj, proj, hs, decay, dol, cw, _row(cb), wa, _row(ba), wx, _row(bx), _row(lam))


def _place():
    return lax.axis_index("x"), lax.axis_index("y"), lax.axis_index("c")


def _all_gather(blocks, name):
    n = len(blocks)
    per = N_DEV - 1

    def body(*refs):
        ins, outs = refs[:n], refs[n:2 * n]
        send_sems, recv_sems, local_sems = refs[2 * n:]
        x, y, c = _place()
        me, sibling = (x, y, c), (x, y, 1 - c)
        chips = [(1 - x, y), (x, 1 - y), (1 - x, 1 - y)]

        def copy(a, k, block, to, src=None):
            dst = outs[a].at[4 * block[0] + 2 * block[1] + block[2]]
            return pltpu.make_async_remote_copy(
                src_ref=dst if src is None else src, dst_ref=dst, send_sem=send_sems.at[a * per + k],
                recv_sem=recv_sems.at[a * per + k], device_id=to, device_id_type=MESH)

        started = []
        for a in range(n):
            mine = pltpu.make_async_copy(ins[a], outs[a].at[4 * x + 2 * y + c], local_sems.at[a])
            mine.start()
            started.append(mine)
        sends = []
        for a in range(n):
            first = [copy(a, 0, me, sibling, src=ins[a])]
            first += [copy(a, 1 + j, me, (*chip, c), src=ins[a]) for j, chip in enumerate(chips)]
            for cp in first:
                cp.start()
            sends += first
        for a in range(n):
            for j, chip in enumerate(chips):
                copy(a, 1 + j, (*chip, c), me).wait_recv()
                passed = copy(a, 4 + j, (*chip, c), sibling)
                passed.start()
                sends.append(passed)
        for a in range(n):
            copy(a, 0, sibling, me).wait_recv()
            for j, chip in enumerate(chips):
                copy(a, 4 + j, (*chip, 1 - c), me).wait_recv()
        for cp in sends:
            cp.wait_send()
        for cp in started:
            cp.wait()

    any_spec = pl.BlockSpec(memory_space=pl.ANY)
    return pl.pallas_call(
        body, name=name, in_specs=[any_spec] * n, out_specs=[any_spec] * n,
        out_shape=[jax.ShapeDtypeStruct((N_DEV,) + b.shape, b.dtype) for b in blocks],
        scratch_shapes=[pltpu.SemaphoreType.DMA((n * per,)), pltpu.SemaphoreType.DMA((n * per,)),
                        pltpu.SemaphoreType.DMA((n,))],
    )(*blocks)


_HBM = pl.BlockSpec(memory_space=pltpu.HBM)
_SEM = pl.BlockSpec(memory_space=pltpu.SEMAPHORE)
_ANY = pl.BlockSpec(memory_space=pl.ANY)
_EFFECT = pltpu.SideEffectType.DATAFLOW_SIDE_EFFECTING


def _peers(x, y, c):
    return [(x, y, 1 - c), (1 - x, y, c), (1 - x, y, 1 - c), (x, 1 - y, c), (x, 1 - y, 1 - c),
            (1 - x, 1 - y, c), (1 - x, 1 - y, 1 - c)]


class _InFlight:
    def __init__(self, send_sem, recv_sem, src, land):
        self.send_sem, self.recv_sem, self.src, self.land = send_sem, recv_sem, src, land


def _send_start(srcs, whole, after, name):
    n = len(srcs)
    lands = [lax.empty(((N_DEV,) + s.shape) if whole else s.shape, s.dtype) for s in srcs]

    def body(*refs):
        src_refs, land_refs = refs[:n], refs[n:2 * n]
        outs = refs[2 * n + 1:]
        send_sems, recv_sems, token = outs[:n], outs[n:2 * n], outs[-1]
        x, y, c = _place()
        me = 4 * x + 2 * y + c
        for a in range(n):
            for p in _peers(x, y, c):
                src = src_refs[a] if whole else src_refs[a].at[4 * p[0] + 2 * p[1] + p[2]]
                pltpu.make_async_remote_copy(src_ref=src, dst_ref=land_refs[a].at[me], send_sem=send_sems[a],
                                             recv_sem=recv_sems[a], device_id=p, device_id_type=MESH).start()
        token[...] = jnp.zeros_like(token)

    hbm = lambda a: pltpu.HBM(a.shape, a.dtype)
    outs = pl.pallas_call(
        body, name=name,
        out_shape=[pltpu.SemaphoreType.DMA(())] * (2 * n) + [hbm(s) for s in srcs] + [hbm(l) for l in lands]
        + [jax.ShapeDtypeStruct((8, LANE), F32)],
        in_specs=[_HBM] * (2 * n) + [_ANY],
        out_specs=[_SEM] * (2 * n) + [_HBM] * (2 * n) + [pl.BlockSpec(memory_space=pltpu.VMEM)],
        input_output_aliases={i: 2 * n + i for i in range(2 * n)},
        compiler_params=pltpu.CompilerParams(has_side_effects=_EFFECT),
    )(*[pltpu.with_memory_space_constraint(s, pltpu.HBM) for s in srcs],
      *[pltpu.with_memory_space_constraint(l, pltpu.HBM) for l in lands], after)
    flights = [_InFlight(outs[a], outs[n + a], outs[2 * n + a], outs[3 * n + a]) for a in range(n)]
    return flights, outs[-1]


def _send_wait(flights, after, name):
    n = len(flights)

    def body(*refs):
        x, y, c = _place()
        for a in range(n):
            land_ref, send_sem, recv_sem = refs[4 * a + 1], refs[4 * a + 2], refs[4 * a + 3]
            seven = land_ref.at[pl.ds(0, N_DEV - 1)]
            copy = pltpu.make_async_remote_copy(src_ref=seven, dst_ref=seven, send_sem=send_sem, recv_sem=recv_sem,
                                                device_id=(x, y, 1 - c), device_id_type=MESH)
            copy.wait_send()
            copy.wait_recv()

    ins, out_shape = [], []
    for f in flights:
        ins += [f.src, f.land, f.send_sem, f.recv_sem]
        out_shape += [pltpu.HBM(f.src.shape, f.src.dtype), pltpu.HBM(f.land.shape, f.land.dtype)]
    outs = pl.pallas_call(
        body, name=name, out_shape=out_shape,
        in_specs=[_HBM, _HBM, _SEM, _SEM] * n + [_ANY], out_specs=[_HBM] * (2 * n),
        input_output_aliases={4 * a + i: 2 * a + i for a in range(n) for i in range(2)},
        compiler_params=pltpu.CompilerParams(has_side_effects=_EFFECT),
    )(*ins, after)
    return [(outs[2 * a], outs[2 * a + 1]) for a in range(n)]


def _adamw_math(w, g, m, v):
    m = ADAM_B1 * m + (1.0 - ADAM_B1) * g
    v = ADAM_B2 * v + (1.0 - ADAM_B2) * (g * g)
    m_hat = m / (1.0 - ADAM_B1 ** ADAM_STEP)
    v_hat = v / (1.0 - ADAM_B2 ** ADAM_STEP)
    delta = -ADAM_LR * (m_hat / (jnp.sqrt(v_hat) + ADAM_EPS) + ADAM_WD * w)
    return delta, m, v


def _adamw_sharded(mine, landed, me, w, m, v, layer, prev, after, name):
    depth, r, c = w.shape
    tr = _tile(r, 128, 8)
    n_prev = 0 if prev is None else 4

    def body(*refs):
        me_ref, own_ref, land_ref, w_ref, m_ref, v_ref = refs[:6]
        g_ref, d_ref, nm_ref, nv_ref = refs[7 + n_prev:]
        g = None
        for k in range(N_DEV):
            part = jnp.where(me_ref[0] == k, own_ref[0], land_ref[k]).astype(F32)
            g = part if g is None else g + part
        delta, nm, nv = _adamw_math(w_ref[...], g, m_ref[...], v_ref[...])
        g_ref[...] = g
        d_ref[...] = delta
        nm_ref[...] = nm
        nv_ref[...] = nv

    lay = pl.BlockSpec((None, tr, c), lambda i, me_ref: (layer, i, 0))
    return pl.pallas_call(
        body, name=name,
        grid_spec=pltpu.PrefetchScalarGridSpec(
            num_scalar_prefetch=1, grid=(r // tr,),
            in_specs=[pl.BlockSpec((1, tr, c), lambda i, me_ref: (me_ref[0], i, 0)),
                      pl.BlockSpec((N_DEV, tr, c), lambda i, me_ref: (0, i, 0)), lay, lay, lay]
            + [_ANY] * (1 + n_prev),
            out_specs=[lay] * 4),
        out_shape=[jax.ShapeDtypeStruct(w.shape, F32)] * 4,
        input_output_aliases={7 + i: i for i in range(n_prev)},
        compiler_params=_cparams(("parallel",)),
    )(me, mine, landed, w, m, v, after, *(prev or ()))


def _sum_devices(stack, name):
    _, r, c = stack.shape
    tr = _tile(r, PACK_ROWS, 8)

    def body(s_ref, o_ref):
        g = s_ref[0]
        for j in range(1, N_DEV):
            g = g + s_ref[j]
        o_ref[...] = g

    return pl.pallas_call(
        body, name=name, grid=(r // tr,), in_specs=[pl.BlockSpec((N_DEV, tr, c), lambda i: (0, i, 0))],
        out_specs=pl.BlockSpec((tr, c), lambda i: (i, 0)), out_shape=jax.ShapeDtypeStruct((r, c), F32),
        compiler_params=_cparams(("parallel",)),
    )(stack)


def _adamw_packed(w, g, m, v, name):
    r, c = w.shape
    tr = _tile(r, PACK_ROWS, 8)

    def body(w_ref, g_ref, m_ref, v_ref, d_ref, nm_ref, nv_ref):
        delta, nm, nv = _adamw_math(w_ref[...], g_ref[...], m_ref[...], v_ref[...])
        d_ref[...] = delta
        nm_ref[...] = nm
        nv_ref[...] = nv

    spec = pl.BlockSpec((tr, c), lambda i: (i, 0))
    return pl.pallas_call(
        body, name=name, grid=(r // tr,), in_specs=[spec] * 4, out_specs=[spec] * 3,
        out_shape=[jax.ShapeDtypeStruct((r, c), F32)] * 3, compiler_params=_cparams(("parallel",)),
    )(w, g, m, v)


def _pack(arrays):
    unit = 8 * LANE
    flat = []
    for a in arrays:
        f = a.reshape(-1)
        flat.append(jnp.pad(f, (0, (-f.shape[0]) % unit)))
    total = sum(f.shape[0] for f in flat)
    flat.append(jnp.zeros(((-total) % (PACK_ROWS * LANE),), flat[0].dtype))
    return jnp.concatenate(flat).reshape(-1, LANE)


def _unpack(packed, like):
    unit = 8 * LANE
    flat = packed.reshape(-1)
    out, off = [], 0
    for a in like:
        n = 1
        for d in a.shape:
            n *= d
        out.append(flat[off:off + n].reshape(a.shape))
        off += n + (-n) % unit
    return out


class _Layout:
    def __init__(self, d):
        self.d = d
        self.n_front = 4 * GDN_WIDTH
        self.n_ab = 2 * HEADS
        self.n_back = 2 * LRU_WIDTH + 2 * d
        self.cols = self.n_front + self.n_ab + self.n_back
        self.pcols = self.n_front + self.n_back + LANE
        self.z_base = 3 * GDN_WIDTH // HEAD_DIM
        self.xb_base = self.n_front // LRU_BLOCK_DIM
        self.gg_off = self.n_front + 2 * LRU_WIDTH
        assert self.gg_off % d == 0
        self.gg_base = self.gg_off // d
        self.ab_base = (self.n_front + self.n_back) // LANE

    def _ranges(self):
        f, ab = self.n_front, self.n_ab
        return [(0, f), (f + ab, self.cols), (f, f + ab)]

    def permuted_from_stack(self, stack):
        blk = stack.shape[2]
        pieces = []
        for lo, hi in self._ranges():
            while lo < hi:
                k = lo // blk
                end = min(hi, (k + 1) * blk)
                pieces.append(stack[k, :, lo - k * blk:end - k * blk])
                lo = end
        pieces.append(jnp.zeros((stack.shape[1], LANE - self.n_ab), stack.dtype))
        return jnp.concatenate(pieces, axis=1)

    def stack_from_permuted(self, w):
        blk = self.cols // N_DEV
        where = {}
        pos = 0
        for lo, hi in self._ranges():
            where[lo] = pos
            pos += hi - lo
        cuts = sorted(lo for lo, _ in self._ranges()) + [self.cols]
        blocks = []
        for k in range(N_DEV):
            lo, pieces = k * blk, []
            while lo < (k + 1) * blk:
                start = max(c for c in cuts if c <= lo)
                end = min((k + 1) * blk, min(c for c in cuts if c > lo))
                pieces.append(w[:, where[start] + lo - start:where[start] + end - start])
                lo = end
            blocks.append(pieces[0] if len(pieces) == 1 else jnp.concatenate(pieces, axis=1))
        return jnp.stack(blocks)


def _cols_from_stack(stack):
    nd, k, n = stack.shape
    return stack.transpose(1, 0, 2).reshape(k, nd * n)


def _stack_from_cols(w):
    k, n = w.shape
    return w.reshape(k, N_DEV, n // N_DEV).transpose(1, 0, 2)


def _after(v, *tokens):
    for t in tokens:
        v = v + t[0, 0]
    return v


def _layer_fwd(x, p, late, lay, tag):
    s, d = x.shape
    nc = s // CHUNK
    h = _rms_fwd(x, p["attn_norm"], f"rms1_fwd{tag}")
    proj = _matmul(h, p["w_in"], "nn", f"in_proj{tag}", [F32], tn=1152)
    qkv = _conv_silu_fwd(proj, p["gdn_conv_w"], f"gdn_conv_fwd{tag}")
    gc, beta = _gdn_gates_fwd(proj, p["gdn_a_log"], p["gdn_dt_bias"], lay.ab_base, f"gdn_gates_fwd{tag}")
    grow = gc[:, :HEADS].reshape(nc, CHUNK, HEADS).transpose(0, 2, 1)
    o, states, tinv = _gdn_chunk_fwd(qkv, gc, beta, grow, f"gdn_chunk_fwd{tag}")
    og = _gdn_post_fwd(o, proj, p["gdn_norm"], lay.z_base, f"gdn_post_fwd{tag}")
    ol, hs, decay = _lru_fwd(proj, p["lru_conv_w"], p["lru_conv_b"], p["lru_w_a"], p["lru_b_a"], p["lru_w_x"], p["lru_b_x"],
                      p["lru_lambda"], lay.xb_base, f"lru_fwd{tag}")
    late[0](hs)
    yg = _matmul(og, p["w_branch_gdn"], "nn", f"branch_gdn{tag}", [BF])
    yl = _matmul(ol, p["w_branch_lru"], "nn", f"branch_lru{tag}", [BF])
    merged = _merge_fwd(proj, yg, yl, lay.gg_base, f"merge_fwd{tag}")
    x1 = _matmul(merged, p["w_out"], "nn", f"out_proj{tag}", [F32], epi=lambda acc, r: (acc + r,), extras=[x])
    h2 = _rms_fwd(x1, p["mlp_norm"], f"rms2_fwd{tag}")
    late[1](h2)
    u, act = _matmul(h2, p["w_up"], "nn", f"mlp_up{tag}", [BF, BF],
                     epi=lambda acc: (acc, jnp.square(jnp.maximum(acc, 0.0))), shards=True)
    x2 = _matmul(act, p["w_down"], "nn", f"mlp_down{tag}", [F32], epi=lambda acc, r: (acc + r,), extras=[x1])
    saved = dict(x=x, h=h, proj=proj, qkv=qkv, gc=gc, beta=beta, grow=grow, o=o, states=states, tinv=tinv, og=og, ol=ol, hs=hs, decay=decay,
                 yg=yg, yl=yl, merged=merged, x1=x1, h2=h2, u=u, act=act)
    return x2, saved


def _layer_bwd(dx2, p, sv, send, send_small, after_last_send, after, lay, tag):
    s, d = dx2.shape
    nc = s // CHUNK
    small = {}
    du = _matmul(dx2, p["w_down"], "nt", f"d_act{tag}", [BF],
                 epi=lambda acc, u: (acc * (2.0 * jnp.maximum(u.astype(F32), 0.0)),), extras=[sv["u"]], after=after)
    sent = send({"w_down": _matmul(sv["act"], dx2, "tn", f"dw_down{tag}", [BF]).reshape(N_DEV, -1, d),
                 "w_up": _matmul(sv["h2"], du, "tn", f"dw_up{tag}", [BF], shards=True)})
    dh2 = _matmul(du, p["w_up"], "nt", f"d_h2{tag}", [F32], shards=True)
    dx1, g = _rms_bwd(sv["x1"], _after(p["mlp_norm"], sent), dh2, dx2, f"rms2_bwd{tag}")
    small["mlp_norm"] = g.reshape(-1)
    dmerged = _matmul(dx1, p["w_out"], "nt", f"d_merged{tag}", [F32])
    dwout = _matmul(sv["merged"], dx1, "tn", f"dw_out{tag}", [BF]).reshape(N_DEV, -1, d)
    dgg, dgl, dyg, dyl = _merge_bwd(sv["proj"], sv["yg"], sv["yl"], dmerged, lay.gg_base, f"merge_bwd{tag}")
    sent = send({"w_out": dwout,
                 "w_branch_gdn": _stack_from_cols(_matmul(sv["og"], dyg, "tn", f"dw_bg{tag}", [BF])),
                 "w_branch_lru": _stack_from_cols(_matmul(sv["ol"], dyl, "tn", f"dw_bl{tag}", [BF]))})
    dog = _matmul(dyg, p["w_branch_gdn"], "nt", f"d_og{tag}", [F32])
    dol = _matmul(dyl, p["w_branch_lru"], "nt", f"d_ol{tag}", [F32])
    (dxb, dyb, dcw, dcb, dwa, dba, dwx, dbx, dlam) = _lru_bwd(
        sv["proj"], sv["hs"], sv["decay"], dol, p["lru_conv_w"], p["lru_conv_b"], p["lru_w_a"], p["lru_b_a"], p["lru_w_x"],
        p["lru_b_x"], _after(p["lru_lambda"], sent), lay.xb_base, f"lru_bwd{tag}")
    small.update(lru_conv_w=dcw, lru_conv_b=dcb.reshape(-1), lru_w_a=dwa, lru_b_a=dba.reshape(-1), lru_w_x=dwx,
                 lru_b_x=dbx.reshape(-1), lru_lambda=dlam.reshape(-1))
    do, dz, g = _gdn_post_bwd(sv["o"], sv["proj"], p["gdn_norm"], dog, lay.z_base, f"gdn_post_bwd{tag}")
    small["gdn_norm"] = g.reshape(-1)
    dqkv_c, dgc_col, dbeta, dgrow = _gdn_chunk_bwd(sv["qkv"], sv["gc"], sv["beta"], sv["grow"], sv["states"], sv["tinv"], do,
                                                   f"gdn_chunk_bwd{tag}")
    dqkv, dgcw = _conv_silu_bwd(sv["proj"], p["gdn_conv_w"], dqkv_c, f"gdn_conv_bwd{tag}")
    small["gdn_conv_w"] = dgcw
    dgc_row = jnp.pad(dgrow.transpose(0, 2, 1).reshape(s, HEADS), ((0, 0), (0, LANE - HEADS)))
    dab, dal, ddt = _gdn_gates_bwd(sv["proj"], p["gdn_a_log"], p["gdn_dt_bias"], dgc_col, dgc_row, dbeta, lay.ab_base,
                                   f"gdn_gates_bwd{tag}")
    small["gdn_a_log"] = dal[0, :HEADS]
    small["gdn_dt_bias"] = ddt[0, :HEADS]
    dproj = jnp.concatenate([dqkv, dz, dxb, dyb, dgg, dgl, dab], axis=1)
    sent = send_small(small, "main")
    dwin = _matmul(sv["h"], dproj, "tn", f"dw_in{tag}", [BF], tn=1152, after=sent)
    sent = send({"w_in": lay.stack_from_permuted(dwin)})
    dh = _matmul(dproj, p["w_in"], "nt", f"d_h{tag}", [F32], tk=3456, after=sent)
    dx, g = _rms_bwd(sv["x"], p["attn_norm"], dh, dx1, f"rms1_bwd{tag}")
    after_last_send(send_small({"attn_norm": g.reshape(-1)}, "attn"))
    return dx


_BIG = ("w_in", "w_branch_gdn", "w_branch_lru", "w_out", "w_up", "w_down")
_ROW_SHARDED = ("w_out", "w_down")
_CONV = ("gdn_conv_w", "lru_conv_w")
_SMALL = ("attn_norm", "gdn_a_log", "gdn_dt_bias", "gdn_norm", "lru_conv_b", "lru_w_a", "lru_b_a", "lru_w_x",
          "lru_b_x", "lru_lambda", "mlp_norm")
_ORDER = ("attn_norm", "w_in", "gdn_conv_w", "gdn_a_log", "gdn_dt_bias", "gdn_norm", "lru_conv_w", "lru_conv_b",
          "lru_w_a", "lru_b_a", "lru_w_x", "lru_b_x", "lru_lambda", "w_branch_gdn", "w_branch_lru", "w_out",
          "mlp_norm", "w_up", "w_down", "final_norm")


def kernel(x, attn_norm, w_in, gdn_conv_w, gdn_a_log, gdn_dt_bias, gdn_norm, lru_conv_w, lru_conv_b, lru_w_a, lru_b_a, lru_w_x, lru_b_x, lru_lambda, w_branch_gdn, w_branch_lru, w_out, mlp_norm, w_up, w_down, final_norm, loss_target, m_attn_norm, m_w_in, m_gdn_conv_w, m_gdn_a_log, m_gdn_dt_bias, m_gdn_norm, m_lru_conv_w, m_lru_conv_b, m_lru_w_a, m_lru_b_a, m_lru_w_x, m_lru_b_x, m_lru_lambda, m_w_branch_gdn, m_w_branch_lru, m_w_out, m_mlp_norm, m_w_up, m_w_down, m_final_norm, v_attn_norm, v_w_in, v_gdn_conv_w, v_gdn_a_log, v_gdn_dt_bias, v_gdn_norm, v_lru_conv_w, v_lru_conv_b, v_lru_w_a, v_lru_b_a, v_lru_w_x, v_lru_b_x, v_lru_lambda, v_w_branch_gdn, v_w_branch_lru, v_w_out, v_mlp_norm, v_w_up, v_w_down, v_final_norm):
    w = dict(attn_norm=attn_norm, w_in=w_in, gdn_conv_w=gdn_conv_w, gdn_a_log=gdn_a_log, gdn_dt_bias=gdn_dt_bias, gdn_norm=gdn_norm, lru_conv_w=lru_conv_w, lru_conv_b=lru_conv_b, lru_w_a=lru_w_a, lru_b_a=lru_b_a, lru_w_x=lru_w_x, lru_b_x=lru_b_x, lru_lambda=lru_lambda, w_branch_gdn=w_branch_gdn, w_branch_lru=w_branch_lru, w_out=w_out, mlp_norm=mlp_norm, w_up=w_up, w_down=w_down, final_norm=final_norm)
    m = dict(attn_norm=m_attn_norm, w_in=m_w_in, gdn_conv_w=m_gdn_conv_w, gdn_a_log=m_gdn_a_log, gdn_dt_bias=m_gdn_dt_bias, gdn_norm=m_gdn_norm, lru_conv_w=m_lru_conv_w, lru_conv_b=m_lru_conv_b, lru_w_a=m_lru_w_a, lru_b_a=m_lru_b_a, lru_w_x=m_lru_w_x, lru_b_x=m_lru_b_x, lru_lambda=m_lru_lambda, w_branch_gdn=m_w_branch_gdn, w_branch_lru=m_w_branch_lru, w_out=m_w_out, mlp_norm=m_mlp_norm, w_up=m_w_up, w_down=m_w_down, final_norm=m_final_norm)
    v = dict(attn_norm=v_attn_norm, w_in=v_w_in, gdn_conv_w=v_gdn_conv_w, gdn_a_log=v_gdn_a_log, gdn_dt_bias=v_gdn_dt_bias, gdn_norm=v_gdn_norm, lru_conv_w=v_lru_conv_w, lru_conv_b=v_lru_conv_b, lru_w_a=v_lru_w_a, lru_b_a=v_lru_b_a, lru_w_x=v_lru_w_x, lru_b_x=v_lru_b_x, lru_lambda=v_lru_lambda, w_branch_gdn=v_w_branch_gdn, w_branch_lru=v_w_branch_lru, w_out=v_w_out, mlp_norm=v_mlp_norm, w_up=v_w_up, w_down=v_w_down, final_norm=v_final_norm)
    depth = w_in.shape[0]
    x = x[0]
    target = loss_target[0]
    s, d = x.shape
    lay = _Layout(d)
    assert lay.cols == w_in.shape[2] * N_DEV
    cx, cy, cc = _place()
    me = 4 * cx + 2 * cy + cc

    me_arr = me.astype(jnp.int32).reshape(1)

    early, mixer, mlp = ("w_in",) + _CONV, ("w_branch_gdn", "w_branch_lru", "w_out"), ("w_up", "w_down")

    def shard(n, l):
        return w[n][l].astype(BF) if n in _BIG else w[n][l]

    def joined(name, stack):
        if name == "w_in":
            return lay.permuted_from_stack(stack)
        if name == "w_up":
            return stack
        return stack.reshape(-1, stack.shape[-1]) if name in _ROW_SHARDED else _cols_from_stack(stack)

    def start_weights(l, names, after):
        flights, token = _send_start([shard(n, l) for n in names], True, after, f"weights_start_l{l}")
        return dict(zip(names, flights)), token

    def wait_weights(l, flights, names, p, after, which):
        got = _send_wait([flights[n] for n in names], after, f"weights_wait_{which}_l{l}")
        for n, (own, landed) in zip(names, got):
            p[n] = joined(n, lax.dynamic_update_slice_in_dim(landed, own[None], me, axis=0))

    saved, params = [], []
    xl = x
    first = dict(zip(early + mixer, _all_gather([shard(n, 0) for n in early + mixer], "gather_first_weights")))
    flights, token = start_weights(0, mlp, first["w_in"])
    for l in range(depth):
        p = {n: w[n][l] for n in _SMALL}
        if l == 0:
            p.update({n: joined(n, g) for n, g in first.items()})
        else:
            wait_weights(l, flights, early, p, xl, "early")
        mine = flights
        if l + 1 < depth:
            flights, token = start_weights(l + 1, early + mixer + mlp, token)
        p["attn_norm"] = _after(p["attn_norm"], token)
        late = [functools.partial(wait_weights, l, mine, names, p, which=which) if names[0] in mine
                else (lambda after: None) for names, which in ((mixer, "mixer"), (mlp, "mlp"))]
        xl, sv = _layer_fwd(xl, p, late, lay, f"_l{l}")
        saved.append(sv)
        params.append(p)
    dx, loss_row, dfinal = _loss_head(xl, final_norm, target, "loss_head")
    loss = lax.psum(loss_row[0, 0], ("x", "y", "c"))

    out = {n: None for n in _BIG}
    pending = {l: {} for l in range(depth)}

    def sender(l):
        def send(stacks):
            names = tuple(stacks)
            fl, token = _send_start([stacks[n] for n in names], False, me_arr, f"grads_start_{'_'.join(names)}_l{l}")
            pending[l].update(zip(names, fl))
            return token
        return send

    def finish(l, names, after, which):
        got = _send_wait([pending[l][n] for n in names], after, f"grads_wait_{which}_l{l}")
        for n, (mine, landed) in zip(names, got):
            out[n] = _adamw_sharded(mine, landed, me_arr, w[n], m[n], v[n], l, out[n], after, f"adamw_{n}_l{l}")
            after = out[n][0]
        return after

    small_names = _SMALL + _CONV
    small_sent = []

    def small_sender(l):
        def send_small(small, which):
            small = dict(small)
            if which == "main" and l == depth - 1:
                small["final_norm"] = dfinal.reshape(-1)
            names = tuple(small)
            flights, token = _send_start([_pack([small[n] for n in names])], True, me_arr,
                                         f"small_grads_start_{which}_l{l}")
            small_sent.append((l, which, names, [small[n] for n in names], flights[0]))
            return token
        return send_small

    last = [None]

    def hook_for(l):
        def hook(token):
            last[0] = finish(l + 1, _BIG, token, "all") if l + 1 < depth else token
        return hook

    for l in reversed(range(depth)):
        dx = _layer_bwd(dx, params[l], saved[l], sender(l), small_sender(l), hook_for(l), last[0], lay, f"_l{l}")

    last[0] = finish(0, mixer + mlp, last[0], "late")
    got = _send_wait([f for *_, f in small_sent], last[0], "small_grads_wait")
    summed = [{} for _ in range(depth)]
    for (l, which, names, arrays, _), (part, landed) in zip(small_sent, got):
        total = _sum_devices(lax.dynamic_update_slice_in_dim(landed, part[None], me, axis=0),
                             f"sum_small_grads_{which}_l{l}")
        summed[l].update(zip(names, _unpack(total, arrays)))
    grads = {n: jnp.stack([summed[l][n] for l in range(depth)]) for n in small_names}
    grads["final_norm"] = summed[depth - 1]["final_norm"]
    for n in _CONV:
        blk = w[n].shape[2]
        grads[n] = lax.dynamic_slice_in_dim(grads[n], me * blk, blk, axis=2)
    names = small_names + ("final_norm",)
    gp = _pack([grads[n] for n in names])
    upd = _adamw_packed(_pack([w[n] for n in names]), gp, _pack([m[n] for n in names]), _pack([v[n] for n in names]),
                        "adamw_small")
    like = [w[n] for n in names]
    unp = [_unpack(u, like) for u in upd]
    for i, n in enumerate(names):
        out[n] = (grads[n], unp[0][i], unp[1][i], unp[2][i])
    finish(0, ("w_in",), upd[0], "w_in")

    res = [loss, dx.reshape(1, s, d)]
    for k in range(4):
        res += [out[n][k] for n in _ORDER]
    return tuple(res)
```

```python
import functools

import jax
import jax.numpy as jnp
from jax import lax
from jax.experimental import pallas as pl
from jax.experimental.pallas import tpu as pltpu

F32 = jnp.float32
BF = jnp.bfloat16
MESH = pl.DeviceIdType.MESH
N_DEV = 8
N_CHIP = 4

HEADS = 8
HEAD_DIM = 128
GDN_WIDTH = HEADS * HEAD_DIM
CHUNK = 64
CONV_WIDTH = 4
LRU_WIDTH = 1024
LRU_BLOCKS = 8
LRU_BLOCK_DIM = 128
LRU_C = 8.0
RMS_EPS = 1e-6
L2_EPS = 1e-6
ADAM_LR = 0.001
ADAM_B1 = 0.9
ADAM_B2 = 0.999
ADAM_EPS = 1e-08
ADAM_WD = 0.01
ADAM_STEP = 10

LANE = 128
PACK_ROWS = 512
VMEM_LIMIT = 56 * 1024 * 1024


def _cparams(sem=None):
    return pltpu.CompilerParams(dimension_semantics=sem, vmem_limit_bytes=VMEM_LIMIT)


def _tile(n, pref, unit=LANE):
    if n <= pref:
        return n
    best = None
    for t in range(unit, pref + 1, unit):
        if n % t == 0:
            best = t
    assert best is not None, (n, pref, unit)
    return best


_NN = (((1,), (0,)), ((), ()))
_NT = (((1,), (1,)), ((), ()))
_TN = (((0,), (0,)), ((), ()))


def _dg(a, b, dims):
    return lax.dot_general(a.astype(BF), b.astype(BF), dims, preferred_element_type=F32)


class _MM:
    def __init__(self, nn, nt, tn):
        self.nn, self.nt, self.tn = nn, nt, tn


def _make_mm(nn, nt, tn):
    def p_nn(a, b):
        return _dg(a, b, nn)

    def p_nt(a, b):
        return _dg(a, b, nt)

    def p_tn(a, b):
        return _dg(a, b, tn)

    d_nn, d_nt, d_tn = jax.custom_vjp(p_nn), jax.custom_vjp(p_nt), jax.custom_vjp(p_tn)

    def save(f):
        return lambda a, b: (f(a, b), (a, b))

    d_nn.defvjp(save(d_nn), lambda r, g: (d_nt(g, r[1]), d_tn(r[0], g)))
    d_nt.defvjp(save(d_nt), lambda r, g: (d_nn(g, r[1]), d_tn(g, r[0])))
    d_tn.defvjp(save(d_tn), lambda r, g: (d_nt(r[1], g), d_nn(r[0], g)))
    return _MM(p_nn, p_nt, p_tn), _MM(d_nn, d_nt, d_tn)


_BNN = (((2,), (1,)), ((0,), (0,)))
_BNT = (((2,), (2,)), ((0,), (0,)))
_BTN = (((1,), (1,)), ((0,), (0,)))

_PLAIN, _DIFF = _make_mm(_NN, _NT, _TN)
_PLAIN_B, _DIFF_B = _make_mm(_BNN, _BNT, _BTN)


def _sigmoid(x):
    return 1.0 / (1.0 + jnp.exp(-x))


def _silu(x):
    return x * _sigmoid(x)


def _softplus(x):
    return jnp.maximum(x, 0.0) + jnp.log(1.0 + jnp.exp(-jnp.maximum(x, -x)))


def _gelu(x):
    return 0.5 * x * (1.0 + jnp.tanh(0.7978845608028654 * (x + 0.044715 * (x * x * x))))


def _expm1(x):
    t = jnp.tanh(0.5 * x)
    return 2.0 * t / (1.0 - t)


def _rms(x, gain):
    ms = jnp.mean(x * x, axis=-1, keepdims=True)
    return x * lax.rsqrt(ms + RMS_EPS) * gain


def _l2n(x):
    return x * lax.rsqrt(jnp.sum(x * x, axis=-1, keepdims=True) + L2_EPS)


def _shift_down(x, r):
    if r == 0:
        return x
    row = lax.broadcasted_iota(jnp.int32, x.shape, 0)
    return jnp.where(row >= r, pltpu.roll(x, r, 0), 0.0)


def _shift_up(x, r):
    if r == 0:
        return x
    n = x.shape[0]
    row = lax.broadcasted_iota(jnp.int32, x.shape, 0)
    return jnp.where(row < n - r, pltpu.roll(x, n - r, 0), 0.0)


def _conv(x, w_ref):
    y = None
    for j in range(CONV_WIDTH):
        t = _shift_down(x, CONV_WIDTH - 1 - j) * w_ref[j:j + 1, :]
        y = t if y is None else y + t
    return y


def _conv_bwd(x, dy, w_ref):
    dx = None
    dws = []
    for j in range(CONV_WIDTH):
        r = CONV_WIDTH - 1 - j
        t = _shift_up(dy, r) * w_ref[j:j + 1, :]
        dx = t if dx is None else dx + t
        dws.append(jnp.sum(dy * _shift_down(x, r), axis=0, keepdims=True))
    return dx, dws


def _scan_tile(a, b, reverse):
    n = a.shape[0]
    row = lax.broadcasted_iota(jnp.int32, a.shape, 0)
    s = 1
    while s < n:
        if reverse:
            keep = row < n - s
            a_sh = jnp.where(keep, pltpu.roll(a, n - s, 0), 1.0)
            b_sh = jnp.where(keep, pltpu.roll(b, n - s, 0), 0.0)
        else:
            keep = row >= s
            a_sh = jnp.where(keep, pltpu.roll(a, s, 0), 1.0)
            b_sh = jnp.where(keep, pltpu.roll(b, s, 0), 0.0)
        b = a * b_sh + b
        a = a * a_sh
        s *= 2
    return a, b


def _dg3(a, b, dims):
    a_hi = a.astype(BF)
    b_hi = b.astype(BF)
    a_lo = a - a_hi.astype(F32)
    b_lo = b - b_hi.astype(F32)
    return _dg(a_hi, b_hi, dims) + (_dg(a_hi, b_lo, dims) + _dg(a_lo, b_hi, dims))


@jax.custom_vjp
def _tri_inv(a):
    n = a.shape[1]
    p = _dg3(a, a, _BNN)
    r = p
    e = 2
    while 2 * e < n:
        p = _dg3(p, p, _BNN)
        r = r + p + _dg3(r, p, _BNN)
        e *= 2
    row = lax.broadcasted_iota(jnp.int32, (1, n, n), 1)
    col = lax.broadcasted_iota(jnp.int32, (1, n, n), 2)
    eye = jnp.where(row == col, 1.0, 0.0)
    return eye - a + r - _dg3(a, r, _BNN)


def _tri_inv_fwd(a):
    t = _tri_inv(a)
    return t, t


def _tri_inv_bwd(t, g):
    return (-_dg3(_dg3(t, g, _BTN), t, _BNT),)


_tri_inv.defvjp(_tri_inv_fwd, _tri_inv_bwd)


@jax.custom_vjp
def _tri_inv_saved(a, t):
    return t


_tri_inv_saved.defvjp(lambda a, t: (t, t), lambda t, g: (_tri_inv_bwd(t, g)[0], jnp.zeros_like(t)))


def _gdn_chunk(q, k, v, beta, gcol, grow, state, mm, t_saved=None):
    c = q.shape[1]
    row = lax.broadcasted_iota(jnp.int32, (1, c, c), 1)
    col = lax.broadcasted_iota(jnp.int32, (1, c, c), 2)
    causal = row >= col
    strict = row > col
    qn = _l2n(q) * (HEAD_DIM ** -0.5)
    kn = _l2n(k)
    dec = jnp.where(causal, jnp.exp(jnp.where(causal, gcol - grow, 0.0)), 0.0)
    kb = kn * beta
    vb = v * beta
    a = jnp.where(strict, mm.nt(kb, kn) * dec, 0.0)
    t = _tri_inv(a) if t_saved is None else _tri_inv_saved(a, t_saved)
    eg = jnp.exp(gcol)
    u = mm.nn(t, vb)
    w = mm.nn(t, kb * eg)
    p = jnp.where(causal, mm.nt(qn, kn) * dec, 0.0)
    vn = u - mm.nn(w, state)
    o = mm.nn(qn * eg, state) + mm.nn(p, vn)
    last = lax.broadcasted_iota(jnp.int32, (1, c, 1), 1) == c - 1
    gl = jnp.sum(jnp.where(last, gcol, 0.0), axis=1, keepdims=True)
    kd = kn * jnp.exp(gl - gcol)
    new_state = state * jnp.exp(gl) + mm.tn(kd, vn)
    return o, new_state, t


def _lru_gates(xc, wa, ba, wx, bx, lam, mm):
    r = _sigmoid(mm.nn(xc, wa) + ba)
    i = _sigmoid(mm.nn(xc, wx) + bx)
    log_a = -LRU_C * r * _softplus(-lam)
    a = jnp.exp(log_a)
    bterm = jnp.sqrt(-_expm1(2.0 * log_a)) * (i * xc)
    return a, bterm


def _matmul(a, b, mode, name, out_dtypes, epi=None, extras=(), tm=1024, tn=1024, tk=2048, after=None,
            shards=False, tail=False):
    if shards:
        nd, rows, per = b.shape if mode != "tn" else (N_DEV, a.shape[1], b.shape[1] // N_DEV)
        b_shape = b.shape if mode == "tn" else (rows, nd * per)
    else:
        b_shape = b.shape
    if mode == "nn":
        (m, k), (k2, n) = a.shape, b_shape
    elif mode == "nt":
        (m, k), (n, k2) = a.shape, b_shape
    else:
        (k, m), (k2, n) = a.shape, b_shape
    assert k == k2, (a.shape, b.shape, mode)
    if shards:
        tn, tk = (min(tn, per), tk) if mode != "nt" else (tn, min(tk, per))
    tm, tn, tk = _tile(m, tm), _tile(n, tn), _tile(k, tk)
    nk = k // tk
    dims = {"nn": _NN, "nt": _NT, "tn": _TN}[mode]
    if mode == "nn":
        a_spec = pl.BlockSpec((tm, tk), lambda i, j, kk: (i, kk))
        b_spec = pl.BlockSpec((tk, tn), lambda i, j, kk: (kk, j))
        if shards:
            assert per % tn == 0
            b_spec = pl.BlockSpec((None, tk, tn), lambda i, j, kk: (j * tn // per, kk, j * tn % per // tn))
    elif mode == "nt":
        a_spec = pl.BlockSpec((tm, tk), lambda i, j, kk: (i, kk))
        b_spec = pl.BlockSpec((tn, tk), lambda i, j, kk: (j, kk))
        if shards:
            assert per % tk == 0
            b_spec = pl.BlockSpec((None, tn, tk), lambda i, j, kk: (kk * tk // per, j, kk * tk % per // tk))
    else:
        a_spec = pl.BlockSpec((tk, tm), lambda i, j, kk: (kk, i))
        b_spec = pl.BlockSpec((tk, tn), lambda i, j, kk: (kk, j))
    o_spec = pl.BlockSpec((tm, tn), lambda i, j, kk: (i, j))
    out_spec, out_dims = o_spec, (m, n)
    if shards and mode == "tn":
        assert per % tn == 0 and not extras
        out_spec = pl.BlockSpec((None, tm, tn), lambda i, j, kk: (j * tn // per, i, j * tn % per // tn))
        out_dims = (N_DEV, m, per)
    n_ex, n_out = len(extras), len(out_dtypes)
    order = [] if after is None else [after]
    n_in = 2 + n_ex + len(order)
    if epi is None:
        epi = lambda acc: (acc,)

    def body(*refs):
        a_ref, b_ref = refs[0], refs[1]
        ex_refs = refs[2:2 + n_ex]
        out_refs = refs[n_in:n_in + n_out]
        kk = pl.program_id(2)
        part = _dg(a_ref[...], b_ref[...], dims)

        def finish(total):
            res = epi(total, *[r[...] for r in ex_refs])
            for r, v in zip(out_refs, res):
                r[...] = v.astype(r.dtype)
            if tail:
                @pl.when(pl.program_id(1) == n // tn - 1)
                def _():
                    refs[n_in + n_out][...] = total[:, tn - LANE:]

        if nk == 1:
            finish(part)
            return
        acc_ref = refs[-1]

        @pl.when(kk == 0)
        def _():
            acc_ref[...] = part

        @pl.when(jnp.logical_and(kk > 0, kk < nk - 1))
        def _():
            acc_ref[...] += part

        @pl.when(kk == nk - 1)
        def _():
            finish(acc_ref[...] + part)

    outs = pl.pallas_call(
        body, name=name, grid=(m // tm, n // tn, nk),
        in_specs=[a_spec, b_spec] + [o_spec] * n_ex + [pl.BlockSpec(memory_space=pl.ANY)] * len(order),
        out_specs=[out_spec] * n_out + [pl.BlockSpec((tm, LANE), lambda i, j, kk: (i, 0))] * tail,
        out_shape=[jax.ShapeDtypeStruct(out_dims, d) for d in out_dtypes]
        + [jax.ShapeDtypeStruct((m, LANE), F32)] * tail,
        scratch_shapes=[pltpu.VMEM((tm, tn), F32)] if nk > 1 else [],
        compiler_params=_cparams(("parallel", "arbitrary" if tail else "parallel", "arbitrary")),
    )(a, b, *extras, *order)
    return outs[0] if n_out + tail == 1 else outs


def _rowwise(fn, name, rows, tr, row_ins, params, row_outs, acc_outs=(), ncol=1):
    tr = _tile(rows, tr, 8)
    n_in, n_par, n_ro, n_acc = len(row_ins), len(params), len(row_outs), len(acc_outs)
    in_specs = []
    for _, width, base, per_col in row_ins:
        if per_col:
            in_specs.append(pl.BlockSpec((tr, width), lambda i, j, base=base: (i, base + j)))
        else:
            in_specs.append(pl.BlockSpec((tr, width), lambda i, j, base=base: (i, base)))
    for p in params:
        in_specs.append(pl.BlockSpec(p.shape, lambda i, j: (0, 0)))
    out_specs, out_shape = [], []
    for total, dtype, width, per_col in row_outs:
        if per_col:
            out_specs.append(pl.BlockSpec((tr, width), lambda i, j: (i, j)))
        else:
            out_specs.append(pl.BlockSpec((tr, width), lambda i, j: (i, 0)))
        out_shape.append(jax.ShapeDtypeStruct((rows, total), dtype))
    for shape, dtype in acc_outs:
        out_specs.append(pl.BlockSpec(shape, lambda i, j: (0, 0)))
        out_shape.append(jax.ShapeDtypeStruct(shape, dtype))

    def body(*refs):
        ins = [r[...].astype(F32) for r in refs[:n_in + n_par]]
        ro = refs[n_in + n_par:n_in + n_par + n_ro]
        ao = refs[n_in + n_par + n_ro:]
        res = fn(*ins)
        for r, v in zip(ro, res[:n_ro]):
            r[...] = v.astype(r.dtype)
        first = jnp.logical_and(pl.program_id(0) == 0, pl.program_id(1) == 0)

        @pl.when(first)
        def _():
            for r in ao:
                r[...] = jnp.zeros_like(r)

        for r, v in zip(ao, res[n_ro:]):
            r[...] += v.astype(r.dtype)

    sem = ("arbitrary", "arbitrary") if n_acc else ("parallel", "parallel")
    outs = pl.pallas_call(
        body, name=name, grid=(rows // tr, ncol), in_specs=in_specs, out_specs=out_specs,
        out_shape=out_shape, compiler_params=_cparams(sem),
    )(*[r[0] for r in row_ins], *params)
    return outs


def _row(x):
    return x.reshape(1, -1)


def _rms_fwd(x, gain, name):
    s, d = x.shape
    return _rowwise(lambda xt, g: (_rms(xt, g),), name, s, 512, [(x, d, 0, False)], [_row(gain)],
                    [(d, BF, d, False)])[0]


def _rms_bwd(x, gain, dh, dres, name):
    s, d = x.shape

    def fn(xt, dht, drt, g):
        _, vjp = jax.vjp(_rms, xt, g)
        dx, dg = vjp(dht)
        return drt + dx, dg

    return _rowwise(fn, name, s, 256, [(x, d, 0, False), (dh, d, 0, False), (dres, d, 0, False)], [_row(gain)],
                    [(d, F32, d, False)], [((1, d), F32)])


def _loss_head(x, gain, target, name):
    s, d = x.shape

    def fn(xt, tt, g):
        y, vjp = jax.vjp(_rms, xt, g)
        err = y - tt
        dx, dg = vjp(err * (1.0 / d))
        part = 0.5 * jnp.sum(jnp.sum(err * err, axis=1, keepdims=True), axis=0, keepdims=True) * (1.0 / d)
        return dx, jnp.broadcast_to(part, (1, LANE)), dg

    return _rowwise(fn, name, s, 256, [(x, d, 0, False), (target, d, 0, False)], [_row(gain)],
                    [(d, F32, d, False)], [((1, LANE), F32), ((1, d), F32)])


def _gdn_post_fn(o, z, g):
    return _rms(o, g) * _silu(z)


def _gdn_post_fwd(o, proj, gain, z_base, name):
    s = o.shape[0]
    return _rowwise(lambda ot, zt, g: (_gdn_post_fn(ot, zt, g),), name, s, 1024,
                    [(o, HEAD_DIM, 0, True), (proj, HEAD_DIM, z_base, True)], [_row(gain)],
                    [(GDN_WIDTH, BF, HEAD_DIM, True)], ncol=HEADS)[0]


def _gdn_post_bwd(o, proj, gain, dog, z_base, name):
    s = o.shape[0]

    def fn(ot, zt, dt, g):
        _, vjp = jax.vjp(_gdn_post_fn, ot, zt, g)
        return vjp(dt)

    return _rowwise(fn, name, s, 1024,
                    [(o, HEAD_DIM, 0, True), (proj, HEAD_DIM, z_base, True), (dog, HEAD_DIM, 0, True)],
                    [_row(gain)], [(GDN_WIDTH, F32, HEAD_DIM, True), (GDN_WIDTH, BF, HEAD_DIM, True)],
                    [((1, HEAD_DIM), F32)], ncol=HEADS)


def _merge_fn(gg, gl, yg, yl):
    return _sigmoid(gg) * yg + _sigmoid(gl) * yl


def _merge_fwd(proj, yg, yl, gg_base, name):
    s, d = yg.shape
    return _rowwise(lambda a, b, c, e: (_merge_fn(a, b, c, e),), name, s, 256,
                    [(proj, d, gg_base, False), (proj, d, gg_base + 1, False), (yg, d, 0, False), (yl, d, 0, False)],
                    [], [(d, BF, d, False)])[0]


def _merge_bwd(proj, yg, yl, dmerged, gg_base, name):
    s, d = yg.shape

    def fn(a, b, c, e, dm):
        _, vjp = jax.vjp(_merge_fn, a, b, c, e)
        return vjp(dm)

    return _rowwise(fn, name, s, 128,
                    [(proj, d, gg_base, False), (proj, d, gg_base + 1, False), (yg, d, 0, False), (yl, d, 0, False),
                     (dmerged, d, 0, False)], [], [(d, BF, d, False)] * 4)


def _conv_silu_fwd(proj, w, name):
    s = proj.shape[0]
    cols = w.shape[1]
    cw = _tile(cols, LANE)

    def body(x_ref, w_ref, o_ref):
        o_ref[...] = _silu(_conv(x_ref[...].astype(F32), w_ref))

    return pl.pallas_call(
        body, name=name, grid=(cols // cw,),
        in_specs=[pl.BlockSpec((s, cw), lambda j: (0, j)), pl.BlockSpec((CONV_WIDTH, cw), lambda j: (0, j))],
        out_specs=pl.BlockSpec((s, cw), lambda j: (0, j)),
        out_shape=jax.ShapeDtypeStruct((s, cols), F32), compiler_params=_cparams(("parallel",)),
    )(proj, w)


def _conv_silu_bwd(proj, w, dout, name):
    s = proj.shape[0]
    cols = w.shape[1]
    cw = _tile(cols, LANE)

    def body(x_ref, w_ref, d_ref, dx_ref, dw_ref):
        x = x_ref[...].astype(F32)
        y = _conv(x, w_ref)
        sg = _sigmoid(y)
        dy = d_ref[...] * (sg * (1.0 + y * (1.0 - sg)))
        dx, dws = _conv_bwd(x, dy, w_ref)
        dx_ref[...] = dx.astype(dx_ref.dtype)
        for j in range(CONV_WIDTH):
            dw_ref[j:j + 1, :] = dws[j]

    return pl.pallas_call(
        body, name=name, grid=(cols // cw,),
        in_specs=[pl.BlockSpec((s, cw), lambda j: (0, j)), pl.BlockSpec((CONV_WIDTH, cw), lambda j: (0, j)),
                  pl.BlockSpec((s, cw), lambda j: (0, j))],
        out_specs=[pl.BlockSpec((s, cw), lambda j: (0, j)), pl.BlockSpec((CONV_WIDTH, cw), lambda j: (0, j))],
        out_shape=[jax.ShapeDtypeStruct((s, cols), BF), jax.ShapeDtypeStruct((CONV_WIDTH, cols), F32)],
        compiler_params=_cparams(("parallel",)),
    )(proj, w, dout)


def _cumsum_mask(tr, transpose):
    row = lax.broadcasted_iota(jnp.int32, (tr, tr), 0)
    col = lax.broadcasted_iota(jnp.int32, (tr, tr), 1)
    same = (row // CHUNK) == (col // CHUNK)
    tri = (row <= col) if transpose else (row >= col)
    return jnp.where(jnp.logical_and(same, tri), 1.0, 0.0)


def _lane_pad(v):
    return jnp.pad(v.reshape(1, -1), ((0, 0), (0, LANE - v.shape[0])))


def _gdn_gates_fwd(proj, a_log, dt_bias, ab_base, name):
    s = proj.shape[0]

    def fn(ab, al, dt):
        g = -jnp.exp(al) * _softplus(ab + dt)
        beta = _sigmoid(pltpu.roll(ab, LANE - HEADS, 1))
        gc = jnp.dot(_cumsum_mask(ab.shape[0], False), g, precision=lax.Precision.HIGHEST,
                     preferred_element_type=F32)
        return gc, beta

    return _rowwise(fn, name, s, 512, [(proj, LANE, ab_base, False)], [_lane_pad(a_log), _lane_pad(dt_bias)],
                    [(LANE, F32, LANE, False)] * 2)


def _gdn_gates_bwd(proj, a_log, dt_bias, dgc_col, dgc_row, dbeta, ab_base, name):
    s = proj.shape[0]

    def fn(ab, dc, dr, db, al, dt):
        lane = lax.broadcasted_iota(jnp.int32, ab.shape, 1)
        dg = jnp.dot(_cumsum_mask(ab.shape[0], True), dc + dr, precision=lax.Precision.HIGHEST,
                     preferred_element_type=F32)
        ea = jnp.exp(al)
        pre = ab + dt
        g = -ea * _softplus(pre)
        da = jnp.where(lane < HEADS, dg * (-ea) * _sigmoid(pre), 0.0)
        dal = jnp.sum(jnp.where(lane < HEADS, dg * g, 0.0), axis=0, keepdims=True)
        ddt = jnp.sum(da, axis=0, keepdims=True)
        beta = _sigmoid(pltpu.roll(ab, LANE - HEADS, 1))
        dbl = jnp.where(lane < HEADS, db * beta * (1.0 - beta), 0.0)
        dab = da + pltpu.roll(dbl, HEADS, 1)
        return dab, dal, ddt

    return _rowwise(fn, name, s, 512,
                    [(proj, LANE, ab_base, False), (dgc_col, LANE, 0, False), (dgc_row, LANE, 0, False),
                     (dbeta, LANE, 0, False)], [_lane_pad(a_log), _lane_pad(dt_bias)],
                    [(LANE, BF, LANE, False)], [((1, LANE), F32), ((1, LANE), F32)])


def _head_col(blk, h):
    lane = lax.broadcasted_iota(jnp.int32, blk.shape, 1)
    return jnp.sum(jnp.where(lane == h, blk, 0.0), axis=1, keepdims=True)


def _chunk_operands(qkv_ref, gc_ref, b_ref, gr_ref):
    heads = range(HEADS)
    q, k, v = (jnp.stack([qkv_ref[:, (j * HEADS + h) * HEAD_DIM:(j * HEADS + h + 1) * HEAD_DIM] for h in heads])
               for j in range(3))
    gcs, betas = gc_ref[...], b_ref[...]
    beta = jnp.stack([_head_col(betas, h) for h in heads])
    gcol = jnp.stack([_head_col(gcs, h) for h in heads])
    grow = jnp.stack([gr_ref[0, h:h + 1, :] for h in heads])
    return q, k, v, beta, gcol, grow


def _gdn_chunk_fwd(qkv, gc, beta, grow, name):
    s = qkv.shape[0]
    nc = s // CHUNK

    def body(qkv_ref, gc_ref, b_ref, gr_ref, o_ref, st_ref, t_ref, state):
        @pl.when(pl.program_id(0) == 0)
        def _():
            state[...] = jnp.zeros_like(state)

        st = state[...]
        st_ref[0] = st
        o, new, t = _gdn_chunk(*_chunk_operands(qkv_ref, gc_ref, b_ref, gr_ref), st, _PLAIN_B)
        t_ref[0] = t
        for h in range(HEADS):
            o_ref[:, h * HEAD_DIM:(h + 1) * HEAD_DIM] = o[h]
        state[...] = new

    return pl.pallas_call(
        body, name=name, grid=(nc,),
        in_specs=[pl.BlockSpec((CHUNK, 3 * GDN_WIDTH), lambda c: (c, 0)),
                  pl.BlockSpec((CHUNK, LANE), lambda c: (c, 0)), pl.BlockSpec((CHUNK, LANE), lambda c: (c, 0)),
                  pl.BlockSpec((1, HEADS, CHUNK), lambda c: (c, 0, 0))],
        out_specs=[pl.BlockSpec((CHUNK, GDN_WIDTH), lambda c: (c, 0)),
                   pl.BlockSpec((1, HEADS, HEAD_DIM, HEAD_DIM), lambda c: (c, 0, 0, 0)),
                   pl.BlockSpec((1, HEADS, CHUNK, CHUNK), lambda c: (c, 0, 0, 0))],
        out_shape=[jax.ShapeDtypeStruct((s, GDN_WIDTH), F32),
                   jax.ShapeDtypeStruct((nc, HEADS, HEAD_DIM, HEAD_DIM), F32),
                   jax.ShapeDtypeStruct((nc, HEADS, CHUNK, CHUNK), F32)],
        scratch_shapes=[pltpu.VMEM((HEADS, HEAD_DIM, HEAD_DIM), F32)],
        compiler_params=_cparams(("arbitrary",)),
    )(qkv, gc, beta, grow)


def _gdn_chunk_bwd(qkv, gc, beta, grow, states, tinv, do, name):
    s = qkv.shape[0]
    nc = s // CHUNK

    def body(qkv_ref, gc_ref, b_ref, gr_ref, st_ref, t_ref, do_ref, dqkv_ref, dgc_ref, db_ref, dgr_ref, dstate):
        @pl.when(pl.program_id(0) == 0)
        def _():
            dstate[...] = jnp.zeros_like(dstate)

        lane = lax.broadcasted_iota(jnp.int32, (CHUNK, LANE), 1)
        chunk = lambda *args: _gdn_chunk(*args, mm=_DIFF_B, t_saved=t_ref[0])[:2]
        _, vjp = jax.vjp(chunk, *_chunk_operands(qkv_ref, gc_ref, b_ref, gr_ref), st_ref[0])
        do = jnp.stack([do_ref[:, h * HEAD_DIM:(h + 1) * HEAD_DIM] for h in range(HEADS)])
        dq, dk, dv, dbeta, dgcol, dgrow, dst = vjp((do, dstate[...]))
        dstate[...] = dst
        dgc = jnp.zeros((CHUNK, LANE), F32)
        db = jnp.zeros((CHUNK, LANE), F32)
        for h in range(HEADS):
            for j, dx in enumerate((dq, dk, dv)):
                dqkv_ref[:, (j * HEADS + h) * HEAD_DIM:(j * HEADS + h + 1) * HEAD_DIM] = dx[h]
            dgc = dgc + jnp.where(lane == h, dgcol[h], 0.0)
            db = db + jnp.where(lane == h, dbeta[h], 0.0)
            dgr_ref[0, h:h + 1, :] = dgrow[h]
        dgc_ref[...] = dgc
        db_ref[...] = db

    rev = lambda c: nc - 1 - c
    wide = pl.BlockSpec((CHUNK, 3 * GDN_WIDTH), lambda c: (rev(c), 0))
    lanes = pl.BlockSpec((CHUNK, LANE), lambda c: (rev(c), 0))
    rows = pl.BlockSpec((1, HEADS, CHUNK), lambda c: (rev(c), 0, 0))
    return pl.pallas_call(
        body, name=name, grid=(nc,),
        in_specs=[wide, lanes, lanes, rows,
                  pl.BlockSpec((1, HEADS, HEAD_DIM, HEAD_DIM), lambda c: (rev(c), 0, 0, 0)),
                  pl.BlockSpec((1, HEADS, CHUNK, CHUNK), lambda c: (rev(c), 0, 0, 0)),
                  pl.BlockSpec((CHUNK, GDN_WIDTH), lambda c: (rev(c), 0))],
        out_specs=[wide, lanes, lanes, rows],
        out_shape=[jax.ShapeDtypeStruct((s, 3 * GDN_WIDTH), F32)]
        + [jax.ShapeDtypeStruct((s, LANE), F32)] * 2 + [jax.ShapeDtypeStruct((nc, HEADS, CHUNK), F32)],
        scratch_shapes=[pltpu.VMEM((HEADS, HEAD_DIM, HEAD_DIM), F32)],
        compiler_params=_cparams(("arbitrary",)),
    )(qkv, gc, beta, grow, states, tinv, do)


def _lru_specs(s, xb_base):
    col = lambda base: pl.BlockSpec((s, LRU_BLOCK_DIM), lambda n, base=base: (0, base + n))
    vec = pl.BlockSpec((1, LRU_BLOCK_DIM), lambda n: (0, n))
    mat = pl.BlockSpec((None, LRU_BLOCK_DIM, LRU_BLOCK_DIM), lambda n: (n, 0, 0))
    cw = pl.BlockSpec((CONV_WIDTH, LRU_BLOCK_DIM), lambda n: (0, n))
    return col, vec, mat, cw


def _lru_fwd(proj, cw, cb, wa, ba, wx, bx, lam, xb_base, name):
    s = proj.shape[0]
    tt = _tile(s, 256, 8)
    col, vec, mat, cws = _lru_specs(s, xb_base)

    def body(xb_ref, yb_ref, cw_ref, cb_ref, wa_ref, ba_ref, wx_ref, bx_ref, lam_ref, o_ref, h_ref, a_ref, xc_s):
        xc_s[...] = _conv(xb_ref[...].astype(F32), cw_ref) + cb_ref[...]
        par = (wa_ref[...], ba_ref[...], wx_ref[...], bx_ref[...], lam_ref[...])

        def step(t, carry):
            sl = pl.ds(pl.multiple_of(t * tt, tt), tt)
            a, bt = _lru_gates(xc_s[sl, :], *par, _PLAIN)
            a_ref[sl, :] = a
            aa, bb = _scan_tile(a, bt, False)
            h = aa * carry + bb
            h_ref[sl, :] = h
            o_ref[sl, :] = (h * _gelu(yb_ref[sl, :].astype(F32))).astype(o_ref.dtype)
            return h_ref[pl.ds(t * tt + tt - 1, 1), :]

        lax.fori_loop(0, s // tt, step, jnp.zeros((1, LRU_BLOCK_DIM), F32))

    nb = LRU_BLOCKS
    return pl.pallas_call(
        body, name=name, grid=(LRU_BLOCKS,),
        in_specs=[col(xb_base), col(xb_base + nb), cws, vec, mat, vec, mat, vec, vec],
        out_specs=[col(0), col(0), col(0)],
        out_shape=[jax.ShapeDtypeStruct((s, LRU_WIDTH), BF), jax.ShapeDtypeStruct((s, LRU_WIDTH), F32),
                   jax.ShapeDtypeStruct((s, LRU_WIDTH), F32)],
        scratch_shapes=[pltpu.VMEM((s, LRU_BLOCK_DIM), F32)],
        compiler_params=_cparams(("parallel",)),
    )(proj, proj, cw, _row(cb), wa, _row(ba), wx, _row(bx), _row(lam))


def _lru_bwd(proj, hs, decay, dol, cw, cb, wa, ba, wx, bx, lam, xb_base, name):
    s = proj.shape[0]
    tt = _tile(s, 256, 8)
    nt = s // tt
    col, vec, mat, cws = _lru_specs(s, xb_base)

    def body(xb_ref, yb_ref, h_ref, a_ref, d_ref, cw_ref, cb_ref, wa_ref, ba_ref, wx_ref, bx_ref, lam_ref,
             dxb_ref, dyb_ref, dcw_ref, dcb_ref, dwa_ref, dba_ref, dwx_ref, dbx_ref, dlam_ref,
             xc_s, dh_s, dxc_s):
        xc_s[...] = _conv(xb_ref[...].astype(F32), cw_ref) + cb_ref[...]
        par = (wa_ref[...], ba_ref[...], wx_ref[...], bx_ref[...], lam_ref[...])
        row = lax.broadcasted_iota(jnp.int32, (tt, LRU_BLOCK_DIM), 0)
        tile = lambda t: pl.ds(pl.multiple_of(t * tt, tt), tt)

        def prep(t, carry):
            sl = tile(t)
            d = d_ref[sl, :]
            gy, vjp_y = jax.vjp(_gelu, yb_ref[sl, :].astype(F32))
            dyb_ref[sl, :] = vjp_y(d * h_ref[sl, :])[0].astype(dyb_ref.dtype)
            dh_s[sl, :] = d * gy
            return carry

        lax.fori_loop(0, nt, prep, 0)

        def rscan(i, carry):
            dh_next, a_next = carry
            t = nt - 1 - i
            sl = tile(t)
            a_sh = jnp.where(row == tt - 1, a_next, pltpu.roll(a_ref[sl, :], tt - 1, 0))
            aa, bb = _scan_tile(a_sh, dh_s[sl, :], True)
            dh_s[sl, :] = aa * dh_next + bb
            first = pl.ds(t * tt, 1)
            return dh_s[first, :], a_ref[first, :]

        zero = jnp.zeros((1, LRU_BLOCK_DIM), F32)
        lax.fori_loop(0, nt, rscan, (zero, zero))

        def gates_vjp(t, acc):
            sl = tile(t)
            _, vjp_g = jax.vjp(functools.partial(_lru_gates, mm=_DIFF), xc_s[sl, :], *par)
            dh = dh_s[sl, :]
            before = jnp.where(t > 0, h_ref[pl.ds(jnp.maximum(t * tt - 1, 0), 1), :], 0.0)
            h_prev = jnp.where(row == 0, before, pltpu.roll(h_ref[sl, :], 1, 0))
            dxc, *dpar = vjp_g((dh * h_prev, dh))
            dxc_s[sl, :] = dxc
            return tuple(x + y for x, y in zip(acc, dpar))

        dwa, dba, dwx, dbx, dlam = lax.fori_loop(0, nt, gates_vjp, tuple(jnp.zeros_like(p) for p in par))
        dwa_ref[...] = dwa
        dba_ref[...] = dba
        dwx_ref[...] = dwx
        dbx_ref[...] = dbx
        dlam_ref[...] = dlam
        dxc = dxc_s[...]
        dcb_ref[...] = jnp.sum(dxc, axis=0, keepdims=True)
        dxb, dws = _conv_bwd(xb_ref[...].astype(F32), dxc, cw_ref)
        dxb_ref[...] = dxb.astype(dxb_ref.dtype)
        for j in range(CONV_WIDTH):
            dcw_ref[j:j + 1, :] = dws[j]

    nb = LRU_BLOCKS
    w = LRU_WIDTH
    return pl.pallas_call(
        body, name=name, grid=(LRU_BLOCKS,),
        in_specs=[col(xb_base), col(xb_base + nb), col(0), col(0), col(0), cws, vec, mat, vec, mat, vec, vec],
        out_specs=[col(0), col(0), cws, vec, mat, vec, mat, vec, vec],
        out_shape=[jax.ShapeDtypeStruct((s, w), BF), jax.ShapeDtypeStruct((s, w), BF),
                   jax.ShapeDtypeStruct((CONV_WIDTH, w), F32), jax.ShapeDtypeStruct((1, w), F32),
                   jax.ShapeDtypeStruct(wa.shape, F32), jax.ShapeDtypeStruct((1, w), F32),
                   jax.ShapeDtypeStruct(wx.shape, F32), jax.ShapeDtypeStruct((1, w), F32),
                   jax.ShapeDtypeStruct((1, w), F32)],
        scratch_shapes=[pltpu.VMEM((s, LRU_BLOCK_DIM), F32)] * 3,
        compiler_params=_cparams(("parallel",)),
    )(proj, proj, hs, decay, dol, cw, _row(cb), wa, _row(ba), wx, _row(bx), _row(lam))


def _place():
    return lax.axis_index("x"), lax.axis_index("y"), lax.axis_index("c")


def _all_gather(blocks, name):
    n = len(blocks)
    per = N_DEV - 1

    def body(*refs):
        ins, outs = refs[:n], refs[n:2 * n]
        send_sems, recv_sems, local_sems = refs[2 * n:]
        x, y, c = _place()
        me, sibling = (x, y, c), (x, y, 1 - c)
        chips = [(1 - x, y), (x, 1 - y), (1 - x, 1 - y)]

        def copy(a, k, block, to, src=None):
            dst = outs[a].at[4 * block[0] + 2 * block[1] + block[2]]
            return pltpu.make_async_remote_copy(
                src_ref=dst if src is None else src, dst_ref=dst, send_sem=send_sems.at[a * per + k],
                recv_sem=recv_sems.at[a * per + k], device_id=to, device_id_type=MESH)

        started = []
        for a in range(n):
            mine = pltpu.make_async_copy(ins[a], outs[a].at[4 * x + 2 * y + c], local_sems.at[a])
            mine.start()
            started.append(mine)
        sends = []
        for a in range(n):
            first = [copy(a, 0, me, sibling, src=ins[a])]
            first += [copy(a, 1 + j, me, (*chip, c), src=ins[a]) for j, chip in enumerate(chips)]
            for cp in first:
                cp.start()
            sends += first
        for a in range(n):
            for j, chip in enumerate(chips):
                copy(a, 1 + j, (*chip, c), me).wait_recv()
                passed = copy(a, 4 + j, (*chip, c), sibling)
                passed.start()
                sends.append(passed)
        for a in range(n):
            copy(a, 0, sibling, me).wait_recv()
            for j, chip in enumerate(chips):
                copy(a, 4 + j, (*chip, 1 - c), me).wait_recv()
        for cp in sends:
            cp.wait_send()
        for cp in started:
            cp.wait()

    any_spec = pl.BlockSpec(memory_space=pl.ANY)
    return pl.pallas_call(
        body, name=name, in_specs=[any_spec] * n, out_specs=[any_spec] * n,
        out_shape=[jax.ShapeDtypeStruct((N_DEV,) + b.shape, b.dtype) for b in blocks],
        scratch_shapes=[pltpu.SemaphoreType.DMA((n * per,)), pltpu.SemaphoreType.DMA((n * per,)),
                        pltpu.SemaphoreType.DMA((n,))],
    )(*blocks)


_HBM = pl.BlockSpec(memory_space=pltpu.HBM)
_SEM = pl.BlockSpec(memory_space=pltpu.SEMAPHORE)
_ANY = pl.BlockSpec(memory_space=pl.ANY)
_EFFECT = pltpu.SideEffectType.DATAFLOW_SIDE_EFFECTING


def _peers(x, y, c):
    return [(x, y, 1 - c), (1 - x, y, c), (1 - x, y, 1 - c), (x, 1 - y, c), (x, 1 - y, 1 - c),
            (1 - x, 1 - y, c), (1 - x, 1 - y, 1 - c)]


class _InFlight:
    def __init__(self, send_sem, recv_sem, src, land):
        self.send_sem, self.recv_sem, self.src, self.land = send_sem, recv_sem, src, land


def _send_start(srcs, whole, after, name):
    n = len(srcs)
    lands = [lax.empty(((N_DEV,) + s.shape) if whole else s.shape, s.dtype) for s in srcs]

    def body(*refs):
        src_refs, land_refs = refs[:n], refs[n:2 * n]
        outs = refs[2 * n + 1:]
        send_sems, recv_sems, token = outs[:n], outs[n:2 * n], outs[-1]
        x, y, c = _place()
        me = 4 * x + 2 * y + c
        for a in range(n):
            for p in _peers(x, y, c):
                src = src_refs[a] if whole else src_refs[a].at[4 * p[0] + 2 * p[1] + p[2]]
                pltpu.make_async_remote_copy(src_ref=src, dst_ref=land_refs[a].at[me], send_sem=send_sems[a],
                                             recv_sem=recv_sems[a], device_id=p, device_id_type=MESH).start()
        token[...] = jnp.zeros_like(token)

    hbm = lambda a: pltpu.HBM(a.shape, a.dtype)
    outs = pl.pallas_call(
        body, name=name,
        out_shape=[pltpu.SemaphoreType.DMA(())] * (2 * n) + [hbm(s) for s in srcs] + [hbm(l) for l in lands]
        + [jax.ShapeDtypeStruct((8, LANE), F32)],
        in_specs=[_HBM] * (2 * n) + [_ANY],
        out_specs=[_SEM] * (2 * n) + [_HBM] * (2 * n) + [pl.BlockSpec(memory_space=pltpu.VMEM)],
        input_output_aliases={i: 2 * n + i for i in range(2 * n)},
        compiler_params=pltpu.CompilerParams(has_side_effects=_EFFECT),
    )(*[pltpu.with_memory_space_constraint(s, pltpu.HBM) for s in srcs],
      *[pltpu.with_memory_space_constraint(l, pltpu.HBM) for l in lands], after)
    flights = [_InFlight(outs[a], outs[n + a], outs[2 * n + a], outs[3 * n + a]) for a in range(n)]
    return flights, outs[-1]


def _send_wait(flights, after, name):
    n = len(flights)

    def body(*refs):
        x, y, c = _place()
        for a in range(n):
            land_ref, send_sem, recv_sem = refs[4 * a + 1], refs[4 * a + 2], refs[4 * a + 3]
            seven = land_ref.at[pl.ds(0, N_DEV - 1)]
            copy = pltpu.make_async_remote_copy(src_ref=seven, dst_ref=seven, send_sem=send_sem, recv_sem=recv_sem,
                                                device_id=(x, y, 1 - c), device_id_type=MESH)
            copy.wait_send()
            copy.wait_recv()

    ins, out_shape = [], []
    for f in flights:
        ins += [f.src, f.land, f.send_sem, f.recv_sem]
        out_shape += [pltpu.HBM(f.src.shape, f.src.dtype), pltpu.HBM(f.land.shape, f.land.dtype)]
    outs = pl.pallas_call(
        body, name=name, out_shape=out_shape,
        in_specs=[_HBM, _HBM, _SEM, _SEM] * n + [_ANY], out_specs=[_HBM] * (2 * n),
        input_output_aliases={4 * a + i: 2 * a + i for a in range(n) for i in range(2)},
        compiler_params=pltpu.CompilerParams(has_side_effects=_EFFECT),
    )(*ins, after)
    return [(outs[2 * a], outs[2 * a + 1]) for a in range(n)]


def _adamw_math(w, g, m, v):
    m = ADAM_B1 * m + (1.0 - ADAM_B1) * g
    v = ADAM_B2 * v + (1.0 - ADAM_B2) * (g * g)
    m_hat = m / (1.0 - ADAM_B1 ** ADAM_STEP)
    v_hat = v / (1.0 - ADAM_B2 ** ADAM_STEP)
    delta = -ADAM_LR * (m_hat / (jnp.sqrt(v_hat) + ADAM_EPS) + ADAM_WD * w)
    return delta, m, v


def _adamw_sharded(mine, landed, me, w, m, v, layer, prev, after, name):
    depth, r, c = w.shape
    tr = _tile(r, 128, 8)
    n_prev = 0 if prev is None else 4

    def body(*refs):
        me_ref, own_ref, land_ref, w_ref, m_ref, v_ref = refs[:6]
        g_ref, d_ref, nm_ref, nv_ref = refs[7 + n_prev:]
        g = None
        for k in range(N_DEV):
            part = jnp.where(me_ref[0] == k, own_ref[0], land_ref[k]).astype(F32)
            g = part if g is None else g + part
        delta, nm, nv = _adamw_math(w_ref[...], g, m_ref[...], v_ref[...])
        g_ref[...] = g
        d_ref[...] = delta
        nm_ref[...] = nm
        nv_ref[...] = nv

    lay = pl.BlockSpec((None, tr, c), lambda i, me_ref: (layer, i, 0))
    return pl.pallas_call(
        body, name=name,
        grid_spec=pltpu.PrefetchScalarGridSpec(
            num_scalar_prefetch=1, grid=(r // tr,),
            in_specs=[pl.BlockSpec((1, tr, c), lambda i, me_ref: (me_ref[0], i, 0)),
                      pl.BlockSpec((N_DEV, tr, c), lambda i, me_ref: (0, i, 0)), lay, lay, lay]
            + [_ANY] * (1 + n_prev),
            out_specs=[lay] * 4),
        out_shape=[jax.ShapeDtypeStruct(w.shape, F32)] * 4,
        input_output_aliases={7 + i: i for i in range(n_prev)},
        compiler_params=_cparams(("parallel",)),
    )(me, mine, landed, w, m, v, after, *(prev or ()))


def _sum_devices(stack, name):
    _, r, c = stack.shape
    tr = _tile(r, PACK_ROWS, 8)

    def body(s_ref, o_ref):
        g = s_ref[0]
        for j in range(1, N_DEV):
            g = g + s_ref[j]
        o_ref[...] = g

    return pl.pallas_call(
        body, name=name, grid=(r // tr,), in_specs=[pl.BlockSpec((N_DEV, tr, c), lambda i: (0, i, 0))],
        out_specs=pl.BlockSpec((tr, c), lambda i: (i, 0)), out_shape=jax.ShapeDtypeStruct((r, c), F32),
        compiler_params=_cparams(("parallel",)),
    )(stack)


def _adamw_packed(w, g, m, v, name):
    r, c = w.shape
    tr = _tile(r, PACK_ROWS, 8)

    def body(w_ref, g_ref, m_ref, v_ref, d_ref, nm_ref, nv_ref):
        delta, nm, nv = _adamw_math(w_ref[...], g_ref[...], m_ref[...], v_ref[...])
        d_ref[...] = delta
        nm_ref[...] = nm
        nv_ref[...] = nv

    spec = pl.BlockSpec((tr, c), lambda i: (i, 0))
    return pl.pallas_call(
        body, name=name, grid=(r // tr,), in_specs=[spec] * 4, out_specs=[spec] * 3,
        out_shape=[jax.ShapeDtypeStruct((r, c), F32)] * 3, compiler_params=_cparams(("parallel",)),
    )(w, g, m, v)


def _pack(arrays):
    unit = 8 * LANE
    flat = []
    for a in arrays:
        f = a.reshape(-1)
        flat.append(jnp.pad(f, (0, (-f.shape[0]) % unit)))
    total = sum(f.shape[0] for f in flat)
    flat.append(jnp.zeros(((-total) % (PACK_ROWS * LANE),), flat[0].dtype))
    return jnp.concatenate(flat).reshape(-1, LANE)


def _unpack(packed, like):
    unit = 8 * LANE
    flat = packed.reshape(-1)
    out, off = [], 0
    for a in like:
        n = 1
        for d in a.shape:
            n *= d
        out.append(flat[off:off + n].reshape(a.shape))
        off += n + (-n) % unit
    return out


class _Layout:
    def __init__(self, d):
        self.d = d
        self.n_front = 4 * GDN_WIDTH
        self.n_ab = 2 * HEADS
        self.n_back = 2 * LRU_WIDTH + 2 * d
        self.cols = self.n_front + self.n_ab + self.n_back
        self.pcols = self.n_front + self.n_back + LANE
        self.z_base = 3 * GDN_WIDTH // HEAD_DIM
        self.xb_base = self.n_front // LRU_BLOCK_DIM
        self.gg_off = self.n_front + 2 * LRU_WIDTH
        assert self.gg_off % d == 0
        self.gg_base = self.gg_off // d
        self.ab_base = (self.n_front + self.n_back) // LANE

    def _ranges(self):
        f, ab = self.n_front, self.n_ab
        return [(0, f), (f + ab, self.cols), (f, f + ab)]

    def permuted_from_stack(self, stack):
        blk = stack.shape[2]
        pieces = []
        for lo, hi in self._ranges():
            while lo < hi:
                k = lo // blk
                end = min(hi, (k + 1) * blk)
                pieces.append(stack[k, :, lo - k * blk:end - k * blk])
                lo = end
        pieces.append(jnp.zeros((stack.shape[1], LANE - self.n_ab), stack.dtype))
        return jnp.concatenate(pieces, axis=1)

    def stack_from_permuted(self, w):
        blk = self.cols // N_DEV
        where = {}
        pos = 0
        for lo, hi in self._ranges():
            where[lo] = pos
            pos += hi - lo
        cuts = sorted(lo for lo, _ in self._ranges()) + [self.cols]
        blocks = []
        for k in range(N_DEV):
            lo, pieces = k * blk, []
            while lo < (k + 1) * blk:
                start = max(c for c in cuts if c <= lo)
                end = min((k + 1) * blk, min(c for c in cuts if c > lo))
                pieces.append(w[:, where[start] + lo - start:where[start] + end - start])
                lo = end
            blocks.append(pieces[0] if len(pieces) == 1 else jnp.concatenate(pieces, axis=1))
        return jnp.stack(blocks)


def _cols_from_stack(stack):
    nd, k, n = stack.shape
    return stack.transpose(1, 0, 2).reshape(k, nd * n)


def _stack_from_cols(w):
    k, n = w.shape
    return w.reshape(k, N_DEV, n // N_DEV).transpose(1, 0, 2)


def _after(v, *tokens):
    for t in tokens:
        v = v + t[0, 0]
    return v


def _layer_fwd(x, p, late, lay, tag):
    s, d = x.shape
    nc = s // CHUNK
    h = _rms_fwd(x, p["attn_norm"], f"rms1_fwd{tag}")
    proj, ab = _matmul(h, p["w_in"], "nn", f"in_proj{tag}", [BF], tn=1152, tail=True)
    qkv = _conv_silu_fwd(proj, p["gdn_conv_w"], f"gdn_conv_fwd{tag}")
    gc, beta = _gdn_gates_fwd(ab, p["gdn_a_log"], p["gdn_dt_bias"], 0, f"gdn_gates_fwd{tag}")
    grow = gc[:, :HEADS].reshape(nc, CHUNK, HEADS).transpose(0, 2, 1)
    o, states, tinv = _gdn_chunk_fwd(qkv, gc, beta, grow, f"gdn_chunk_fwd{tag}")
    og = _gdn_post_fwd(o, proj, p["gdn_norm"], lay.z_base, f"gdn_post_fwd{tag}")
    ol, hs, decay = _lru_fwd(proj, p["lru_conv_w"], p["lru_conv_b"], p["lru_w_a"], p["lru_b_a"], p["lru_w_x"], p["lru_b_x"],
                      p["lru_lambda"], lay.xb_base, f"lru_fwd{tag}")
    late[0](hs)
    yg = _matmul(og, p["w_branch_gdn"], "nn", f"branch_gdn{tag}", [BF])
    yl = _matmul(ol, p["w_branch_lru"], "nn", f"branch_lru{tag}", [BF])
    merged = _merge_fwd(proj, yg, yl, lay.gg_base, f"merge_fwd{tag}")
    x1 = _matmul(merged, p["w_out"], "nn", f"out_proj{tag}", [F32], epi=lambda acc, r: (acc + r,), extras=[x])
    h2 = _rms_fwd(x1, p["mlp_norm"], f"rms2_fwd{tag}")
    late[1](h2)
    u, act = _matmul(h2, p["w_up"], "nn", f"mlp_up{tag}", [BF, BF],
                     epi=lambda acc: (acc, jnp.square(jnp.maximum(acc, 0.0))), shards=True)
    x2 = _matmul(act, p["w_down"], "nn", f"mlp_down{tag}", [F32], epi=lambda acc, r: (acc + r,), extras=[x1])
    saved = dict(x=x, h=h, proj=proj, ab=ab, qkv=qkv, gc=gc, beta=beta, grow=grow, o=o, states=states, tinv=tinv, og=og, ol=ol, hs=hs, decay=decay,
                 yg=yg, yl=yl, merged=merged, x1=x1, h2=h2, u=u, act=act)
    return x2, saved


def _layer_bwd(dx2, p, sv, send, send_small, after_last_send, after, lay, tag):
    s, d = dx2.shape
    nc = s // CHUNK
    small = {}
    du = _matmul(dx2, p["w_down"], "nt", f"d_act{tag}", [BF],
                 epi=lambda acc, u: (acc * (2.0 * jnp.maximum(u.astype(F32), 0.0)),), extras=[sv["u"]], after=after)
    sent = send({"w_down": _matmul(sv["act"], dx2, "tn", f"dw_down{tag}", [BF]).reshape(N_DEV, -1, d),
                 "w_up": _matmul(sv["h2"], du, "tn", f"dw_up{tag}", [BF], shards=True)})
    dh2 = _matmul(du, p["w_up"], "nt", f"d_h2{tag}", [F32], shards=True)
    dx1, g = _rms_bwd(sv["x1"], _after(p["mlp_norm"], sent), dh2, dx2, f"rms2_bwd{tag}")
    small["mlp_norm"] = g.reshape(-1)
    dmerged = _matmul(dx1, p["w_out"], "nt", f"d_merged{tag}", [F32])
    dwout = _matmul(sv["merged"], dx1, "tn", f"dw_out{tag}", [BF]).reshape(N_DEV, -1, d)
    dgg, dgl, dyg, dyl = _merge_bwd(sv["proj"], sv["yg"], sv["yl"], dmerged, lay.gg_base, f"merge_bwd{tag}")
    sent = send({"w_out": dwout,
                 "w_branch_gdn": _stack_from_cols(_matmul(sv["og"], dyg, "tn", f"dw_bg{tag}", [BF])),
                 "w_branch_lru": _stack_from_cols(_matmul(sv["ol"], dyl, "tn", f"dw_bl{tag}", [BF]))})
    dog = _matmul(dyg, p["w_branch_gdn"], "nt", f"d_og{tag}", [F32])
    dol = _matmul(dyl, p["w_branch_lru"], "nt", f"d_ol{tag}", [F32])
    (dxb, dyb, dcw, dcb, dwa, dba, dwx, dbx, dlam) = _lru_bwd(
        sv["proj"], sv["hs"], sv["decay"], dol, p["lru_conv_w"], p["lru_conv_b"], p["lru_w_a"], p["lru_b_a"], p["lru_w_x"],
        p["lru_b_x"], _after(p["lru_lambda"], sent), lay.xb_base, f"lru_bwd{tag}")
    small.update(lru_conv_w=dcw, lru_conv_b=dcb.reshape(-1), lru_w_a=dwa, lru_b_a=dba.reshape(-1), lru_w_x=dwx,
                 lru_b_x=dbx.reshape(-1), lru_lambda=dlam.reshape(-1))
    do, dz, g = _gdn_post_bwd(sv["o"], sv["proj"], p["gdn_norm"], dog, lay.z_base, f"gdn_post_bwd{tag}")
    small["gdn_norm"] = g.reshape(-1)
    dqkv_c, dgc_col, dbeta, dgrow = _gdn_chunk_bwd(sv["qkv"], sv["gc"], sv["beta"], sv["grow"], sv["states"], sv["tinv"], do,
                                                   f"gdn_chunk_bwd{tag}")
    dqkv, dgcw = _conv_silu_bwd(sv["proj"], p["gdn_conv_w"], dqkv_c, f"gdn_conv_bwd{tag}")
    small["gdn_conv_w"] = dgcw
    dgc_row = jnp.pad(dgrow.transpose(0, 2, 1).reshape(s, HEADS), ((0, 0), (0, LANE - HEADS)))
    dab, dal, ddt = _gdn_gates_bwd(sv["ab"], p["gdn_a_log"], p["gdn_dt_bias"], dgc_col, dgc_row, dbeta, 0,
                                   f"gdn_gates_bwd{tag}")
    small["gdn_a_log"] = dal[0, :HEADS]
    small["gdn_dt_bias"] = ddt[0, :HEADS]
    dproj = jnp.concatenate([dqkv, dz, dxb, dyb, dgg, dgl, dab], axis=1)
    sent = send_small(small, "main")
    dwin = _matmul(sv["h"], dproj, "tn", f"dw_in{tag}", [BF], tn=1152, after=sent)
    sent = send({"w_in": lay.stack_from_permuted(dwin)})
    dh = _matmul(dproj, p["w_in"], "nt", f"d_h{tag}", [F32], tk=3456, after=sent)
    dx, g = _rms_bwd(sv["x"], p["attn_norm"], dh, dx1, f"rms1_bwd{tag}")
    after_last_send(send_small({"attn_norm": g.reshape(-1)}, "attn"))
    return dx


_BIG = ("w_in", "w_branch_gdn", "w_branch_lru", "w_out", "w_up", "w_down")
_ROW_SHARDED = ("w_out", "w_down")
_CONV = ("gdn_conv_w", "lru_conv_w")
_SMALL = ("attn_norm", "gdn_a_log", "gdn_dt_bias", "gdn_norm", "lru_conv_b", "lru_w_a", "lru_b_a", "lru_w_x",
          "lru_b_x", "lru_lambda", "mlp_norm")
_ORDER = ("attn_norm", "w_in", "gdn_conv_w", "gdn_a_log", "gdn_dt_bias", "gdn_norm", "lru_conv_w", "lru_conv_b",
          "lru_w_a", "lru_b_a", "lru_w_x", "lru_b_x", "lru_lambda", "w_branch_gdn", "w_branch_lru", "w_out",
          "mlp_norm", "w_up", "w_down", "final_norm")


def kernel(x, attn_norm, w_in, gdn_conv_w, gdn_a_log, gdn_dt_bias, gdn_norm, lru_conv_w, lru_conv_b, lru_w_a, lru_b_a, lru_w_x, lru_b_x, lru_lambda, w_branch_gdn, w_branch_lru, w_out, mlp_norm, w_up, w_down, final_norm, loss_target, m_attn_norm, m_w_in, m_gdn_conv_w, m_gdn_a_log, m_gdn_dt_bias, m_gdn_norm, m_lru_conv_w, m_lru_conv_b, m_lru_w_a, m_lru_b_a, m_lru_w_x, m_lru_b_x, m_lru_lambda, m_w_branch_gdn, m_w_branch_lru, m_w_out, m_mlp_norm, m_w_up, m_w_down, m_final_norm, v_attn_norm, v_w_in, v_gdn_conv_w, v_gdn_a_log, v_gdn_dt_bias, v_gdn_norm, v_lru_conv_w, v_lru_conv_b, v_lru_w_a, v_lru_b_a, v_lru_w_x, v_lru_b_x, v_lru_lambda, v_w_branch_gdn, v_w_branch_lru, v_w_out, v_mlp_norm, v_w_up, v_w_down, v_final_norm):
    w = dict(attn_norm=attn_norm, w_in=w_in, gdn_conv_w=gdn_conv_w, gdn_a_log=gdn_a_log, gdn_dt_bias=gdn_dt_bias, gdn_norm=gdn_norm, lru_conv_w=lru_conv_w, lru_conv_b=lru_conv_b, lru_w_a=lru_w_a, lru_b_a=lru_b_a, lru_w_x=lru_w_x, lru_b_x=lru_b_x, lru_lambda=lru_lambda, w_branch_gdn=w_branch_gdn, w_branch_lru=w_branch_lru, w_out=w_out, mlp_norm=mlp_norm, w_up=w_up, w_down=w_down, final_norm=final_norm)
    m = dict(attn_norm=m_attn_norm, w_in=m_w_in, gdn_conv_w=m_gdn_conv_w, gdn_a_log=m_gdn_a_log, gdn_dt_bias=m_gdn_dt_bias, gdn_norm=m_gdn_norm, lru_conv_w=m_lru_conv_w, lru_conv_b=m_lru_conv_b, lru_w_a=m_lru_w_a, lru_b_a=m_lru_b_a, lru_w_x=m_lru_w_x, lru_b_x=m_lru_b_x, lru_lambda=m_lru_lambda, w_branch_gdn=m_w_branch_gdn, w_branch_lru=m_w_branch_lru, w_out=m_w_out, mlp_norm=m_mlp_norm, w_up=m_w_up, w_down=m_w_down, final_norm=m_final_norm)
    v = dict(attn_norm=v_attn_norm, w_in=v_w_in, gdn_conv_w=v_gdn_conv_w, gdn_a_log=v_gdn_a_log, gdn_dt_bias=v_gdn_dt_bias, gdn_norm=v_gdn_norm, lru_conv_w=v_lru_conv_w, lru_conv_b=v_lru_conv_b, lru_w_a=v_lru_w_a, lru_b_a=v_lru_b_a, lru_w_x=v_lru_w_x, lru_b_x=v_lru_b_x, lru_lambda=v_lru_lambda, w_branch_gdn=v_w_branch_gdn, w_branch_lru=v_w_branch_lru, w_out=v_w_out, mlp_norm=v_mlp_norm, w_up=v_w_up, w_down=v_w_down, final_norm=v_final_norm)
    depth = w_in.shape[0]
    x = x[0]
    target = loss_target[0]
    s, d = x.shape
    lay = _Layout(d)
    assert lay.cols == w_in.shape[2] * N_DEV
    cx, cy, cc = _place()
    me = 4 * cx + 2 * cy + cc

    me_arr = me.astype(jnp.int32).reshape(1)

    early, mixer, mlp = ("w_in",) + _CONV, ("w_branch_gdn", "w_branch_lru", "w_out"), ("w_up", "w_down")

    def shard(n, l):
        return w[n][l].astype(BF) if n in _BIG else w[n][l]

    def joined(name, stack):
        if name == "w_in":
            return lay.permuted_from_stack(stack)
        if name == "w_up":
            return stack
        return stack.reshape(-1, stack.shape[-1]) if name in _ROW_SHARDED else _cols_from_stack(stack)

    def start_weights(l, names, after):
        flights, token = _send_start([shard(n, l) for n in names], True, after, f"weights_start_l{l}")
        return dict(zip(names, flights)), token

    def wait_weights(l, flights, names, p, after, which):
        got = _send_wait([flights[n] for n in names], after, f"weights_wait_{which}_l{l}")
        for n, (own, landed) in zip(names, got):
            p[n] = joined(n, lax.dynamic_update_slice_in_dim(landed, own[None], me, axis=0))

    saved, params = [], []
    xl = x
    first = dict(zip(early + mixer, _all_gather([shard(n, 0) for n in early + mixer], "gather_first_weights")))
    flights, token = start_weights(0, mlp, first["w_in"])
    for l in range(depth):
        p = {n: w[n][l] for n in _SMALL}
        if l == 0:
            p.update({n: joined(n, g) for n, g in first.items()})
        else:
            wait_weights(l, flights, early, p, xl, "early")
        mine = flights
        if l + 1 < depth:
            flights, token = start_weights(l + 1, early + mixer + mlp, token)
        p["attn_norm"] = _after(p["attn_norm"], token)
        late = [functools.partial(wait_weights, l, mine, names, p, which=which) if names[0] in mine
                else (lambda after: None) for names, which in ((mixer, "mixer"), (mlp, "mlp"))]
        xl, sv = _layer_fwd(xl, p, late, lay, f"_l{l}")
        saved.append(sv)
        params.append(p)
    dx, loss_row, dfinal = _loss_head(xl, final_norm, target, "loss_head")
    loss = lax.psum(loss_row[0, 0], ("x", "y", "c"))

    out = {n: None for n in _BIG}
    pending = {l: {} for l in range(depth)}

    def sender(l):
        def send(stacks):
            names = tuple(stacks)
            fl, token = _send_start([stacks[n] for n in names], False, me_arr, f"grads_start_{'_'.join(names)}_l{l}")
            pending[l].update(zip(names, fl))
            return token
        return send

    def finish(l, names, after, which):
        got = _send_wait([pending[l][n] for n in names], after, f"grads_wait_{which}_l{l}")
        for n, (mine, landed) in zip(names, got):
            out[n] = _adamw_sharded(mine, landed, me_arr, w[n], m[n], v[n], l, out[n], after, f"adamw_{n}_l{l}")
            after = out[n][0]
        return after

    small_names = _SMALL + _CONV
    small_sent = []

    def small_sender(l):
        def send_small(small, which):
            small = dict(small)
            if which == "main" and l == depth - 1:
                small["final_norm"] = dfinal.reshape(-1)
            names = tuple(small)
            flights, token = _send_start([_pack([small[n] for n in names])], True, me_arr,
                                         f"small_grads_start_{which}_l{l}")
            small_sent.append((l, which, names, [small[n] for n in names], flights[0]))
            return token
        return send_small

    last = [None]

    def hook_for(l):
        def hook(token):
            last[0] = finish(l + 1, _BIG, token, "all") if l + 1 < depth else token
        return hook

    for l in reversed(range(depth)):
        dx = _layer_bwd(dx, params[l], saved[l], sender(l), small_sender(l), hook_for(l), last[0], lay, f"_l{l}")

    last[0] = finish(0, mixer + mlp, last[0], "late")
    got = _send_wait([f for *_, f in small_sent], last[0], "small_grads_wait")
    summed = [{} for _ in range(depth)]
    for (l, which, names, arrays, _), (part, landed) in zip(small_sent, got):
        total = _sum_devices(lax.dynamic_update_slice_in_dim(landed, part[None], me, axis=0),
                             f"sum_small_grads_{which}_l{l}")
        summed[l].update(zip(names, _unpack(total, arrays)))
    grads = {n: jnp.stack([summed[l][n] for l in range(depth)]) for n in small_names}
    grads["final_norm"] = summed[depth - 1]["final_norm"]
    for n in _CONV:
        blk = w[n].shape[2]
        grads[n] = lax.dynamic_slice_in_dim(grads[n], me * blk, blk, axis=2)
    names = small_names + ("final_norm",)
    gp = _pack([grads[n] for n in names])
    upd = _adamw_packed(_pack([w[n] for n in names]), gp, _pack([m[n] for n in names]), _pack([v[n] for n in names]),
                        "adamw_small")
    like = [w[n] for n in names]
    unp = [_unpack(u, like) for u in upd]
    for i, n in enumerate(names):
        out[n] = (grads[n], unp[0][i], unp[1][i], unp[2][i])
    finish(0, ("w_in",), upd[0], "w_in")

    res = [loss, dx.reshape(1, s, d)]
    for k in range(4):
        res += [out[n][k] for n in _ORDER]
    return tuple(res)
```

```python
import functools

import jax
import jax.numpy as jnp
from jax import lax
from jax.experimental import pallas as pl
from jax.experimental.pallas import tpu as pltpu

F32 = jnp.float32
BF = jnp.bfloat16
MESH = pl.DeviceIdType.MESH
N_DEV = 8
N_CHIP = 4

HEADS = 8
HEAD_DIM = 128
GDN_WIDTH = HEADS * HEAD_DIM
CHUNK = 64
CONV_WIDTH = 4
LRU_WIDTH = 1024
LRU_BLOCKS = 8
LRU_BLOCK_DIM = 128
LRU_C = 8.0
RMS_EPS = 1e-6
L2_EPS = 1e-6
ADAM_LR = 0.001
ADAM_B1 = 0.9
ADAM_B2 = 0.999
ADAM_EPS = 1e-08
ADAM_WD = 0.01
ADAM_STEP = 10

LANE = 128
PACK_ROWS = 512
VMEM_LIMIT = 56 * 1024 * 1024


def _cparams(sem=None):
    return pltpu.CompilerParams(dimension_semantics=sem, vmem_limit_bytes=VMEM_LIMIT)


def _tile(n, pref, unit=LANE):
    if n <= pref:
        return n
    best = None
    for t in range(unit, pref + 1, unit):
        if n % t == 0:
            best = t
    assert best is not None, (n, pref, unit)
    return best


_NN = (((1,), (0,)), ((), ()))
_NT = (((1,), (1,)), ((), ()))
_TN = (((0,), (0,)), ((), ()))


def _dg(a, b, dims):
    return lax.dot_general(a.astype(BF), b.astype(BF), dims, preferred_element_type=F32)


class _MM:
    def __init__(self, nn, nt, tn):
        self.nn, self.nt, self.tn = nn, nt, tn


def _make_mm(nn, nt, tn):
    def p_nn(a, b):
        return _dg(a, b, nn)

    def p_nt(a, b):
        return _dg(a, b, nt)

    def p_tn(a, b):
        return _dg(a, b, tn)

    d_nn, d_nt, d_tn = jax.custom_vjp(p_nn), jax.custom_vjp(p_nt), jax.custom_vjp(p_tn)

    def save(f):
        return lambda a, b: (f(a, b), (a, b))

    d_nn.defvjp(save(d_nn), lambda r, g: (d_nt(g, r[1]), d_tn(r[0], g)))
    d_nt.defvjp(save(d_nt), lambda r, g: (d_nn(g, r[1]), d_tn(g, r[0])))
    d_tn.defvjp(save(d_tn), lambda r, g: (d_nt(r[1], g), d_nn(r[0], g)))
    return _MM(p_nn, p_nt, p_tn), _MM(d_nn, d_nt, d_tn)


_BNN = (((2,), (1,)), ((0,), (0,)))
_BNT = (((2,), (2,)), ((0,), (0,)))
_BTN = (((1,), (1,)), ((0,), (0,)))

_PLAIN, _DIFF = _make_mm(_NN, _NT, _TN)
_PLAIN_B, _DIFF_B = _make_mm(_BNN, _BNT, _BTN)


def _sigmoid(x):
    return 1.0 / (1.0 + jnp.exp(-x))


def _silu(x):
    return x * _sigmoid(x)


def _softplus(x):
    return jnp.maximum(x, 0.0) + jnp.log(1.0 + jnp.exp(-jnp.maximum(x, -x)))


def _gelu(x):
    return 0.5 * x * (1.0 + jnp.tanh(0.7978845608028654 * (x + 0.044715 * (x * x * x))))


def _expm1(x):
    t = jnp.tanh(0.5 * x)
    return 2.0 * t / (1.0 - t)


def _rms(x, gain):
    ms = jnp.mean(x * x, axis=-1, keepdims=True)
    return x * lax.rsqrt(ms + RMS_EPS) * gain


def _l2n(x):
    return x * lax.rsqrt(jnp.sum(x * x, axis=-1, keepdims=True) + L2_EPS)


def _shift_down(x, r):
    if r == 0:
        return x
    row = lax.broadcasted_iota(jnp.int32, x.shape, 0)
    return jnp.where(row >= r, pltpu.roll(x, r, 0), 0.0)


def _shift_up(x, r):
    if r == 0:
        return x
    n = x.shape[0]
    row = lax.broadcasted_iota(jnp.int32, x.shape, 0)
    return jnp.where(row < n - r, pltpu.roll(x, n - r, 0), 0.0)


def _conv(x, w_ref):
    y = None
    for j in range(CONV_WIDTH):
        t = _shift_down(x, CONV_WIDTH - 1 - j) * w_ref[j:j + 1, :]
        y = t if y is None else y + t
    return y


def _conv_bwd(x, dy, w_ref):
    dx = None
    dws = []
    for j in range(CONV_WIDTH):
        r = CONV_WIDTH - 1 - j
        t = _shift_up(dy, r) * w_ref[j:j + 1, :]
        dx = t if dx is None else dx + t
        dws.append(jnp.sum(dy * _shift_down(x, r), axis=0, keepdims=True))
    return dx, dws


def _scan_tile(a, b, reverse):
    n = a.shape[0]
    row = lax.broadcasted_iota(jnp.int32, a.shape, 0)
    s = 1
    while s < n:
        if reverse:
            keep = row < n - s
            a_sh = jnp.where(keep, pltpu.roll(a, n - s, 0), 1.0)
            b_sh = jnp.where(keep, pltpu.roll(b, n - s, 0), 0.0)
        else:
            keep = row >= s
            a_sh = jnp.where(keep, pltpu.roll(a, s, 0), 1.0)
            b_sh = jnp.where(keep, pltpu.roll(b, s, 0), 0.0)
        b = a * b_sh + b
        a = a * a_sh
        s *= 2
    return a, b


def _dg3(a, b, dims):
    a_hi = a.astype(BF)
    b_hi = b.astype(BF)
    a_lo = a - a_hi.astype(F32)
    b_lo = b - b_hi.astype(F32)
    return _dg(a_hi, b_hi, dims) + (_dg(a_hi, b_lo, dims) + _dg(a_lo, b_hi, dims))


@jax.custom_vjp
def _tri_inv(a):
    n = a.shape[1]
    p = _dg3(a, a, _BNN)
    r = p
    e = 2
    while 2 * e < n:
        p = _dg3(p, p, _BNN)
        r = r + p + _dg3(r, p, _BNN)
        e *= 2
    row = lax.broadcasted_iota(jnp.int32, (1, n, n), 1)
    col = lax.broadcasted_iota(jnp.int32, (1, n, n), 2)
    eye = jnp.where(row == col, 1.0, 0.0)
    return eye - a + r - _dg3(a, r, _BNN)


def _tri_inv_fwd(a):
    t = _tri_inv(a)
    return t, t


def _tri_inv_bwd(t, g):
    return (-_dg3(_dg3(t, g, _BTN), t, _BNT),)


_tri_inv.defvjp(_tri_inv_fwd, _tri_inv_bwd)


@jax.custom_vjp
def _tri_inv_saved(a, t):
    return t


_tri_inv_saved.defvjp(lambda a, t: (t, t), lambda t, g: (_tri_inv_bwd(t, g)[0], jnp.zeros_like(t)))


def _gdn_chunk(q, k, v, beta, gcol, grow, state, mm, t_saved=None):
    c = q.shape[1]
    row = lax.broadcasted_iota(jnp.int32, (1, c, c), 1)
    col = lax.broadcasted_iota(jnp.int32, (1, c, c), 2)
    causal = row >= col
    strict = row > col
    qn = _l2n(q) * (HEAD_DIM ** -0.5)
    kn = _l2n(k)
    dec = jnp.where(causal, jnp.exp(jnp.where(causal, gcol - grow, 0.0)), 0.0)
    kb = kn * beta
    vb = v * beta
    a = jnp.where(strict, mm.nt(kb, kn) * dec, 0.0)
    t = _tri_inv(a) if t_saved is None else _tri_inv_saved(a, t_saved)
    eg = jnp.exp(gcol)
    u = mm.nn(t, vb)
    w = mm.nn(t, kb * eg)
    p = jnp.where(causal, mm.nt(qn, kn) * dec, 0.0)
    vn = u - mm.nn(w, state)
    o = mm.nn(qn * eg, state) + mm.nn(p, vn)
    last = lax.broadcasted_iota(jnp.int32, (1, c, 1), 1) == c - 1
    gl = jnp.sum(jnp.where(last, gcol, 0.0), axis=1, keepdims=True)
    kd = kn * jnp.exp(gl - gcol)
    new_state = state * jnp.exp(gl) + mm.tn(kd, vn)
    return o, new_state, t


def _lru_gates(xc, wa, ba, wx, bx, lam, mm):
    r = _sigmoid(mm.nn(xc, wa) + ba)
    i = _sigmoid(mm.nn(xc, wx) + bx)
    log_a = -LRU_C * r * _softplus(-lam)
    a = jnp.exp(log_a)
    bterm = jnp.sqrt(-_expm1(2.0 * log_a)) * (i * xc)
    return a, bterm


def _matmul(a, b, mode, name, out_dtypes, epi=None, extras=(), tm=1024, tn=1024, tk=2048, after=None,
            shards=False, tail=False):
    if shards:
        nd, rows, per = b.shape if mode != "tn" else (N_DEV, a.shape[1], b.shape[1] // N_DEV)
        b_shape = b.shape if mode == "tn" else (rows, nd * per)
    else:
        b_shape = b.shape
    if mode == "nn":
        (m, k), (k2, n) = a.shape, b_shape
    elif mode == "nt":
        (m, k), (n, k2) = a.shape, b_shape
    else:
        (k, m), (k2, n) = a.shape, b_shape
    assert k == k2, (a.shape, b.shape, mode)
    if shards:
        tn, tk = (min(tn, per), tk) if mode != "nt" else (tn, min(tk, per))
    tm, tn, tk = _tile(m, tm), _tile(n, tn), _tile(k, tk)
    nk = k // tk
    dims = {"nn": _NN, "nt": _NT, "tn": _TN}[mode]
    if mode == "nn":
        a_spec = pl.BlockSpec((tm, tk), lambda i, j, kk: (i, kk))
        b_spec = pl.BlockSpec((tk, tn), lambda i, j, kk: (kk, j))
        if shards:
            assert per % tn == 0
            b_spec = pl.BlockSpec((None, tk, tn), lambda i, j, kk: (j * tn // per, kk, j * tn % per // tn))
    elif mode == "nt":
        a_spec = pl.BlockSpec((tm, tk), lambda i, j, kk: (i, kk))
        b_spec = pl.BlockSpec((tn, tk), lambda i, j, kk: (j, kk))
        if shards:
            assert per % tk == 0
            b_spec = pl.BlockSpec((None, tn, tk), lambda i, j, kk: (kk * tk // per, j, kk * tk % per // tk))
    else:
        a_spec = pl.BlockSpec((tk, tm), lambda i, j, kk: (kk, i))
        b_spec = pl.BlockSpec((tk, tn), lambda i, j, kk: (kk, j))
    o_spec = pl.BlockSpec((tm, tn), lambda i, j, kk: (i, j))
    out_spec, out_dims = o_spec, (m, n)
    if shards and mode == "tn":
        assert per % tn == 0 and not extras
        out_spec = pl.BlockSpec((None, tm, tn), lambda i, j, kk: (j * tn // per, i, j * tn % per // tn))
        out_dims = (N_DEV, m, per)
    n_ex, n_out = len(extras), len(out_dtypes)
    order = [] if after is None else [after]
    n_in = 2 + n_ex + len(order)
    if epi is None:
        epi = lambda acc: (acc,)

    def body(*refs):
        a_ref, b_ref = refs[0], refs[1]
        ex_refs = refs[2:2 + n_ex]
        out_refs = refs[n_in:n_in + n_out]
        kk = pl.program_id(2)
        part = _dg(a_ref[...], b_ref[...], dims)

        def finish(total):
            res = epi(total, *[r[...] for r in ex_refs])
            for r, v in zip(out_refs, res):
                r[...] = v.astype(r.dtype)
            if tail:
                @pl.when(pl.program_id(1) == n // tn - 1)
                def _():
                    refs[n_in + n_out][...] = total[:, tn - LANE:]

        if nk == 1:
            finish(part)
            return
        acc_ref = refs[-1]

        @pl.when(kk == 0)
        def _():
            acc_ref[...] = part

        @pl.when(jnp.logical_and(kk > 0, kk < nk - 1))
        def _():
            acc_ref[...] += part

        @pl.when(kk == nk - 1)
        def _():
            finish(acc_ref[...] + part)

    outs = pl.pallas_call(
        body, name=name, grid=(m // tm, n // tn, nk),
        in_specs=[a_spec, b_spec] + [o_spec] * n_ex + [pl.BlockSpec(memory_space=pl.ANY)] * len(order),
        out_specs=[out_spec] * n_out + [pl.BlockSpec((tm, LANE), lambda i, j, kk: (i, 0))] * tail,
        out_shape=[jax.ShapeDtypeStruct(out_dims, d) for d in out_dtypes]
        + [jax.ShapeDtypeStruct((m, LANE), F32)] * tail,
        scratch_shapes=[pltpu.VMEM((tm, tn), F32)] if nk > 1 else [],
        compiler_params=_cparams(("parallel", "arbitrary" if tail else "parallel", "arbitrary")),
    )(a, b, *extras, *order)
    return outs[0] if n_out + tail == 1 else outs


def _rowwise(fn, name, rows, tr, row_ins, params, row_outs, acc_outs=(), ncol=1):
    tr = _tile(rows, tr, 8)
    n_in, n_par, n_ro, n_acc = len(row_ins), len(params), len(row_outs), len(acc_outs)
    in_specs = []
    for _, width, base, per_col in row_ins:
        if per_col:
            in_specs.append(pl.BlockSpec((tr, width), lambda i, j, base=base: (i, base + j)))
        else:
            in_specs.append(pl.BlockSpec((tr, width), lambda i, j, base=base: (i, base)))
    for p in params:
        in_specs.append(pl.BlockSpec(p.shape, lambda i, j: (0, 0)))
    out_specs, out_shape = [], []
    for total, dtype, width, per_col in row_outs:
        if per_col:
            out_specs.append(pl.BlockSpec((tr, width), lambda i, j: (i, j)))
        else:
            out_specs.append(pl.BlockSpec((tr, width), lambda i, j: (i, 0)))
        out_shape.append(jax.ShapeDtypeStruct((rows, total), dtype))
    for shape, dtype in acc_outs:
        out_specs.append(pl.BlockSpec(shape, lambda i, j: (0, 0)))
        out_shape.append(jax.ShapeDtypeStruct(shape, dtype))

    def body(*refs):
        ins = [r[...].astype(F32) for r in refs[:n_in + n_par]]
        ro = refs[n_in + n_par:n_in + n_par + n_ro]
        ao = refs[n_in + n_par + n_ro:]
        res = fn(*ins)
        for r, v in zip(ro, res[:n_ro]):
            r[...] = v.astype(r.dtype)
        first = jnp.logical_and(pl.program_id(0) == 0, pl.program_id(1) == 0)

        @pl.when(first)
        def _():
            for r in ao:
                r[...] = jnp.zeros_like(r)

        for r, v in zip(ao, res[n_ro:]):
            r[...] += v.astype(r.dtype)

    sem = ("arbitrary", "arbitrary") if n_acc else ("parallel", "parallel")
    outs = pl.pallas_call(
        body, name=name, grid=(rows // tr, ncol), in_specs=in_specs, out_specs=out_specs,
        out_shape=out_shape, compiler_params=_cparams(sem),
    )(*[r[0] for r in row_ins], *params)
    return outs


def _row(x):
    return x.reshape(1, -1)


def _rms_fwd(x, gain, name):
    s, d = x.shape
    return _rowwise(lambda xt, g: (_rms(xt, g),), name, s, 512, [(x, d, 0, False)], [_row(gain)],
                    [(d, BF, d, False)])[0]


def _rms_bwd(x, gain, dh, dres, name):
    s, d = x.shape

    def fn(xt, dht, drt, g):
        _, vjp = jax.vjp(_rms, xt, g)
        dx, dg = vjp(dht)
        return drt + dx, dg

    return _rowwise(fn, name, s, 256, [(x, d, 0, False), (dh, d, 0, False), (dres, d, 0, False)], [_row(gain)],
                    [(d, F32, d, False)], [((1, d), F32)])


def _loss_head(x, gain, target, name):
    s, d = x.shape

    def fn(xt, tt, g):
        y, vjp = jax.vjp(_rms, xt, g)
        err = y - tt
        dx, dg = vjp(err * (1.0 / d))
        part = 0.5 * jnp.sum(jnp.sum(err * err, axis=1, keepdims=True), axis=0, keepdims=True) * (1.0 / d)
        return dx, jnp.broadcast_to(part, (1, LANE)), dg

    return _rowwise(fn, name, s, 256, [(x, d, 0, False), (target, d, 0, False)], [_row(gain)],
                    [(d, F32, d, False)], [((1, LANE), F32), ((1, d), F32)])


def _gdn_post_fn(o, z, g):
    return _rms(o, g) * _silu(z)


def _gdn_post_fwd(o, proj, gain, z_base, name):
    s = o.shape[0]
    return _rowwise(lambda ot, zt, g: (_gdn_post_fn(ot, zt, g),), name, s, 1024,
                    [(o, HEAD_DIM, 0, True), (proj, HEAD_DIM, z_base, True)], [_row(gain)],
                    [(GDN_WIDTH, BF, HEAD_DIM, True)], ncol=HEADS)[0]


def _gdn_post_bwd(o, proj, gain, dog, z_base, name):
    s = o.shape[0]

    def fn(ot, zt, dt, g):
        _, vjp = jax.vjp(_gdn_post_fn, ot, zt, g)
        return vjp(dt)

    return _rowwise(fn, name, s, 1024,
                    [(o, HEAD_DIM, 0, True), (proj, HEAD_DIM, z_base, True), (dog, HEAD_DIM, 0, True)],
                    [_row(gain)], [(GDN_WIDTH, F32, HEAD_DIM, True), (GDN_WIDTH, BF, HEAD_DIM, True)],
                    [((1, HEAD_DIM), F32)], ncol=HEADS)


def _merge_fn(gg, gl, yg, yl):
    return _sigmoid(gg) * yg + _sigmoid(gl) * yl


def _merge_fwd(proj, yg, yl, gg_base, name):
    s, d = yg.shape
    return _rowwise(lambda a, b, c, e: (_merge_fn(a, b, c, e),), name, s, 256,
                    [(proj, d, gg_base, False), (proj, d, gg_base + 1, False), (yg, d, 0, False), (yl, d, 0, False)],
                    [], [(d, BF, d, False)])[0]


def _merge_bwd(proj, yg, yl, dmerged, gg_base, name):
    s, d = yg.shape

    def fn(a, b, c, e, dm):
        _, vjp = jax.vjp(_merge_fn, a, b, c, e)
        return vjp(dm)

    return _rowwise(fn, name, s, 128,
                    [(proj, d, gg_base, False), (proj, d, gg_base + 1, False), (yg, d, 0, False), (yl, d, 0, False),
                     (dmerged, d, 0, False)], [], [(d, BF, d, False)] * 4)


def _conv_silu_fwd(proj, w, name):
    s = proj.shape[0]
    cols = w.shape[1]
    cw = _tile(cols, LANE)

    def body(x_ref, w_ref, o_ref):
        o_ref[...] = _silu(_conv(x_ref[...].astype(F32), w_ref))

    return pl.pallas_call(
        body, name=name, grid=(cols // cw,),
        in_specs=[pl.BlockSpec((s, cw), lambda j: (0, j)), pl.BlockSpec((CONV_WIDTH, cw), lambda j: (0, j))],
        out_specs=pl.BlockSpec((s, cw), lambda j: (0, j)),
        out_shape=jax.ShapeDtypeStruct((s, cols), F32), compiler_params=_cparams(("parallel",)),
    )(proj, w)


def _conv_silu_bwd(proj, w, dout, name):
    s = proj.shape[0]
    cols = w.shape[1]
    cw = _tile(cols, LANE)

    def body(x_ref, w_ref, d_ref, dx_ref, dw_ref):
        x = x_ref[...].astype(F32)
        y = _conv(x, w_ref)
        sg = _sigmoid(y)
        dy = d_ref[...] * (sg * (1.0 + y * (1.0 - sg)))
        dx, dws = _conv_bwd(x, dy, w_ref)
        dx_ref[...] = dx.astype(dx_ref.dtype)
        for j in range(CONV_WIDTH):
            dw_ref[j:j + 1, :] = dws[j]

    return pl.pallas_call(
        body, name=name, grid=(cols // cw,),
        in_specs=[pl.BlockSpec((s, cw), lambda j: (0, j)), pl.BlockSpec((CONV_WIDTH, cw), lambda j: (0, j)),
                  pl.BlockSpec((s, cw), lambda j: (0, j))],
        out_specs=[pl.BlockSpec((s, cw), lambda j: (0, j)), pl.BlockSpec((CONV_WIDTH, cw), lambda j: (0, j))],
        out_shape=[jax.ShapeDtypeStruct((s, cols), BF), jax.ShapeDtypeStruct((CONV_WIDTH, cols), F32)],
        compiler_params=_cparams(("parallel",)),
    )(proj, w, dout)


def _cumsum_mask(tr, transpose):
    row = lax.broadcasted_iota(jnp.int32, (tr, tr), 0)
    col = lax.broadcasted_iota(jnp.int32, (tr, tr), 1)
    same = (row // CHUNK) == (col // CHUNK)
    tri = (row <= col) if transpose else (row >= col)
    return jnp.where(jnp.logical_and(same, tri), 1.0, 0.0)


def _lane_pad(v):
    return jnp.pad(v.reshape(1, -1), ((0, 0), (0, LANE - v.shape[0])))


def _gdn_gates_fwd(proj, a_log, dt_bias, ab_base, name):
    s = proj.shape[0]

    def fn(ab, al, dt):
        g = -jnp.exp(al) * _softplus(ab + dt)
        beta = _sigmoid(pltpu.roll(ab, LANE - HEADS, 1))
        gc = jnp.dot(_cumsum_mask(ab.shape[0], False), g, precision=lax.Precision.HIGHEST,
                     preferred_element_type=F32)
        return gc, beta

    return _rowwise(fn, name, s, 512, [(proj, LANE, ab_base, False)], [_lane_pad(a_log), _lane_pad(dt_bias)],
                    [(LANE, F32, LANE, False)] * 2)


def _gdn_gates_bwd(proj, a_log, dt_bias, dgc_col, dgc_row, dbeta, ab_base, name):
    s = proj.shape[0]

    def fn(ab, dc, dr, db, al, dt):
        lane = lax.broadcasted_iota(jnp.int32, ab.shape, 1)
        dg = jnp.dot(_cumsum_mask(ab.shape[0], True), dc + dr, precision=lax.Precision.HIGHEST,
                     preferred_element_type=F32)
        ea = jnp.exp(al)
        pre = ab + dt
        g = -ea * _softplus(pre)
        da = jnp.where(lane < HEADS, dg * (-ea) * _sigmoid(pre), 0.0)
        dal = jnp.sum(jnp.where(lane < HEADS, dg * g, 0.0), axis=0, keepdims=True)
        ddt = jnp.sum(da, axis=0, keepdims=True)
        beta = _sigmoid(pltpu.roll(ab, LANE - HEADS, 1))
        dbl = jnp.where(lane < HEADS, db * beta * (1.0 - beta), 0.0)
        dab = da + pltpu.roll(dbl, HEADS, 1)
        return dab, dal, ddt

    return _rowwise(fn, name, s, 512,
                    [(proj, LANE, ab_base, False), (dgc_col, LANE, 0, False), (dgc_row, LANE, 0, False),
                     (dbeta, LANE, 0, False)], [_lane_pad(a_log), _lane_pad(dt_bias)],
                    [(LANE, BF, LANE, False)], [((1, LANE), F32), ((1, LANE), F32)])


def _head_col(blk, h):
    lane = lax.broadcasted_iota(jnp.int32, blk.shape, 1)
    return jnp.sum(jnp.where(lane == h, blk, 0.0), axis=1, keepdims=True)


def _chunk_operands(qkv_ref, gc_ref, b_ref, gr_ref):
    heads = range(HEADS)
    q, k, v = (jnp.stack([qkv_ref[:, (j * HEADS + h) * HEAD_DIM:(j * HEADS + h + 1) * HEAD_DIM] for h in heads])
               for j in range(3))
    gcs, betas = gc_ref[...], b_ref[...]
    beta = jnp.stack([_head_col(betas, h) for h in heads])
    gcol = jnp.stack([_head_col(gcs, h) for h in heads])
    grow = jnp.stack([gr_ref[0, h:h + 1, :] for h in heads])
    return q, k, v, beta, gcol, grow


def _gdn_chunk_fwd(qkv, gc, beta, grow, name):
    s = qkv.shape[0]
    nc = s // CHUNK

    def body(qkv_ref, gc_ref, b_ref, gr_ref, o_ref, st_ref, t_ref, state):
        @pl.when(pl.program_id(0) == 0)
        def _():
            state[...] = jnp.zeros_like(state)

        st = state[...]
        st_ref[0] = st
        o, new, t = _gdn_chunk(*_chunk_operands(qkv_ref, gc_ref, b_ref, gr_ref), st, _PLAIN_B)
        t_ref[0] = t
        for h in range(HEADS):
            o_ref[:, h * HEAD_DIM:(h + 1) * HEAD_DIM] = o[h]
        state[...] = new

    return pl.pallas_call(
        body, name=name, grid=(nc,),
        in_specs=[pl.BlockSpec((CHUNK, 3 * GDN_WIDTH), lambda c: (c, 0)),
                  pl.BlockSpec((CHUNK, LANE), lambda c: (c, 0)), pl.BlockSpec((CHUNK, LANE), lambda c: (c, 0)),
                  pl.BlockSpec((1, HEADS, CHUNK), lambda c: (c, 0, 0))],
        out_specs=[pl.BlockSpec((CHUNK, GDN_WIDTH), lambda c: (c, 0)),
                   pl.BlockSpec((1, HEADS, HEAD_DIM, HEAD_DIM), lambda c: (c, 0, 0, 0)),
                   pl.BlockSpec((1, HEADS, CHUNK, CHUNK), lambda c: (c, 0, 0, 0))],
        out_shape=[jax.ShapeDtypeStruct((s, GDN_WIDTH), F32),
                   jax.ShapeDtypeStruct((nc, HEADS, HEAD_DIM, HEAD_DIM), F32),
                   jax.ShapeDtypeStruct((nc, HEADS, CHUNK, CHUNK), F32)],
        scratch_shapes=[pltpu.VMEM((HEADS, HEAD_DIM, HEAD_DIM), F32)],
        compiler_params=_cparams(("arbitrary",)),
    )(qkv, gc, beta, grow)


def _gdn_chunk_bwd(qkv, gc, beta, grow, states, tinv, do, name):
    s = qkv.shape[0]
    nc = s // CHUNK

    def body(qkv_ref, gc_ref, b_ref, gr_ref, st_ref, t_ref, do_ref, dqkv_ref, dgc_ref, db_ref, dgr_ref, dstate):
        @pl.when(pl.program_id(0) == 0)
        def _():
            dstate[...] = jnp.zeros_like(dstate)

        lane = lax.broadcasted_iota(jnp.int32, (CHUNK, LANE), 1)
        chunk = lambda *args: _gdn_chunk(*args, mm=_DIFF_B, t_saved=t_ref[0])[:2]
        _, vjp = jax.vjp(chunk, *_chunk_operands(qkv_ref, gc_ref, b_ref, gr_ref), st_ref[0])
        do = jnp.stack([do_ref[:, h * HEAD_DIM:(h + 1) * HEAD_DIM] for h in range(HEADS)])
        dq, dk, dv, dbeta, dgcol, dgrow, dst = vjp((do, dstate[...]))
        dstate[...] = dst
        dgc = jnp.zeros((CHUNK, LANE), F32)
        db = jnp.zeros((CHUNK, LANE), F32)
        for h in range(HEADS):
            for j, dx in enumerate((dq, dk, dv)):
                dqkv_ref[:, (j * HEADS + h) * HEAD_DIM:(j * HEADS + h + 1) * HEAD_DIM] = dx[h]
            dgc = dgc + jnp.where(lane == h, dgcol[h], 0.0)
            db = db + jnp.where(lane == h, dbeta[h], 0.0)
            dgr_ref[0, h:h + 1, :] = dgrow[h]
        dgc_ref[...] = dgc
        db_ref[...] = db

    rev = lambda c: nc - 1 - c
    wide = pl.BlockSpec((CHUNK, 3 * GDN_WIDTH), lambda c: (rev(c), 0))
    lanes = pl.BlockSpec((CHUNK, LANE), lambda c: (rev(c), 0))
    rows = pl.BlockSpec((1, HEADS, CHUNK), lambda c: (rev(c), 0, 0))
    return pl.pallas_call(
        body, name=name, grid=(nc,),
        in_specs=[wide, lanes, lanes, rows,
                  pl.BlockSpec((1, HEADS, HEAD_DIM, HEAD_DIM), lambda c: (rev(c), 0, 0, 0)),
                  pl.BlockSpec((1, HEADS, CHUNK, CHUNK), lambda c: (rev(c), 0, 0, 0)),
                  pl.BlockSpec((CHUNK, GDN_WIDTH), lambda c: (rev(c), 0))],
        out_specs=[wide, lanes, lanes, rows],
        out_shape=[jax.ShapeDtypeStruct((s, 3 * GDN_WIDTH), F32)]
        + [jax.ShapeDtypeStruct((s, LANE), F32)] * 2 + [jax.ShapeDtypeStruct((nc, HEADS, CHUNK), F32)],
        scratch_shapes=[pltpu.VMEM((HEADS, HEAD_DIM, HEAD_DIM), F32)],
        compiler_params=_cparams(("arbitrary",)),
    )(qkv, gc, beta, grow, states, tinv, do)


def _lru_specs(s, xb_base):
    col = lambda base: pl.BlockSpec((s, LRU_BLOCK_DIM), lambda n, base=base: (0, base + n))
    vec = pl.BlockSpec((1, LRU_BLOCK_DIM), lambda n: (0, n))
    mat = pl.BlockSpec((None, LRU_BLOCK_DIM, LRU_BLOCK_DIM), lambda n: (n, 0, 0))
    cw = pl.BlockSpec((CONV_WIDTH, LRU_BLOCK_DIM), lambda n: (0, n))
    return col, vec, mat, cw


def _lru_fwd(proj, cw, cb, wa, ba, wx, bx, lam, xb_base, name):
    s = proj.shape[0]
    tt = _tile(s, 256, 8)
    col, vec, mat, cws = _lru_specs(s, xb_base)

    def body(xb_ref, yb_ref, cw_ref, cb_ref, wa_ref, ba_ref, wx_ref, bx_ref, lam_ref, o_ref, h_ref, a_ref, xc_s):
        xc_s[...] = _conv(xb_ref[...].astype(F32), cw_ref) + cb_ref[...]
        par = (wa_ref[...], ba_ref[...], wx_ref[...], bx_ref[...], lam_ref[...])

        def step(t, carry):
            sl = pl.ds(pl.multiple_of(t * tt, tt), tt)
            a, bt = _lru_gates(xc_s[sl, :], *par, _PLAIN)
            a_ref[sl, :] = a
            aa, bb = _scan_tile(a, bt, False)
            h = aa * carry + bb
            h_ref[sl, :] = h
            o_ref[sl, :] = (h * _gelu(yb_ref[sl, :].astype(F32))).astype(o_ref.dtype)
            return h_ref[pl.ds(t * tt + tt - 1, 1), :]

        lax.fori_loop(0, s // tt, step, jnp.zeros((1, LRU_BLOCK_DIM), F32))

    nb = LRU_BLOCKS
    return pl.pallas_call(
        body, name=name, grid=(LRU_BLOCKS,),
        in_specs=[col(xb_base), col(xb_base + nb), cws, vec, mat, vec, mat, vec, vec],
        out_specs=[col(0), col(0), col(0)],
        out_shape=[jax.ShapeDtypeStruct((s, LRU_WIDTH), BF), jax.ShapeDtypeStruct((s, LRU_WIDTH), F32),
                   jax.ShapeDtypeStruct((s, LRU_WIDTH), F32)],
        scratch_shapes=[pltpu.VMEM((s, LRU_BLOCK_DIM), F32)],
        compiler_params=_cparams(("parallel",)),
    )(proj, proj, cw, _row(cb), wa, _row(ba), wx, _row(bx), _row(lam))


def _lru_bwd(proj, hs, decay, dol, cw, cb, wa, ba, wx, bx, lam, xb_base, name):
    s = proj.shape[0]
    tt = _tile(s, 256, 8)
    nt = s // tt
    col, vec, mat, cws = _lru_specs(s, xb_base)

    def body(xb_ref, yb_ref, h_ref, a_ref, d_ref, cw_ref, cb_ref, wa_ref, ba_ref, wx_ref, bx_ref, lam_ref,
             dxb_ref, dyb_ref, dcw_ref, dcb_ref, dwa_ref, dba_ref, dwx_ref, dbx_ref, dlam_ref,
             xc_s, dh_s, dxc_s):
        xc_s[...] = _conv(xb_ref[...].astype(F32), cw_ref) + cb_ref[...]
        par = (wa_ref[...], ba_ref[...], wx_ref[...], bx_ref[...], lam_ref[...])
        row = lax.broadcasted_iota(jnp.int32, (tt, LRU_BLOCK_DIM), 0)
        tile = lambda t: pl.ds(pl.multiple_of(t * tt, tt), tt)

        def prep(t, carry):
            sl = tile(t)
            d = d_ref[sl, :].astype(F32)
            gy, vjp_y = jax.vjp(_gelu, yb_ref[sl, :].astype(F32))
            dyb_ref[sl, :] = vjp_y(d * h_ref[sl, :])[0].astype(dyb_ref.dtype)
            dh_s[sl, :] = d * gy
            return carry

        lax.fori_loop(0, nt, prep, 0)

        def rscan(i, carry):
            dh_next, a_next = carry
            t = nt - 1 - i
            sl = tile(t)
            a_sh = jnp.where(row == tt - 1, a_next, pltpu.roll(a_ref[sl, :], tt - 1, 0))
            aa, bb = _scan_tile(a_sh, dh_s[sl, :], True)
            dh_s[sl, :] = aa * dh_next + bb
            first = pl.ds(t * tt, 1)
            return dh_s[first, :], a_ref[first, :]

        zero = jnp.zeros((1, LRU_BLOCK_DIM), F32)
        lax.fori_loop(0, nt, rscan, (zero, zero))

        def gates_vjp(t, acc):
            sl = tile(t)
            _, vjp_g = jax.vjp(functools.partial(_lru_gates, mm=_DIFF), xc_s[sl, :], *par)
            dh = dh_s[sl, :]
            before = jnp.where(t > 0, h_ref[pl.ds(jnp.maximum(t * tt - 1, 0), 1), :], 0.0)
            h_prev = jnp.where(row == 0, before, pltpu.roll(h_ref[sl, :], 1, 0))
            dxc, *dpar = vjp_g((dh * h_prev, dh))
            dxc_s[sl, :] = dxc
            return tuple(x + y for x, y in zip(acc, dpar))

        dwa, dba, dwx, dbx, dlam = lax.fori_loop(0, nt, gates_vjp, tuple(jnp.zeros_like(p) for p in par))
        dwa_ref[...] = dwa
        dba_ref[...] = dba
        dwx_ref[...] = dwx
        dbx_ref[...] = dbx
        dlam_ref[...] = dlam
        dxc = dxc_s[...]
        dcb_ref[...] = jnp.sum(dxc, axis=0, keepdims=True)
        dxb, dws = _conv_bwd(xb_ref[...].astype(F32), dxc, cw_ref)
        dxb_ref[...] = dxb.astype(dxb_ref.dtype)
        for j in range(CONV_WIDTH):
            dcw_ref[j:j + 1, :] = dws[j]

    nb = LRU_BLOCKS
    w = LRU_WIDTH
    return pl.pallas_call(
        body, name=name, grid=(LRU_BLOCKS,),
        in_specs=[col(xb_base), col(xb_base + nb), col(0), col(0), col(0), cws, vec, mat, vec, mat, vec, vec],
        out_specs=[col(0), col(0), cws, vec, mat, vec, mat, vec, vec],
        out_shape=[jax.ShapeDtypeStruct((s, w), BF), jax.ShapeDtypeStruct((s, w), BF),
                   jax.ShapeDtypeStruct((CONV_WIDTH, w), F32), jax.ShapeDtypeStruct((1, w), F32),
                   jax.ShapeDtypeStruct(wa.shape, F32), jax.ShapeDtypeStruct((1, w), F32),
                   jax.ShapeDtypeStruct(wx.shape, F32), jax.ShapeDtypeStruct((1, w), F32),
                   jax.ShapeDtypeStruct((1, w), F32)],
        scratch_shapes=[pltpu.VMEM((s, LRU_BLOCK_DIM), F32)] * 3,
        compiler_params=_cparams(("parallel",)),
    )(proj, proj, hs, decay, dol, cw, _row(cb), wa, _row(ba), wx, _row(bx), _row(lam))


def _place():
    return lax.axis_index("x"), lax.axis_index("y"), lax.axis_index("c")


def _all_gather(blocks, name):
    n = len(blocks)
    per = N_DEV - 1

    def body(*refs):
        ins, outs = refs[:n], refs[n:2 * n]
        send_sems, recv_sems, local_sems = refs[2 * n:]
        x, y, c = _place()
        me, sibling = (x, y, c), (x, y, 1 - c)
        chips = [(1 - x, y), (x, 1 - y), (1 - x, 1 - y)]

        def copy(a, k, block, to, src=None):
            dst = outs[a].at[4 * block[0] + 2 * block[1] + block[2]]
            return pltpu.make_async_remote_copy(
                src_ref=dst if src is None else src, dst_ref=dst, send_sem=send_sems.at[a * per + k],
                recv_sem=recv_sems.at[a * per + k], device_id=to, device_id_type=MESH)

        started = []
        for a in range(n):
            mine = pltpu.make_async_copy(ins[a], outs[a].at[4 * x + 2 * y + c], local_sems.at[a])
            mine.start()
            started.append(mine)
        sends = []
        for a in range(n):
            first = [copy(a, 0, me, sibling, src=ins[a])]
            first += [copy(a, 1 + j, me, (*chip, c), src=ins[a]) for j, chip in enumerate(chips)]
            for cp in first:
                cp.start()
            sends += first
        for a in range(n):
            for j, chip in enumerate(chips):
                copy(a, 1 + j, (*chip, c), me).wait_recv()
                passed = copy(a, 4 + j, (*chip, c), sibling)
                passed.start()
                sends.append(passed)
        for a in range(n):
            copy(a, 0, sibling, me).wait_recv()
            for j, chip in enumerate(chips):
                copy(a, 4 + j, (*chip, 1 - c), me).wait_recv()
        for cp in sends:
            cp.wait_send()
        for cp in started:
            cp.wait()

    any_spec = pl.BlockSpec(memory_space=pl.ANY)
    return pl.pallas_call(
        body, name=name, in_specs=[any_spec] * n, out_specs=[any_spec] * n,
        out_shape=[jax.ShapeDtypeStruct((N_DEV,) + b.shape, b.dtype) for b in blocks],
        scratch_shapes=[pltpu.SemaphoreType.DMA((n * per,)), pltpu.SemaphoreType.DMA((n * per,)),
                        pltpu.SemaphoreType.DMA((n,))],
    )(*blocks)


_HBM = pl.BlockSpec(memory_space=pltpu.HBM)
_SEM = pl.BlockSpec(memory_space=pltpu.SEMAPHORE)
_ANY = pl.BlockSpec(memory_space=pl.ANY)
_EFFECT = pltpu.SideEffectType.DATAFLOW_SIDE_EFFECTING


def _peers(x, y, c):
    return [(x, y, 1 - c), (1 - x, y, c), (1 - x, y, 1 - c), (x, 1 - y, c), (x, 1 - y, 1 - c),
            (1 - x, 1 - y, c), (1 - x, 1 - y, 1 - c)]


class _InFlight:
    def __init__(self, send_sem, recv_sem, src, land):
        self.send_sem, self.recv_sem, self.src, self.land = send_sem, recv_sem, src, land


def _send_start(srcs, whole, after, name):
    n = len(srcs)
    lands = [lax.empty(((N_DEV,) + s.shape) if whole else s.shape, s.dtype) for s in srcs]

    def body(*refs):
        src_refs, land_refs = refs[:n], refs[n:2 * n]
        outs = refs[2 * n + 1:]
        send_sems, recv_sems, token = outs[:n], outs[n:2 * n], outs[-1]
        x, y, c = _place()
        me = 4 * x + 2 * y + c
        for a in range(n):
            for p in _peers(x, y, c):
                src = src_refs[a] if whole else src_refs[a].at[4 * p[0] + 2 * p[1] + p[2]]
                pltpu.make_async_remote_copy(src_ref=src, dst_ref=land_refs[a].at[me], send_sem=send_sems[a],
                                             recv_sem=recv_sems[a], device_id=p, device_id_type=MESH).start()
        token[...] = jnp.zeros_like(token)

    hbm = lambda a: pltpu.HBM(a.shape, a.dtype)
    outs = pl.pallas_call(
        body, name=name,
        out_shape=[pltpu.SemaphoreType.DMA(())] * (2 * n) + [hbm(s) for s in srcs] + [hbm(l) for l in lands]
        + [jax.ShapeDtypeStruct((8, LANE), F32)],
        in_specs=[_HBM] * (2 * n) + [_ANY],
        out_specs=[_SEM] * (2 * n) + [_HBM] * (2 * n) + [pl.BlockSpec(memory_space=pltpu.VMEM)],
        input_output_aliases={i: 2 * n + i for i in range(2 * n)},
        compiler_params=pltpu.CompilerParams(has_side_effects=_EFFECT),
    )(*[pltpu.with_memory_space_constraint(s, pltpu.HBM) for s in srcs],
      *[pltpu.with_memory_space_constraint(l, pltpu.HBM) for l in lands], after)
    flights = [_InFlight(outs[a], outs[n + a], outs[2 * n + a], outs[3 * n + a]) for a in range(n)]
    return flights, outs[-1]


def _send_wait(flights, after, name):
    n = len(flights)

    def body(*refs):
        x, y, c = _place()
        for a in range(n):
            land_ref, send_sem, recv_sem = refs[4 * a + 1], refs[4 * a + 2], refs[4 * a + 3]
            seven = land_ref.at[pl.ds(0, N_DEV - 1)]
            copy = pltpu.make_async_remote_copy(src_ref=seven, dst_ref=seven, send_sem=send_sem, recv_sem=recv_sem,
                                                device_id=(x, y, 1 - c), device_id_type=MESH)
            copy.wait_send()
            copy.wait_recv()

    ins, out_shape = [], []
    for f in flights:
        ins += [f.src, f.land, f.send_sem, f.recv_sem]
        out_shape += [pltpu.HBM(f.src.shape, f.src.dtype), pltpu.HBM(f.land.shape, f.land.dtype)]
    outs = pl.pallas_call(
        body, name=name, out_shape=out_shape,
        in_specs=[_HBM, _HBM, _SEM, _SEM] * n + [_ANY], out_specs=[_HBM] * (2 * n),
        input_output_aliases={4 * a + i: 2 * a + i for a in range(n) for i in range(2)},
        compiler_params=pltpu.CompilerParams(has_side_effects=_EFFECT),
    )(*ins, after)
    return [(outs[2 * a], outs[2 * a + 1]) for a in range(n)]


def _adamw_math(w, g, m, v):
    m = ADAM_B1 * m + (1.0 - ADAM_B1) * g
    v = ADAM_B2 * v + (1.0 - ADAM_B2) * (g * g)
    m_hat = m / (1.0 - ADAM_B1 ** ADAM_STEP)
    v_hat = v / (1.0 - ADAM_B2 ** ADAM_STEP)
    delta = -ADAM_LR * (m_hat / (jnp.sqrt(v_hat) + ADAM_EPS) + ADAM_WD * w)
    return delta, m, v


def _adamw_sharded(mine, landed, me, w, m, v, layer, prev, after, name):
    depth, r, c = w.shape
    tr = _tile(r, 128, 8)
    n_prev = 0 if prev is None else 4

    def body(*refs):
        me_ref, own_ref, land_ref, w_ref, m_ref, v_ref = refs[:6]
        g_ref, d_ref, nm_ref, nv_ref = refs[7 + n_prev:]
        g = None
        for k in range(N_DEV):
            part = jnp.where(me_ref[0] == k, own_ref[0], land_ref[k]).astype(F32)
            g = part if g is None else g + part
        delta, nm, nv = _adamw_math(w_ref[...], g, m_ref[...], v_ref[...])
        g_ref[...] = g
        d_ref[...] = delta
        nm_ref[...] = nm
        nv_ref[...] = nv

    lay = pl.BlockSpec((None, tr, c), lambda i, me_ref: (layer, i, 0))
    return pl.pallas_call(
        body, name=name,
        grid_spec=pltpu.PrefetchScalarGridSpec(
            num_scalar_prefetch=1, grid=(r // tr,),
            in_specs=[pl.BlockSpec((1, tr, c), lambda i, me_ref: (me_ref[0], i, 0)),
                      pl.BlockSpec((N_DEV, tr, c), lambda i, me_ref: (0, i, 0)), lay, lay, lay]
            + [_ANY] * (1 + n_prev),
            out_specs=[lay] * 4),
        out_shape=[jax.ShapeDtypeStruct(w.shape, F32)] * 4,
        input_output_aliases={7 + i: i for i in range(n_prev)},
        compiler_params=_cparams(("parallel",)),
    )(me, mine, landed, w, m, v, after, *(prev or ()))


def _sum_devices(stack, name):
    _, r, c = stack.shape
    tr = _tile(r, PACK_ROWS, 8)

    def body(s_ref, o_ref):
        g = s_ref[0]
        for j in range(1, N_DEV):
            g = g + s_ref[j]
        o_ref[...] = g

    return pl.pallas_call(
        body, name=name, grid=(r // tr,), in_specs=[pl.BlockSpec((N_DEV, tr, c), lambda i: (0, i, 0))],
        out_specs=pl.BlockSpec((tr, c), lambda i: (i, 0)), out_shape=jax.ShapeDtypeStruct((r, c), F32),
        compiler_params=_cparams(("parallel",)),
    )(stack)


def _adamw_packed(w, g, m, v, name):
    r, c = w.shape
    tr = _tile(r, PACK_ROWS, 8)

    def body(w_ref, g_ref, m_ref, v_ref, d_ref, nm_ref, nv_ref):
        delta, nm, nv = _adamw_math(w_ref[...], g_ref[...], m_ref[...], v_ref[...])
        d_ref[...] = delta
        nm_ref[...] = nm
        nv_ref[...] = nv

    spec = pl.BlockSpec((tr, c), lambda i: (i, 0))
    return pl.pallas_call(
        body, name=name, grid=(r // tr,), in_specs=[spec] * 4, out_specs=[spec] * 3,
        out_shape=[jax.ShapeDtypeStruct((r, c), F32)] * 3, compiler_params=_cparams(("parallel",)),
    )(w, g, m, v)


def _pack(arrays):
    unit = 8 * LANE
    flat = []
    for a in arrays:
        f = a.reshape(-1)
        flat.append(jnp.pad(f, (0, (-f.shape[0]) % unit)))
    total = sum(f.shape[0] for f in flat)
    flat.append(jnp.zeros(((-total) % (PACK_ROWS * LANE),), flat[0].dtype))
    return jnp.concatenate(flat).reshape(-1, LANE)


def _unpack(packed, like):
    unit = 8 * LANE
    flat = packed.reshape(-1)
    out, off = [], 0
    for a in like:
        n = 1
        for d in a.shape:
            n *= d
        out.append(flat[off:off + n].reshape(a.shape))
        off += n + (-n) % unit
    return out


class _Layout:
    def __init__(self, d):
        self.d = d
        self.n_front = 4 * GDN_WIDTH
        self.n_ab = 2 * HEADS
        self.n_back = 2 * LRU_WIDTH + 2 * d
        self.cols = self.n_front + self.n_ab + self.n_back
        self.pcols = self.n_front + self.n_back + LANE
        self.z_base = 3 * GDN_WIDTH // HEAD_DIM
        self.xb_base = self.n_front // LRU_BLOCK_DIM
        self.gg_off = self.n_front + 2 * LRU_WIDTH
        assert self.gg_off % d == 0
        self.gg_base = self.gg_off // d
        self.ab_base = (self.n_front + self.n_back) // LANE

    def _ranges(self):
        f, ab = self.n_front, self.n_ab
        return [(0, f), (f + ab, self.cols), (f, f + ab)]

    def permuted_from_stack(self, stack):
        blk = stack.shape[2]
        pieces = []
        for lo, hi in self._ranges():
            while lo < hi:
                k = lo // blk
                end = min(hi, (k + 1) * blk)
                pieces.append(stack[k, :, lo - k * blk:end - k * blk])
                lo = end
        pieces.append(jnp.zeros((stack.shape[1], LANE - self.n_ab), stack.dtype))
        return jnp.concatenate(pieces, axis=1)

    def stack_from_permuted(self, w):
        blk = self.cols // N_DEV
        where = {}
        pos = 0
        for lo, hi in self._ranges():
            where[lo] = pos
            pos += hi - lo
        cuts = sorted(lo for lo, _ in self._ranges()) + [self.cols]
        blocks = []
        for k in range(N_DEV):
            lo, pieces = k * blk, []
            while lo < (k + 1) * blk:
                start = max(c for c in cuts if c <= lo)
                end = min((k + 1) * blk, min(c for c in cuts if c > lo))
                pieces.append(w[:, where[start] + lo - start:where[start] + end - start])
                lo = end
            blocks.append(pieces[0] if len(pieces) == 1 else jnp.concatenate(pieces, axis=1))
        return jnp.stack(blocks)


def _cols_from_stack(stack):
    nd, k, n = stack.shape
    return stack.transpose(1, 0, 2).reshape(k, nd * n)


def _stack_from_cols(w):
    k, n = w.shape
    return w.reshape(k, N_DEV, n // N_DEV).transpose(1, 0, 2)


def _after(v, *tokens):
    for t in tokens:
        v = v + t[0, 0]
    return v


def _layer_fwd(x, p, late, lay, tag):
    s, d = x.shape
    nc = s // CHUNK
    h = _rms_fwd(x, p["attn_norm"], f"rms1_fwd{tag}")
    proj, ab = _matmul(h, p["w_in"], "nn", f"in_proj{tag}", [BF], tm=2048, tn=1152, tail=True)
    qkv = _conv_silu_fwd(proj, p["gdn_conv_w"], f"gdn_conv_fwd{tag}")
    gc, beta = _gdn_gates_fwd(ab, p["gdn_a_log"], p["gdn_dt_bias"], 0, f"gdn_gates_fwd{tag}")
    grow = gc[:, :HEADS].reshape(nc, CHUNK, HEADS).transpose(0, 2, 1)
    o, states, tinv = _gdn_chunk_fwd(qkv, gc, beta, grow, f"gdn_chunk_fwd{tag}")
    og = _gdn_post_fwd(o, proj, p["gdn_norm"], lay.z_base, f"gdn_post_fwd{tag}")
    ol, hs, decay = _lru_fwd(proj, p["lru_conv_w"], p["lru_conv_b"], p["lru_w_a"], p["lru_b_a"], p["lru_w_x"], p["lru_b_x"],
                      p["lru_lambda"], lay.xb_base, f"lru_fwd{tag}")
    late[0](hs)
    yg = _matmul(og, p["w_branch_gdn"], "nn", f"branch_gdn{tag}", [BF])
    yl = _matmul(ol, p["w_branch_lru"], "nn", f"branch_lru{tag}", [BF])
    merged = _merge_fwd(proj, yg, yl, lay.gg_base, f"merge_fwd{tag}")
    x1 = _matmul(merged, p["w_out"], "nn", f"out_proj{tag}", [F32], epi=lambda acc, r: (acc + r,), extras=[x])
    h2 = _rms_fwd(x1, p["mlp_norm"], f"rms2_fwd{tag}")
    late[1](h2)
    u, act = _matmul(h2, p["w_up"], "nn", f"mlp_up{tag}", [BF, BF],
                     epi=lambda acc: (acc, jnp.square(jnp.maximum(acc, 0.0))), tm=2048, shards=True)
    x2 = _matmul(act, p["w_down"], "nn", f"mlp_down{tag}", [F32], epi=lambda acc, r: (acc + r,), extras=[x1])
    saved = dict(x=x, h=h, proj=proj, ab=ab, qkv=qkv, gc=gc, beta=beta, grow=grow, o=o, states=states, tinv=tinv, og=og, ol=ol, hs=hs, decay=decay,
                 yg=yg, yl=yl, merged=merged, x1=x1, h2=h2, u=u, act=act)
    return x2, saved


def _layer_bwd(dx2, p, sv, send, send_small, after_last_send, after, lay, tag):
    s, d = dx2.shape
    nc = s // CHUNK
    small = {}
    du = _matmul(dx2, p["w_down"], "nt", f"d_act{tag}", [BF],
                 epi=lambda acc, u: (acc * (2.0 * jnp.maximum(u.astype(F32), 0.0)),), extras=[sv["u"]], after=after)
    sent = send({"w_down": _matmul(sv["act"], dx2, "tn", f"dw_down{tag}", [BF]).reshape(N_DEV, -1, d),
                 "w_up": _matmul(sv["h2"], du, "tn", f"dw_up{tag}", [BF], shards=True)})
    dh2 = _matmul(du, p["w_up"], "nt", f"d_h2{tag}", [BF], shards=True)
    dx1, g = _rms_bwd(sv["x1"], _after(p["mlp_norm"], sent), dh2, dx2, f"rms2_bwd{tag}")
    small["mlp_norm"] = g.reshape(-1)
    dmerged = _matmul(dx1, p["w_out"], "nt", f"d_merged{tag}", [BF])
    dwout = _matmul(sv["merged"], dx1, "tn", f"dw_out{tag}", [BF]).reshape(N_DEV, -1, d)
    dgg, dgl, dyg, dyl = _merge_bwd(sv["proj"], sv["yg"], sv["yl"], dmerged, lay.gg_base, f"merge_bwd{tag}")
    sent = send({"w_out": dwout,
                 "w_branch_gdn": _stack_from_cols(_matmul(sv["og"], dyg, "tn", f"dw_bg{tag}", [BF])),
                 "w_branch_lru": _stack_from_cols(_matmul(sv["ol"], dyl, "tn", f"dw_bl{tag}", [BF]))})
    dog = _matmul(dyg, p["w_branch_gdn"], "nt", f"d_og{tag}", [BF])
    dol = _matmul(dyl, p["w_branch_lru"], "nt", f"d_ol{tag}", [BF])
    (dxb, dyb, dcw, dcb, dwa, dba, dwx, dbx, dlam) = _lru_bwd(
        sv["proj"], sv["hs"], sv["decay"], dol, p["lru_conv_w"], p["lru_conv_b"], p["lru_w_a"], p["lru_b_a"], p["lru_w_x"],
        p["lru_b_x"], _after(p["lru_lambda"], sent), lay.xb_base, f"lru_bwd{tag}")
    small.update(lru_conv_w=dcw, lru_conv_b=dcb.reshape(-1), lru_w_a=dwa, lru_b_a=dba.reshape(-1), lru_w_x=dwx,
                 lru_b_x=dbx.reshape(-1), lru_lambda=dlam.reshape(-1))
    do, dz, g = _gdn_post_bwd(sv["o"], sv["proj"], p["gdn_norm"], dog, lay.z_base, f"gdn_post_bwd{tag}")
    small["gdn_norm"] = g.reshape(-1)
    dqkv_c, dgc_col, dbeta, dgrow = _gdn_chunk_bwd(sv["qkv"], sv["gc"], sv["beta"], sv["grow"], sv["states"], sv["tinv"], do,
                                                   f"gdn_chunk_bwd{tag}")
    dqkv, dgcw = _conv_silu_bwd(sv["proj"], p["gdn_conv_w"], dqkv_c, f"gdn_conv_bwd{tag}")
    small["gdn_conv_w"] = dgcw
    dgc_row = jnp.pad(dgrow.transpose(0, 2, 1).reshape(s, HEADS), ((0, 0), (0, LANE - HEADS)))
    dab, dal, ddt = _gdn_gates_bwd(sv["ab"], p["gdn_a_log"], p["gdn_dt_bias"], dgc_col, dgc_row, dbeta, 0,
                                   f"gdn_gates_bwd{tag}")
    small["gdn_a_log"] = dal[0, :HEADS]
    small["gdn_dt_bias"] = ddt[0, :HEADS]
    dproj = jnp.concatenate([dqkv, dz, dxb, dyb, dgg, dgl, dab], axis=1)
    sent = send_small(small, "main")
    dwin = _matmul(sv["h"], dproj, "tn", f"dw_in{tag}", [BF], tn=1152, after=sent)
    sent = send({"w_in": lay.stack_from_permuted(dwin)})
    dh = _matmul(dproj, p["w_in"], "nt", f"d_h{tag}", [BF], tk=3456, after=sent)
    dx, g = _rms_bwd(sv["x"], p["attn_norm"], dh, dx1, f"rms1_bwd{tag}")
    after_last_send(send_small({"attn_norm": g.reshape(-1)}, "attn"))
    return dx


_BIG = ("w_in", "w_branch_gdn", "w_branch_lru", "w_out", "w_up", "w_down")
_ROW_SHARDED = ("w_out", "w_down")
_CONV = ("gdn_conv_w", "lru_conv_w")
_SMALL = ("attn_norm", "gdn_a_log", "gdn_dt_bias", "gdn_norm", "lru_conv_b", "lru_w_a", "lru_b_a", "lru_w_x",
          "lru_b_x", "lru_lambda", "mlp_norm")
_ORDER = ("attn_norm", "w_in", "gdn_conv_w", "gdn_a_log", "gdn_dt_bias", "gdn_norm", "lru_conv_w", "lru_conv_b",
          "lru_w_a", "lru_b_a", "lru_w_x", "lru_b_x", "lru_lambda", "w_branch_gdn", "w_branch_lru", "w_out",
          "mlp_norm", "w_up", "w_down", "final_norm")


def kernel(x, attn_norm, w_in, gdn_conv_w, gdn_a_log, gdn_dt_bias, gdn_norm, lru_conv_w, lru_conv_b, lru_w_a, lru_b_a, lru_w_x, lru_b_x, lru_lambda, w_branch_gdn, w_branch_lru, w_out, mlp_norm, w_up, w_down, final_norm, loss_target, m_attn_norm, m_w_in, m_gdn_conv_w, m_gdn_a_log, m_gdn_dt_bias, m_gdn_norm, m_lru_conv_w, m_lru_conv_b, m_lru_w_a, m_lru_b_a, m_lru_w_x, m_lru_b_x, m_lru_lambda, m_w_branch_gdn, m_w_branch_lru, m_w_out, m_mlp_norm, m_w_up, m_w_down, m_final_norm, v_attn_norm, v_w_in, v_gdn_conv_w, v_gdn_a_log, v_gdn_dt_bias, v_gdn_norm, v_lru_conv_w, v_lru_conv_b, v_lru_w_a, v_lru_b_a, v_lru_w_x, v_lru_b_x, v_lru_lambda, v_w_branch_gdn, v_w_branch_lru, v_w_out, v_mlp_norm, v_w_up, v_w_down, v_final_norm):
    w = dict(attn_norm=attn_norm, w_in=w_in, gdn_conv_w=gdn_conv_w, gdn_a_log=gdn_a_log, gdn_dt_bias=gdn_dt_bias, gdn_norm=gdn_norm, lru_conv_w=lru_conv_w, lru_conv_b=lru_conv_b, lru_w_a=lru_w_a, lru_b_a=lru_b_a, lru_w_x=lru_w_x, lru_b_x=lru_b_x, lru_lambda=lru_lambda, w_branch_gdn=w_branch_gdn, w_branch_lru=w_branch_lru, w_out=w_out, mlp_norm=mlp_norm, w_up=w_up, w_down=w_down, final_norm=final_norm)
    m = dict(attn_norm=m_attn_norm, w_in=m_w_in, gdn_conv_w=m_gdn_conv_w, gdn_a_log=m_gdn_a_log, gdn_dt_bias=m_gdn_dt_bias, gdn_norm=m_gdn_norm, lru_conv_w=m_lru_conv_w, lru_conv_b=m_lru_conv_b, lru_w_a=m_lru_w_a, lru_b_a=m_lru_b_a, lru_w_x=m_lru_w_x, lru_b_x=m_lru_b_x, lru_lambda=m_lru_lambda, w_branch_gdn=m_w_branch_gdn, w_branch_lru=m_w_branch_lru, w_out=m_w_out, mlp_norm=m_mlp_norm, w_up=m_w_up, w_down=m_w_down, final_norm=m_final_norm)
    v = dict(attn_norm=v_attn_norm, w_in=v_w_in, gdn_conv_w=v_gdn_conv_w, gdn_a_log=v_gdn_a_log, gdn_dt_bias=v_gdn_dt_bias, gdn_norm=v_gdn_norm, lru_conv_w=v_lru_conv_w, lru_conv_b=v_lru_conv_b, lru_w_a=v_lru_w_a, lru_b_a=v_lru_b_a, lru_w_x=v_lru_w_x, lru_b_x=v_lru_b_x, lru_lambda=v_lru_lambda, w_branch_gdn=v_w_branch_gdn, w_branch_lru=v_w_branch_lru, w_out=v_w_out, mlp_norm=v_mlp_norm, w_up=v_w_up, w_down=v_w_down, final_norm=v_final_norm)
    depth = w_in.shape[0]
    x = x[0]
    target = loss_target[0]
    s, d = x.shape
    lay = _Layout(d)
    assert lay.cols == w_in.shape[2] * N_DEV
    cx, cy, cc = _place()
    me = 4 * cx + 2 * cy + cc

    me_arr = me.astype(jnp.int32).reshape(1)

    early, mixer, mlp = ("w_in",) + _CONV, ("w_branch_gdn", "w_branch_lru", "w_out"), ("w_up", "w_down")

    def shard(n, l):
        return w[n][l].astype(BF) if n in _BIG else w[n][l]

    def joined(name, stack):
        if name == "w_in":
            return lay.permuted_from_stack(stack)
        if name == "w_up":
            return stack
        return stack.reshape(-1, stack.shape[-1]) if name in _ROW_SHARDED else _cols_from_stack(stack)

    def start_weights(l, names, after):
        flights, token = _send_start([shard(n, l) for n in names], True, after, f"weights_start_l{l}")
        return dict(zip(names, flights)), token

    def wait_weights(l, flights, names, p, after, which):
        got = _send_wait([flights[n] for n in names], after, f"weights_wait_{which}_l{l}")
        for n, (own, landed) in zip(names, got):
            p[n] = joined(n, lax.dynamic_update_slice_in_dim(landed, own[None], me, axis=0))

    saved, params = [], []
    xl = x
    first = dict(zip(early + mixer, _all_gather([shard(n, 0) for n in early + mixer], "gather_first_weights")))
    flights, token = start_weights(0, mlp, first["w_in"])
    for l in range(depth):
        p = {n: w[n][l] for n in _SMALL}
        if l == 0:
            p.update({n: joined(n, g) for n, g in first.items()})
        else:
            wait_weights(l, flights, early, p, xl, "early")
        mine = flights
        if l + 1 < depth:
            flights, token = start_weights(l + 1, early + mixer + mlp, token)
        p["attn_norm"] = _after(p["attn_norm"], token)
        late = [functools.partial(wait_weights, l, mine, names, p, which=which) if names[0] in mine
                else (lambda after: None) for names, which in ((mixer, "mixer"), (mlp, "mlp"))]
        xl, sv = _layer_fwd(xl, p, late, lay, f"_l{l}")
        saved.append(sv)
        params.append(p)
    dx, loss_row, dfinal = _loss_head(xl, final_norm, target, "loss_head")
    loss = lax.psum(loss_row[0, 0], ("x", "y", "c"))

    out = {n: None for n in _BIG}
    pending = {l: {} for l in range(depth)}

    def sender(l):
        def send(stacks):
            names = tuple(stacks)
            fl, token = _send_start([stacks[n] for n in names], False, me_arr, f"grads_start_{'_'.join(names)}_l{l}")
            pending[l].update(zip(names, fl))
            return token
        return send

    def finish(l, names, after, which):
        got = _send_wait([pending[l][n] for n in names], after, f"grads_wait_{which}_l{l}")
        for n, (mine, landed) in zip(names, got):
            out[n] = _adamw_sharded(mine, landed, me_arr, w[n], m[n], v[n], l, out[n], after, f"adamw_{n}_l{l}")
            after = out[n][0]
        return after

    small_names = _SMALL + _CONV
    small_sent = []

    def small_sender(l):
        def send_small(small, which):
            small = dict(small)
            if which == "main" and l == depth - 1:
                small["final_norm"] = dfinal.reshape(-1)
            names = tuple(small)
            flights, token = _send_start([_pack([small[n] for n in names])], True, me_arr,
                                         f"small_grads_start_{which}_l{l}")
            small_sent.append((l, which, names, [small[n] for n in names], flights[0]))
            return token
        return send_small

    last = [None]

    def hook_for(l):
        def hook(token):
            last[0] = finish(l + 1, _BIG, token, "all") if l + 1 < depth else token
        return hook

    for l in reversed(range(depth)):
        dx = _layer_bwd(dx, params[l], saved[l], sender(l), small_sender(l), hook_for(l), last[0], lay, f"_l{l}")

    last[0] = finish(0, mixer + mlp, last[0], "late")
    got = _send_wait([f for *_, f in small_sent], last[0], "small_grads_wait")
    summed = [{} for _ in range(depth)]
    for (l, which, names, arrays, _), (part, landed) in zip(small_sent, got):
        total = _sum_devices(lax.dynamic_update_slice_in_dim(landed, part[None], me, axis=0),
                             f"sum_small_grads_{which}_l{l}")
        summed[l].update(zip(names, _unpack(total, arrays)))
    grads = {n: jnp.stack([summed[l][n] for l in range(depth)]) for n in small_names}
    grads["final_norm"] = summed[depth - 1]["final_norm"]
    for n in _CONV:
        blk = w[n].shape[2]
        grads[n] = lax.dynamic_slice_in_dim(grads[n], me * blk, blk, axis=2)
    names = small_names + ("final_norm",)
    gp = _pack([grads[n] for n in names])
    upd = _adamw_packed(_pack([w[n] for n in names]), gp, _pack([m[n] for n in names]), _pack([v[n] for n in names]),
                        "adamw_small")
    like = [w[n] for n in names]
    unp = [_unpack(u, like) for u in upd]
    for i, n in enumerate(names):
        out[n] = (grads[n], unp[0][i], unp[1][i], unp[2][i])
    finish(0, ("w_in",), upd[0], "w_in")

    res = [loss, dx.reshape(1, s, d)]
    for k in range(4):
        res += [out[n][k] for n in _ORDER]
    return tuple(res)
```

```python
import functools

import jax
import jax.numpy as jnp
from jax import lax
from jax.experimental import pallas as pl
from jax.experimental.pallas import tpu as pltpu

F32 = jnp.float32
BF = jnp.bfloat16
MESH = pl.DeviceIdType.MESH
N_DEV = 8
N_CHIP = 4

HEADS = 8
HEAD_DIM = 128
GDN_WIDTH = HEADS * HEAD_DIM
CHUNK = 64
CONV_WIDTH = 4
LRU_WIDTH = 1024
LRU_BLOCKS = 8
LRU_BLOCK_DIM = 128
LRU_C = 8.0
RMS_EPS = 1e-6
L2_EPS = 1e-6
ADAM_LR = 0.001
ADAM_B1 = 0.9
ADAM_B2 = 0.999
ADAM_EPS = 1e-08
ADAM_WD = 0.01
ADAM_STEP = 10

LANE = 128
PACK_ROWS = 512
VMEM_LIMIT = 56 * 1024 * 1024


def _cparams(sem=None):
    return pltpu.CompilerParams(dimension_semantics=sem, vmem_limit_bytes=VMEM_LIMIT)


def _tile(n, pref, unit=LANE):
    if n <= pref:
        return n
    best = None
    for t in range(unit, pref + 1, unit):
        if n % t == 0:
            best = t
    assert best is not None, (n, pref, unit)
    return best


_NN = (((1,), (0,)), ((), ()))
_NT = (((1,), (1,)), ((), ()))
_TN = (((0,), (0,)), ((), ()))


def _dg(a, b, dims):
    return lax.dot_general(a.astype(BF), b.astype(BF), dims, preferred_element_type=F32)


class _MM:
    def __init__(self, nn, nt, tn):
        self.nn, self.nt, self.tn = nn, nt, tn


def _make_mm(nn, nt, tn):
    def p_nn(a, b):
        return _dg(a, b, nn)

    def p_nt(a, b):
        return _dg(a, b, nt)

    def p_tn(a, b):
        return _dg(a, b, tn)

    d_nn, d_nt, d_tn = jax.custom_vjp(p_nn), jax.custom_vjp(p_nt), jax.custom_vjp(p_tn)

    def save(f):
        return lambda a, b: (f(a, b), (a, b))

    d_nn.defvjp(save(d_nn), lambda r, g: (d_nt(g, r[1]), d_tn(r[0], g)))
    d_nt.defvjp(save(d_nt), lambda r, g: (d_nn(g, r[1]), d_tn(g, r[0])))
    d_tn.defvjp(save(d_tn), lambda r, g: (d_nt(r[1], g), d_nn(r[0], g)))
    return _MM(p_nn, p_nt, p_tn), _MM(d_nn, d_nt, d_tn)


_BNN = (((2,), (1,)), ((0,), (0,)))
_BNT = (((2,), (2,)), ((0,), (0,)))
_BTN = (((1,), (1,)), ((0,), (0,)))

_PLAIN, _DIFF = _make_mm(_NN, _NT, _TN)
_PLAIN_B, _DIFF_B = _make_mm(_BNN, _BNT, _BTN)


def _sigmoid(x):
    return 1.0 / (1.0 + jnp.exp(-x))


def _silu(x):
    return x * _sigmoid(x)


def _softplus(x):
    return jnp.maximum(x, 0.0) + jnp.log(1.0 + jnp.exp(-jnp.maximum(x, -x)))


def _gelu(x):
    return 0.5 * x * (1.0 + jnp.tanh(0.7978845608028654 * (x + 0.044715 * (x * x * x))))


def _expm1(x):
    t = jnp.tanh(0.5 * x)
    return 2.0 * t / (1.0 - t)


def _rms(x, gain):
    ms = jnp.mean(x * x, axis=-1, keepdims=True)
    return x * lax.rsqrt(ms + RMS_EPS) * gain


def _l2n(x):
    return x * lax.rsqrt(jnp.sum(x * x, axis=-1, keepdims=True) + L2_EPS)


def _shift_down(x, r):
    if r == 0:
        return x
    row = lax.broadcasted_iota(jnp.int32, x.shape, 0)
    return jnp.where(row >= r, pltpu.roll(x, r, 0), 0.0)


def _shift_up(x, r):
    if r == 0:
        return x
    n = x.shape[0]
    row = lax.broadcasted_iota(jnp.int32, x.shape, 0)
    return jnp.where(row < n - r, pltpu.roll(x, n - r, 0), 0.0)


def _conv(x, w_ref):
    y = None
    for j in range(CONV_WIDTH):
        t = _shift_down(x, CONV_WIDTH - 1 - j) * w_ref[j:j + 1, :]
        y = t if y is None else y + t
    return y


def _conv_bwd(x, dy, w_ref):
    dx = None
    dws = []
    for j in range(CONV_WIDTH):
        r = CONV_WIDTH - 1 - j
        t = _shift_up(dy, r) * w_ref[j:j + 1, :]
        dx = t if dx is None else dx + t
        dws.append(jnp.sum(dy * _shift_down(x, r), axis=0, keepdims=True))
    return dx, dws


def _scan_tile(a, b, reverse):
    n = a.shape[0]
    row = lax.broadcasted_iota(jnp.int32, a.shape, 0)
    s = 1
    while s < n:
        if reverse:
            keep = row < n - s
            a_sh = jnp.where(keep, pltpu.roll(a, n - s, 0), 1.0)
            b_sh = jnp.where(keep, pltpu.roll(b, n - s, 0), 0.0)
        else:
            keep = row >= s
            a_sh = jnp.where(keep, pltpu.roll(a, s, 0), 1.0)
            b_sh = jnp.where(keep, pltpu.roll(b, s, 0), 0.0)
        b = a * b_sh + b
        a = a * a_sh
        s *= 2
    return a, b


def _dg3(a, b, dims):
    a_hi = a.astype(BF)
    b_hi = b.astype(BF)
    a_lo = a - a_hi.astype(F32)
    b_lo = b - b_hi.astype(F32)
    return _dg(a_hi, b_hi, dims) + (_dg(a_hi, b_lo, dims) + _dg(a_lo, b_hi, dims))


@jax.custom_vjp
def _tri_inv(a):
    n = a.shape[1]
    p = _dg3(a, a, _BNN)
    r = p
    e = 2
    while 2 * e < n:
        p = _dg3(p, p, _BNN)
        r = r + p + _dg3(r, p, _BNN)
        e *= 2
    row = lax.broadcasted_iota(jnp.int32, (1, n, n), 1)
    col = lax.broadcasted_iota(jnp.int32, (1, n, n), 2)
    eye = jnp.where(row == col, 1.0, 0.0)
    return eye - a + r - _dg3(a, r, _BNN)


def _tri_inv_fwd(a):
    t = _tri_inv(a)
    return t, t


def _tri_inv_bwd(t, g):
    return (-_dg3(_dg3(t, g, _BTN), t, _BNT),)


_tri_inv.defvjp(_tri_inv_fwd, _tri_inv_bwd)


@jax.custom_vjp
def _tri_inv_saved(a, t):
    return t


_tri_inv_saved.defvjp(lambda a, t: (t, t), lambda t, g: (_tri_inv_bwd(t, g)[0], jnp.zeros_like(t)))


def _gdn_chunk(q, k, v, beta, gcol, grow, state, mm, t_saved=None):
    c = q.shape[1]
    row = lax.broadcasted_iota(jnp.int32, (1, c, c), 1)
    col = lax.broadcasted_iota(jnp.int32, (1, c, c), 2)
    causal = row >= col
    strict = row > col
    qn = _l2n(q) * (HEAD_DIM ** -0.5)
    kn = _l2n(k)
    dec = jnp.where(causal, jnp.exp(jnp.where(causal, gcol - grow, 0.0)), 0.0)
    kb = kn * beta
    vb = v * beta
    a = jnp.where(strict, mm.nt(kb, kn) * dec, 0.0)
    t = _tri_inv(a) if t_saved is None else _tri_inv_saved(a, t_saved)
    eg = jnp.exp(gcol)
    u = mm.nn(t, vb)
    w = mm.nn(t, kb * eg)
    p = jnp.where(causal, mm.nt(qn, kn) * dec, 0.0)
    vn = u - mm.nn(w, state)
    o = mm.nn(qn * eg, state) + mm.nn(p, vn)
    last = lax.broadcasted_iota(jnp.int32, (1, c, 1), 1) == c - 1
    gl = jnp.sum(jnp.where(last, gcol, 0.0), axis=1, keepdims=True)
    kd = kn * jnp.exp(gl - gcol)
    new_state = state * jnp.exp(gl) + mm.tn(kd, vn)
    return o, new_state, t


def _lru_gates(xc, wa, ba, wx, bx, lam, mm):
    r = _sigmoid(mm.nn(xc, wa) + ba)
    i = _sigmoid(mm.nn(xc, wx) + bx)
    log_a = -LRU_C * r * _softplus(-lam)
    a = jnp.exp(log_a)
    bterm = jnp.sqrt(-_expm1(2.0 * log_a)) * (i * xc)
    return a, bterm


def _matmul(a, b, mode, name, out_dtypes, epi=None, extras=(), tm=1024, tn=1024, tk=2048, after=None,
            shards=False, tail=False):
    if shards:
        nd, rows, per = b.shape if mode != "tn" else (N_DEV, a.shape[1], b.shape[1] // N_DEV)
        b_shape = b.shape if mode == "tn" else (rows, nd * per)
    else:
        b_shape = b.shape
    if mode == "nn":
        (m, k), (k2, n) = a.shape, b_shape
    elif mode == "nt":
        (m, k), (n, k2) = a.shape, b_shape
    else:
        (k, m), (k2, n) = a.shape, b_shape
    assert k == k2, (a.shape, b.shape, mode)
    if shards:
        tn, tk = (min(tn, per), tk) if mode != "nt" else (tn, min(tk, per))
    tm, tn, tk = _tile(m, tm), _tile(n, tn), _tile(k, tk)
    nk = k // tk
    dims = {"nn": _NN, "nt": _NT, "tn": _TN}[mode]
    if mode == "nn":
        a_spec = pl.BlockSpec((tm, tk), lambda i, j, kk: (i, kk))
        b_spec = pl.BlockSpec((tk, tn), lambda i, j, kk: (kk, j))
        if shards:
            assert per % tn == 0
            b_spec = pl.BlockSpec((None, tk, tn), lambda i, j, kk: (j * tn // per, kk, j * tn % per // tn))
    elif mode == "nt":
        a_spec = pl.BlockSpec((tm, tk), lambda i, j, kk: (i, kk))
        b_spec = pl.BlockSpec((tn, tk), lambda i, j, kk: (j, kk))
        if shards:
            assert per % tk == 0
            b_spec = pl.BlockSpec((None, tn, tk), lambda i, j, kk: (kk * tk // per, j, kk * tk % per // tk))
    else:
        a_spec = pl.BlockSpec((tk, tm), lambda i, j, kk: (kk, i))
        b_spec = pl.BlockSpec((tk, tn), lambda i, j, kk: (kk, j))
    o_spec = pl.BlockSpec((tm, tn), lambda i, j, kk: (i, j))
    out_spec, out_dims = o_spec, (m, n)
    if shards and mode == "tn":
        assert per % tn == 0 and not extras
        out_spec = pl.BlockSpec((None, tm, tn), lambda i, j, kk: (j * tn // per, i, j * tn % per // tn))
        out_dims = (N_DEV, m, per)
    n_ex, n_out = len(extras), len(out_dtypes)
    order = [] if after is None else [after]
    n_in = 2 + n_ex + len(order)
    if epi is None:
        epi = lambda acc: (acc,)

    def body(*refs):
        a_ref, b_ref = refs[0], refs[1]
        ex_refs = refs[2:2 + n_ex]
        out_refs = refs[n_in:n_in + n_out]
        kk = pl.program_id(2)
        part = _dg(a_ref[...], b_ref[...], dims)

        def finish(total):
            res = epi(total, *[r[...] for r in ex_refs])
            for r, v in zip(out_refs, res):
                r[...] = v.astype(r.dtype)
            if tail:
                @pl.when(pl.program_id(1) == n // tn - 1)
                def _():
                    refs[n_in + n_out][...] = total[:, tn - LANE:]

        if nk == 1:
            finish(part)
            return
        acc_ref = refs[-1]

        @pl.when(kk == 0)
        def _():
            acc_ref[...] = part

        @pl.when(jnp.logical_and(kk > 0, kk < nk - 1))
        def _():
            acc_ref[...] += part

        @pl.when(kk == nk - 1)
        def _():
            finish(acc_ref[...] + part)

    outs = pl.pallas_call(
        body, name=name, grid=(m // tm, n // tn, nk),
        in_specs=[a_spec, b_spec] + [o_spec] * n_ex + [pl.BlockSpec(memory_space=pl.ANY)] * len(order),
        out_specs=[out_spec] * n_out + [pl.BlockSpec((tm, LANE), lambda i, j, kk: (i, 0))] * tail,
        out_shape=[jax.ShapeDtypeStruct(out_dims, d) for d in out_dtypes]
        + [jax.ShapeDtypeStruct((m, LANE), F32)] * tail,
        scratch_shapes=[pltpu.VMEM((tm, tn), F32)] if nk > 1 else [],
        compiler_params=_cparams(("parallel", "arbitrary" if tail else "parallel", "arbitrary")),
    )(a, b, *extras, *order)
    return outs[0] if n_out + tail == 1 else outs


def _rowwise(fn, name, rows, tr, row_ins, params, row_outs, acc_outs=(), ncol=1):
    tr = _tile(rows, tr, 8)
    n_in, n_par, n_ro, n_acc = len(row_ins), len(params), len(row_outs), len(acc_outs)
    in_specs = []
    for _, width, base, per_col in row_ins:
        if per_col:
            in_specs.append(pl.BlockSpec((tr, width), lambda i, j, base=base: (i, base + j)))
        else:
            in_specs.append(pl.BlockSpec((tr, width), lambda i, j, base=base: (i, base)))
    for p in params:
        in_specs.append(pl.BlockSpec(p.shape, lambda i, j: (0, 0)))
    out_specs, out_shape = [], []
    for total, dtype, width, per_col in row_outs:
        if per_col:
            out_specs.append(pl.BlockSpec((tr, width), lambda i, j: (i, j)))
        else:
            out_specs.append(pl.BlockSpec((tr, width), lambda i, j: (i, 0)))
        out_shape.append(jax.ShapeDtypeStruct((rows, total), dtype))
    for shape, dtype in acc_outs:
        out_specs.append(pl.BlockSpec(shape, lambda i, j: (0, 0)))
        out_shape.append(jax.ShapeDtypeStruct(shape, dtype))

    def body(*refs):
        ins = [r[...].astype(F32) for r in refs[:n_in + n_par]]
        ro = refs[n_in + n_par:n_in + n_par + n_ro]
        ao = refs[n_in + n_par + n_ro:]
        res = fn(*ins)
        for r, v in zip(ro, res[:n_ro]):
            r[...] = v.astype(r.dtype)
        first = jnp.logical_and(pl.program_id(0) == 0, pl.program_id(1) == 0)

        @pl.when(first)
        def _():
            for r in ao:
                r[...] = jnp.zeros_like(r)

        for r, v in zip(ao, res[n_ro:]):
            r[...] += v.astype(r.dtype)

    sem = ("arbitrary", "arbitrary") if n_acc else ("parallel", "parallel")
    outs = pl.pallas_call(
        body, name=name, grid=(rows // tr, ncol), in_specs=in_specs, out_specs=out_specs,
        out_shape=out_shape, compiler_params=_cparams(sem),
    )(*[r[0] for r in row_ins], *params)
    return outs


def _row(x):
    return x.reshape(1, -1)


def _rms_fwd(x, gain, name):
    s, d = x.shape
    return _rowwise(lambda xt, g: (_rms(xt, g),), name, s, 512, [(x, d, 0, False)], [_row(gain)],
                    [(d, BF, d, False)])[0]


def _rms_bwd(x, gain, dh, dres, name):
    s, d = x.shape

    def fn(xt, dht, drt, g):
        _, vjp = jax.vjp(_rms, xt, g)
        dx, dg = vjp(dht)
        return drt + dx, dg

    return _rowwise(fn, name, s, 256, [(x, d, 0, False), (dh, d, 0, False), (dres, d, 0, False)], [_row(gain)],
                    [(d, F32, d, False)], [((1, d), F32)])


def _loss_head(x, gain, target, name):
    s, d = x.shape

    def fn(xt, tt, g):
        y, vjp = jax.vjp(_rms, xt, g)
        err = y - tt
        dx, dg = vjp(err * (1.0 / d))
        part = 0.5 * jnp.sum(jnp.sum(err * err, axis=1, keepdims=True), axis=0, keepdims=True) * (1.0 / d)
        return dx, jnp.broadcast_to(part, (1, LANE)), dg

    return _rowwise(fn, name, s, 256, [(x, d, 0, False), (target, d, 0, False)], [_row(gain)],
                    [(d, F32, d, False)], [((1, LANE), F32), ((1, d), F32)])


def _gdn_post_fn(o, z, g):
    return _rms(o, g) * _silu(z)


def _gdn_post_fwd(o, proj, gain, z_base, name):
    s = o.shape[0]
    return _rowwise(lambda ot, zt, g: (_gdn_post_fn(ot, zt, g),), name, s, 1024,
                    [(o, HEAD_DIM, 0, True), (proj, HEAD_DIM, z_base, True)], [_row(gain)],
                    [(GDN_WIDTH, BF, HEAD_DIM, True)], ncol=HEADS)[0]


def _gdn_post_bwd(o, proj, gain, dog, z_base, name):
    s = o.shape[0]

    def fn(ot, zt, dt, g):
        _, vjp = jax.vjp(_gdn_post_fn, ot, zt, g)
        return vjp(dt)

    return _rowwise(fn, name, s, 1024,
                    [(o, HEAD_DIM, 0, True), (proj, HEAD_DIM, z_base, True), (dog, HEAD_DIM, 0, True)],
                    [_row(gain)], [(GDN_WIDTH, F32, HEAD_DIM, True), (GDN_WIDTH, BF, HEAD_DIM, True)],
                    [((1, HEAD_DIM), F32)], ncol=HEADS)


def _merge_fn(gg, gl, yg, yl):
    return _sigmoid(gg) * yg + _sigmoid(gl) * yl


def _merge_fwd(proj, yg, yl, gg_base, name):
    s, d = yg.shape
    return _rowwise(lambda a, b, c, e: (_merge_fn(a, b, c, e),), name, s, 256,
                    [(proj, d, gg_base, False), (proj, d, gg_base + 1, False), (yg, d, 0, False), (yl, d, 0, False)],
                    [], [(d, BF, d, False)])[0]


def _merge_bwd(proj, yg, yl, dmerged, gg_base, name):
    s, d = yg.shape

    def fn(a, b, c, e, dm):
        _, vjp = jax.vjp(_merge_fn, a, b, c, e)
        return vjp(dm)

    return _rowwise(fn, name, s, 128,
                    [(proj, d, gg_base, False), (proj, d, gg_base + 1, False), (yg, d, 0, False), (yl, d, 0, False),
                     (dmerged, d, 0, False)], [], [(d, BF, d, False)] * 4)


def _conv_silu_fwd(proj, w, name):
    s = proj.shape[0]
    cols = w.shape[1]
    cw = _tile(cols, LANE)

    def body(x_ref, w_ref, o_ref):
        o_ref[...] = _silu(_conv(x_ref[...].astype(F32), w_ref))

    return pl.pallas_call(
        body, name=name, grid=(cols // cw,),
        in_specs=[pl.BlockSpec((s, cw), lambda j: (0, j)), pl.BlockSpec((CONV_WIDTH, cw), lambda j: (0, j))],
        out_specs=pl.BlockSpec((s, cw), lambda j: (0, j)),
        out_shape=jax.ShapeDtypeStruct((s, cols), F32), compiler_params=_cparams(("parallel",)),
    )(proj, w)


def _conv_silu_bwd(proj, w, dout, name):
    s = proj.shape[0]
    cols = w.shape[1]
    cw = _tile(cols, LANE)

    def body(x_ref, w_ref, d_ref, dx_ref, dw_ref):
        x = x_ref[...].astype(F32)
        y = _conv(x, w_ref)
        sg = _sigmoid(y)
        dy = d_ref[...] * (sg * (1.0 + y * (1.0 - sg)))
        dx, dws = _conv_bwd(x, dy, w_ref)
        dx_ref[...] = dx.astype(dx_ref.dtype)
        for j in range(CONV_WIDTH):
            dw_ref[j:j + 1, :] = dws[j]

    return pl.pallas_call(
        body, name=name, grid=(cols // cw,),
        in_specs=[pl.BlockSpec((s, cw), lambda j: (0, j)), pl.BlockSpec((CONV_WIDTH, cw), lambda j: (0, j)),
                  pl.BlockSpec((s, cw), lambda j: (0, j))],
        out_specs=[pl.BlockSpec((s, cw), lambda j: (0, j)), pl.BlockSpec((CONV_WIDTH, cw), lambda j: (0, j))],
        out_shape=[jax.ShapeDtypeStruct((s, cols), BF), jax.ShapeDtypeStruct((CONV_WIDTH, cols), F32)],
        compiler_params=_cparams(("parallel",)),
    )(proj, w, dout)


def _cumsum_mask(tr, transpose):
    row = lax.broadcasted_iota(jnp.int32, (tr, tr), 0)
    col = lax.broadcasted_iota(jnp.int32, (tr, tr), 1)
    same = (row // CHUNK) == (col // CHUNK)
    tri = (row <= col) if transpose else (row >= col)
    return jnp.where(jnp.logical_and(same, tri), 1.0, 0.0)


def _lane_pad(v):
    return jnp.pad(v.reshape(1, -1), ((0, 0), (0, LANE - v.shape[0])))


def _gdn_gates_fwd(proj, a_log, dt_bias, ab_base, name):
    s = proj.shape[0]

    def fn(ab, al, dt):
        g = -jnp.exp(al) * _softplus(ab + dt)
        beta = _sigmoid(pltpu.roll(ab, LANE - HEADS, 1))
        gc = jnp.dot(_cumsum_mask(ab.shape[0], False), g, precision=lax.Precision.HIGHEST,
                     preferred_element_type=F32)
        return gc, beta

    return _rowwise(fn, name, s, 512, [(proj, LANE, ab_base, False)], [_lane_pad(a_log), _lane_pad(dt_bias)],
                    [(LANE, F32, LANE, False)] * 2)


def _gdn_gates_bwd(proj, a_log, dt_bias, dgc_col, dgc_row, dbeta, ab_base, name):
    s = proj.shape[0]

    def fn(ab, dc, dr, db, al, dt):
        lane = lax.broadcasted_iota(jnp.int32, ab.shape, 1)
        dg = jnp.dot(_cumsum_mask(ab.shape[0], True), dc + dr, precision=lax.Precision.HIGHEST,
                     preferred_element_type=F32)
        ea = jnp.exp(al)
        pre = ab + dt
        g = -ea * _softplus(pre)
        da = jnp.where(lane < HEADS, dg * (-ea) * _sigmoid(pre), 0.0)
        dal = jnp.sum(jnp.where(lane < HEADS, dg * g, 0.0), axis=0, keepdims=True)
        ddt = jnp.sum(da, axis=0, keepdims=True)
        beta = _sigmoid(pltpu.roll(ab, LANE - HEADS, 1))
        dbl = jnp.where(lane < HEADS, db * beta * (1.0 - beta), 0.0)
        dab = da + pltpu.roll(dbl, HEADS, 1)
        return dab, dal, ddt

    return _rowwise(fn, name, s, 512,
                    [(proj, LANE, ab_base, False), (dgc_col, LANE, 0, False), (dgc_row, LANE, 0, False),
                     (dbeta, LANE, 0, False)], [_lane_pad(a_log), _lane_pad(dt_bias)],
                    [(LANE, BF, LANE, False)], [((1, LANE), F32), ((1, LANE), F32)])


def _head_col(blk, h):
    lane = lax.broadcasted_iota(jnp.int32, blk.shape, 1)
    return jnp.sum(jnp.where(lane == h, blk, 0.0), axis=1, keepdims=True)


def _chunk_operands(qkv_ref, gc_ref, b_ref, gr_ref):
    heads = range(HEADS)
    q, k, v = (jnp.stack([qkv_ref[:, (j * HEADS + h) * HEAD_DIM:(j * HEADS + h + 1) * HEAD_DIM] for h in heads])
               for j in range(3))
    gcs, betas = gc_ref[...], b_ref[...]
    beta = jnp.stack([_head_col(betas, h) for h in heads])
    gcol = jnp.stack([_head_col(gcs, h) for h in heads])
    grow = jnp.stack([gr_ref[0, h:h + 1, :] for h in heads])
    return q, k, v, beta, gcol, grow


def _gdn_chunk_fwd(qkv, gc, beta, grow, name):
    s = qkv.shape[0]
    nc = s // CHUNK

    def body(qkv_ref, gc_ref, b_ref, gr_ref, o_ref, st_ref, t_ref, state):
        @pl.when(pl.program_id(0) == 0)
        def _():
            state[...] = jnp.zeros_like(state)

        st = state[...]
        st_ref[0] = st
        o, new, t = _gdn_chunk(*_chunk_operands(qkv_ref, gc_ref, b_ref, gr_ref), st, _PLAIN_B)
        t_ref[0] = t
        for h in range(HEADS):
            o_ref[:, h * HEAD_DIM:(h + 1) * HEAD_DIM] = o[h]
        state[...] = new

    return pl.pallas_call(
        body, name=name, grid=(nc,),
        in_specs=[pl.BlockSpec((CHUNK, 3 * GDN_WIDTH), lambda c: (c, 0)),
                  pl.BlockSpec((CHUNK, LANE), lambda c: (c, 0)), pl.BlockSpec((CHUNK, LANE), lambda c: (c, 0)),
                  pl.BlockSpec((1, HEADS, CHUNK), lambda c: (c, 0, 0))],
        out_specs=[pl.BlockSpec((CHUNK, GDN_WIDTH), lambda c: (c, 0)),
                   pl.BlockSpec((1, HEADS, HEAD_DIM, HEAD_DIM), lambda c: (c, 0, 0, 0)),
                   pl.BlockSpec((1, HEADS, CHUNK, CHUNK), lambda c: (c, 0, 0, 0))],
        out_shape=[jax.ShapeDtypeStruct((s, GDN_WIDTH), F32),
                   jax.ShapeDtypeStruct((nc, HEADS, HEAD_DIM, HEAD_DIM), F32),
                   jax.ShapeDtypeStruct((nc, HEADS, CHUNK, CHUNK), F32)],
        scratch_shapes=[pltpu.VMEM((HEADS, HEAD_DIM, HEAD_DIM), F32)],
        compiler_params=_cparams(("arbitrary",)),
    )(qkv, gc, beta, grow)


def _gdn_chunk_bwd(qkv, gc, beta, grow, states, tinv, do, name):
    s = qkv.shape[0]
    nc = s // CHUNK

    def body(qkv_ref, gc_ref, b_ref, gr_ref, st_ref, t_ref, do_ref, dqkv_ref, dgc_ref, db_ref, dgr_ref, dstate):
        @pl.when(pl.program_id(0) == 0)
        def _():
            dstate[...] = jnp.zeros_like(dstate)

        lane = lax.broadcasted_iota(jnp.int32, (CHUNK, LANE), 1)
        chunk = lambda *args: _gdn_chunk(*args, mm=_DIFF_B, t_saved=t_ref[0])[:2]
        _, vjp = jax.vjp(chunk, *_chunk_operands(qkv_ref, gc_ref, b_ref, gr_ref), st_ref[0])
        do = jnp.stack([do_ref[:, h * HEAD_DIM:(h + 1) * HEAD_DIM] for h in range(HEADS)])
        dq, dk, dv, dbeta, dgcol, dgrow, dst = vjp((do, dstate[...]))
        dstate[...] = dst
        dgc = jnp.zeros((CHUNK, LANE), F32)
        db = jnp.zeros((CHUNK, LANE), F32)
        for h in range(HEADS):
            for j, dx in enumerate((dq, dk, dv)):
                dqkv_ref[:, (j * HEADS + h) * HEAD_DIM:(j * HEADS + h + 1) * HEAD_DIM] = dx[h]
            dgc = dgc + jnp.where(lane == h, dgcol[h], 0.0)
            db = db + jnp.where(lane == h, dbeta[h], 0.0)
            dgr_ref[0, h:h + 1, :] = dgrow[h]
        dgc_ref[...] = dgc
        db_ref[...] = db

    rev = lambda c: nc - 1 - c
    wide = pl.BlockSpec((CHUNK, 3 * GDN_WIDTH), lambda c: (rev(c), 0))
    lanes = pl.BlockSpec((CHUNK, LANE), lambda c: (rev(c), 0))
    rows = pl.BlockSpec((1, HEADS, CHUNK), lambda c: (rev(c), 0, 0))
    return pl.pallas_call(
        body, name=name, grid=(nc,),
        in_specs=[wide, lanes, lanes, rows,
                  pl.BlockSpec((1, HEADS, HEAD_DIM, HEAD_DIM), lambda c: (rev(c), 0, 0, 0)),
                  pl.BlockSpec((1, HEADS, CHUNK, CHUNK), lambda c: (rev(c), 0, 0, 0)),
                  pl.BlockSpec((CHUNK, GDN_WIDTH), lambda c: (rev(c), 0))],
        out_specs=[wide, lanes, lanes, rows],
        out_shape=[jax.ShapeDtypeStruct((s, 3 * GDN_WIDTH), F32)]
        + [jax.ShapeDtypeStruct((s, LANE), F32)] * 2 + [jax.ShapeDtypeStruct((nc, HEADS, CHUNK), F32)],
        scratch_shapes=[pltpu.VMEM((HEADS, HEAD_DIM, HEAD_DIM), F32)],
        compiler_params=_cparams(("arbitrary",)),
    )(qkv, gc, beta, grow, states, tinv, do)


def _lru_specs(s, xb_base):
    col = lambda base: pl.BlockSpec((s, LRU_BLOCK_DIM), lambda n, base=base: (0, base + n))
    vec = pl.BlockSpec((1, LRU_BLOCK_DIM), lambda n: (0, n))
    mat = pl.BlockSpec((None, LRU_BLOCK_DIM, LRU_BLOCK_DIM), lambda n: (n, 0, 0))
    cw = pl.BlockSpec((CONV_WIDTH, LRU_BLOCK_DIM), lambda n: (0, n))
    return col, vec, mat, cw


def _lru_fwd(proj, cw, cb, wa, ba, wx, bx, lam, xb_base, name):
    s = proj.shape[0]
    tt = _tile(s, 256, 8)
    col, vec, mat, cws = _lru_specs(s, xb_base)

    def body(xb_ref, yb_ref, cw_ref, cb_ref, wa_ref, ba_ref, wx_ref, bx_ref, lam_ref, o_ref, h_ref, a_ref, xc_s):
        xc_s[...] = _conv(xb_ref[...].astype(F32), cw_ref) + cb_ref[...]
        par = (wa_ref[...], ba_ref[...], wx_ref[...], bx_ref[...], lam_ref[...])

        def step(t, carry):
            sl = pl.ds(pl.multiple_of(t * tt, tt), tt)
            a, bt = _lru_gates(xc_s[sl, :], *par, _PLAIN)
            a_ref[sl, :] = a
            aa, bb = _scan_tile(a, bt, False)
            h = aa * carry + bb
            h_ref[sl, :] = h
            o_ref[sl, :] = (h * _gelu(yb_ref[sl, :].astype(F32))).astype(o_ref.dtype)
            return h_ref[pl.ds(t * tt + tt - 1, 1), :]

        lax.fori_loop(0, s // tt, step, jnp.zeros((1, LRU_BLOCK_DIM), F32))

    nb = LRU_BLOCKS
    return pl.pallas_call(
        body, name=name, grid=(LRU_BLOCKS,),
        in_specs=[col(xb_base), col(xb_base + nb), cws, vec, mat, vec, mat, vec, vec],
        out_specs=[col(0), col(0), col(0)],
        out_shape=[jax.ShapeDtypeStruct((s, LRU_WIDTH), BF), jax.ShapeDtypeStruct((s, LRU_WIDTH), F32),
                   jax.ShapeDtypeStruct((s, LRU_WIDTH), F32)],
        scratch_shapes=[pltpu.VMEM((s, LRU_BLOCK_DIM), F32)],
        compiler_params=_cparams(("parallel",)),
    )(proj, proj, cw, _row(cb), wa, _row(ba), wx, _row(bx), _row(lam))


def _lru_bwd(proj, hs, decay, dol, cw, cb, wa, ba, wx, bx, lam, xb_base, name):
    s = proj.shape[0]
    tt = _tile(s, 256, 8)
    nt = s // tt
    col, vec, mat, cws = _lru_specs(s, xb_base)

    def body(xb_ref, yb_ref, h_ref, a_ref, d_ref, cw_ref, cb_ref, wa_ref, ba_ref, wx_ref, bx_ref, lam_ref,
             dxb_ref, dyb_ref, dcw_ref, dcb_ref, dwa_ref, dba_ref, dwx_ref, dbx_ref, dlam_ref,
             xc_s, dh_s, dxc_s):
        xc_s[...] = _conv(xb_ref[...].astype(F32), cw_ref) + cb_ref[...]
        par = (wa_ref[...], ba_ref[...], wx_ref[...], bx_ref[...], lam_ref[...])
        row = lax.broadcasted_iota(jnp.int32, (tt, LRU_BLOCK_DIM), 0)
        tile = lambda t: pl.ds(pl.multiple_of(t * tt, tt), tt)

        def prep(t, carry):
            sl = tile(t)
            d = d_ref[sl, :].astype(F32)
            gy, vjp_y = jax.vjp(_gelu, yb_ref[sl, :].astype(F32))
            dyb_ref[sl, :] = vjp_y(d * h_ref[sl, :])[0].astype(dyb_ref.dtype)
            dh_s[sl, :] = d * gy
            return carry

        lax.fori_loop(0, nt, prep, 0)

        def rscan(i, carry):
            dh_next, a_next = carry
            t = nt - 1 - i
            sl = tile(t)
            a_sh = jnp.where(row == tt - 1, a_next, pltpu.roll(a_ref[sl, :], tt - 1, 0))
            aa, bb = _scan_tile(a_sh, dh_s[sl, :], True)
            dh_s[sl, :] = aa * dh_next + bb
            first = pl.ds(t * tt, 1)
            return dh_s[first, :], a_ref[first, :]

        zero = jnp.zeros((1, LRU_BLOCK_DIM), F32)
        lax.fori_loop(0, nt, rscan, (zero, zero))

        def gates_vjp(t, acc):
            sl = tile(t)
            _, vjp_g = jax.vjp(functools.partial(_lru_gates, mm=_DIFF), xc_s[sl, :], *par)
            dh = dh_s[sl, :]
            before = jnp.where(t > 0, h_ref[pl.ds(jnp.maximum(t * tt - 1, 0), 1), :], 0.0)
            h_prev = jnp.where(row == 0, before, pltpu.roll(h_ref[sl, :], 1, 0))
            dxc, *dpar = vjp_g((dh * h_prev, dh))
            dxc_s[sl, :] = dxc
            return tuple(x + y for x, y in zip(acc, dpar))

        dwa, dba, dwx, dbx, dlam = lax.fori_loop(0, nt, gates_vjp, tuple(jnp.zeros_like(p) for p in par))
        dwa_ref[...] = dwa
        dba_ref[...] = dba
        dwx_ref[...] = dwx
        dbx_ref[...] = dbx
        dlam_ref[...] = dlam
        dxc = dxc_s[...]
        dcb_ref[...] = jnp.sum(dxc, axis=0, keepdims=True)
        dxb, dws = _conv_bwd(xb_ref[...].astype(F32), dxc, cw_ref)
        dxb_ref[...] = dxb.astype(dxb_ref.dtype)
        for j in range(CONV_WIDTH):
            dcw_ref[j:j + 1, :] = dws[j]

    nb = LRU_BLOCKS
    w = LRU_WIDTH
    return pl.pallas_call(
        body, name=name, grid=(LRU_BLOCKS,),
        in_specs=[col(xb_base), col(xb_base + nb), col(0), col(0), col(0), cws, vec, mat, vec, mat, vec, vec],
        out_specs=[col(0), col(0), cws, vec, mat, vec, mat, vec, vec],
        out_shape=[jax.ShapeDtypeStruct((s, w), BF), jax.ShapeDtypeStruct((s, w), BF),
                   jax.ShapeDtypeStruct((CONV_WIDTH, w), F32), jax.ShapeDtypeStruct((1, w), F32),
                   jax.ShapeDtypeStruct(wa.shape, F32), jax.ShapeDtypeStruct((1, w), F32),
                   jax.ShapeDtypeStruct(wx.shape, F32), jax.ShapeDtypeStruct((1, w), F32),
                   jax.ShapeDtypeStruct((1, w), F32)],
        scratch_shapes=[pltpu.VMEM((s, LRU_BLOCK_DIM), F32)] * 3,
        compiler_params=_cparams(("parallel",)),
    )(proj, proj, hs, decay, dol, cw, _row(cb), wa, _row(ba), wx, _row(bx), _row(lam))


def _place():
    return lax.axis_index("x"), lax.axis_index("y"), lax.axis_index("c")


def _all_gather(blocks, name):
    n = len(blocks)
    per = N_DEV - 1

    def body(*refs):
        ins, outs = refs[:n], refs[n:2 * n]
        send_sems, recv_sems, local_sems = refs[2 * n:]
        x, y, c = _place()
        me, sibling = (x, y, c), (x, y, 1 - c)
        chips = [(1 - x, y), (x, 1 - y), (1 - x, 1 - y)]

        def copy(a, k, block, to, src=None):
            dst = outs[a].at[4 * block[0] + 2 * block[1] + block[2]]
            return pltpu.make_async_remote_copy(
                src_ref=dst if src is None else src, dst_ref=dst, send_sem=send_sems.at[a * per + k],
                recv_sem=recv_sems.at[a * per + k], device_id=to, device_id_type=MESH)

        started = []
        for a in range(n):
            mine = pltpu.make_async_copy(ins[a], outs[a].at[4 * x + 2 * y + c], local_sems.at[a])
            mine.start()
            started.append(mine)
        sends = []
        for a in range(n):
            first = [copy(a, 0, me, sibling, src=ins[a])]
            first += [copy(a, 1 + j, me, (*chip, c), src=ins[a]) for j, chip in enumerate(chips)]
            for cp in first:
                cp.start()
            sends += first
        for a in range(n):
            for j, chip in enumerate(chips):
                copy(a, 1 + j, (*chip, c), me).wait_recv()
                passed = copy(a, 4 + j, (*chip, c), sibling)
                passed.start()
                sends.append(passed)
        for a in range(n):
            copy(a, 0, sibling, me).wait_recv()
            for j, chip in enumerate(chips):
                copy(a, 4 + j, (*chip, 1 - c), me).wait_recv()
        for cp in sends:
            cp.wait_send()
        for cp in started:
            cp.wait()

    any_spec = pl.BlockSpec(memory_space=pl.ANY)
    return pl.pallas_call(
        body, name=name, in_specs=[any_spec] * n, out_specs=[any_spec] * n,
        out_shape=[jax.ShapeDtypeStruct((N_DEV,) + b.shape, b.dtype) for b in blocks],
        scratch_shapes=[pltpu.SemaphoreType.DMA((n * per,)), pltpu.SemaphoreType.DMA((n * per,)),
                        pltpu.SemaphoreType.DMA((n,))],
    )(*blocks)


_HBM = pl.BlockSpec(memory_space=pltpu.HBM)
_SEM = pl.BlockSpec(memory_space=pltpu.SEMAPHORE)
_ANY = pl.BlockSpec(memory_space=pl.ANY)
_EFFECT = pltpu.SideEffectType.DATAFLOW_SIDE_EFFECTING


def _peers(x, y, c):
    return [(x, y, 1 - c), (1 - x, y, c), (1 - x, y, 1 - c), (x, 1 - y, c), (x, 1 - y, 1 - c),
            (1 - x, 1 - y, c), (1 - x, 1 - y, 1 - c)]


class _InFlight:
    def __init__(self, send_sem, recv_sem, src, land):
        self.send_sem, self.recv_sem, self.src, self.land = send_sem, recv_sem, src, land


def _send_start(srcs, whole, after, name):
    n = len(srcs)
    lands = [lax.empty(((N_DEV,) + s.shape) if whole else s.shape, s.dtype) for s in srcs]

    def body(*refs):
        src_refs, land_refs = refs[:n], refs[n:2 * n]
        outs = refs[2 * n + 1:]
        send_sems, recv_sems, token = outs[:n], outs[n:2 * n], outs[-1]
        x, y, c = _place()
        me = 4 * x + 2 * y + c
        for a in range(n):
            for p in _peers(x, y, c):
                src = src_refs[a] if whole else src_refs[a].at[4 * p[0] + 2 * p[1] + p[2]]
                pltpu.make_async_remote_copy(src_ref=src, dst_ref=land_refs[a].at[me], send_sem=send_sems[a],
                                             recv_sem=recv_sems[a], device_id=p, device_id_type=MESH).start()
        token[...] = jnp.zeros_like(token)

    hbm = lambda a: pltpu.HBM(a.shape, a.dtype)
    outs = pl.pallas_call(
        body, name=name,
        out_shape=[pltpu.SemaphoreType.DMA(())] * (2 * n) + [hbm(s) for s in srcs] + [hbm(l) for l in lands]
        + [jax.ShapeDtypeStruct((8, LANE), F32)],
        in_specs=[_HBM] * (2 * n) + [_ANY],
        out_specs=[_SEM] * (2 * n) + [_HBM] * (2 * n) + [pl.BlockSpec(memory_space=pltpu.VMEM)],
        input_output_aliases={i: 2 * n + i for i in range(2 * n)},
        compiler_params=pltpu.CompilerParams(has_side_effects=_EFFECT),
    )(*[pltpu.with_memory_space_constraint(s, pltpu.HBM) for s in srcs],
      *[pltpu.with_memory_space_constraint(l, pltpu.HBM) for l in lands], after)
    flights = [_InFlight(outs[a], outs[n + a], outs[2 * n + a], outs[3 * n + a]) for a in range(n)]
    return flights, outs[-1]


def _send_wait(flights, after, name, fill=False):
    n = len(flights)

    def body(*refs):
        x, y, c = _place()
        local = []
        if fill:
            for a in range(n):
                own = pltpu.make_async_copy(refs[4 * a], refs[4 * a + 1].at[4 * x + 2 * y + c], refs[-1].at[a])
                own.start()
                local.append(own)
        for a in range(n):
            land_ref, send_sem, recv_sem = refs[4 * a + 1], refs[4 * a + 2], refs[4 * a + 3]
            seven = land_ref.at[pl.ds(0, N_DEV - 1)]
            copy = pltpu.make_async_remote_copy(src_ref=seven, dst_ref=seven, send_sem=send_sem, recv_sem=recv_sem,
                                                device_id=(x, y, 1 - c), device_id_type=MESH)
            copy.wait_send()
            copy.wait_recv()
        for own in local:
            own.wait()

    ins, out_shape = [], []
    for f in flights:
        ins += [f.src, f.land, f.send_sem, f.recv_sem]
        out_shape += [pltpu.HBM(f.src.shape, f.src.dtype), pltpu.HBM(f.land.shape, f.land.dtype)]
    outs = pl.pallas_call(
        body, name=name, out_shape=out_shape,
        in_specs=[_HBM, _HBM, _SEM, _SEM] * n + [_ANY], out_specs=[_HBM] * (2 * n),
        input_output_aliases={4 * a + i: 2 * a + i for a in range(n) for i in range(2)},
        scratch_shapes=[pltpu.SemaphoreType.DMA((n,))] if fill else [],
        compiler_params=pltpu.CompilerParams(has_side_effects=_EFFECT),
    )(*ins, after)
    return [(outs[2 * a], outs[2 * a + 1]) for a in range(n)]


def _adamw_math(w, g, m, v):
    m = ADAM_B1 * m + (1.0 - ADAM_B1) * g
    v = ADAM_B2 * v + (1.0 - ADAM_B2) * (g * g)
    m_hat = m / (1.0 - ADAM_B1 ** ADAM_STEP)
    v_hat = v / (1.0 - ADAM_B2 ** ADAM_STEP)
    delta = -ADAM_LR * (m_hat / (jnp.sqrt(v_hat) + ADAM_EPS) + ADAM_WD * w)
    return delta, m, v


def _adamw_sharded(mine, landed, me, w, m, v, layer, prev, after, name):
    depth, r, c = w.shape
    tr = _tile(r, 128, 8)
    n_prev = 0 if prev is None else 4

    def body(*refs):
        me_ref, own_ref, land_ref, w_ref, m_ref, v_ref = refs[:6]
        g_ref, d_ref, nm_ref, nv_ref = refs[7 + n_prev:]
        g = None
        for k in range(N_DEV):
            part = jnp.where(me_ref[0] == k, own_ref[0], land_ref[k]).astype(F32)
            g = part if g is None else g + part
        delta, nm, nv = _adamw_math(w_ref[...], g, m_ref[...], v_ref[...])
        g_ref[...] = g
        d_ref[...] = delta
        nm_ref[...] = nm
        nv_ref[...] = nv

    lay = pl.BlockSpec((None, tr, c), lambda i, me_ref: (layer, i, 0))
    return pl.pallas_call(
        body, name=name,
        grid_spec=pltpu.PrefetchScalarGridSpec(
            num_scalar_prefetch=1, grid=(r // tr,),
            in_specs=[pl.BlockSpec((1, tr, c), lambda i, me_ref: (me_ref[0], i, 0)),
                      pl.BlockSpec((N_DEV, tr, c), lambda i, me_ref: (0, i, 0)), lay, lay, lay]
            + [_ANY] * (1 + n_prev),
            out_specs=[lay] * 4),
        out_shape=[jax.ShapeDtypeStruct(w.shape, F32)] * 4,
        input_output_aliases={7 + i: i for i in range(n_prev)},
        compiler_params=_cparams(("parallel",)),
    )(me, mine, landed, w, m, v, after, *(prev or ()))


def _sum_devices(stack, name):
    _, r, c = stack.shape
    tr = _tile(r, PACK_ROWS, 8)

    def body(s_ref, o_ref):
        g = s_ref[0]
        for j in range(1, N_DEV):
            g = g + s_ref[j]
        o_ref[...] = g

    return pl.pallas_call(
        body, name=name, grid=(r // tr,), in_specs=[pl.BlockSpec((N_DEV, tr, c), lambda i: (0, i, 0))],
        out_specs=pl.BlockSpec((tr, c), lambda i: (i, 0)), out_shape=jax.ShapeDtypeStruct((r, c), F32),
        compiler_params=_cparams(("parallel",)),
    )(stack)


def _adamw_packed(w, g, m, v, name):
    r, c = w.shape
    tr = _tile(r, PACK_ROWS, 8)

    def body(w_ref, g_ref, m_ref, v_ref, d_ref, nm_ref, nv_ref):
        delta, nm, nv = _adamw_math(w_ref[...], g_ref[...], m_ref[...], v_ref[...])
        d_ref[...] = delta
        nm_ref[...] = nm
        nv_ref[...] = nv

    spec = pl.BlockSpec((tr, c), lambda i: (i, 0))
    return pl.pallas_call(
        body, name=name, grid=(r // tr,), in_specs=[spec] * 4, out_specs=[spec] * 3,
        out_shape=[jax.ShapeDtypeStruct((r, c), F32)] * 3, compiler_params=_cparams(("parallel",)),
    )(w, g, m, v)


def _pack(arrays):
    unit = 8 * LANE
    flat = []
    for a in arrays:
        f = a.reshape(-1)
        flat.append(jnp.pad(f, (0, (-f.shape[0]) % unit)))
    total = sum(f.shape[0] for f in flat)
    flat.append(jnp.zeros(((-total) % (PACK_ROWS * LANE),), flat[0].dtype))
    return jnp.concatenate(flat).reshape(-1, LANE)


def _unpack(packed, like):
    unit = 8 * LANE
    flat = packed.reshape(-1)
    out, off = [], 0
    for a in like:
        n = 1
        for d in a.shape:
            n *= d
        out.append(flat[off:off + n].reshape(a.shape))
        off += n + (-n) % unit
    return out


class _Layout:
    def __init__(self, d):
        self.d = d
        self.n_front = 4 * GDN_WIDTH
        self.n_ab = 2 * HEADS
        self.n_back = 2 * LRU_WIDTH + 2 * d
        self.cols = self.n_front + self.n_ab + self.n_back
        self.pcols = self.n_front + self.n_back + LANE
        self.z_base = 3 * GDN_WIDTH // HEAD_DIM
        self.xb_base = self.n_front // LRU_BLOCK_DIM
        self.gg_off = self.n_front + 2 * LRU_WIDTH
        assert self.gg_off % d == 0
        self.gg_base = self.gg_off // d
        self.ab_base = (self.n_front + self.n_back) // LANE

    def _ranges(self):
        f, ab = self.n_front, self.n_ab
        return [(0, f), (f + ab, self.cols), (f, f + ab)]

    def permuted_from_stack(self, stack):
        blk = stack.shape[2]
        pieces = []
        for lo, hi in self._ranges():
            while lo < hi:
                k = lo // blk
                end = min(hi, (k + 1) * blk)
                pieces.append(stack[k, :, lo - k * blk:end - k * blk])
                lo = end
        pieces.append(jnp.zeros((stack.shape[1], LANE - self.n_ab), stack.dtype))
        return jnp.concatenate(pieces, axis=1)

    def stack_from_permuted(self, w):
        blk = self.cols // N_DEV
        where = {}
        pos = 0
        for lo, hi in self._ranges():
            where[lo] = pos
            pos += hi - lo
        cuts = sorted(lo for lo, _ in self._ranges()) + [self.cols]
        blocks = []
        for k in range(N_DEV):
            lo, pieces = k * blk, []
            while lo < (k + 1) * blk:
                start = max(c for c in cuts if c <= lo)
                end = min((k + 1) * blk, min(c for c in cuts if c > lo))
                pieces.append(w[:, where[start] + lo - start:where[start] + end - start])
                lo = end
            blocks.append(pieces[0] if len(pieces) == 1 else jnp.concatenate(pieces, axis=1))
        return jnp.stack(blocks)


def _cols_from_stack(stack):
    nd, k, n = stack.shape
    return stack.transpose(1, 0, 2).reshape(k, nd * n)


def _stack_from_cols(w):
    k, n = w.shape
    return w.reshape(k, N_DEV, n // N_DEV).transpose(1, 0, 2)


def _after(v, *tokens):
    for t in tokens:
        v = v + t[0, 0]
    return v


def _layer_fwd(x, p, late, lay, tag):
    s, d = x.shape
    nc = s // CHUNK
    h = _rms_fwd(x, p["attn_norm"], f"rms1_fwd{tag}")
    proj, ab = _matmul(h, p["w_in"], "nn", f"in_proj{tag}", [BF], tm=2048, tn=1152, tail=True)
    qkv = _conv_silu_fwd(proj, p["gdn_conv_w"], f"gdn_conv_fwd{tag}")
    gc, beta = _gdn_gates_fwd(ab, p["gdn_a_log"], p["gdn_dt_bias"], 0, f"gdn_gates_fwd{tag}")
    grow = gc[:, :HEADS].reshape(nc, CHUNK, HEADS).transpose(0, 2, 1)
    o, states, tinv = _gdn_chunk_fwd(qkv, gc, beta, grow, f"gdn_chunk_fwd{tag}")
    og = _gdn_post_fwd(o, proj, p["gdn_norm"], lay.z_base, f"gdn_post_fwd{tag}")
    ol, hs, decay = _lru_fwd(proj, p["lru_conv_w"], p["lru_conv_b"], p["lru_w_a"], p["lru_b_a"], p["lru_w_x"], p["lru_b_x"],
                      p["lru_lambda"], lay.xb_base, f"lru_fwd{tag}")
    late[0](hs)
    yg = _matmul(og, p["w_branch_gdn"], "nn", f"branch_gdn{tag}", [BF])
    yl = _matmul(ol, p["w_branch_lru"], "nn", f"branch_lru{tag}", [BF])
    merged = _merge_fwd(proj, yg, yl, lay.gg_base, f"merge_fwd{tag}")
    x1 = _matmul(merged, p["w_out"], "nn", f"out_proj{tag}", [F32], epi=lambda acc, r: (acc + r,), extras=[x])
    h2 = _rms_fwd(x1, p["mlp_norm"], f"rms2_fwd{tag}")
    late[1](h2)
    u, act = _matmul(h2, p["w_up"], "nn", f"mlp_up{tag}", [BF, BF],
                     epi=lambda acc: (acc, jnp.square(jnp.maximum(acc, 0.0))), tm=2048, shards=True)
    x2 = _matmul(act, p["w_down"], "nn", f"mlp_down{tag}", [F32], epi=lambda acc, r: (acc + r,), extras=[x1])
    saved = dict(x=x, h=h, proj=proj, ab=ab, qkv=qkv, gc=gc, beta=beta, grow=grow, o=o, states=states, tinv=tinv, og=og, ol=ol, hs=hs, decay=decay,
                 yg=yg, yl=yl, merged=merged, x1=x1, h2=h2, u=u, act=act)
    return x2, saved


def _layer_bwd(dx2, p, sv, send, send_small, after_last_send, after, lay, tag):
    s, d = dx2.shape
    nc = s // CHUNK
    small = {}
    du = _matmul(dx2, p["w_down"], "nt", f"d_act{tag}", [BF],
                 epi=lambda acc, u: (acc * (2.0 * jnp.maximum(u.astype(F32), 0.0)),), extras=[sv["u"]], after=after)
    sent = send({"w_down": _matmul(sv["act"], dx2, "tn", f"dw_down{tag}", [BF]).reshape(N_DEV, -1, d),
                 "w_up": _matmul(sv["h2"], du, "tn", f"dw_up{tag}", [BF], shards=True)})
    dh2 = _matmul(du, p["w_up"], "nt", f"d_h2{tag}", [BF], shards=True)
    dx1, g = _rms_bwd(sv["x1"], _after(p["mlp_norm"], sent), dh2, dx2, f"rms2_bwd{tag}")
    small["mlp_norm"] = g.reshape(-1)
    dmerged = _matmul(dx1, p["w_out"], "nt", f"d_merged{tag}", [BF])
    dwout = _matmul(sv["merged"], dx1, "tn", f"dw_out{tag}", [BF]).reshape(N_DEV, -1, d)
    dgg, dgl, dyg, dyl = _merge_bwd(sv["proj"], sv["yg"], sv["yl"], dmerged, lay.gg_base, f"merge_bwd{tag}")
    sent = send({"w_out": dwout,
                 "w_branch_gdn": _stack_from_cols(_matmul(sv["og"], dyg, "tn", f"dw_bg{tag}", [BF])),
                 "w_branch_lru": _stack_from_cols(_matmul(sv["ol"], dyl, "tn", f"dw_bl{tag}", [BF]))})
    dog = _matmul(dyg, p["w_branch_gdn"], "nt", f"d_og{tag}", [BF])
    dol = _matmul(dyl, p["w_branch_lru"], "nt", f"d_ol{tag}", [BF])
    (dxb, dyb, dcw, dcb, dwa, dba, dwx, dbx, dlam) = _lru_bwd(
        sv["proj"], sv["hs"], sv["decay"], dol, p["lru_conv_w"], p["lru_conv_b"], p["lru_w_a"], p["lru_b_a"], p["lru_w_x"],
        p["lru_b_x"], _after(p["lru_lambda"], sent), lay.xb_base, f"lru_bwd{tag}")
    small.update(lru_conv_w=dcw, lru_conv_b=dcb.reshape(-1), lru_w_a=dwa, lru_b_a=dba.reshape(-1), lru_w_x=dwx,
                 lru_b_x=dbx.reshape(-1), lru_lambda=dlam.reshape(-1))
    do, dz, g = _gdn_post_bwd(sv["o"], sv["proj"], p["gdn_norm"], dog, lay.z_base, f"gdn_post_bwd{tag}")
    small["gdn_norm"] = g.reshape(-1)
    dqkv_c, dgc_col, dbeta, dgrow = _gdn_chunk_bwd(sv["qkv"], sv["gc"], sv["beta"], sv["grow"], sv["states"], sv["tinv"], do,
                                                   f"gdn_chunk_bwd{tag}")
    dqkv, dgcw = _conv_silu_bwd(sv["proj"], p["gdn_conv_w"], dqkv_c, f"gdn_conv_bwd{tag}")
    small["gdn_conv_w"] = dgcw
    dgc_row = jnp.pad(dgrow.transpose(0, 2, 1).reshape(s, HEADS), ((0, 0), (0, LANE - HEADS)))
    dab, dal, ddt = _gdn_gates_bwd(sv["ab"], p["gdn_a_log"], p["gdn_dt_bias"], dgc_col, dgc_row, dbeta, 0,
                                   f"gdn_gates_bwd{tag}")
    small["gdn_a_log"] = dal[0, :HEADS]
    small["gdn_dt_bias"] = ddt[0, :HEADS]
    dproj = jnp.concatenate([dqkv, dz, dxb, dyb, dgg, dgl, dab], axis=1)
    sent = send_small(small, "main")
    dwin = _matmul(sv["h"], dproj, "tn", f"dw_in{tag}", [BF], tn=1152, after=sent)
    sent = send({"w_in": lay.stack_from_permuted(dwin)})
    dh = _matmul(dproj, p["w_in"], "nt", f"d_h{tag}", [BF], tk=3456, after=sent)
    dx, g = _rms_bwd(sv["x"], p["attn_norm"], dh, dx1, f"rms1_bwd{tag}")
    after_last_send(send_small({"attn_norm": g.reshape(-1)}, "attn"))
    return dx


_BIG = ("w_in", "w_branch_gdn", "w_branch_lru", "w_out", "w_up", "w_down")
_ROW_SHARDED = ("w_out", "w_down")
_CONV = ("gdn_conv_w", "lru_conv_w")
_SMALL = ("attn_norm", "gdn_a_log", "gdn_dt_bias", "gdn_norm", "lru_conv_b", "lru_w_a", "lru_b_a", "lru_w_x",
          "lru_b_x", "lru_lambda", "mlp_norm")
_ORDER = ("attn_norm", "w_in", "gdn_conv_w", "gdn_a_log", "gdn_dt_bias", "gdn_norm", "lru_conv_w", "lru_conv_b",
          "lru_w_a", "lru_b_a", "lru_w_x", "lru_b_x", "lru_lambda", "w_branch_gdn", "w_branch_lru", "w_out",
          "mlp_norm", "w_up", "w_down", "final_norm")


def kernel(x, attn_norm, w_in, gdn_conv_w, gdn_a_log, gdn_dt_bias, gdn_norm, lru_conv_w, lru_conv_b, lru_w_a, lru_b_a, lru_w_x, lru_b_x, lru_lambda, w_branch_gdn, w_branch_lru, w_out, mlp_norm, w_up, w_down, final_norm, loss_target, m_attn_norm, m_w_in, m_gdn_conv_w, m_gdn_a_log, m_gdn_dt_bias, m_gdn_norm, m_lru_conv_w, m_lru_conv_b, m_lru_w_a, m_lru_b_a, m_lru_w_x, m_lru_b_x, m_lru_lambda, m_w_branch_gdn, m_w_branch_lru, m_w_out, m_mlp_norm, m_w_up, m_w_down, m_final_norm, v_attn_norm, v_w_in, v_gdn_conv_w, v_gdn_a_log, v_gdn_dt_bias, v_gdn_norm, v_lru_conv_w, v_lru_conv_b, v_lru_w_a, v_lru_b_a, v_lru_w_x, v_lru_b_x, v_lru_lambda, v_w_branch_gdn, v_w_branch_lru, v_w_out, v_mlp_norm, v_w_up, v_w_down, v_final_norm):
    w = dict(attn_norm=attn_norm, w_in=w_in, gdn_conv_w=gdn_conv_w, gdn_a_log=gdn_a_log, gdn_dt_bias=gdn_dt_bias, gdn_norm=gdn_norm, lru_conv_w=lru_conv_w, lru_conv_b=lru_conv_b, lru_w_a=lru_w_a, lru_b_a=lru_b_a, lru_w_x=lru_w_x, lru_b_x=lru_b_x, lru_lambda=lru_lambda, w_branch_gdn=w_branch_gdn, w_branch_lru=w_branch_lru, w_out=w_out, mlp_norm=mlp_norm, w_up=w_up, w_down=w_down, final_norm=final_norm)
    m = dict(attn_norm=m_attn_norm, w_in=m_w_in, gdn_conv_w=m_gdn_conv_w, gdn_a_log=m_gdn_a_log, gdn_dt_bias=m_gdn_dt_bias, gdn_norm=m_gdn_norm, lru_conv_w=m_lru_conv_w, lru_conv_b=m_lru_conv_b, lru_w_a=m_lru_w_a, lru_b_a=m_lru_b_a, lru_w_x=m_lru_w_x, lru_b_x=m_lru_b_x, lru_lambda=m_lru_lambda, w_branch_gdn=m_w_branch_gdn, w_branch_lru=m_w_branch_lru, w_out=m_w_out, mlp_norm=m_mlp_norm, w_up=m_w_up, w_down=m_w_down, final_norm=m_final_norm)
    v = dict(attn_norm=v_attn_norm, w_in=v_w_in, gdn_conv_w=v_gdn_conv_w, gdn_a_log=v_gdn_a_log, gdn_dt_bias=v_gdn_dt_bias, gdn_norm=v_gdn_norm, lru_conv_w=v_lru_conv_w, lru_conv_b=v_lru_conv_b, lru_w_a=v_lru_w_a, lru_b_a=v_lru_b_a, lru_w_x=v_lru_w_x, lru_b_x=v_lru_b_x, lru_lambda=v_lru_lambda, w_branch_gdn=v_w_branch_gdn, w_branch_lru=v_w_branch_lru, w_out=v_w_out, mlp_norm=v_mlp_norm, w_up=v_w_up, w_down=v_w_down, final_norm=v_final_norm)
    depth = w_in.shape[0]
    x = x[0]
    target = loss_target[0]
    s, d = x.shape
    lay = _Layout(d)
    assert lay.cols == w_in.shape[2] * N_DEV
    cx, cy, cc = _place()
    me = 4 * cx + 2 * cy + cc

    me_arr = me.astype(jnp.int32).reshape(1)

    early, mixer, mlp = ("w_in",) + _CONV, ("w_branch_gdn", "w_branch_lru", "w_out"), ("w_up", "w_down")

    def shard(n, l):
        return w[n][l].astype(BF) if n in _BIG else w[n][l]

    def joined(name, stack):
        if name == "w_in":
            return lay.permuted_from_stack(stack)
        if name == "w_up":
            return stack
        return stack.reshape(-1, stack.shape[-1]) if name in _ROW_SHARDED else _cols_from_stack(stack)

    def start_weights(l, names, after):
        flights, token = _send_start([shard(n, l) for n in names], True, after, f"weights_start_l{l}")
        return dict(zip(names, flights)), token

    def wait_weights(l, flights, names, p, after, which):
        got = _send_wait([flights[n] for n in names], after, f"weights_wait_{which}_l{l}", fill=True)
        for n, (_, gathered) in zip(names, got):
            p[n] = joined(n, gathered)

    saved, params = [], []
    xl = x
    first = dict(zip(early + mixer, _all_gather([shard(n, 0) for n in early + mixer], "gather_first_weights")))
    flights, token = start_weights(0, mlp, first["w_in"])
    for l in range(depth):
        p = {n: w[n][l] for n in _SMALL}
        if l == 0:
            p.update({n: joined(n, g) for n, g in first.items()})
        else:
            wait_weights(l, flights, early, p, xl, "early")
        mine = flights
        if l + 1 < depth:
            flights, token = start_weights(l + 1, early + mixer + mlp, token)
        p["attn_norm"] = _after(p["attn_norm"], token)
        late = [functools.partial(wait_weights, l, mine, names, p, which=which) if names[0] in mine
                else (lambda after: None) for names, which in ((mixer, "mixer"), (mlp, "mlp"))]
        xl, sv = _layer_fwd(xl, p, late, lay, f"_l{l}")
        saved.append(sv)
        params.append(p)
    dx, loss_row, dfinal = _loss_head(xl, final_norm, target, "loss_head")
    loss = lax.psum(loss_row[0, 0], ("x", "y", "c"))

    out = {n: None for n in _BIG}
    pending = {l: {} for l in range(depth)}

    def sender(l):
        def send(stacks):
            names = tuple(stacks)
            fl, token = _send_start([stacks[n] for n in names], False, me_arr, f"grads_start_{'_'.join(names)}_l{l}")
            pending[l].update(zip(names, fl))
            return token
        return send

    def finish(l, names, after, which):
        got = _send_wait([pending[l][n] for n in names], after, f"grads_wait_{which}_l{l}")
        for n, (mine, landed) in zip(names, got):
            out[n] = _adamw_sharded(mine, landed, me_arr, w[n], m[n], v[n], l, out[n], after, f"adamw_{n}_l{l}")
            after = out[n][0]
        return after

    small_names = _SMALL + _CONV
    small_sent = []

    def small_sender(l):
        def send_small(small, which):
            small = dict(small)
            if which == "main" and l == depth - 1:
                small["final_norm"] = dfinal.reshape(-1)
            names = tuple(small)
            flights, token = _send_start([_pack([small[n] for n in names])], True, me_arr,
                                         f"small_grads_start_{which}_l{l}")
            small_sent.append((l, which, names, [small[n] for n in names], flights[0]))
            return token
        return send_small

    last = [None]

    def hook_for(l):
        def hook(token):
            last[0] = finish(l + 1, _BIG, token, "all") if l + 1 < depth else token
        return hook

    for l in reversed(range(depth)):
        dx = _layer_bwd(dx, params[l], saved[l], sender(l), small_sender(l), hook_for(l), last[0], lay, f"_l{l}")

    last[0] = finish(0, mixer + mlp, last[0], "late")
    got = _send_wait([f for *_, f in small_sent], last[0], "small_grads_wait", fill=True)
    summed = [{} for _ in range(depth)]
    for (l, which, names, arrays, _), (_, gathered) in zip(small_sent, got):
        total = _sum_devices(gathered, f"sum_small_grads_{which}_l{l}")
        summed[l].update(zip(names, _unpack(total, arrays)))
    grads = {n: jnp.stack([summed[l][n] for l in range(depth)]) for n in small_names}
    grads["final_norm"] = summed[depth - 1]["final_norm"]
    for n in _CONV:
        blk = w[n].shape[2]
        grads[n] = lax.dynamic_slice_in_dim(grads[n], me * blk, blk, axis=2)
    names = small_names + ("final_norm",)
    gp = _pack([grads[n] for n in names])
    upd = _adamw_packed(_pack([w[n] for n in names]), gp, _pack([m[n] for n in names]), _pack([v[n] for n in names]),
                        "adamw_small")
    like = [w[n] for n in names]
    unp = [_unpack(u, like) for u in upd]
    for i, n in enumerate(names):
        out[n] = (grads[n], unp[0][i], unp[1][i], unp[2][i])
    finish(0, ("w_in",), upd[0], "w_in")

    res = [loss, dx.reshape(1, s, d)]
    for k in range(4):
        res += [out[n][k] for n in _ORDER]
    return tuple(res)
```

```python
import functools

import jax
import jax.numpy as jnp
from jax import lax
from jax.experimental import pallas as pl
from jax.experimental.pallas import tpu as pltpu

F32 = jnp.float32
BF = jnp.bfloat16
MESH = pl.DeviceIdType.MESH
N_DEV = 8

HEADS = 8
HEAD_DIM = 128
GDN_WIDTH = HEADS * HEAD_DIM
CHUNK = 64
CONV_WIDTH = 4
LRU_WIDTH = 1024
LRU_BLOCKS = 8
LRU_BLOCK_DIM = 128
LRU_C = 8.0
RMS_EPS = 1e-6
L2_EPS = 1e-6
ADAM_LR = 0.001
ADAM_B1 = 0.9
ADAM_B2 = 0.999
ADAM_EPS = 1e-08
ADAM_WD = 0.01
ADAM_STEP = 10

LANE = 128
PACK_ROWS = 512
VMEM_LIMIT = 56 * 1024 * 1024


def _cparams(sem=None):
    return pltpu.CompilerParams(dimension_semantics=sem, vmem_limit_bytes=VMEM_LIMIT)


def _tile(n, pref, unit=LANE):
    if n <= pref:
        return n
    best = None
    for t in range(unit, pref + 1, unit):
        if n % t == 0:
            best = t
    assert best is not None, (n, pref, unit)
    return best


_NN = (((1,), (0,)), ((), ()))
_NT = (((1,), (1,)), ((), ()))
_TN = (((0,), (0,)), ((), ()))


def _dg(a, b, dims):
    return lax.dot_general(a.astype(BF), b.astype(BF), dims, preferred_element_type=F32)


class _MM:
    def __init__(self, nn, nt, tn):
        self.nn, self.nt, self.tn = nn, nt, tn


def _make_mm(nn, nt, tn):
    def p_nn(a, b):
        return _dg(a, b, nn)

    def p_nt(a, b):
        return _dg(a, b, nt)

    def p_tn(a, b):
        return _dg(a, b, tn)

    d_nn, d_nt, d_tn = jax.custom_vjp(p_nn), jax.custom_vjp(p_nt), jax.custom_vjp(p_tn)

    def save(f):
        return lambda a, b: (f(a, b), (a, b))

    d_nn.defvjp(save(d_nn), lambda r, g: (d_nt(g, r[1]), d_tn(r[0], g)))
    d_nt.defvjp(save(d_nt), lambda r, g: (d_nn(g, r[1]), d_tn(g, r[0])))
    d_tn.defvjp(save(d_tn), lambda r, g: (d_nt(r[1], g), d_nn(r[0], g)))
    return _MM(p_nn, p_nt, p_tn), _MM(d_nn, d_nt, d_tn)


_BNN = (((2,), (1,)), ((0,), (0,)))
_BNT = (((2,), (2,)), ((0,), (0,)))
_BTN = (((1,), (1,)), ((0,), (0,)))

_PLAIN, _DIFF = _make_mm(_NN, _NT, _TN)
_PLAIN_B, _DIFF_B = _make_mm(_BNN, _BNT, _BTN)


def _sigmoid(x):
    return 1.0 / (1.0 + jnp.exp(-x))


def _silu(x):
    return x * _sigmoid(x)


def _softplus(x):
    return jnp.maximum(x, 0.0) + jnp.log(1.0 + jnp.exp(-jnp.maximum(x, -x)))


def _gelu(x):
    return 0.5 * x * (1.0 + jnp.tanh(0.7978845608028654 * (x + 0.044715 * (x * x * x))))


def _expm1(x):
    t = jnp.tanh(0.5 * x)
    return 2.0 * t / (1.0 - t)


def _rms(x, gain):
    ms = jnp.mean(x * x, axis=-1, keepdims=True)
    return x * lax.rsqrt(ms + RMS_EPS) * gain


def _l2n(x):
    return x * lax.rsqrt(jnp.sum(x * x, axis=-1, keepdims=True) + L2_EPS)


def _shift_down(x, r):
    if r == 0:
        return x
    row = lax.broadcasted_iota(jnp.int32, x.shape, 0)
    return jnp.where(row >= r, pltpu.roll(x, r, 0), 0.0)


def _shift_up(x, r):
    if r == 0:
        return x
    n = x.shape[0]
    row = lax.broadcasted_iota(jnp.int32, x.shape, 0)
    return jnp.where(row < n - r, pltpu.roll(x, n - r, 0), 0.0)


def _conv(x, w_ref):
    y = None
    for j in range(CONV_WIDTH):
        t = _shift_down(x, CONV_WIDTH - 1 - j) * w_ref[j:j + 1, :]
        y = t if y is None else y + t
    return y


def _conv_bwd(x, dy, w_ref):
    dx = None
    dws = []
    for j in range(CONV_WIDTH):
        r = CONV_WIDTH - 1 - j
        t = _shift_up(dy, r) * w_ref[j:j + 1, :]
        dx = t if dx is None else dx + t
        dws.append(jnp.sum(dy * _shift_down(x, r), axis=0, keepdims=True))
    return dx, dws


def _scan_tile(a, b, reverse):
    n = a.shape[0]
    row = lax.broadcasted_iota(jnp.int32, a.shape, 0)
    s = 1
    while s < n:
        if reverse:
            keep = row < n - s
            a_sh = jnp.where(keep, pltpu.roll(a, n - s, 0), 1.0)
            b_sh = jnp.where(keep, pltpu.roll(b, n - s, 0), 0.0)
        else:
            keep = row >= s
            a_sh = jnp.where(keep, pltpu.roll(a, s, 0), 1.0)
            b_sh = jnp.where(keep, pltpu.roll(b, s, 0), 0.0)
        b = a * b_sh + b
        a = a * a_sh
        s *= 2
    return a, b


def _dg3(a, b, dims):
    a_hi = a.astype(BF)
    b_hi = b.astype(BF)
    a_lo = a - a_hi.astype(F32)
    b_lo = b - b_hi.astype(F32)
    return _dg(a_hi, b_hi, dims) + (_dg(a_hi, b_lo, dims) + _dg(a_lo, b_hi, dims))


@jax.custom_vjp
def _tri_inv(a):
    n = a.shape[1]
    p = _dg(a, a, _BNN)
    r = p
    e = 2
    while 2 * e < n:
        p = _dg(p, p, _BNN)
        r = r + p + _dg(r, p, _BNN)
        e *= 2
    row = lax.broadcasted_iota(jnp.int32, (1, n, n), 1)
    col = lax.broadcasted_iota(jnp.int32, (1, n, n), 2)
    eye = jnp.where(row == col, 1.0, 0.0)
    x = eye - a + r - _dg(a, r, _BNN)
    for _ in range(2):
        x = x + _dg(x, eye - x - _dg3(a, x, _BNN), _BNN)
    return x


def _tri_inv_fwd(a):
    t = _tri_inv(a)
    return t, t


def _tri_inv_bwd(t, g):
    return (-_dg3(_dg3(t, g, _BTN), t, _BNT),)


_tri_inv.defvjp(_tri_inv_fwd, _tri_inv_bwd)


@jax.custom_vjp
def _tri_inv_saved(a, t):
    return t


_tri_inv_saved.defvjp(lambda a, t: (t, t), lambda t, g: (_tri_inv_bwd(t, g)[0], jnp.zeros_like(t)))


def _gdn_chunk(q, k, v, beta, gcol, grow, state, mm, t_saved=None):
    c = q.shape[1]
    row = lax.broadcasted_iota(jnp.int32, (1, c, c), 1)
    col = lax.broadcasted_iota(jnp.int32, (1, c, c), 2)
    causal = row >= col
    strict = row > col
    qn = _l2n(q) * (HEAD_DIM ** -0.5)
    kn = _l2n(k)
    dec = jnp.where(causal, jnp.exp(jnp.where(causal, gcol - grow, 0.0)), 0.0)
    kb = kn * beta
    vb = v * beta
    a = jnp.where(strict, mm.nt(kb, kn) * dec, 0.0)
    t = _tri_inv(a) if t_saved is None else _tri_inv_saved(a, t_saved)
    eg = jnp.exp(gcol)
    u = mm.nn(t, vb)
    w = mm.nn(t, kb * eg)
    p = jnp.where(causal, mm.nt(qn, kn) * dec, 0.0)
    vn = u - mm.nn(w, state)
    o = mm.nn(qn * eg, state) + mm.nn(p, vn)
    last = lax.broadcasted_iota(jnp.int32, (1, c, 1), 1) == c - 1
    gl = jnp.sum(jnp.where(last, gcol, 0.0), axis=1, keepdims=True)
    kd = kn * jnp.exp(gl - gcol)
    new_state = state * jnp.exp(gl) + mm.tn(kd, vn)
    return o, new_state, t


def _lru_gates(xc, wa, ba, wx, bx, lam, mm):
    r = _sigmoid(mm.nn(xc, wa) + ba)
    i = _sigmoid(mm.nn(xc, wx) + bx)
    log_a = -LRU_C * r * _softplus(-lam)
    a = jnp.exp(log_a)
    bterm = jnp.sqrt(-_expm1(2.0 * log_a)) * (i * xc)
    return a, bterm


def _matmul(a, b, mode, name, out_dtypes, epi=None, extras=(), tm=1024, tn=1024, tk=2048, after=None,
            shards=False, tail=False):
    if shards:
        nd, rows, per = b.shape if mode != "tn" else (N_DEV, a.shape[1], b.shape[1] // N_DEV)
        b_shape = b.shape if mode == "tn" else (rows, nd * per)
    else:
        b_shape = b.shape
    if mode == "nn":
        (m, k), (k2, n) = a.shape, b_shape
    elif mode == "nt":
        (m, k), (n, k2) = a.shape, b_shape
    else:
        (k, m), (k2, n) = a.shape, b_shape
    assert k == k2, (a.shape, b.shape, mode)
    if shards:
        tn, tk = (min(tn, per), tk) if mode != "nt" else (tn, min(tk, per))
    tm, tn, tk = _tile(m, tm), _tile(n, tn), _tile(k, tk)
    nk = k // tk
    dims = {"nn": _NN, "nt": _NT, "tn": _TN}[mode]
    if mode == "nn":
        a_spec = pl.BlockSpec((tm, tk), lambda i, j, kk: (i, kk))
        b_spec = pl.BlockSpec((tk, tn), lambda i, j, kk: (kk, j))
        if shards:
            assert per % tn == 0
            b_spec = pl.BlockSpec((None, tk, tn), lambda i, j, kk: (j * tn // per, kk, j * tn % per // tn))
    elif mode == "nt":
        a_spec = pl.BlockSpec((tm, tk), lambda i, j, kk: (i, kk))
        b_spec = pl.BlockSpec((tn, tk), lambda i, j, kk: (j, kk))
        if shards:
            assert per % tk == 0
            b_spec = pl.BlockSpec((None, tn, tk), lambda i, j, kk: (kk * tk // per, j, kk * tk % per // tk))
    else:
        a_spec = pl.BlockSpec((tk, tm), lambda i, j, kk: (kk, i))
        b_spec = pl.BlockSpec((tk, tn), lambda i, j, kk: (kk, j))
    o_spec = pl.BlockSpec((tm, tn), lambda i, j, kk: (i, j))
    out_spec, out_dims = o_spec, (m, n)
    if shards and mode == "tn":
        assert per % tn == 0 and not extras
        out_spec = pl.BlockSpec((None, tm, tn), lambda i, j, kk: (j * tn // per, i, j * tn % per // tn))
        out_dims = (N_DEV, m, per)
    n_ex, n_out = len(extras), len(out_dtypes)
    order = [] if after is None else [after]
    n_in = 2 + n_ex + len(order)
    if epi is None:
        epi = lambda acc: (acc,)

    def body(*refs):
        a_ref, b_ref = refs[0], refs[1]
        ex_refs = refs[2:2 + n_ex]
        out_refs = refs[n_in:n_in + n_out]
        kk = pl.program_id(2)
        part = _dg(a_ref[...], b_ref[...], dims)

        def finish(total):
            res = epi(total, *[r[...] for r in ex_refs])
            for r, v in zip(out_refs, res):
                r[...] = v.astype(r.dtype)
            if tail:
                @pl.when(pl.program_id(1) == n // tn - 1)
                def _():
                    refs[n_in + n_out][...] = total[:, tn - LANE:]

        if nk == 1:
            finish(part)
            return
        acc_ref = refs[-1]

        @pl.when(kk == 0)
        def _():
            acc_ref[...] = part

        @pl.when(jnp.logical_and(kk > 0, kk < nk - 1))
        def _():
            acc_ref[...] += part

        @pl.when(kk == nk - 1)
        def _():
            finish(acc_ref[...] + part)

    outs = pl.pallas_call(
        body, name=name, grid=(m // tm, n // tn, nk),
        in_specs=[a_spec, b_spec] + [o_spec] * n_ex + [pl.BlockSpec(memory_space=pl.ANY)] * len(order),
        out_specs=[out_spec] * n_out + [pl.BlockSpec((tm, LANE), lambda i, j, kk: (i, 0))] * tail,
        out_shape=[jax.ShapeDtypeStruct(out_dims, d) for d in out_dtypes]
        + [jax.ShapeDtypeStruct((m, LANE), F32)] * tail,
        scratch_shapes=[pltpu.VMEM((tm, tn), F32)] if nk > 1 else [],
        compiler_params=_cparams(("parallel", "arbitrary" if tail else "parallel", "arbitrary")),
    )(a, b, *extras, *order)
    return outs[0] if n_out + tail == 1 else outs


def _rowwise(fn, name, rows, tr, row_ins, params, row_outs, acc_outs=(), ncol=1):
    tr = _tile(rows, tr, 8)
    n_in, n_par, n_ro, n_acc = len(row_ins), len(params), len(row_outs), len(acc_outs)
    in_specs = []
    for _, width, base, per_col in row_ins:
        if per_col:
            in_specs.append(pl.BlockSpec((tr, width), lambda i, j, base=base: (i, base + j)))
        else:
            in_specs.append(pl.BlockSpec((tr, width), lambda i, j, base=base: (i, base)))
    for p in params:
        in_specs.append(pl.BlockSpec(p.shape, lambda i, j: (0, 0)))
    out_specs, out_shape = [], []
    for total, dtype, width, per_col in row_outs:
        if per_col:
            out_specs.append(pl.BlockSpec((tr, width), lambda i, j: (i, j)))
        else:
            out_specs.append(pl.BlockSpec((tr, width), lambda i, j: (i, 0)))
        out_shape.append(jax.ShapeDtypeStruct((rows, total), dtype))
    for shape, dtype in acc_outs:
        out_specs.append(pl.BlockSpec(shape, lambda i, j: (0, 0)))
        out_shape.append(jax.ShapeDtypeStruct(shape, dtype))

    def body(*refs):
        ins = [r[...].astype(F32) for r in refs[:n_in + n_par]]
        ro = refs[n_in + n_par:n_in + n_par + n_ro]
        ao = refs[n_in + n_par + n_ro:]
        res = fn(*ins)
        for r, v in zip(ro, res[:n_ro]):
            r[...] = v.astype(r.dtype)
        first = jnp.logical_and(pl.program_id(0) == 0, pl.program_id(1) == 0)

        @pl.when(first)
        def _():
            for r in ao:
                r[...] = jnp.zeros_like(r)

        for r, v in zip(ao, res[n_ro:]):
            r[...] += v.astype(r.dtype)

    sem = ("arbitrary", "arbitrary") if n_acc else ("parallel", "parallel")
    outs = pl.pallas_call(
        body, name=name, grid=(rows // tr, ncol), in_specs=in_specs, out_specs=out_specs,
        out_shape=out_shape, compiler_params=_cparams(sem),
    )(*[r[0] for r in row_ins], *params)
    return outs


def _row(x):
    return x.reshape(1, -1)


def _rms_fwd(x, gain, name):
    s, d = x.shape
    return _rowwise(lambda xt, g: (_rms(xt, g),), name, s, 512, [(x, d, 0, False)], [_row(gain)],
                    [(d, BF, d, False)])[0]


def _rms_bwd(x, gain, dh, dres, name):
    s, d = x.shape

    def fn(xt, dht, drt, g):
        _, vjp = jax.vjp(_rms, xt, g)
        dx, dg = vjp(dht)
        return drt + dx, dg

    return _rowwise(fn, name, s, 256, [(x, d, 0, False), (dh, d, 0, False), (dres, d, 0, False)], [_row(gain)],
                    [(d, F32, d, False)], [((1, d), F32)])


def _loss_head(x, gain, target, name):
    s, d = x.shape

    def fn(xt, tt, g):
        y, vjp = jax.vjp(_rms, xt, g)
        err = y - tt
        dx, dg = vjp(err * (1.0 / d))
        part = 0.5 * jnp.sum(jnp.sum(err * err, axis=1, keepdims=True), axis=0, keepdims=True) * (1.0 / d)
        return dx, jnp.broadcast_to(part, (1, LANE)), dg

    return _rowwise(fn, name, s, 256, [(x, d, 0, False), (target, d, 0, False)], [_row(gain)],
                    [(d, F32, d, False)], [((1, LANE), F32), ((1, d), F32)])


def _gdn_post_fn(o, z, g):
    return _rms(o, g) * _silu(z)


def _gdn_post_fwd(o, proj, gain, z_base, name):
    s = o.shape[0]
    return _rowwise(lambda ot, zt, g: (_gdn_post_fn(ot, zt, g),), name, s, 1024,
                    [(o, HEAD_DIM, 0, True), (proj, HEAD_DIM, z_base, True)], [_row(gain)],
                    [(GDN_WIDTH, BF, HEAD_DIM, True)], ncol=HEADS)[0]


def _gdn_post_bwd(o, proj, gain, dog, z_base, name):
    s = o.shape[0]

    def fn(ot, zt, dt, g):
        _, vjp = jax.vjp(_gdn_post_fn, ot, zt, g)
        return vjp(dt)

    return _rowwise(fn, name, s, 1024,
                    [(o, HEAD_DIM, 0, True), (proj, HEAD_DIM, z_base, True), (dog, HEAD_DIM, 0, True)],
                    [_row(gain)], [(GDN_WIDTH, F32, HEAD_DIM, True), (GDN_WIDTH, BF, HEAD_DIM, True)],
                    [((1, HEAD_DIM), F32)], ncol=HEADS)


def _merge_fn(gg, gl, yg, yl):
    return _sigmoid(gg) * yg + _sigmoid(gl) * yl


def _merge_fwd(proj, yg, yl, gg_base, name):
    s, d = yg.shape
    return _rowwise(lambda a, b, c, e: (_merge_fn(a, b, c, e),), name, s, 256,
                    [(proj, d, gg_base, False), (proj, d, gg_base + 1, False), (yg, d, 0, False), (yl, d, 0, False)],
                    [], [(d, BF, d, False)])[0]


def _merge_bwd(proj, yg, yl, dmerged, gg_base, name):
    s, d = yg.shape

    def fn(a, b, c, e, dm):
        _, vjp = jax.vjp(_merge_fn, a, b, c, e)
        return vjp(dm)

    return _rowwise(fn, name, s, 128,
                    [(proj, d, gg_base, False), (proj, d, gg_base + 1, False), (yg, d, 0, False), (yl, d, 0, False),
                     (dmerged, d, 0, False)], [], [(d, BF, d, False)] * 4)


def _conv_silu_fwd(proj, w, name):
    s = proj.shape[0]
    cols = w.shape[1]
    cw = _tile(cols, LANE)

    def body(x_ref, w_ref, o_ref):
        o_ref[...] = _silu(_conv(x_ref[...].astype(F32), w_ref))

    return pl.pallas_call(
        body, name=name, grid=(cols // cw,),
        in_specs=[pl.BlockSpec((s, cw), lambda j: (0, j)), pl.BlockSpec((CONV_WIDTH, cw), lambda j: (0, j))],
        out_specs=pl.BlockSpec((s, cw), lambda j: (0, j)),
        out_shape=jax.ShapeDtypeStruct((s, cols), F32), compiler_params=_cparams(("parallel",)),
    )(proj, w)


def _conv_silu_bwd(proj, w, dout, name):
    s = proj.shape[0]
    cols = w.shape[1]
    cw = _tile(cols, LANE)

    def body(x_ref, w_ref, d_ref, dx_ref, dw_ref):
        x = x_ref[...].astype(F32)
        y = _conv(x, w_ref)
        sg = _sigmoid(y)
        dy = d_ref[...] * (sg * (1.0 + y * (1.0 - sg)))
        dx, dws = _conv_bwd(x, dy, w_ref)
        dx_ref[...] = dx.astype(dx_ref.dtype)
        for j in range(CONV_WIDTH):
            dw_ref[j:j + 1, :] = dws[j]

    return pl.pallas_call(
        body, name=name, grid=(cols // cw,),
        in_specs=[pl.BlockSpec((s, cw), lambda j: (0, j)), pl.BlockSpec((CONV_WIDTH, cw), lambda j: (0, j)),
                  pl.BlockSpec((s, cw), lambda j: (0, j))],
        out_specs=[pl.BlockSpec((s, cw), lambda j: (0, j)), pl.BlockSpec((CONV_WIDTH, cw), lambda j: (0, j))],
        out_shape=[jax.ShapeDtypeStruct((s, cols), BF), jax.ShapeDtypeStruct((CONV_WIDTH, cols), F32)],
        compiler_params=_cparams(("parallel",)),
    )(proj, w, dout)


def _cumsum_mask(tr, transpose):
    row = lax.broadcasted_iota(jnp.int32, (tr, tr), 0)
    col = lax.broadcasted_iota(jnp.int32, (tr, tr), 1)
    same = (row // CHUNK) == (col // CHUNK)
    tri = (row <= col) if transpose else (row >= col)
    return jnp.where(jnp.logical_and(same, tri), 1.0, 0.0)


def _lane_pad(v):
    return jnp.pad(v.reshape(1, -1), ((0, 0), (0, LANE - v.shape[0])))


def _gdn_gates_fwd(proj, a_log, dt_bias, ab_base, name):
    s = proj.shape[0]

    def fn(ab, al, dt):
        g = -jnp.exp(al) * _softplus(ab + dt)
        beta = _sigmoid(pltpu.roll(ab, LANE - HEADS, 1))
        gc = jnp.dot(_cumsum_mask(ab.shape[0], False), g, precision=lax.Precision.HIGHEST,
                     preferred_element_type=F32)
        return gc, beta

    return _rowwise(fn, name, s, 512, [(proj, LANE, ab_base, False)], [_lane_pad(a_log), _lane_pad(dt_bias)],
                    [(LANE, F32, LANE, False)] * 2)


def _gdn_gates_bwd(proj, a_log, dt_bias, dgc_col, dgc_row, dbeta, ab_base, name):
    s = proj.shape[0]

    def fn(ab, dc, dr, db, al, dt):
        lane = lax.broadcasted_iota(jnp.int32, ab.shape, 1)
        dg = jnp.dot(_cumsum_mask(ab.shape[0], True), dc + dr, precision=lax.Precision.HIGHEST,
                     preferred_element_type=F32)
        ea = jnp.exp(al)
        pre = ab + dt
        g = -ea * _softplus(pre)
        da = jnp.where(lane < HEADS, dg * (-ea) * _sigmoid(pre), 0.0)
        dal = jnp.sum(jnp.where(lane < HEADS, dg * g, 0.0), axis=0, keepdims=True)
        ddt = jnp.sum(da, axis=0, keepdims=True)
        beta = _sigmoid(pltpu.roll(ab, LANE - HEADS, 1))
        dbl = jnp.where(lane < HEADS, db * beta * (1.0 - beta), 0.0)
        dab = da + pltpu.roll(dbl, HEADS, 1)
        return dab, dal, ddt

    return _rowwise(fn, name, s, 512,
                    [(proj, LANE, ab_base, False), (dgc_col, LANE, 0, False), (dgc_row, LANE, 0, False),
                     (dbeta, LANE, 0, False)], [_lane_pad(a_log), _lane_pad(dt_bias)],
                    [(LANE, BF, LANE, False)], [((1, LANE), F32), ((1, LANE), F32)])


def _head_col(blk, h):
    lane = lax.broadcasted_iota(jnp.int32, blk.shape, 1)
    return jnp.sum(jnp.where(lane == h, blk, 0.0), axis=1, keepdims=True)


def _chunk_operands(qkv_ref, gc_ref, b_ref, gr_ref):
    heads = range(HEADS)
    q, k, v = (jnp.stack([qkv_ref[:, (j * HEADS + h) * HEAD_DIM:(j * HEADS + h + 1) * HEAD_DIM] for h in heads])
               for j in range(3))
    gcs, betas = gc_ref[...], b_ref[...]
    beta = jnp.stack([_head_col(betas, h) for h in heads])
    gcol = jnp.stack([_head_col(gcs, h) for h in heads])
    grow = jnp.stack([gr_ref[0, h:h + 1, :] for h in heads])
    return q, k, v, beta, gcol, grow


def _gdn_chunk_fwd(qkv, gc, beta, grow, name):
    s = qkv.shape[0]
    nc = s // CHUNK

    def body(qkv_ref, gc_ref, b_ref, gr_ref, o_ref, st_ref, t_ref, state):
        @pl.when(pl.program_id(0) == 0)
        def _():
            state[...] = jnp.zeros_like(state)

        st = state[...]
        st_ref[0] = st
        o, new, t = _gdn_chunk(*_chunk_operands(qkv_ref, gc_ref, b_ref, gr_ref), st, _PLAIN_B)
        t_ref[0] = t
        for h in range(HEADS):
            o_ref[:, h * HEAD_DIM:(h + 1) * HEAD_DIM] = o[h]
        state[...] = new

    return pl.pallas_call(
        body, name=name, grid=(nc,),
        in_specs=[pl.BlockSpec((CHUNK, 3 * GDN_WIDTH), lambda c: (c, 0)),
                  pl.BlockSpec((CHUNK, LANE), lambda c: (c, 0)), pl.BlockSpec((CHUNK, LANE), lambda c: (c, 0)),
                  pl.BlockSpec((1, HEADS, CHUNK), lambda c: (c, 0, 0))],
        out_specs=[pl.BlockSpec((CHUNK, GDN_WIDTH), lambda c: (c, 0)),
                   pl.BlockSpec((1, HEADS, HEAD_DIM, HEAD_DIM), lambda c: (c, 0, 0, 0)),
                   pl.BlockSpec((1, HEADS, CHUNK, CHUNK), lambda c: (c, 0, 0, 0))],
        out_shape=[jax.ShapeDtypeStruct((s, GDN_WIDTH), F32),
                   jax.ShapeDtypeStruct((nc, HEADS, HEAD_DIM, HEAD_DIM), F32),
                   jax.ShapeDtypeStruct((nc, HEADS, CHUNK, CHUNK), F32)],
        scratch_shapes=[pltpu.VMEM((HEADS, HEAD_DIM, HEAD_DIM), F32)],
        compiler_params=_cparams(("arbitrary",)),
    )(qkv, gc, beta, grow)


def _gdn_chunk_bwd(qkv, gc, beta, grow, states, tinv, do, name):
    s = qkv.shape[0]
    nc = s // CHUNK

    def body(qkv_ref, gc_ref, b_ref, gr_ref, st_ref, t_ref, do_ref, dqkv_ref, dgc_ref, db_ref, dgr_ref, dstate):
        @pl.when(pl.program_id(0) == 0)
        def _():
            dstate[...] = jnp.zeros_like(dstate)

        lane = lax.broadcasted_iota(jnp.int32, (CHUNK, LANE), 1)
        chunk = lambda *args: _gdn_chunk(*args, mm=_DIFF_B, t_saved=t_ref[0])[:2]
        _, vjp = jax.vjp(chunk, *_chunk_operands(qkv_ref, gc_ref, b_ref, gr_ref), st_ref[0])
        do = jnp.stack([do_ref[:, h * HEAD_DIM:(h + 1) * HEAD_DIM] for h in range(HEADS)])
        dq, dk, dv, dbeta, dgcol, dgrow, dst = vjp((do, dstate[...]))
        dstate[...] = dst
        dgc = jnp.zeros((CHUNK, LANE), F32)
        db = jnp.zeros((CHUNK, LANE), F32)
        for h in range(HEADS):
            for j, dx in enumerate((dq, dk, dv)):
                dqkv_ref[:, (j * HEADS + h) * HEAD_DIM:(j * HEADS + h + 1) * HEAD_DIM] = dx[h]
            dgc = dgc + jnp.where(lane == h, dgcol[h], 0.0)
            db = db + jnp.where(lane == h, dbeta[h], 0.0)
            dgr_ref[0, h:h + 1, :] = dgrow[h]
        dgc_ref[...] = dgc
        db_ref[...] = db

    rev = lambda c: nc - 1 - c
    wide = pl.BlockSpec((CHUNK, 3 * GDN_WIDTH), lambda c: (rev(c), 0))
    lanes = pl.BlockSpec((CHUNK, LANE), lambda c: (rev(c), 0))
    rows = pl.BlockSpec((1, HEADS, CHUNK), lambda c: (rev(c), 0, 0))
    return pl.pallas_call(
        body, name=name, grid=(nc,),
        in_specs=[wide, lanes, lanes, rows,
                  pl.BlockSpec((1, HEADS, HEAD_DIM, HEAD_DIM), lambda c: (rev(c), 0, 0, 0)),
                  pl.BlockSpec((1, HEADS, CHUNK, CHUNK), lambda c: (rev(c), 0, 0, 0)),
                  pl.BlockSpec((CHUNK, GDN_WIDTH), lambda c: (rev(c), 0))],
        out_specs=[wide, lanes, lanes, rows],
        out_shape=[jax.ShapeDtypeStruct((s, 3 * GDN_WIDTH), F32)]
        + [jax.ShapeDtypeStruct((s, LANE), F32)] * 2 + [jax.ShapeDtypeStruct((nc, HEADS, CHUNK), F32)],
        scratch_shapes=[pltpu.VMEM((HEADS, HEAD_DIM, HEAD_DIM), F32)],
        compiler_params=_cparams(("arbitrary",)),
    )(qkv, gc, beta, grow, states, tinv, do)


def _lru_specs(s, xb_base):
    col = lambda base: pl.BlockSpec((s, LRU_BLOCK_DIM), lambda n, base=base: (0, base + n))
    vec = pl.BlockSpec((1, LRU_BLOCK_DIM), lambda n: (0, n))
    mat = pl.BlockSpec((None, LRU_BLOCK_DIM, LRU_BLOCK_DIM), lambda n: (n, 0, 0))
    cw = pl.BlockSpec((CONV_WIDTH, LRU_BLOCK_DIM), lambda n: (0, n))
    return col, vec, mat, cw


def _lru_fwd(proj, cw, cb, wa, ba, wx, bx, lam, xb_base, name):
    s = proj.shape[0]
    tt = _tile(s, 256, 8)
    col, vec, mat, cws = _lru_specs(s, xb_base)

    def body(xb_ref, yb_ref, cw_ref, cb_ref, wa_ref, ba_ref, wx_ref, bx_ref, lam_ref, o_ref, h_ref, a_ref, xc_s):
        xc_s[...] = _conv(xb_ref[...].astype(F32), cw_ref) + cb_ref[...]
        par = (wa_ref[...], ba_ref[...], wx_ref[...], bx_ref[...], lam_ref[...])

        def step(t, carry):
            sl = pl.ds(pl.multiple_of(t * tt, tt), tt)
            a, bt = _lru_gates(xc_s[sl, :], *par, _PLAIN)
            a_ref[sl, :] = a
            aa, bb = _scan_tile(a, bt, False)
            h = aa * carry + bb
            h_ref[sl, :] = h
            o_ref[sl, :] = (h * _gelu(yb_ref[sl, :].astype(F32))).astype(o_ref.dtype)
            return h_ref[pl.ds(t * tt + tt - 1, 1), :]

        lax.fori_loop(0, s // tt, step, jnp.zeros((1, LRU_BLOCK_DIM), F32))

    nb = LRU_BLOCKS
    return pl.pallas_call(
        body, name=name, grid=(LRU_BLOCKS,),
        in_specs=[col(xb_base), col(xb_base + nb), cws, vec, mat, vec, mat, vec, vec],
        out_specs=[col(0), col(0), col(0)],
        out_shape=[jax.ShapeDtypeStruct((s, LRU_WIDTH), BF), jax.ShapeDtypeStruct((s, LRU_WIDTH), F32),
                   jax.ShapeDtypeStruct((s, LRU_WIDTH), F32)],
        scratch_shapes=[pltpu.VMEM((s, LRU_BLOCK_DIM), F32)],
        compiler_params=_cparams(("parallel",)),
    )(proj, proj, cw, _row(cb), wa, _row(ba), wx, _row(bx), _row(lam))


def _lru_bwd(proj, hs, decay, dol, cw, cb, wa, ba, wx, bx, lam, xb_base, name):
    s = proj.shape[0]
    tt = _tile(s, 256, 8)
    nt = s // tt
    col, vec, mat, cws = _lru_specs(s, xb_base)

    def body(xb_ref, yb_ref, h_ref, a_ref, d_ref, cw_ref, cb_ref, wa_ref, ba_ref, wx_ref, bx_ref, lam_ref,
             dxb_ref, dyb_ref, dcw_ref, dcb_ref, dwa_ref, dba_ref, dwx_ref, dbx_ref, dlam_ref,
             xc_s, dh_s, dxc_s):
        xc_s[...] = _conv(xb_ref[...].astype(F32), cw_ref) + cb_ref[...]
        par = (wa_ref[...], ba_ref[...], wx_ref[...], bx_ref[...], lam_ref[...])
        row = lax.broadcasted_iota(jnp.int32, (tt, LRU_BLOCK_DIM), 0)
        tile = lambda t: pl.ds(pl.multiple_of(t * tt, tt), tt)

        def prep(t, carry):
            sl = tile(t)
            d = d_ref[sl, :].astype(F32)
            gy, vjp_y = jax.vjp(_gelu, yb_ref[sl, :].astype(F32))
            dyb_ref[sl, :] = vjp_y(d * h_ref[sl, :])[0].astype(dyb_ref.dtype)
            dh_s[sl, :] = d * gy
            return carry

        lax.fori_loop(0, nt, prep, 0)

        def rscan(i, carry):
            dh_next, a_next = carry
            t = nt - 1 - i
            sl = tile(t)
            a_sh = jnp.where(row == tt - 1, a_next, pltpu.roll(a_ref[sl, :], tt - 1, 0))
            aa, bb = _scan_tile(a_sh, dh_s[sl, :], True)
            dh_s[sl, :] = aa * dh_next + bb
            first = pl.ds(t * tt, 1)
            return dh_s[first, :], a_ref[first, :]

        zero = jnp.zeros((1, LRU_BLOCK_DIM), F32)
        lax.fori_loop(0, nt, rscan, (zero, zero))

        def gates_vjp(t, acc):
            sl = tile(t)
            _, vjp_g = jax.vjp(functools.partial(_lru_gates, mm=_DIFF), xc_s[sl, :], *par)
            dh = dh_s[sl, :]
            before = jnp.where(t > 0, h_ref[pl.ds(jnp.maximum(t * tt - 1, 0), 1), :], 0.0)
            h_prev = jnp.where(row == 0, before, pltpu.roll(h_ref[sl, :], 1, 0))
            dxc, *dpar = vjp_g((dh * h_prev, dh))
            dxc_s[sl, :] = dxc
            return tuple(x + y for x, y in zip(acc, dpar))

        dwa, dba, dwx, dbx, dlam = lax.fori_loop(0, nt, gates_vjp, tuple(jnp.zeros_like(p) for p in par))
        dwa_ref[...] = dwa
        dba_ref[...] = dba
        dwx_ref[...] = dwx
        dbx_ref[...] = dbx
        dlam_ref[...] = dlam
        dxc = dxc_s[...]
        dcb_ref[...] = jnp.sum(dxc, axis=0, keepdims=True)
        dxb, dws = _conv_bwd(xb_ref[...].astype(F32), dxc, cw_ref)
        dxb_ref[...] = dxb.astype(dxb_ref.dtype)
        for j in range(CONV_WIDTH):
            dcw_ref[j:j + 1, :] = dws[j]

    nb = LRU_BLOCKS
    w = LRU_WIDTH
    return pl.pallas_call(
        body, name=name, grid=(LRU_BLOCKS,),
        in_specs=[col(xb_base), col(xb_base + nb), col(0), col(0), col(0), cws, vec, mat, vec, mat, vec, vec],
        out_specs=[col(0), col(0), cws, vec, mat, vec, mat, vec, vec],
        out_shape=[jax.ShapeDtypeStruct((s, w), BF), jax.ShapeDtypeStruct((s, w), BF),
                   jax.ShapeDtypeStruct((CONV_WIDTH, w), F32), jax.ShapeDtypeStruct((1, w), F32),
                   jax.ShapeDtypeStruct(wa.shape, F32), jax.ShapeDtypeStruct((1, w), F32),
                   jax.ShapeDtypeStruct(wx.shape, F32), jax.ShapeDtypeStruct((1, w), F32),
                   jax.ShapeDtypeStruct((1, w), F32)],
        scratch_shapes=[pltpu.VMEM((s, LRU_BLOCK_DIM), F32)] * 3,
        compiler_params=_cparams(("parallel",)),
    )(proj, proj, hs, decay, dol, cw, _row(cb), wa, _row(ba), wx, _row(bx), _row(lam))


def _place():
    return lax.axis_index("x"), lax.axis_index("y"), lax.axis_index("c")


def _all_gather(blocks, name):
    n = len(blocks)
    per = N_DEV - 1

    def body(*refs):
        ins, outs = refs[:n], refs[n:2 * n]
        send_sems, recv_sems, local_sems = refs[2 * n:]
        x, y, c = _place()
        me, sibling = (x, y, c), (x, y, 1 - c)
        chips = [(1 - x, y), (x, 1 - y), (1 - x, 1 - y)]

        def copy(a, k, block, to, src=None):
            dst = outs[a].at[4 * block[0] + 2 * block[1] + block[2]]
            return pltpu.make_async_remote_copy(
                src_ref=dst if src is None else src, dst_ref=dst, send_sem=send_sems.at[a * per + k],
                recv_sem=recv_sems.at[a * per + k], device_id=to, device_id_type=MESH)

        started = []
        for a in range(n):
            mine = pltpu.make_async_copy(ins[a], outs[a].at[4 * x + 2 * y + c], local_sems.at[a])
            mine.start()
            started.append(mine)
        sends = []
        for a in range(n):
            first = [copy(a, 0, me, sibling, src=ins[a])]
            first += [copy(a, 1 + j, me, (*chip, c), src=ins[a]) for j, chip in enumerate(chips)]
            for cp in first:
                cp.start()
            sends += first
        for a in range(n):
            for j, chip in enumerate(chips):
                copy(a, 1 + j, (*chip, c), me).wait_recv()
                passed = copy(a, 4 + j, (*chip, c), sibling)
                passed.start()
                sends.append(passed)
        for a in range(n):
            copy(a, 0, sibling, me).wait_recv()
            for j, chip in enumerate(chips):
                copy(a, 4 + j, (*chip, 1 - c), me).wait_recv()
        for cp in sends:
            cp.wait_send()
        for cp in started:
            cp.wait()

    any_spec = pl.BlockSpec(memory_space=pl.ANY)
    return pl.pallas_call(
        body, name=name, in_specs=[any_spec] * n, out_specs=[any_spec] * n,
        out_shape=[jax.ShapeDtypeStruct((N_DEV,) + b.shape, b.dtype) for b in blocks],
        scratch_shapes=[pltpu.SemaphoreType.DMA((n * per,)), pltpu.SemaphoreType.DMA((n * per,)),
                        pltpu.SemaphoreType.DMA((n,))],
    )(*blocks)


_HBM = pl.BlockSpec(memory_space=pltpu.HBM)
_SEM = pl.BlockSpec(memory_space=pltpu.SEMAPHORE)
_ANY = pl.BlockSpec(memory_space=pl.ANY)
_EFFECT = pltpu.SideEffectType.DATAFLOW_SIDE_EFFECTING


def _peers(x, y, c):
    return [(x, y, 1 - c), (1 - x, y, c), (1 - x, y, 1 - c), (x, 1 - y, c), (x, 1 - y, 1 - c),
            (1 - x, 1 - y, c), (1 - x, 1 - y, 1 - c)]


class _InFlight:
    def __init__(self, send_sem, recv_sem, src, land):
        self.send_sem, self.recv_sem, self.src, self.land = send_sem, recv_sem, src, land


def _send_start(srcs, whole, after, name):
    n = len(srcs)
    lands = [lax.empty(((N_DEV,) + s.shape) if whole else s.shape, s.dtype) for s in srcs]

    def body(*refs):
        src_refs, land_refs = refs[:n], refs[n:2 * n]
        outs = refs[2 * n + 1:]
        send_sems, recv_sems, token = outs[:n], outs[n:2 * n], outs[-1]
        x, y, c = _place()
        me = 4 * x + 2 * y + c
        for a in range(n):
            for p in _peers(x, y, c):
                src = src_refs[a] if whole else src_refs[a].at[4 * p[0] + 2 * p[1] + p[2]]
                pltpu.make_async_remote_copy(src_ref=src, dst_ref=land_refs[a].at[me], send_sem=send_sems[a],
                                             recv_sem=recv_sems[a], device_id=p, device_id_type=MESH).start()
        token[...] = jnp.zeros_like(token)

    hbm = lambda a: pltpu.HBM(a.shape, a.dtype)
    outs = pl.pallas_call(
        body, name=name,
        out_shape=[pltpu.SemaphoreType.DMA(())] * (2 * n) + [hbm(s) for s in srcs] + [hbm(l) for l in lands]
        + [jax.ShapeDtypeStruct((8, LANE), F32)],
        in_specs=[_HBM] * (2 * n) + [_ANY],
        out_specs=[_SEM] * (2 * n) + [_HBM] * (2 * n) + [pl.BlockSpec(memory_space=pltpu.VMEM)],
        input_output_aliases={i: 2 * n + i for i in range(2 * n)},
        compiler_params=pltpu.CompilerParams(has_side_effects=_EFFECT),
    )(*[pltpu.with_memory_space_constraint(s, pltpu.HBM) for s in srcs],
      *[pltpu.with_memory_space_constraint(l, pltpu.HBM) for l in lands], after)
    flights = [_InFlight(outs[a], outs[n + a], outs[2 * n + a], outs[3 * n + a]) for a in range(n)]
    return flights, outs[-1]


def _send_wait(flights, after, name):
    n = len(flights)

    def body(*refs):
        x, y, c = _place()
        for a in range(n):
            land_ref, send_sem, recv_sem = refs[4 * a + 1], refs[4 * a + 2], refs[4 * a + 3]
            seven = land_ref.at[pl.ds(0, N_DEV - 1)]
            copy = pltpu.make_async_remote_copy(src_ref=seven, dst_ref=seven, send_sem=send_sem, recv_sem=recv_sem,
                                                device_id=(x, y, 1 - c), device_id_type=MESH)
            copy.wait_send()
            copy.wait_recv()

    ins, out_shape = [], []
    for f in flights:
        ins += [f.src, f.land, f.send_sem, f.recv_sem]
        out_shape += [pltpu.HBM(f.src.shape, f.src.dtype), pltpu.HBM(f.land.shape, f.land.dtype)]
    outs = pl.pallas_call(
        body, name=name, out_shape=out_shape,
        in_specs=[_HBM, _HBM, _SEM, _SEM] * n + [_ANY], out_specs=[_HBM] * (2 * n),
        input_output_aliases={4 * a + i: 2 * a + i for a in range(n) for i in range(2)},
        compiler_params=pltpu.CompilerParams(has_side_effects=_EFFECT),
    )(*ins, after)
    return [(outs[2 * a], outs[2 * a + 1]) for a in range(n)]


def _adamw_math(w, g, m, v):
    m = ADAM_B1 * m + (1.0 - ADAM_B1) * g
    v = ADAM_B2 * v + (1.0 - ADAM_B2) * (g * g)
    m_hat = m / (1.0 - ADAM_B1 ** ADAM_STEP)
    v_hat = v / (1.0 - ADAM_B2 ** ADAM_STEP)
    delta = -ADAM_LR * (m_hat / (jnp.sqrt(v_hat) + ADAM_EPS) + ADAM_WD * w)
    return delta, m, v


def _adamw_sharded(mine, landed, me, w, m, v, layer, prev, after, name):
    depth, r, c = w.shape
    tr = _tile(r, 128, 8)
    n_prev = 0 if prev is None else 4

    def body(*refs):
        me_ref, own_ref, land_ref, w_ref, m_ref, v_ref = refs[:6]
        g_ref, d_ref, nm_ref, nv_ref = refs[7 + n_prev:]
        g = None
        for k in range(N_DEV):
            part = jnp.where(me_ref[0] == k, own_ref[0], land_ref[k]).astype(F32)
            g = part if g is None else g + part
        delta, nm, nv = _adamw_math(w_ref[...], g, m_ref[...], v_ref[...])
        g_ref[...] = g
        d_ref[...] = delta
        nm_ref[...] = nm
        nv_ref[...] = nv

    lay = pl.BlockSpec((None, tr, c), lambda i, me_ref: (layer, i, 0))
    return pl.pallas_call(
        body, name=name,
        grid_spec=pltpu.PrefetchScalarGridSpec(
            num_scalar_prefetch=1, grid=(r // tr,),
            in_specs=[pl.BlockSpec((1, tr, c), lambda i, me_ref: (me_ref[0], i, 0)),
                      pl.BlockSpec((N_DEV, tr, c), lambda i, me_ref: (0, i, 0)), lay, lay, lay]
            + [_ANY] * (1 + n_prev),
            out_specs=[lay] * 4),
        out_shape=[jax.ShapeDtypeStruct(w.shape, F32)] * 4,
        input_output_aliases={7 + i: i for i in range(n_prev)},
        compiler_params=_cparams(("parallel",)),
    )(me, mine, landed, w, m, v, after, *(prev or ()))


def _sum_devices(stack, name):
    _, r, c = stack.shape
    tr = _tile(r, PACK_ROWS, 8)

    def body(s_ref, o_ref):
        g = s_ref[0]
        for j in range(1, N_DEV):
            g = g + s_ref[j]
        o_ref[...] = g

    return pl.pallas_call(
        body, name=name, grid=(r // tr,), in_specs=[pl.BlockSpec((N_DEV, tr, c), lambda i: (0, i, 0))],
        out_specs=pl.BlockSpec((tr, c), lambda i: (i, 0)), out_shape=jax.ShapeDtypeStruct((r, c), F32),
        compiler_params=_cparams(("parallel",)),
    )(stack)


def _adamw_packed(w, g, m, v, name):
    r, c = w.shape
    tr = _tile(r, PACK_ROWS, 8)

    def body(w_ref, g_ref, m_ref, v_ref, d_ref, nm_ref, nv_ref):
        delta, nm, nv = _adamw_math(w_ref[...], g_ref[...], m_ref[...], v_ref[...])
        d_ref[...] = delta
        nm_ref[...] = nm
        nv_ref[...] = nv

    spec = pl.BlockSpec((tr, c), lambda i: (i, 0))
    return pl.pallas_call(
        body, name=name, grid=(r // tr,), in_specs=[spec] * 4, out_specs=[spec] * 3,
        out_shape=[jax.ShapeDtypeStruct((r, c), F32)] * 3, compiler_params=_cparams(("parallel",)),
    )(w, g, m, v)


def _pack(arrays):
    unit = 8 * LANE
    flat = []
    for a in arrays:
        f = a.reshape(-1)
        flat.append(jnp.pad(f, (0, (-f.shape[0]) % unit)))
    total = sum(f.shape[0] for f in flat)
    flat.append(jnp.zeros(((-total) % (PACK_ROWS * LANE),), flat[0].dtype))
    return jnp.concatenate(flat).reshape(-1, LANE)


def _unpack(packed, like):
    unit = 8 * LANE
    flat = packed.reshape(-1)
    out, off = [], 0
    for a in like:
        n = 1
        for d in a.shape:
            n *= d
        out.append(flat[off:off + n].reshape(a.shape))
        off += n + (-n) % unit
    return out


class _Layout:
    def __init__(self, d):
        self.d = d
        self.n_front = 4 * GDN_WIDTH
        self.n_ab = 2 * HEADS
        self.n_back = 2 * LRU_WIDTH + 2 * d
        self.cols = self.n_front + self.n_ab + self.n_back
        self.pcols = self.n_front + self.n_back + LANE
        self.z_base = 3 * GDN_WIDTH // HEAD_DIM
        self.xb_base = self.n_front // LRU_BLOCK_DIM
        self.gg_off = self.n_front + 2 * LRU_WIDTH
        assert self.gg_off % d == 0
        self.gg_base = self.gg_off // d
        self.ab_base = (self.n_front + self.n_back) // LANE

    def _ranges(self):
        f, ab = self.n_front, self.n_ab
        return [(0, f), (f + ab, self.cols), (f, f + ab)]

    def permuted_from_stack(self, stack):
        blk = stack.shape[2]
        pieces = []
        for lo, hi in self._ranges():
            while lo < hi:
                k = lo // blk
                end = min(hi, (k + 1) * blk)
                pieces.append(stack[k, :, lo - k * blk:end - k * blk])
                lo = end
        pieces.append(jnp.zeros((stack.shape[1], LANE - self.n_ab), stack.dtype))
        return jnp.concatenate(pieces, axis=1)

    def stack_from_permuted(self, w):
        blk = self.cols // N_DEV
        where = {}
        pos = 0
        for lo, hi in self._ranges():
            where[lo] = pos
            pos += hi - lo
        cuts = sorted(lo for lo, _ in self._ranges()) + [self.cols]
        blocks = []
        for k in range(N_DEV):
            lo, pieces = k * blk, []
            while lo < (k + 1) * blk:
                start = max(c for c in cuts if c <= lo)
                end = min((k + 1) * blk, min(c for c in cuts if c > lo))
                pieces.append(w[:, where[start] + lo - start:where[start] + end - start])
                lo = end
            blocks.append(pieces[0] if len(pieces) == 1 else jnp.concatenate(pieces, axis=1))
        return jnp.stack(blocks)


def _cols_from_stack(stack):
    nd, k, n = stack.shape
    return stack.transpose(1, 0, 2).reshape(k, nd * n)


def _stack_from_cols(w):
    k, n = w.shape
    return w.reshape(k, N_DEV, n // N_DEV).transpose(1, 0, 2)


def _after(v, *tokens):
    for t in tokens:
        v = v + t[0, 0]
    return v


def _layer_fwd(x, p, late, lay, tag):
    s, d = x.shape
    nc = s // CHUNK
    h = _rms_fwd(x, p["attn_norm"], f"rms1_fwd{tag}")
    proj, ab = _matmul(h, p["w_in"], "nn", f"in_proj{tag}", [BF], tm=2048, tn=1152, tail=True)
    qkv = _conv_silu_fwd(proj, p["gdn_conv_w"], f"gdn_conv_fwd{tag}")
    gc, beta = _gdn_gates_fwd(ab, p["gdn_a_log"], p["gdn_dt_bias"], 0, f"gdn_gates_fwd{tag}")
    grow = gc[:, :HEADS].reshape(nc, CHUNK, HEADS).transpose(0, 2, 1)
    o, states, tinv = _gdn_chunk_fwd(qkv, gc, beta, grow, f"gdn_chunk_fwd{tag}")
    og = _gdn_post_fwd(o, proj, p["gdn_norm"], lay.z_base, f"gdn_post_fwd{tag}")
    ol, hs, decay = _lru_fwd(proj, p["lru_conv_w"], p["lru_conv_b"], p["lru_w_a"], p["lru_b_a"], p["lru_w_x"], p["lru_b_x"],
                      p["lru_lambda"], lay.xb_base, f"lru_fwd{tag}")
    late[0](hs)
    yg = _matmul(og, p["w_branch_gdn"], "nn", f"branch_gdn{tag}", [BF])
    yl = _matmul(ol, p["w_branch_lru"], "nn", f"branch_lru{tag}", [BF])
    merged = _merge_fwd(proj, yg, yl, lay.gg_base, f"merge_fwd{tag}")
    x1 = _matmul(merged, p["w_out"], "nn", f"out_proj{tag}", [F32], epi=lambda acc, r: (acc + r,), extras=[x])
    h2 = _rms_fwd(x1, p["mlp_norm"], f"rms2_fwd{tag}")
    late[1](h2)
    u, act = _matmul(h2, p["w_up"], "nn", f"mlp_up{tag}", [BF, BF],
                     epi=lambda acc: (acc, jnp.square(jnp.maximum(acc, 0.0))), tm=2048, shards=True)
    x2 = _matmul(act, p["w_down"], "nn", f"mlp_down{tag}", [F32], epi=lambda acc, r: (acc + r,), extras=[x1])
    saved = dict(x=x, h=h, proj=proj, ab=ab, qkv=qkv, gc=gc, beta=beta, grow=grow, o=o, states=states, tinv=tinv, og=og, ol=ol, hs=hs, decay=decay,
                 yg=yg, yl=yl, merged=merged, x1=x1, h2=h2, u=u, act=act)
    return x2, saved


def _layer_bwd(dx2, p, sv, send, send_small, after_last_send, after, lay, tag):
    s, d = dx2.shape
    nc = s // CHUNK
    small = {}
    du = _matmul(dx2, p["w_down"], "nt", f"d_act{tag}", [BF],
                 epi=lambda acc, u: (acc * (2.0 * jnp.maximum(u.astype(F32), 0.0)),), extras=[sv["u"]], after=after)
    sent = send({"w_down": _matmul(sv["act"], dx2, "tn", f"dw_down{tag}", [BF]).reshape(N_DEV, -1, d),
                 "w_up": _matmul(sv["h2"], du, "tn", f"dw_up{tag}", [BF], shards=True)})
    dh2 = _matmul(du, p["w_up"], "nt", f"d_h2{tag}", [BF], shards=True)
    dx1, g = _rms_bwd(sv["x1"], _after(p["mlp_norm"], sent), dh2, dx2, f"rms2_bwd{tag}")
    small["mlp_norm"] = g.reshape(-1)
    dmerged = _matmul(dx1, p["w_out"], "nt", f"d_merged{tag}", [BF])
    dwout = _matmul(sv["merged"], dx1, "tn", f"dw_out{tag}", [BF]).reshape(N_DEV, -1, d)
    dgg, dgl, dyg, dyl = _merge_bwd(sv["proj"], sv["yg"], sv["yl"], dmerged, lay.gg_base, f"merge_bwd{tag}")
    sent = send({"w_out": dwout,
                 "w_branch_gdn": _stack_from_cols(_matmul(sv["og"], dyg, "tn", f"dw_bg{tag}", [BF])),
                 "w_branch_lru": _stack_from_cols(_matmul(sv["ol"], dyl, "tn", f"dw_bl{tag}", [BF]))})
    dog = _matmul(dyg, p["w_branch_gdn"], "nt", f"d_og{tag}", [BF])
    dol = _matmul(dyl, p["w_branch_lru"], "nt", f"d_ol{tag}", [BF])
    (dxb, dyb, dcw, dcb, dwa, dba, dwx, dbx, dlam) = _lru_bwd(
        sv["proj"], sv["hs"], sv["decay"], dol, p["lru_conv_w"], p["lru_conv_b"], p["lru_w_a"], p["lru_b_a"], p["lru_w_x"],
        p["lru_b_x"], _after(p["lru_lambda"], sent), lay.xb_base, f"lru_bwd{tag}")
    small.update(lru_conv_w=dcw, lru_conv_b=dcb.reshape(-1), lru_w_a=dwa, lru_b_a=dba.reshape(-1), lru_w_x=dwx,
                 lru_b_x=dbx.reshape(-1), lru_lambda=dlam.reshape(-1))
    do, dz, g = _gdn_post_bwd(sv["o"], sv["proj"], p["gdn_norm"], dog, lay.z_base, f"gdn_post_bwd{tag}")
    small["gdn_norm"] = g.reshape(-1)
    dqkv_c, dgc_col, dbeta, dgrow = _gdn_chunk_bwd(sv["qkv"], sv["gc"], sv["beta"], sv["grow"], sv["states"], sv["tinv"], do,
                                                   f"gdn_chunk_bwd{tag}")
    dqkv, dgcw = _conv_silu_bwd(sv["proj"], p["gdn_conv_w"], dqkv_c, f"gdn_conv_bwd{tag}")
    small["gdn_conv_w"] = dgcw
    dgc_row = jnp.pad(dgrow.transpose(0, 2, 1).reshape(s, HEADS), ((0, 0), (0, LANE - HEADS)))
    dab, dal, ddt = _gdn_gates_bwd(sv["ab"], p["gdn_a_log"], p["gdn_dt_bias"], dgc_col, dgc_row, dbeta, 0,
                                   f"gdn_gates_bwd{tag}")
    small["gdn_a_log"] = dal[0, :HEADS]
    small["gdn_dt_bias"] = ddt[0, :HEADS]
    dproj = jnp.concatenate([dqkv, dz, dxb, dyb, dgg, dgl, dab], axis=1)
    sent = send_small(small, "main")
    dwin = _matmul(sv["h"], dproj, "tn", f"dw_in{tag}", [BF], tn=1152, after=sent)
    sent = send({"w_in": lay.stack_from_permuted(dwin)})
    dh = _matmul(dproj, p["w_in"], "nt", f"d_h{tag}", [BF], tk=3456, after=sent)
    dx, g = _rms_bwd(sv["x"], p["attn_norm"], dh, dx1, f"rms1_bwd{tag}")
    after_last_send(send_small({"attn_norm": g.reshape(-1)}, "attn"))
    return dx


_BIG = ("w_in", "w_branch_gdn", "w_branch_lru", "w_out", "w_up", "w_down")
_ROW_SHARDED = ("w_out", "w_down")
_CONV = ("gdn_conv_w", "lru_conv_w")
_SMALL = ("attn_norm", "gdn_a_log", "gdn_dt_bias", "gdn_norm", "lru_conv_b", "lru_w_a", "lru_b_a", "lru_w_x",
          "lru_b_x", "lru_lambda", "mlp_norm")
_ORDER = ("attn_norm", "w_in", "gdn_conv_w", "gdn_a_log", "gdn_dt_bias", "gdn_norm", "lru_conv_w", "lru_conv_b",
          "lru_w_a", "lru_b_a", "lru_w_x", "lru_b_x", "lru_lambda", "w_branch_gdn", "w_branch_lru", "w_out",
          "mlp_norm", "w_up", "w_down", "final_norm")


def kernel(x, attn_norm, w_in, gdn_conv_w, gdn_a_log, gdn_dt_bias, gdn_norm, lru_conv_w, lru_conv_b, lru_w_a, lru_b_a, lru_w_x, lru_b_x, lru_lambda, w_branch_gdn, w_branch_lru, w_out, mlp_norm, w_up, w_down, final_norm, loss_target, m_attn_norm, m_w_in, m_gdn_conv_w, m_gdn_a_log, m_gdn_dt_bias, m_gdn_norm, m_lru_conv_w, m_lru_conv_b, m_lru_w_a, m_lru_b_a, m_lru_w_x, m_lru_b_x, m_lru_lambda, m_w_branch_gdn, m_w_branch_lru, m_w_out, m_mlp_norm, m_w_up, m_w_down, m_final_norm, v_attn_norm, v_w_in, v_gdn_conv_w, v_gdn_a_log, v_gdn_dt_bias, v_gdn_norm, v_lru_conv_w, v_lru_conv_b, v_lru_w_a, v_lru_b_a, v_lru_w_x, v_lru_b_x, v_lru_lambda, v_w_branch_gdn, v_w_branch_lru, v_w_out, v_mlp_norm, v_w_up, v_w_down, v_final_norm):
    w = dict(attn_norm=attn_norm, w_in=w_in, gdn_conv_w=gdn_conv_w, gdn_a_log=gdn_a_log, gdn_dt_bias=gdn_dt_bias, gdn_norm=gdn_norm, lru_conv_w=lru_conv_w, lru_conv_b=lru_conv_b, lru_w_a=lru_w_a, lru_b_a=lru_b_a, lru_w_x=lru_w_x, lru_b_x=lru_b_x, lru_lambda=lru_lambda, w_branch_gdn=w_branch_gdn, w_branch_lru=w_branch_lru, w_out=w_out, mlp_norm=mlp_norm, w_up=w_up, w_down=w_down, final_norm=final_norm)
    m = dict(attn_norm=m_attn_norm, w_in=m_w_in, gdn_conv_w=m_gdn_conv_w, gdn_a_log=m_gdn_a_log, gdn_dt_bias=m_gdn_dt_bias, gdn_norm=m_gdn_norm, lru_conv_w=m_lru_conv_w, lru_conv_b=m_lru_conv_b, lru_w_a=m_lru_w_a, lru_b_a=m_lru_b_a, lru_w_x=m_lru_w_x, lru_b_x=m_lru_b_x, lru_lambda=m_lru_lambda, w_branch_gdn=m_w_branch_gdn, w_branch_lru=m_w_branch_lru, w_out=m_w_out, mlp_norm=m_mlp_norm, w_up=m_w_up, w_down=m_w_down, final_norm=m_final_norm)
    v = dict(attn_norm=v_attn_norm, w_in=v_w_in, gdn_conv_w=v_gdn_conv_w, gdn_a_log=v_gdn_a_log, gdn_dt_bias=v_gdn_dt_bias, gdn_norm=v_gdn_norm, lru_conv_w=v_lru_conv_w, lru_conv_b=v_lru_conv_b, lru_w_a=v_lru_w_a, lru_b_a=v_lru_b_a, lru_w_x=v_lru_w_x, lru_b_x=v_lru_b_x, lru_lambda=v_lru_lambda, w_branch_gdn=v_w_branch_gdn, w_branch_lru=v_w_branch_lru, w_out=v_w_out, mlp_norm=v_mlp_norm, w_up=v_w_up, w_down=v_w_down, final_norm=v_final_norm)
    depth = w_in.shape[0]
    x = x[0]
    target = loss_target[0]
    s, d = x.shape
    lay = _Layout(d)
    assert lay.cols == w_in.shape[2] * N_DEV
    cx, cy, cc = _place()
    me = 4 * cx + 2 * cy + cc

    me_arr = me.astype(jnp.int32).reshape(1)

    early, mixer, mlp = ("w_in",) + _CONV, ("w_branch_gdn", "w_branch_lru", "w_out"), ("w_up", "w_down")

    def shard(n, l):
        return w[n][l].astype(BF) if n in _BIG else w[n][l]

    def joined(name, stack):
        if name == "w_in":
            return lay.permuted_from_stack(stack)
        if name == "w_up":
            return stack
        return stack.reshape(-1, stack.shape[-1]) if name in _ROW_SHARDED else _cols_from_stack(stack)

    def start_weights(l, names, after):
        flights, token = _send_start([shard(n, l) for n in names], True, after, f"weights_start_l{l}")
        return dict(zip(names, flights)), token

    def wait_weights(l, flights, names, p, after, which):
        got = _send_wait([flights[n] for n in names], after, f"weights_wait_{which}_l{l}")
        for n, (own, landed) in zip(names, got):
            p[n] = joined(n, lax.dynamic_update_slice_in_dim(landed, own[None], me, axis=0))

    saved, params = [], []
    xl = x
    first = dict(zip(early + mixer, _all_gather([shard(n, 0) for n in early + mixer], "gather_first_weights")))
    flights, token = start_weights(0, mlp, first["w_in"])
    for l in range(depth):
        p = {n: w[n][l] for n in _SMALL}
        if l == 0:
            p.update({n: joined(n, g) for n, g in first.items()})
        else:
            wait_weights(l, flights, early, p, xl, "early")
        mine = flights
        if l + 1 < depth:
            flights, token = start_weights(l + 1, early + mixer + mlp, token)
        p["attn_norm"] = _after(p["attn_norm"], token)
        late = [functools.partial(wait_weights, l, mine, names, p, which=which) if names[0] in mine
                else (lambda after: None) for names, which in ((mixer, "mixer"), (mlp, "mlp"))]
        xl, sv = _layer_fwd(xl, p, late, lay, f"_l{l}")
        saved.append(sv)
        params.append(p)
    dx, loss_row, dfinal = _loss_head(xl, final_norm, target, "loss_head")
    loss = lax.psum(loss_row[0, 0], ("x", "y", "c"))

    out = {n: None for n in _BIG}
    pending = {l: {} for l in range(depth)}

    def sender(l):
        def send(stacks):
            names = tuple(stacks)
            fl, token = _send_start([stacks[n] for n in names], False, me_arr, f"grads_start_{'_'.join(names)}_l{l}")
            pending[l].update(zip(names, fl))
            return token
        return send

    def finish(l, names, after, which):
        got = _send_wait([pending[l][n] for n in names], after, f"grads_wait_{which}_l{l}")
        for n, (mine, landed) in zip(names, got):
            out[n] = _adamw_sharded(mine, landed, me_arr, w[n], m[n], v[n], l, out[n], after, f"adamw_{n}_l{l}")
            after = out[n][0]
        return after

    small_names = _SMALL + _CONV
    small_sent = []

    def small_sender(l):
        def send_small(small, which):
            small = dict(small)
            if which == "main" and l == depth - 1:
                small["final_norm"] = dfinal.reshape(-1)
            names = tuple(small)
            flights, token = _send_start([_pack([small[n] for n in names])], True, me_arr,
                                         f"small_grads_start_{which}_l{l}")
            small_sent.append((l, which, names, [small[n] for n in names], flights[0]))
            return token
        return send_small

    last = [None]

    def hook_for(l):
        def hook(token):
            last[0] = finish(l + 1, _BIG, token, "all") if l + 1 < depth else token
        return hook

    for l in reversed(range(depth)):
        dx = _layer_bwd(dx, params[l], saved[l], sender(l), small_sender(l), hook_for(l), last[0], lay, f"_l{l}")

    last[0] = finish(0, mixer + mlp, last[0], "late")
    got = _send_wait([f for *_, f in small_sent], last[0], "small_grads_wait")
    summed = [{} for _ in range(depth)]
    for (l, which, names, arrays, _), (part, landed) in zip(small_sent, got):
        total = _sum_devices(lax.dynamic_update_slice_in_dim(landed, part[None], me, axis=0),
                             f"sum_small_grads_{which}_l{l}")
        summed[l].update(zip(names, _unpack(total, arrays)))
    grads = {n: jnp.stack([summed[l][n] for l in range(depth)]) for n in small_names}
    grads["final_norm"] = summed[depth - 1]["final_norm"]
    for n in _CONV:
        blk = w[n].shape[2]
        grads[n] = lax.dynamic_slice_in_dim(grads[n], me * blk, blk, axis=2)
    names = small_names + ("final_norm",)
    gp = _pack([grads[n] for n in names])
    upd = _adamw_packed(_pack([w[n] for n in names]), gp, _pack([m[n] for n in names]), _pack([v[n] for n in names]),
                        "adamw_small")
    like = [w[n] for n in names]
    unp = [_unpack(u, like) for u in upd]
    for i, n in enumerate(names):
        out[n] = (grads[n], unp[0][i], unp[1][i], unp[2][i])
    finish(0, ("w_in",), upd[0], "w_in")

    res = [loss, dx.reshape(1, s, d)]
    for k in range(4):
        res += [out[n][k] for n in _ORDER]
    return tuple(res)
```

```python
import functools

import jax
import jax.numpy as jnp
from jax import lax
from jax.experimental import pallas as pl
from jax.experimental.pallas import tpu as pltpu

F32 = jnp.float32
BF = jnp.bfloat16
MESH = pl.DeviceIdType.MESH
N_DEV = 8

HEADS = 8
HEAD_DIM = 128
GDN_WIDTH = HEADS * HEAD_DIM
CHUNK = 64
CONV_WIDTH = 4
LRU_WIDTH = 1024
LRU_BLOCKS = 8
LRU_BLOCK_DIM = 128
LRU_C = 8.0
RMS_EPS = 1e-6
L2_EPS = 1e-6
ADAM_LR = 0.001
ADAM_B1 = 0.9
ADAM_B2 = 0.999
ADAM_EPS = 1e-08
ADAM_WD = 0.01
ADAM_STEP = 10

LANE = 128
PACK_ROWS = 512
VMEM_LIMIT = 56 * 1024 * 1024


def _cparams(sem=None):
    return pltpu.CompilerParams(dimension_semantics=sem, vmem_limit_bytes=VMEM_LIMIT)


def _tile(n, pref, unit=LANE):
    if n <= pref:
        return n
    best = None
    for t in range(unit, pref + 1, unit):
        if n % t == 0:
            best = t
    assert best is not None, (n, pref, unit)
    return best


_NN = (((1,), (0,)), ((), ()))
_NT = (((1,), (1,)), ((), ()))
_TN = (((0,), (0,)), ((), ()))


def _dg(a, b, dims):
    return lax.dot_general(a.astype(BF), b.astype(BF), dims, preferred_element_type=F32)


class _MM:
    def __init__(self, nn, nt, tn):
        self.nn, self.nt, self.tn = nn, nt, tn


def _make_mm(nn, nt, tn):
    def p_nn(a, b):
        return _dg(a, b, nn)

    def p_nt(a, b):
        return _dg(a, b, nt)

    def p_tn(a, b):
        return _dg(a, b, tn)

    d_nn, d_nt, d_tn = jax.custom_vjp(p_nn), jax.custom_vjp(p_nt), jax.custom_vjp(p_tn)

    def save(f):
        return lambda a, b: (f(a, b), (a, b))

    d_nn.defvjp(save(d_nn), lambda r, g: (d_nt(g, r[1]), d_tn(r[0], g)))
    d_nt.defvjp(save(d_nt), lambda r, g: (d_nn(g, r[1]), d_tn(g, r[0])))
    d_tn.defvjp(save(d_tn), lambda r, g: (d_nt(r[1], g), d_nn(r[0], g)))
    return _MM(p_nn, p_nt, p_tn), _MM(d_nn, d_nt, d_tn)


_BNN = (((2,), (1,)), ((0,), (0,)))
_BNT = (((2,), (2,)), ((0,), (0,)))
_BTN = (((1,), (1,)), ((0,), (0,)))

_PLAIN, _DIFF = _make_mm(_NN, _NT, _TN)
_PLAIN_B, _DIFF_B = _make_mm(_BNN, _BNT, _BTN)


def _sigmoid(x):
    return 1.0 / (1.0 + jnp.exp(-x))


def _silu(x):
    return x * _sigmoid(x)


def _softplus(x):
    return jnp.maximum(x, 0.0) + jnp.log(1.0 + jnp.exp(-jnp.maximum(x, -x)))


def _gelu(x):
    return 0.5 * x * (1.0 + jnp.tanh(0.7978845608028654 * (x + 0.044715 * (x * x * x))))


def _expm1(x):
    t = jnp.tanh(0.5 * x)
    return 2.0 * t / (1.0 - t)


def _rms(x, gain):
    ms = jnp.mean(x * x, axis=-1, keepdims=True)
    return x * lax.rsqrt(ms + RMS_EPS) * gain


def _l2n(x):
    return x * lax.rsqrt(jnp.sum(x * x, axis=-1, keepdims=True) + L2_EPS)


def _shift_down(x, r):
    if r == 0:
        return x
    row = lax.broadcasted_iota(jnp.int32, x.shape, 0)
    return jnp.where(row >= r, pltpu.roll(x, r, 0), 0.0)


def _shift_up(x, r):
    if r == 0:
        return x
    n = x.shape[0]
    row = lax.broadcasted_iota(jnp.int32, x.shape, 0)
    return jnp.where(row < n - r, pltpu.roll(x, n - r, 0), 0.0)


def _conv(x, w_ref):
    y = None
    for j in range(CONV_WIDTH):
        t = _shift_down(x, CONV_WIDTH - 1 - j) * w_ref[j:j + 1, :]
        y = t if y is None else y + t
    return y


def _conv_bwd(x, dy, w_ref):
    dx = None
    dws = []
    for j in range(CONV_WIDTH):
        r = CONV_WIDTH - 1 - j
        t = _shift_up(dy, r) * w_ref[j:j + 1, :]
        dx = t if dx is None else dx + t
        dws.append(jnp.sum(dy * _shift_down(x, r), axis=0, keepdims=True))
    return dx, dws


def _scan_tile(a, b, reverse):
    n = a.shape[0]
    row = lax.broadcasted_iota(jnp.int32, a.shape, 0)
    s = 1
    while s < n:
        if reverse:
            keep = row < n - s
            a_sh = jnp.where(keep, pltpu.roll(a, n - s, 0), 1.0)
            b_sh = jnp.where(keep, pltpu.roll(b, n - s, 0), 0.0)
        else:
            keep = row >= s
            a_sh = jnp.where(keep, pltpu.roll(a, s, 0), 1.0)
            b_sh = jnp.where(keep, pltpu.roll(b, s, 0), 0.0)
        b = a * b_sh + b
        a = a * a_sh
        s *= 2
    return a, b


def _dg3(a, b, dims):
    a_hi = a.astype(BF)
    b_hi = b.astype(BF)
    a_lo = a - a_hi.astype(F32)
    b_lo = b - b_hi.astype(F32)
    return _dg(a_hi, b_hi, dims) + (_dg(a_hi, b_lo, dims) + _dg(a_lo, b_hi, dims))


@jax.custom_vjp
def _tri_inv(a):
    n = a.shape[1]
    p = _dg3(a, a, _BNN)
    r = p
    e = 2
    while 2 * e < n:
        p = _dg3(p, p, _BNN)
        r = r + p + _dg3(r, p, _BNN)
        e *= 2
    row = lax.broadcasted_iota(jnp.int32, (1, n, n), 1)
    col = lax.broadcasted_iota(jnp.int32, (1, n, n), 2)
    eye = jnp.where(row == col, 1.0, 0.0)
    return eye - a + r - _dg3(a, r, _BNN)


def _tri_inv_fwd(a):
    t = _tri_inv(a)
    return t, t


def _tri_inv_bwd(t, g):
    return (-_dg3(_dg3(t, g, _BTN), t, _BNT),)


_tri_inv.defvjp(_tri_inv_fwd, _tri_inv_bwd)


@jax.custom_vjp
def _tri_inv_saved(a, t):
    return t


_tri_inv_saved.defvjp(lambda a, t: (t, t), lambda t, g: (_tri_inv_bwd(t, g)[0], jnp.zeros_like(t)))


def _gdn_chunk(q, k, v, beta, gcol, grow, state, mm, t_saved=None):
    c = q.shape[1]
    row = lax.broadcasted_iota(jnp.int32, (1, c, c), 1)
    col = lax.broadcasted_iota(jnp.int32, (1, c, c), 2)
    causal = row >= col
    strict = row > col
    qn = _l2n(q) * (HEAD_DIM ** -0.5)
    kn = _l2n(k)
    dec = jnp.where(causal, jnp.exp(jnp.where(causal, gcol - grow, 0.0)), 0.0)
    kb = kn * beta
    vb = v * beta
    a = jnp.where(strict, mm.nt(kb, kn) * dec, 0.0)
    t = _tri_inv(a) if t_saved is None else _tri_inv_saved(a, t_saved)
    eg = jnp.exp(gcol)
    u = mm.nn(t, vb)
    w = mm.nn(t, kb * eg)
    p = jnp.where(causal, mm.nt(qn, kn) * dec, 0.0)
    vn = u - mm.nn(w, state)
    o = mm.nn(qn * eg, state) + mm.nn(p, vn)
    last = lax.broadcasted_iota(jnp.int32, (1, c, 1), 1) == c - 1
    gl = jnp.sum(jnp.where(last, gcol, 0.0), axis=1, keepdims=True)
    kd = kn * jnp.exp(gl - gcol)
    new_state = state * jnp.exp(gl) + mm.tn(kd, vn)
    return o, new_state, t


def _lru_gates(xc, wa, ba, wx, bx, lam, mm):
    r = _sigmoid(mm.nn(xc, wa) + ba)
    i = _sigmoid(mm.nn(xc, wx) + bx)
    log_a = -LRU_C * r * _softplus(-lam)
    a = jnp.exp(log_a)
    bterm = jnp.sqrt(-_expm1(2.0 * log_a)) * (i * xc)
    return a, bterm


def _matmul(a, b, mode, name, out_dtypes, epi=None, extras=(), tm=1024, tn=1024, tk=2048, after=None,
            shards=False, tail=False):
    if shards:
        nd, rows, per = b.shape if mode != "tn" else (N_DEV, a.shape[1], b.shape[1] // N_DEV)
        b_shape = b.shape if mode == "tn" else (rows, nd * per)
    else:
        b_shape = b.shape
    if mode == "nn":
        (m, k), (k2, n) = a.shape, b_shape
    elif mode == "nt":
        (m, k), (n, k2) = a.shape, b_shape
    else:
        (k, m), (k2, n) = a.shape, b_shape
    assert k == k2, (a.shape, b.shape, mode)
    if shards:
        tn, tk = (min(tn, per), tk) if mode != "nt" else (tn, min(tk, per))
    tm, tn, tk = _tile(m, tm), _tile(n, tn), _tile(k, tk)
    nk = k // tk
    dims = {"nn": _NN, "nt": _NT, "tn": _TN}[mode]
    if mode == "nn":
        a_spec = pl.BlockSpec((tm, tk), lambda i, j, kk: (i, kk))
        b_spec = pl.BlockSpec((tk, tn), lambda i, j, kk: (kk, j))
        if shards:
            assert per % tn == 0
            b_spec = pl.BlockSpec((None, tk, tn), lambda i, j, kk: (j * tn // per, kk, j * tn % per // tn))
    elif mode == "nt":
        a_spec = pl.BlockSpec((tm, tk), lambda i, j, kk: (i, kk))
        b_spec = pl.BlockSpec((tn, tk), lambda i, j, kk: (j, kk))
        if shards:
            assert per % tk == 0
            b_spec = pl.BlockSpec((None, tn, tk), lambda i, j, kk: (kk * tk // per, j, kk * tk % per // tk))
    else:
        a_spec = pl.BlockSpec((tk, tm), lambda i, j, kk: (kk, i))
        b_spec = pl.BlockSpec((tk, tn), lambda i, j, kk: (kk, j))
    o_spec = pl.BlockSpec((tm, tn), lambda i, j, kk: (i, j))
    out_spec, out_dims = o_spec, (m, n)
    if shards and mode == "tn":
        assert per % tn == 0 and not extras
        out_spec = pl.BlockSpec((None, tm, tn), lambda i, j, kk: (j * tn // per, i, j * tn % per // tn))
        out_dims = (N_DEV, m, per)
    n_ex, n_out = len(extras), len(out_dtypes)
    order = [] if after is None else [after]
    n_in = 2 + n_ex + len(order)
    if epi is None:
        epi = lambda acc: (acc,)

    def body(*refs):
        a_ref, b_ref = refs[0], refs[1]
        ex_refs = refs[2:2 + n_ex]
        out_refs = refs[n_in:n_in + n_out]
        kk = pl.program_id(2)
        part = _dg(a_ref[...], b_ref[...], dims)

        def finish(total):
            res = epi(total, *[r[...] for r in ex_refs])
            for r, v in zip(out_refs, res):
                r[...] = v.astype(r.dtype)
            if tail:
                @pl.when(pl.program_id(1) == n // tn - 1)
                def _():
                    refs[n_in + n_out][...] = total[:, tn - LANE:]

        if nk == 1:
            finish(part)
            return
        acc_ref = refs[-1]

        @pl.when(kk == 0)
        def _():
            acc_ref[...] = part

        @pl.when(jnp.logical_and(kk > 0, kk < nk - 1))
        def _():
            acc_ref[...] += part

        @pl.when(kk == nk - 1)
        def _():
            finish(acc_ref[...] + part)

    outs = pl.pallas_call(
        body, name=name, grid=(m // tm, n // tn, nk),
        in_specs=[a_spec, b_spec] + [o_spec] * n_ex + [pl.BlockSpec(memory_space=pl.ANY)] * len(order),
        out_specs=[out_spec] * n_out + [pl.BlockSpec((tm, LANE), lambda i, j, kk: (i, 0))] * tail,
        out_shape=[jax.ShapeDtypeStruct(out_dims, d) for d in out_dtypes]
        + [jax.ShapeDtypeStruct((m, LANE), F32)] * tail,
        scratch_shapes=[pltpu.VMEM((tm, tn), F32)] if nk > 1 else [],
        compiler_params=_cparams(("parallel", "arbitrary" if tail else "parallel", "arbitrary")),
    )(a, b, *extras, *order)
    return outs[0] if n_out + tail == 1 else outs


def _rowwise(fn, name, rows, tr, row_ins, params, row_outs, acc_outs=(), ncol=1):
    tr = _tile(rows, tr, 8)
    n_in, n_par, n_ro, n_acc = len(row_ins), len(params), len(row_outs), len(acc_outs)
    in_specs = []
    for _, width, base, per_col in row_ins:
        if per_col:
            in_specs.append(pl.BlockSpec((tr, width), lambda i, j, base=base: (i, base + j)))
        else:
            in_specs.append(pl.BlockSpec((tr, width), lambda i, j, base=base: (i, base)))
    for p in params:
        in_specs.append(pl.BlockSpec(p.shape, lambda i, j: (0, 0)))
    out_specs, out_shape = [], []
    for total, dtype, width, per_col in row_outs:
        if per_col:
            out_specs.append(pl.BlockSpec((tr, width), lambda i, j: (i, j)))
        else:
            out_specs.append(pl.BlockSpec((tr, width), lambda i, j: (i, 0)))
        out_shape.append(jax.ShapeDtypeStruct((rows, total), dtype))
    for shape, dtype in acc_outs:
        out_specs.append(pl.BlockSpec(shape, lambda i, j: (0, 0)))
        out_shape.append(jax.ShapeDtypeStruct(shape, dtype))

    def body(*refs):
        ins = [r[...].astype(F32) for r in refs[:n_in + n_par]]
        ro = refs[n_in + n_par:n_in + n_par + n_ro]
        ao = refs[n_in + n_par + n_ro:]
        res = fn(*ins)
        for r, v in zip(ro, res[:n_ro]):
            r[...] = v.astype(r.dtype)
        first = jnp.logical_and(pl.program_id(0) == 0, pl.program_id(1) == 0)

        @pl.when(first)
        def _():
            for r in ao:
                r[...] = jnp.zeros_like(r)

        for r, v in zip(ao, res[n_ro:]):
            r[...] += v.astype(r.dtype)

    sem = ("arbitrary", "arbitrary") if n_acc else ("parallel", "parallel")
    outs = pl.pallas_call(
        body, name=name, grid=(rows // tr, ncol), in_specs=in_specs, out_specs=out_specs,
        out_shape=out_shape, compiler_params=_cparams(sem),
    )(*[r[0] for r in row_ins], *params)
    return outs


def _row(x):
    return x.reshape(1, -1)


def _rms_fwd(x, gain, name):
    s, d = x.shape
    return _rowwise(lambda xt, g: (_rms(xt, g),), name, s, 512, [(x, d, 0, False)], [_row(gain)],
                    [(d, BF, d, False)])[0]


def _rms_bwd(x, gain, dh, dres, name):
    s, d = x.shape

    def fn(xt, dht, drt, g):
        _, vjp = jax.vjp(_rms, xt, g)
        dx, dg = vjp(dht)
        return drt + dx, dg

    return _rowwise(fn, name, s, 256, [(x, d, 0, False), (dh, d, 0, False), (dres, d, 0, False)], [_row(gain)],
                    [(d, F32, d, False)], [((1, d), F32)])


def _loss_head(x, gain, target, name):
    s, d = x.shape

    def fn(xt, tt, g):
        y, vjp = jax.vjp(_rms, xt, g)
        err = y - tt
        dx, dg = vjp(err * (1.0 / d))
        part = 0.5 * jnp.sum(jnp.sum(err * err, axis=1, keepdims=True), axis=0, keepdims=True) * (1.0 / d)
        return dx, jnp.broadcast_to(part, (1, LANE)), dg

    return _rowwise(fn, name, s, 256, [(x, d, 0, False), (target, d, 0, False)], [_row(gain)],
                    [(d, F32, d, False)], [((1, LANE), F32), ((1, d), F32)])


def _gdn_post_fn(o, z, g):
    return _rms(o, g) * _silu(z)


def _gdn_post_fwd(o, proj, gain, z_base, name):
    s = o.shape[0]
    return _rowwise(lambda ot, zt, g: (_gdn_post_fn(ot, zt, g),), name, s, 1024,
                    [(o, HEAD_DIM, 0, True), (proj, HEAD_DIM, z_base, True)], [_row(gain)],
                    [(GDN_WIDTH, BF, HEAD_DIM, True)], ncol=HEADS)[0]


def _gdn_post_bwd(o, proj, gain, dog, z_base, name):
    s = o.shape[0]

    def fn(ot, zt, dt, g):
        _, vjp = jax.vjp(_gdn_post_fn, ot, zt, g)
        return vjp(dt)

    return _rowwise(fn, name, s, 1024,
                    [(o, HEAD_DIM, 0, True), (proj, HEAD_DIM, z_base, True), (dog, HEAD_DIM, 0, True)],
                    [_row(gain)], [(GDN_WIDTH, F32, HEAD_DIM, True), (GDN_WIDTH, BF, HEAD_DIM, True)],
                    [((1, HEAD_DIM), F32)], ncol=HEADS)


def _merge_fn(gg, gl, yg, yl):
    return _sigmoid(gg) * yg + _sigmoid(gl) * yl


def _merge_fwd(proj, yg, yl, gg_base, name):
    s, d = yg.shape
    return _rowwise(lambda a, b, c, e: (_merge_fn(a, b, c, e),), name, s, 256,
                    [(proj, d, gg_base, False), (proj, d, gg_base + 1, False), (yg, d, 0, False), (yl, d, 0, False)],
                    [], [(d, BF, d, False)])[0]


def _merge_bwd(proj, yg, yl, dmerged, gg_base, name):
    s, d = yg.shape

    def fn(a, b, c, e, dm):
        _, vjp = jax.vjp(_merge_fn, a, b, c, e)
        return vjp(dm)

    return _rowwise(fn, name, s, 128,
                    [(proj, d, gg_base, False), (proj, d, gg_base + 1, False), (yg, d, 0, False), (yl, d, 0, False),
                     (dmerged, d, 0, False)], [], [(d, BF, d, False)] * 4)


def _conv_silu_fwd(proj, w, name):
    s = proj.shape[0]
    cols = w.shape[1]
    cw = _tile(cols, LANE)

    def body(x_ref, w_ref, o_ref):
        o_ref[...] = _silu(_conv(x_ref[...].astype(F32), w_ref))

    return pl.pallas_call(
        body, name=name, grid=(cols // cw,),
        in_specs=[pl.BlockSpec((s, cw), lambda j: (0, j)), pl.BlockSpec((CONV_WIDTH, cw), lambda j: (0, j))],
        out_specs=pl.BlockSpec((s, cw), lambda j: (0, j)),
        out_shape=jax.ShapeDtypeStruct((s, cols), F32), compiler_params=_cparams(("parallel",)),
    )(proj, w)


def _conv_silu_bwd(proj, w, dout, name):
    s = proj.shape[0]
    cols = w.shape[1]
    cw = _tile(cols, LANE)

    def body(x_ref, w_ref, d_ref, dx_ref, dw_ref):
        x = x_ref[...].astype(F32)
        y = _conv(x, w_ref)
        sg = _sigmoid(y)
        dy = d_ref[...] * (sg * (1.0 + y * (1.0 - sg)))
        dx, dws = _conv_bwd(x, dy, w_ref)
        dx_ref[...] = dx.astype(dx_ref.dtype)
        for j in range(CONV_WIDTH):
            dw_ref[j:j + 1, :] = dws[j]

    return pl.pallas_call(
        body, name=name, grid=(cols // cw,),
        in_specs=[pl.BlockSpec((s, cw), lambda j: (0, j)), pl.BlockSpec((CONV_WIDTH, cw), lambda j: (0, j)),
                  pl.BlockSpec((s, cw), lambda j: (0, j))],
        out_specs=[pl.BlockSpec((s, cw), lambda j: (0, j)), pl.BlockSpec((CONV_WIDTH, cw), lambda j: (0, j))],
        out_shape=[jax.ShapeDtypeStruct((s, cols), BF), jax.ShapeDtypeStruct((CONV_WIDTH, cols), F32)],
        compiler_params=_cparams(("parallel",)),
    )(proj, w, dout)


def _cumsum_mask(tr, transpose):
    row = lax.broadcasted_iota(jnp.int32, (tr, tr), 0)
    col = lax.broadcasted_iota(jnp.int32, (tr, tr), 1)
    same = (row // CHUNK) == (col // CHUNK)
    tri = (row <= col) if transpose else (row >= col)
    return jnp.where(jnp.logical_and(same, tri), 1.0, 0.0)


def _lane_pad(v):
    return jnp.pad(v.reshape(1, -1), ((0, 0), (0, LANE - v.shape[0])))


def _gdn_gates_fwd(proj, a_log, dt_bias, ab_base, name):
    s = proj.shape[0]

    def fn(ab, al, dt):
        g = -jnp.exp(al) * _softplus(ab + dt)
        beta = _sigmoid(pltpu.roll(ab, LANE - HEADS, 1))
        gc = jnp.dot(_cumsum_mask(ab.shape[0], False), g, precision=lax.Precision.HIGHEST,
                     preferred_element_type=F32)
        return gc, beta

    return _rowwise(fn, name, s, 512, [(proj, LANE, ab_base, False)], [_lane_pad(a_log), _lane_pad(dt_bias)],
                    [(LANE, F32, LANE, False)] * 2)


def _gdn_gates_bwd(proj, a_log, dt_bias, dgc_col, dgc_row, dbeta, ab_base, name):
    s = proj.shape[0]

    def fn(ab, dc, dr, db, al, dt):
        lane = lax.broadcasted_iota(jnp.int32, ab.shape, 1)
        dg = jnp.dot(_cumsum_mask(ab.shape[0], True), dc + dr, precision=lax.Precision.HIGHEST,
                     preferred_element_type=F32)
        ea = jnp.exp(al)
        pre = ab + dt
        g = -ea * _softplus(pre)
        da = jnp.where(lane < HEADS, dg * (-ea) * _sigmoid(pre), 0.0)
        dal = jnp.sum(jnp.where(lane < HEADS, dg * g, 0.0), axis=0, keepdims=True)
        ddt = jnp.sum(da, axis=0, keepdims=True)
        beta = _sigmoid(pltpu.roll(ab, LANE - HEADS, 1))
        dbl = jnp.where(lane < HEADS, db * beta * (1.0 - beta), 0.0)
        dab = da + pltpu.roll(dbl, HEADS, 1)
        return dab, dal, ddt

    return _rowwise(fn, name, s, 512,
                    [(proj, LANE, ab_base, False), (dgc_col, LANE, 0, False), (dgc_row, LANE, 0, False),
                     (dbeta, LANE, 0, False)], [_lane_pad(a_log), _lane_pad(dt_bias)],
                    [(LANE, BF, LANE, False)], [((1, LANE), F32), ((1, LANE), F32)])


def _head_col(blk, h):
    lane = lax.broadcasted_iota(jnp.int32, blk.shape, 1)
    return jnp.sum(jnp.where(lane == h, blk, 0.0), axis=1, keepdims=True)


def _chunk_operands(qkv_ref, gc_ref, b_ref, gr_ref):
    heads = range(HEADS)
    q, k, v = (jnp.stack([qkv_ref[:, (j * HEADS + h) * HEAD_DIM:(j * HEADS + h + 1) * HEAD_DIM] for h in heads])
               for j in range(3))
    gcs, betas = gc_ref[...], b_ref[...]
    beta = jnp.stack([_head_col(betas, h) for h in heads])
    gcol = jnp.stack([_head_col(gcs, h) for h in heads])
    grow = jnp.stack([gr_ref[0, h:h + 1, :] for h in heads])
    return q, k, v, beta, gcol, grow


def _gdn_chunk_fwd(qkv, gc, beta, grow, name):
    s = qkv.shape[0]
    nc = s // CHUNK

    def body(qkv_ref, gc_ref, b_ref, gr_ref, o_ref, st_ref, t_ref, state):
        @pl.when(pl.program_id(0) == 0)
        def _():
            state[...] = jnp.zeros_like(state)

        st = state[...]
        st_ref[0] = st
        o, new, t = _gdn_chunk(*_chunk_operands(qkv_ref, gc_ref, b_ref, gr_ref), st, _PLAIN_B)
        t_ref[0] = t
        for h in range(HEADS):
            o_ref[:, h * HEAD_DIM:(h + 1) * HEAD_DIM] = o[h]
        state[...] = new

    return pl.pallas_call(
        body, name=name, grid=(nc,),
        in_specs=[pl.BlockSpec((CHUNK, 3 * GDN_WIDTH), lambda c: (c, 0)),
                  pl.BlockSpec((CHUNK, LANE), lambda c: (c, 0)), pl.BlockSpec((CHUNK, LANE), lambda c: (c, 0)),
                  pl.BlockSpec((1, HEADS, CHUNK), lambda c: (c, 0, 0))],
        out_specs=[pl.BlockSpec((CHUNK, GDN_WIDTH), lambda c: (c, 0)),
                   pl.BlockSpec((1, HEADS, HEAD_DIM, HEAD_DIM), lambda c: (c, 0, 0, 0)),
                   pl.BlockSpec((1, HEADS, CHUNK, CHUNK), lambda c: (c, 0, 0, 0))],
        out_shape=[jax.ShapeDtypeStruct((s, GDN_WIDTH), F32),
                   jax.ShapeDtypeStruct((nc, HEADS, HEAD_DIM, HEAD_DIM), F32),
                   jax.ShapeDtypeStruct((nc, HEADS, CHUNK, CHUNK), F32)],
        scratch_shapes=[pltpu.VMEM((HEADS, HEAD_DIM, HEAD_DIM), F32)],
        compiler_params=_cparams(("arbitrary",)),
    )(qkv, gc, beta, grow)


def _gdn_chunk_bwd(qkv, gc, beta, grow, states, tinv, do, name):
    s = qkv.shape[0]
    nc = s // CHUNK

    def body(qkv_ref, gc_ref, b_ref, gr_ref, st_ref, t_ref, do_ref, dqkv_ref, dgc_ref, db_ref, dgr_ref, dstate):
        @pl.when(pl.program_id(0) == 0)
        def _():
            dstate[...] = jnp.zeros_like(dstate)

        lane = lax.broadcasted_iota(jnp.int32, (CHUNK, LANE), 1)
        chunk = lambda *args: _gdn_chunk(*args, mm=_DIFF_B, t_saved=t_ref[0])[:2]
        _, vjp = jax.vjp(chunk, *_chunk_operands(qkv_ref, gc_ref, b_ref, gr_ref), st_ref[0])
        do = jnp.stack([do_ref[:, h * HEAD_DIM:(h + 1) * HEAD_DIM] for h in range(HEADS)])
        dq, dk, dv, dbeta, dgcol, dgrow, dst = vjp((do, dstate[...]))
        dstate[...] = dst
        dgc = jnp.zeros((CHUNK, LANE), F32)
        db = jnp.zeros((CHUNK, LANE), F32)
        for h in range(HEADS):
            for j, dx in enumerate((dq, dk, dv)):
                dqkv_ref[:, (j * HEADS + h) * HEAD_DIM:(j * HEADS + h + 1) * HEAD_DIM] = dx[h]
            dgc = dgc + jnp.where(lane == h, dgcol[h], 0.0)
            db = db + jnp.where(lane == h, dbeta[h], 0.0)
            dgr_ref[0, h:h + 1, :] = dgrow[h]
        dgc_ref[...] = dgc
        db_ref[...] = db

    rev = lambda c: nc - 1 - c
    wide = pl.BlockSpec((CHUNK, 3 * GDN_WIDTH), lambda c: (rev(c), 0))
    lanes = pl.BlockSpec((CHUNK, LANE), lambda c: (rev(c), 0))
    rows = pl.BlockSpec((1, HEADS, CHUNK), lambda c: (rev(c), 0, 0))
    return pl.pallas_call(
        body, name=name, grid=(nc,),
        in_specs=[wide, lanes, lanes, rows,
                  pl.BlockSpec((1, HEADS, HEAD_DIM, HEAD_DIM), lambda c: (rev(c), 0, 0, 0)),
                  pl.BlockSpec((1, HEADS, CHUNK, CHUNK), lambda c: (rev(c), 0, 0, 0)),
                  pl.BlockSpec((CHUNK, GDN_WIDTH), lambda c: (rev(c), 0))],
        out_specs=[wide, lanes, lanes, rows],
        out_shape=[jax.ShapeDtypeStruct((s, 3 * GDN_WIDTH), F32)]
        + [jax.ShapeDtypeStruct((s, LANE), F32)] * 2 + [jax.ShapeDtypeStruct((nc, HEADS, CHUNK), F32)],
        scratch_shapes=[pltpu.VMEM((HEADS, HEAD_DIM, HEAD_DIM), F32)],
        compiler_params=_cparams(("arbitrary",)),
    )(qkv, gc, beta, grow, states, tinv, do)


def _lru_specs(s, xb_base):
    col = lambda base: pl.BlockSpec((s, LRU_BLOCK_DIM), lambda n, base=base: (0, base + n))
    vec = pl.BlockSpec((1, LRU_BLOCK_DIM), lambda n: (0, n))
    mat = pl.BlockSpec((None, LRU_BLOCK_DIM, LRU_BLOCK_DIM), lambda n: (n, 0, 0))
    cw = pl.BlockSpec((CONV_WIDTH, LRU_BLOCK_DIM), lambda n: (0, n))
    return col, vec, mat, cw


def _lru_fwd(proj, cw, cb, wa, ba, wx, bx, lam, xb_base, name):
    s = proj.shape[0]
    tt = _tile(s, 256, 8)
    col, vec, mat, cws = _lru_specs(s, xb_base)

    def body(xb_ref, yb_ref, cw_ref, cb_ref, wa_ref, ba_ref, wx_ref, bx_ref, lam_ref, o_ref, h_ref, a_ref, xc_s):
        xc_s[...] = _conv(xb_ref[...].astype(F32), cw_ref) + cb_ref[...]
        par = (wa_ref[...], ba_ref[...], wx_ref[...], bx_ref[...], lam_ref[...])

        def step(t, carry):
            sl = pl.ds(pl.multiple_of(t * tt, tt), tt)
            a, bt = _lru_gates(xc_s[sl, :], *par, _PLAIN)
            a_ref[sl, :] = a
            aa, bb = _scan_tile(a, bt, False)
            h = aa * carry + bb
            h_ref[sl, :] = h
            o_ref[sl, :] = (h * _gelu(yb_ref[sl, :].astype(F32))).astype(o_ref.dtype)
            return h_ref[pl.ds(t * tt + tt - 1, 1), :]

        lax.fori_loop(0, s // tt, step, jnp.zeros((1, LRU_BLOCK_DIM), F32))

    nb = LRU_BLOCKS
    return pl.pallas_call(
        body, name=name, grid=(LRU_BLOCKS,),
        in_specs=[col(xb_base), col(xb_base + nb), cws, vec, mat, vec, mat, vec, vec],
        out_specs=[col(0), col(0), col(0)],
        out_shape=[jax.ShapeDtypeStruct((s, LRU_WIDTH), BF), jax.ShapeDtypeStruct((s, LRU_WIDTH), F32),
                   jax.ShapeDtypeStruct((s, LRU_WIDTH), F32)],
        scratch_shapes=[pltpu.VMEM((s, LRU_BLOCK_DIM), F32)],
        compiler_params=_cparams(("parallel",)),
    )(proj, proj, cw, _row(cb), wa, _row(ba), wx, _row(bx), _row(lam))


def _lru_bwd(proj, hs, decay, dol, cw, cb, wa, ba, wx, bx, lam, xb_base, name):
    s = proj.shape[0]
    tt = _tile(s, 256, 8)
    nt = s // tt
    col, vec, mat, cws = _lru_specs(s, xb_base)

    def body(xb_ref, yb_ref, h_ref, a_ref, d_ref, cw_ref, cb_ref, wa_ref, ba_ref, wx_ref, bx_ref, lam_ref,
             dxb_ref, dyb_ref, dcw_ref, dcb_ref, dwa_ref, dba_ref, dwx_ref, dbx_ref, dlam_ref,
             xc_s, dh_s, dxc_s):
        xc_s[...] = _conv(xb_ref[...].astype(F32), cw_ref) + cb_ref[...]
        par = (wa_ref[...], ba_ref[...], wx_ref[...], bx_ref[...], lam_ref[...])
        row = lax.broadcasted_iota(jnp.int32, (tt, LRU_BLOCK_DIM), 0)
        tile = lambda t: pl.ds(pl.multiple_of(t * tt, tt), tt)

        def prep(t, carry):
            sl = tile(t)
            d = d_ref[sl, :].astype(F32)
            gy, vjp_y = jax.vjp(_gelu, yb_ref[sl, :].astype(F32))
            dyb_ref[sl, :] = vjp_y(d * h_ref[sl, :])[0].astype(dyb_ref.dtype)
            dh_s[sl, :] = d * gy
            return carry

        lax.fori_loop(0, nt, prep, 0)

        def rscan(i, carry):
            dh_next, a_next = carry
            t = nt - 1 - i
            sl = tile(t)
            a_sh = jnp.where(row == tt - 1, a_next, pltpu.roll(a_ref[sl, :], tt - 1, 0))
            aa, bb = _scan_tile(a_sh, dh_s[sl, :], True)
            dh_s[sl, :] = aa * dh_next + bb
            first = pl.ds(t * tt, 1)
            return dh_s[first, :], a_ref[first, :]

        zero = jnp.zeros((1, LRU_BLOCK_DIM), F32)
        lax.fori_loop(0, nt, rscan, (zero, zero))

        def gates_vjp(t, acc):
            sl = tile(t)
            _, vjp_g = jax.vjp(functools.partial(_lru_gates, mm=_DIFF), xc_s[sl, :], *par)
            dh = dh_s[sl, :]
            before = jnp.where(t > 0, h_ref[pl.ds(jnp.maximum(t * tt - 1, 0), 1), :], 0.0)
            h_prev = jnp.where(row == 0, before, pltpu.roll(h_ref[sl, :], 1, 0))
            dxc, *dpar = vjp_g((dh * h_prev, dh))
            dxc_s[sl, :] = dxc
            return tuple(x + y for x, y in zip(acc, dpar))

        dwa, dba, dwx, dbx, dlam = lax.fori_loop(0, nt, gates_vjp, tuple(jnp.zeros_like(p) for p in par))
        dwa_ref[...] = dwa
        dba_ref[...] = dba
        dwx_ref[...] = dwx
        dbx_ref[...] = dbx
        dlam_ref[...] = dlam
        dxc = dxc_s[...]
        dcb_ref[...] = jnp.sum(dxc, axis=0, keepdims=True)
        dxb, dws = _conv_bwd(xb_ref[...].astype(F32), dxc, cw_ref)
        dxb_ref[...] = dxb.astype(dxb_ref.dtype)
        for j in range(CONV_WIDTH):
            dcw_ref[j:j + 1, :] = dws[j]

    nb = LRU_BLOCKS
    w = LRU_WIDTH
    return pl.pallas_call(
        body, name=name, grid=(LRU_BLOCKS,),
        in_specs=[col(xb_base), col(xb_base + nb), col(0), col(0), col(0), cws, vec, mat, vec, mat, vec, vec],
        out_specs=[col(0), col(0), cws, vec, mat, vec, mat, vec, vec],
        out_shape=[jax.ShapeDtypeStruct((s, w), BF), jax.ShapeDtypeStruct((s, w), BF),
                   jax.ShapeDtypeStruct((CONV_WIDTH, w), F32), jax.ShapeDtypeStruct((1, w), F32),
                   jax.ShapeDtypeStruct(wa.shape, F32), jax.ShapeDtypeStruct((1, w), F32),
                   jax.ShapeDtypeStruct(wx.shape, F32), jax.ShapeDtypeStruct((1, w), F32),
                   jax.ShapeDtypeStruct((1, w), F32)],
        scratch_shapes=[pltpu.VMEM((s, LRU_BLOCK_DIM), F32)] * 3,
        compiler_params=_cparams(("parallel",)),
    )(proj, proj, hs, decay, dol, cw, _row(cb), wa, _row(ba), wx, _row(bx), _row(lam))


def _place():
    return lax.axis_index("x"), lax.axis_index("y"), lax.axis_index("c")


def _all_gather(blocks, name):
    n = len(blocks)
    per = N_DEV - 1

    def body(*refs):
        ins, outs = refs[:n], refs[n:2 * n]
        send_sems, recv_sems, local_sems = refs[2 * n:]
        x, y, c = _place()
        me, sibling = (x, y, c), (x, y, 1 - c)
        chips = [(1 - x, y), (x, 1 - y), (1 - x, 1 - y)]

        def copy(a, k, block, to, src=None):
            dst = outs[a].at[4 * block[0] + 2 * block[1] + block[2]]
            return pltpu.make_async_remote_copy(
                src_ref=dst if src is None else src, dst_ref=dst, send_sem=send_sems.at[a * per + k],
                recv_sem=recv_sems.at[a * per + k], device_id=to, device_id_type=MESH)

        started = []
        for a in range(n):
            mine = pltpu.make_async_copy(ins[a], outs[a].at[4 * x + 2 * y + c], local_sems.at[a])
            mine.start()
            started.append(mine)
        sends = []
        for a in range(n):
            first = [copy(a, 0, me, sibling, src=ins[a])]
            first += [copy(a, 1 + j, me, (*chip, c), src=ins[a]) for j, chip in enumerate(chips)]
            for cp in first:
                cp.start()
            sends += first
        for a in range(n):
            for j, chip in enumerate(chips):
                copy(a, 1 + j, (*chip, c), me).wait_recv()
                passed = copy(a, 4 + j, (*chip, c), sibling)
                passed.start()
                sends.append(passed)
        for a in range(n):
            copy(a, 0, sibling, me).wait_recv()
            for j, chip in enumerate(chips):
                copy(a, 4 + j, (*chip, 1 - c), me).wait_recv()
        for cp in sends:
            cp.wait_send()
        for cp in started:
            cp.wait()

    any_spec = pl.BlockSpec(memory_space=pl.ANY)
    return pl.pallas_call(
        body, name=name, in_specs=[any_spec] * n, out_specs=[any_spec] * n,
        out_shape=[jax.ShapeDtypeStruct((N_DEV,) + b.shape, b.dtype) for b in blocks],
        scratch_shapes=[pltpu.SemaphoreType.DMA((n * per,)), pltpu.SemaphoreType.DMA((n * per,)),
                        pltpu.SemaphoreType.DMA((n,))],
    )(*blocks)


_HBM = pl.BlockSpec(memory_space=pltpu.HBM)
_SEM = pl.BlockSpec(memory_space=pltpu.SEMAPHORE)
_ANY = pl.BlockSpec(memory_space=pl.ANY)
_EFFECT = pltpu.SideEffectType.DATAFLOW_SIDE_EFFECTING


def _peers(x, y, c):
    return [(x, y, 1 - c), (1 - x, y, c), (1 - x, y, 1 - c), (x, 1 - y, c), (x, 1 - y, 1 - c),
            (1 - x, 1 - y, c), (1 - x, 1 - y, 1 - c)]


class _InFlight:
    def __init__(self, send_sem, recv_sem, src, land):
        self.send_sem, self.recv_sem, self.src, self.land = send_sem, recv_sem, src, land


def _send_start(srcs, whole, after, name):
    n = len(srcs)
    lands = [lax.empty(((N_DEV,) + s.shape) if whole else s.shape, s.dtype) for s in srcs]

    def body(*refs):
        src_refs, land_refs = refs[:n], refs[n:2 * n]
        outs = refs[2 * n + 1:]
        send_sems, recv_sems, token = outs[:n], outs[n:2 * n], outs[-1]
        x, y, c = _place()
        me = 4 * x + 2 * y + c
        for a in range(n):
            for p in _peers(x, y, c):
                src = src_refs[a] if whole else src_refs[a].at[4 * p[0] + 2 * p[1] + p[2]]
                pltpu.make_async_remote_copy(src_ref=src, dst_ref=land_refs[a].at[me], send_sem=send_sems[a],
                                             recv_sem=recv_sems[a], device_id=p, device_id_type=MESH).start()
        token[...] = jnp.zeros_like(token)

    hbm = lambda a: pltpu.HBM(a.shape, a.dtype)
    outs = pl.pallas_call(
        body, name=name,
        out_shape=[pltpu.SemaphoreType.DMA(())] * (2 * n) + [hbm(s) for s in srcs] + [hbm(l) for l in lands]
        + [jax.ShapeDtypeStruct((8, LANE), F32)],
        in_specs=[_HBM] * (2 * n) + [_ANY],
        out_specs=[_SEM] * (2 * n) + [_HBM] * (2 * n) + [pl.BlockSpec(memory_space=pltpu.VMEM)],
        input_output_aliases={i: 2 * n + i for i in range(2 * n)},
        compiler_params=pltpu.CompilerParams(has_side_effects=_EFFECT),
    )(*[pltpu.with_memory_space_constraint(s, pltpu.HBM) for s in srcs],
      *[pltpu.with_memory_space_constraint(l, pltpu.HBM) for l in lands], after)
    flights = [_InFlight(outs[a], outs[n + a], outs[2 * n + a], outs[3 * n + a]) for a in range(n)]
    return flights, outs[-1]


def _send_wait(flights, after, name):
    n = len(flights)

    def body(*refs):
        x, y, c = _place()
        for a in range(n):
            land_ref, send_sem, recv_sem = refs[4 * a + 1], refs[4 * a + 2], refs[4 * a + 3]
            seven = land_ref.at[pl.ds(0, N_DEV - 1)]
            copy = pltpu.make_async_remote_copy(src_ref=seven, dst_ref=seven, send_sem=send_sem, recv_sem=recv_sem,
                                                device_id=(x, y, 1 - c), device_id_type=MESH)
            copy.wait_send()
            copy.wait_recv()

    ins, out_shape = [], []
    for f in flights:
        ins += [f.src, f.land, f.send_sem, f.recv_sem]
        out_shape += [pltpu.HBM(f.src.shape, f.src.dtype), pltpu.HBM(f.land.shape, f.land.dtype)]
    outs = pl.pallas_call(
        body, name=name, out_shape=out_shape,
        in_specs=[_HBM, _HBM, _SEM, _SEM] * n + [_ANY], out_specs=[_HBM] * (2 * n),
        input_output_aliases={4 * a + i: 2 * a + i for a in range(n) for i in range(2)},
        compiler_params=pltpu.CompilerParams(has_side_effects=_EFFECT),
    )(*ins, after)
    return [(outs[2 * a], outs[2 * a + 1]) for a in range(n)]


def _adamw_math(w, g, m, v):
    m = ADAM_B1 * m + (1.0 - ADAM_B1) * g
    v = ADAM_B2 * v + (1.0 - ADAM_B2) * (g * g)
    m_hat = m / (1.0 - ADAM_B1 ** ADAM_STEP)
    v_hat = v / (1.0 - ADAM_B2 ** ADAM_STEP)
    delta = -ADAM_LR * (m_hat / (jnp.sqrt(v_hat) + ADAM_EPS) + ADAM_WD * w)
    return delta, m, v


def _adamw_sharded(mine, landed, me, w, m, v, layer, prev, after, name):
    depth, r, c = w.shape
    tr = _tile(r, 128, 8)
    n_prev = 0 if prev is None else 4

    def body(*refs):
        me_ref, own_ref, land_ref, w_ref, m_ref, v_ref = refs[:6]
        g_ref, d_ref, nm_ref, nv_ref = refs[7 + n_prev:]
        g = None
        for k in range(N_DEV):
            part = jnp.where(me_ref[0] == k, own_ref[0], land_ref[k]).astype(F32)
            g = part if g is None else g + part
        delta, nm, nv = _adamw_math(w_ref[...], g, m_ref[...], v_ref[...])
        g_ref[...] = g
        d_ref[...] = delta
        nm_ref[...] = nm
        nv_ref[...] = nv

    lay = pl.BlockSpec((None, tr, c), lambda i, me_ref: (layer, i, 0))
    return pl.pallas_call(
        body, name=name,
        grid_spec=pltpu.PrefetchScalarGridSpec(
            num_scalar_prefetch=1, grid=(r // tr,),
            in_specs=[pl.BlockSpec((1, tr, c), lambda i, me_ref: (me_ref[0], i, 0)),
                      pl.BlockSpec((N_DEV, tr, c), lambda i, me_ref: (0, i, 0)), lay, lay, lay]
            + [_ANY] * (1 + n_prev),
            out_specs=[lay] * 4),
        out_shape=[jax.ShapeDtypeStruct(w.shape, F32)] * 4,
        input_output_aliases={7 + i: i for i in range(n_prev)},
        compiler_params=_cparams(("parallel",)),
    )(me, mine, landed, w, m, v, after, *(prev or ()))


def _sum_devices(stack, name):
    _, r, c = stack.shape
    tr = _tile(r, PACK_ROWS, 8)

    def body(s_ref, o_ref):
        g = s_ref[0]
        for j in range(1, N_DEV):
            g = g + s_ref[j]
        o_ref[...] = g

    return pl.pallas_call(
        body, name=name, grid=(r // tr,), in_specs=[pl.BlockSpec((N_DEV, tr, c), lambda i: (0, i, 0))],
        out_specs=pl.BlockSpec((tr, c), lambda i: (i, 0)), out_shape=jax.ShapeDtypeStruct((r, c), F32),
        compiler_params=_cparams(("parallel",)),
    )(stack)


def _adamw_packed(w, g, m, v, name):
    r, c = w.shape
    tr = _tile(r, PACK_ROWS, 8)

    def body(w_ref, g_ref, m_ref, v_ref, d_ref, nm_ref, nv_ref):
        delta, nm, nv = _adamw_math(w_ref[...], g_ref[...], m_ref[...], v_ref[...])
        d_ref[...] = delta
        nm_ref[...] = nm
        nv_ref[...] = nv

    spec = pl.BlockSpec((tr, c), lambda i: (i, 0))
    return pl.pallas_call(
        body, name=name, grid=(r // tr,), in_specs=[spec] * 4, out_specs=[spec] * 3,
        out_shape=[jax.ShapeDtypeStruct((r, c), F32)] * 3, compiler_params=_cparams(("parallel",)),
    )(w, g, m, v)


def _pack(arrays):
    unit = 8 * LANE
    flat = []
    for a in arrays:
        f = a.reshape(-1)
        flat.append(jnp.pad(f, (0, (-f.shape[0]) % unit)))
    total = sum(f.shape[0] for f in flat)
    flat.append(jnp.zeros(((-total) % (PACK_ROWS * LANE),), flat[0].dtype))
    return jnp.concatenate(flat).reshape(-1, LANE)


def _unpack(packed, like):
    unit = 8 * LANE
    flat = packed.reshape(-1)
    out, off = [], 0
    for a in like:
        n = 1
        for d in a.shape:
            n *= d
        out.append(flat[off:off + n].reshape(a.shape))
        off += n + (-n) % unit
    return out


class _Layout:
    def __init__(self, d):
        self.d = d
        self.n_front = 4 * GDN_WIDTH
        self.n_ab = 2 * HEADS
        self.n_back = 2 * LRU_WIDTH + 2 * d
        self.cols = self.n_front + self.n_ab + self.n_back
        self.pcols = self.n_front + self.n_back + LANE
        self.z_base = 3 * GDN_WIDTH // HEAD_DIM
        self.xb_base = self.n_front // LRU_BLOCK_DIM
        self.gg_off = self.n_front + 2 * LRU_WIDTH
        assert self.gg_off % d == 0
        self.gg_base = self.gg_off // d
        self.ab_base = (self.n_front + self.n_back) // LANE

    def _ranges(self):
        f, ab = self.n_front, self.n_ab
        return [(0, f), (f + ab, self.cols), (f, f + ab)]

    def permuted_from_stack(self, stack):
        blk = stack.shape[2]
        pieces = []
        for lo, hi in self._ranges():
            while lo < hi:
                k = lo // blk
                end = min(hi, (k + 1) * blk)
                pieces.append(stack[k, :, lo - k * blk:end - k * blk])
                lo = end
        pieces.append(jnp.zeros((stack.shape[1], LANE - self.n_ab), stack.dtype))
        return jnp.concatenate(pieces, axis=1)

    def stack_from_permuted(self, w):
        blk = self.cols // N_DEV
        where = {}
        pos = 0
        for lo, hi in self._ranges():
            where[lo] = pos
            pos += hi - lo
        cuts = sorted(lo for lo, _ in self._ranges()) + [self.cols]
        blocks = []
        for k in range(N_DEV):
            lo, pieces = k * blk, []
            while lo < (k + 1) * blk:
                start = max(c for c in cuts if c <= lo)
                end = min((k + 1) * blk, min(c for c in cuts if c > lo))
                pieces.append(w[:, where[start] + lo - start:where[start] + end - start])
                lo = end
            blocks.append(pieces[0] if len(pieces) == 1 else jnp.concatenate(pieces, axis=1))
        return jnp.stack(blocks)


def _cols_from_stack(stack):
    return jnp.concatenate([stack[j] for j in range(stack.shape[0])], axis=1)


def _stack_from_cols(w):
    n = w.shape[1] // N_DEV
    return jnp.stack([w[:, j * n:(j + 1) * n] for j in range(N_DEV)])


def _after(v, *tokens):
    for t in tokens:
        v = v + t[0, 0]
    return v


def _layer_fwd(x, p, late, lay, tag):
    s, d = x.shape
    nc = s // CHUNK
    h = _rms_fwd(x, p["attn_norm"], f"rms1_fwd{tag}")
    proj, ab = _matmul(h, p["w_in"], "nn", f"in_proj{tag}", [BF], tm=2048, tn=1152, tail=True)
    qkv = _conv_silu_fwd(proj, p["gdn_conv_w"], f"gdn_conv_fwd{tag}")
    gc, beta = _gdn_gates_fwd(ab, p["gdn_a_log"], p["gdn_dt_bias"], 0, f"gdn_gates_fwd{tag}")
    grow = gc[:, :HEADS].reshape(nc, CHUNK, HEADS).transpose(0, 2, 1)
    o, states, tinv = _gdn_chunk_fwd(qkv, gc, beta, grow, f"gdn_chunk_fwd{tag}")
    og = _gdn_post_fwd(o, proj, p["gdn_norm"], lay.z_base, f"gdn_post_fwd{tag}")
    ol, hs, decay = _lru_fwd(proj, p["lru_conv_w"], p["lru_conv_b"], p["lru_w_a"], p["lru_b_a"], p["lru_w_x"], p["lru_b_x"],
                      p["lru_lambda"], lay.xb_base, f"lru_fwd{tag}")
    late[0](hs)
    yg = _matmul(og, p["w_branch_gdn"], "nn", f"branch_gdn{tag}", [BF])
    yl = _matmul(ol, p["w_branch_lru"], "nn", f"branch_lru{tag}", [BF])
    merged = _merge_fwd(proj, yg, yl, lay.gg_base, f"merge_fwd{tag}")
    x1 = _matmul(merged, p["w_out"], "nn", f"out_proj{tag}", [F32], epi=lambda acc, r: (acc + r,), extras=[x])
    h2 = _rms_fwd(x1, p["mlp_norm"], f"rms2_fwd{tag}")
    late[1](h2)
    u, act = _matmul(h2, p["w_up"], "nn", f"mlp_up{tag}", [BF, BF],
                     epi=lambda acc: (acc, jnp.square(jnp.maximum(acc, 0.0))), tm=2048, shards=True)
    x2 = _matmul(act, p["w_down"], "nn", f"mlp_down{tag}", [F32], epi=lambda acc, r: (acc + r,), extras=[x1])
    saved = dict(x=x, h=h, proj=proj, ab=ab, qkv=qkv, gc=gc, beta=beta, grow=grow, o=o, states=states, tinv=tinv, og=og, ol=ol, hs=hs, decay=decay,
                 yg=yg, yl=yl, merged=merged, x1=x1, h2=h2, u=u, act=act)
    return x2, saved


def _layer_bwd(dx2, p, sv, send, send_small, after_last_send, after, lay, tag):
    s, d = dx2.shape
    nc = s // CHUNK
    small = {}
    du = _matmul(dx2, p["w_down"], "nt", f"d_act{tag}", [BF],
                 epi=lambda acc, u: (acc * (2.0 * jnp.maximum(u.astype(F32), 0.0)),), extras=[sv["u"]], after=after)
    sent = send({"w_down": _matmul(sv["act"], dx2, "tn", f"dw_down{tag}", [BF]).reshape(N_DEV, -1, d),
                 "w_up": _matmul(sv["h2"], du, "tn", f"dw_up{tag}", [BF], shards=True)})
    dh2 = _matmul(du, p["w_up"], "nt", f"d_h2{tag}", [BF], shards=True)
    dx1, g = _rms_bwd(sv["x1"], _after(p["mlp_norm"], sent), dh2, dx2, f"rms2_bwd{tag}")
    small["mlp_norm"] = g.reshape(-1)
    dmerged = _matmul(dx1, p["w_out"], "nt", f"d_merged{tag}", [BF])
    dwout = _matmul(sv["merged"], dx1, "tn", f"dw_out{tag}", [BF]).reshape(N_DEV, -1, d)
    dgg, dgl, dyg, dyl = _merge_bwd(sv["proj"], sv["yg"], sv["yl"], dmerged, lay.gg_base, f"merge_bwd{tag}")
    sent = send({"w_out": dwout,
                 "w_branch_gdn": _stack_from_cols(_matmul(sv["og"], dyg, "tn", f"dw_bg{tag}", [BF])),
                 "w_branch_lru": _stack_from_cols(_matmul(sv["ol"], dyl, "tn", f"dw_bl{tag}", [BF]))})
    dog = _matmul(dyg, p["w_branch_gdn"], "nt", f"d_og{tag}", [BF])
    dol = _matmul(dyl, p["w_branch_lru"], "nt", f"d_ol{tag}", [BF])
    (dxb, dyb, dcw, dcb, dwa, dba, dwx, dbx, dlam) = _lru_bwd(
        sv["proj"], sv["hs"], sv["decay"], dol, p["lru_conv_w"], p["lru_conv_b"], p["lru_w_a"], p["lru_b_a"], p["lru_w_x"],
        p["lru_b_x"], _after(p["lru_lambda"], sent), lay.xb_base, f"lru_bwd{tag}")
    small.update(lru_conv_w=dcw, lru_conv_b=dcb.reshape(-1), lru_w_a=dwa, lru_b_a=dba.reshape(-1), lru_w_x=dwx,
                 lru_b_x=dbx.reshape(-1), lru_lambda=dlam.reshape(-1))
    do, dz, g = _gdn_post_bwd(sv["o"], sv["proj"], p["gdn_norm"], dog, lay.z_base, f"gdn_post_bwd{tag}")
    small["gdn_norm"] = g.reshape(-1)
    dqkv_c, dgc_col, dbeta, dgrow = _gdn_chunk_bwd(sv["qkv"], sv["gc"], sv["beta"], sv["grow"], sv["states"], sv["tinv"], do,
                                                   f"gdn_chunk_bwd{tag}")
    dqkv, dgcw = _conv_silu_bwd(sv["proj"], p["gdn_conv_w"], dqkv_c, f"gdn_conv_bwd{tag}")
    small["gdn_conv_w"] = dgcw
    dgc_row = jnp.pad(dgrow.transpose(0, 2, 1).reshape(s, HEADS), ((0, 0), (0, LANE - HEADS)))
    dab, dal, ddt = _gdn_gates_bwd(sv["ab"], p["gdn_a_log"], p["gdn_dt_bias"], dgc_col, dgc_row, dbeta, 0,
                                   f"gdn_gates_bwd{tag}")
    small["gdn_a_log"] = dal[0, :HEADS]
    small["gdn_dt_bias"] = ddt[0, :HEADS]
    dproj = jnp.concatenate([dqkv, dz, dxb, dyb, dgg, dgl, dab], axis=1)
    sent = send_small(small, "main")
    dwin = _matmul(sv["h"], dproj, "tn", f"dw_in{tag}", [BF], tn=1152, after=sent)
    sent = send({"w_in": lay.stack_from_permuted(dwin)})
    dh = _matmul(dproj, p["w_in"], "nt", f"d_h{tag}", [BF], tk=3456, after=sent)
    dx, g = _rms_bwd(sv["x"], p["attn_norm"], dh, dx1, f"rms1_bwd{tag}")
    after_last_send(send_small({"attn_norm": g.reshape(-1)}, "attn"))
    return dx


_BIG = ("w_in", "w_branch_gdn", "w_branch_lru", "w_out", "w_up", "w_down")
_ROW_SHARDED = ("w_out", "w_down")
_CONV = ("gdn_conv_w", "lru_conv_w")
_SMALL = ("attn_norm", "gdn_a_log", "gdn_dt_bias", "gdn_norm", "lru_conv_b", "lru_w_a", "lru_b_a", "lru_w_x",
          "lru_b_x", "lru_lambda", "mlp_norm")
_ORDER = ("attn_norm", "w_in", "gdn_conv_w", "gdn_a_log", "gdn_dt_bias", "gdn_norm", "lru_conv_w", "lru_conv_b",
          "lru_w_a", "lru_b_a", "lru_w_x", "lru_b_x", "lru_lambda", "w_branch_gdn", "w_branch_lru", "w_out",
          "mlp_norm", "w_up", "w_down", "final_norm")


def kernel(x, attn_norm, w_in, gdn_conv_w, gdn_a_log, gdn_dt_bias, gdn_norm, lru_conv_w, lru_conv_b, lru_w_a, lru_b_a, lru_w_x, lru_b_x, lru_lambda, w_branch_gdn, w_branch_lru, w_out, mlp_norm, w_up, w_down, final_norm, loss_target, m_attn_norm, m_w_in, m_gdn_conv_w, m_gdn_a_log, m_gdn_dt_bias, m_gdn_norm, m_lru_conv_w, m_lru_conv_b, m_lru_w_a, m_lru_b_a, m_lru_w_x, m_lru_b_x, m_lru_lambda, m_w_branch_gdn, m_w_branch_lru, m_w_out, m_mlp_norm, m_w_up, m_w_down, m_final_norm, v_attn_norm, v_w_in, v_gdn_conv_w, v_gdn_a_log, v_gdn_dt_bias, v_gdn_norm, v_lru_conv_w, v_lru_conv_b, v_lru_w_a, v_lru_b_a, v_lru_w_x, v_lru_b_x, v_lru_lambda, v_w_branch_gdn, v_w_branch_lru, v_w_out, v_mlp_norm, v_w_up, v_w_down, v_final_norm):
    w = dict(attn_norm=attn_norm, w_in=w_in, gdn_conv_w=gdn_conv_w, gdn_a_log=gdn_a_log, gdn_dt_bias=gdn_dt_bias, gdn_norm=gdn_norm, lru_conv_w=lru_conv_w, lru_conv_b=lru_conv_b, lru_w_a=lru_w_a, lru_b_a=lru_b_a, lru_w_x=lru_w_x, lru_b_x=lru_b_x, lru_lambda=lru_lambda, w_branch_gdn=w_branch_gdn, w_branch_lru=w_branch_lru, w_out=w_out, mlp_norm=mlp_norm, w_up=w_up, w_down=w_down, final_norm=final_norm)
    m = dict(attn_norm=m_attn_norm, w_in=m_w_in, gdn_conv_w=m_gdn_conv_w, gdn_a_log=m_gdn_a_log, gdn_dt_bias=m_gdn_dt_bias, gdn_norm=m_gdn_norm, lru_conv_w=m_lru_conv_w, lru_conv_b=m_lru_conv_b, lru_w_a=m_lru_w_a, lru_b_a=m_lru_b_a, lru_w_x=m_lru_w_x, lru_b_x=m_lru_b_x, lru_lambda=m_lru_lambda, w_branch_gdn=m_w_branch_gdn, w_branch_lru=m_w_branch_lru, w_out=m_w_out, mlp_norm=m_mlp_norm, w_up=m_w_up, w_down=m_w_down, final_norm=m_final_norm)
    v = dict(attn_norm=v_attn_norm, w_in=v_w_in, gdn_conv_w=v_gdn_conv_w, gdn_a_log=v_gdn_a_log, gdn_dt_bias=v_gdn_dt_bias, gdn_norm=v_gdn_norm, lru_conv_w=v_lru_conv_w, lru_conv_b=v_lru_conv_b, lru_w_a=v_lru_w_a, lru_b_a=v_lru_b_a, lru_w_x=v_lru_w_x, lru_b_x=v_lru_b_x, lru_lambda=v_lru_lambda, w_branch_gdn=v_w_branch_gdn, w_branch_lru=v_w_branch_lru, w_out=v_w_out, mlp_norm=v_mlp_norm, w_up=v_w_up, w_down=v_w_down, final_norm=v_final_norm)
    depth = w_in.shape[0]
    x = x[0]
    target = loss_target[0]
    s, d = x.shape
    lay = _Layout(d)
    assert lay.cols == w_in.shape[2] * N_DEV
    cx, cy, cc = _place()
    me = 4 * cx + 2 * cy + cc

    me_arr = me.astype(jnp.int32).reshape(1)

    early, mixer, mlp = ("w_in",) + _CONV, ("w_branch_gdn", "w_branch_lru", "w_out"), ("w_up", "w_down")

    def shard(n, l):
        return w[n][l].astype(BF) if n in _BIG else w[n][l]

    def joined(name, stack):
        if name == "w_in":
            return lay.permuted_from_stack(stack)
        if name == "w_up":
            return stack
        return stack.reshape(-1, stack.shape[-1]) if name in _ROW_SHARDED else _cols_from_stack(stack)

    def start_weights(l, names, after):
        flights, token = _send_start([shard(n, l) for n in names], True, after, f"weights_start_l{l}")
        return dict(zip(names, flights)), token

    def wait_weights(l, flights, names, p, after, which):
        got = _send_wait([flights[n] for n in names], after, f"weights_wait_{which}_l{l}")
        for n, (own, landed) in zip(names, got):
            p[n] = joined(n, lax.dynamic_update_slice_in_dim(landed, own[None], me, axis=0))

    saved, params = [], []
    xl = x
    first = dict(zip(early + mixer, _all_gather([shard(n, 0) for n in early + mixer], "gather_first_weights")))
    flights, token = start_weights(0, mlp, first["w_in"])
    for l in range(depth):
        p = {n: w[n][l] for n in _SMALL}
        if l == 0:
            p.update({n: joined(n, g) for n, g in first.items()})
        else:
            wait_weights(l, flights, early, p, xl, "early")
        mine = flights
        if l + 1 < depth:
            flights, token = start_weights(l + 1, early + mixer + mlp, token)
        p["attn_norm"] = _after(p["attn_norm"], token)
        late = [functools.partial(wait_weights, l, mine, names, p, which=which) if names[0] in mine
                else (lambda after: None) for names, which in ((mixer, "mixer"), (mlp, "mlp"))]
        xl, sv = _layer_fwd(xl, p, late, lay, f"_l{l}")
        saved.append(sv)
        params.append(p)
    dx, loss_row, dfinal = _loss_head(xl, final_norm, target, "loss_head")
    loss = lax.psum(loss_row[0, 0], ("x", "y", "c"))

    out = {n: None for n in _BIG}
    pending = {l: {} for l in range(depth)}

    def sender(l):
        def send(stacks):
            names = tuple(stacks)
            fl, token = _send_start([stacks[n] for n in names], False, me_arr, f"grads_start_{'_'.join(names)}_l{l}")
            pending[l].update(zip(names, fl))
            return token
        return send

    def finish(l, names, after, which):
        got = _send_wait([pending[l][n] for n in names], after, f"grads_wait_{which}_l{l}")
        for n, (mine, landed) in zip(names, got):
            out[n] = _adamw_sharded(mine, landed, me_arr, w[n], m[n], v[n], l, out[n], after, f"adamw_{n}_l{l}")
            after = out[n][0]
        return after

    small_names = _SMALL + _CONV
    small_sent = []

    def small_sender(l):
        def send_small(small, which):
            small = dict(small)
            if which == "main" and l == depth - 1:
                small["final_norm"] = dfinal.reshape(-1)
            names = tuple(small)
            flights, token = _send_start([_pack([small[n] for n in names])], True, me_arr,
                                         f"small_grads_start_{which}_l{l}")
            small_sent.append((l, which, names, [small[n] for n in names], flights[0]))
            return token
        return send_small

    last = [None]

    def hook_for(l):
        def hook(token):
            last[0] = finish(l + 1, _BIG, token, "all") if l + 1 < depth else token
        return hook

    for l in reversed(range(depth)):
        dx = _layer_bwd(dx, params[l], saved[l], sender(l), small_sender(l), hook_for(l), last[0], lay, f"_l{l}")

    last[0] = finish(0, mixer + mlp, last[0], "late")
    got = _send_wait([f for *_, f in small_sent], last[0], "small_grads_wait")
    summed = [{} for _ in range(depth)]
    for (l, which, names, arrays, _), (part, landed) in zip(small_sent, got):
        total = _sum_devices(lax.dynamic_update_slice_in_dim(landed, part[None], me, axis=0),
                             f"sum_small_grads_{which}_l{l}")
        summed[l].update(zip(names, _unpack(total, arrays)))
    grads = {n: jnp.stack([summed[l][n] for l in range(depth)]) for n in small_names}
    grads["final_norm"] = summed[depth - 1]["final_norm"]
    for n in _CONV:
        blk = w[n].shape[2]
        grads[n] = lax.dynamic_slice_in_dim(grads[n], me * blk, blk, axis=2)
    names = small_names + ("final_norm",)
    gp = _pack([grads[n] for n in names])
    upd = _adamw_packed(_pack([w[n] for n in names]), gp, _pack([m[n] for n in names]), _pack([v[n] for n in names]),
                        "adamw_small")
    like = [w[n] for n in names]
    unp = [_unpack(u, like) for u in upd]
    for i, n in enumerate(names):
        out[n] = (grads[n], unp[0][i], unp[1][i], unp[2][i])
    finish(0, ("w_in",), upd[0], "w_in")

    res = [loss, dx.reshape(1, s, d)]
    for k in range(4):
        res += [out[n][k] for n in _ORDER]
    return tuple(res)
```
